```python
import jax, jax.numpy as jnp
from jax import lax
import numpy as np

D_MODEL = 1024
BATCH = 4
SEQ = 8192
DEPTH = 1
DEC_BATCH = 32
DEC_SEQ = 16
PAST_LEN = 4096

CHUNK = 64
LEFT_CHUNKS = 8
ATT_WINDOW = LEFT_CHUNKS * CHUNK
BAND = (LEFT_CHUNKS + 1) * CHUNK
N_HEADS = 8
HEAD_DIM = 64
D_ATT = N_HEADS * HEAD_DIM
MAX_REL = 128
D_CONV = D_MODEL // 2
CONV_W = 3
N_GROUPS = 4
EXPERTS_PER_GROUP = 8
N_EXPERTS = N_GROUPS * EXPERTS_PER_GROUP
TOP_K = 2
D_EXPERT = 512
MOE_BLOCK = 128
EPS = 1e-6
IN_SIZES = [D_ATT, D_ATT, D_ATT, D_CONV, D_CONV, D_CONV, D_MODEL, D_MODEL]
IN_COLS = sum(IN_SIZES)
IN_SPLITS = [int(i) for i in np.cumsum(IN_SIZES)[:-1]]

kernel_name = 'chunk_stream_hybrid_step'


def rmsnorm(x, g):
    x32 = x.astype(jnp.float32)
    y = x32 * lax.rsqrt(jnp.mean(x32 * x32, axis=-1, keepdims=True) + EPS)
    return (y * g.astype(jnp.float32)).astype(x.dtype)


def ada_mod(c, w_ada, b_ada):
    m = jax.nn.silu(c) @ w_ada + b_ada
    return jnp.split(m[:, None, :], 6, axis=-1)


def band_attention(q, k, v, q_pos, k_pos, k_valid, rel_bias):
    s = jnp.einsum('nqhd,nkhd->nhqk', q, k).astype(jnp.float32) * (HEAD_DIM ** -0.5)
    rel = jnp.clip(q_pos[:, None] - k_pos[None, :], -MAX_REL, MAX_REL) + MAX_REL
    s = s + rel_bias[:, rel].astype(jnp.float32)[None]
    s = jnp.where(k_valid[None, None, None, :], s, -1e30)
    p = jax.nn.softmax(s, axis=-1)
    return jnp.einsum('nhqk,nkhd->nqhd', p.astype(v.dtype), v)


def prompt_attention(q, k, v, rel_bias):
    n, L = q.shape[:2]
    nc = L // CHUNK
    pad = ((0, 0), (ATT_WINDOW, 0), (0, 0), (0, 0))
    k_pad = jnp.pad(k, pad)
    v_pad = jnp.pad(v, pad)
    q_ch = q.reshape(n, nc, CHUNK, N_HEADS, HEAD_DIM).swapaxes(0, 1)

    def one_chunk(args):
        q_c, i = args
        start = i * CHUNK
        k_b = lax.dynamic_slice_in_dim(k_pad, start, BAND, axis=1)
        v_b = lax.dynamic_slice_in_dim(v_pad, start, BAND, axis=1)
        q_pos = start + jnp.arange(CHUNK, dtype=jnp.int32)
        k_pos = start - ATT_WINDOW + jnp.arange(BAND, dtype=jnp.int32)
        return band_attention(q_c, k_b, v_b, q_pos, k_pos, k_pos >= 0, rel_bias)

    o = lax.map(one_chunk, (q_ch, jnp.arange(nc, dtype=jnp.int32)))
    return o.swapaxes(0, 1).reshape(n, L, D_ATT)


def sample_attention(q, k_new, v_new, k_cache, v_cache, rel_bias):
    n, L = q.shape[:2]
    n_old = k_cache.shape[1]
    k = jnp.concatenate([k_cache.astype(k_new.dtype), k_new], axis=1)
    v = jnp.concatenate([v_cache.astype(v_new.dtype), v_new], axis=1)
    q_pos = PAST_LEN + jnp.arange(L, dtype=jnp.int32)
    k_pos = jnp.concatenate([PAST_LEN - n_old + jnp.arange(n_old, dtype=jnp.int32), q_pos])
    o = band_attention(q, k, v, q_pos, k_pos, k_pos >= 0, rel_bias)
    return o.reshape(n, L, D_ATT)


def short_conv(cb, cc, cv, prev, conv_w, conv_b):
    u = cc * cv
    u_pad = jnp.concatenate([prev.astype(u.dtype), u], axis=1)
    L = u.shape[1]
    y = sum(conv_w[j] * u_pad[:, j:j + L] for j in range(CONV_W)) + conv_b
    return cb * y, u_pad[:, -(CONV_W - 1):]


def route(h, w_group, b_group, w_expert, b_expert):
    lg = (h @ w_group + b_group).astype(jnp.float32)
    pg = jax.nn.softmax(lg, axis=-1)
    g = jnp.argmax(lg, axis=-1).astype(jnp.int32)
    le = (h @ w_expert + b_expert).astype(jnp.float32).reshape(-1, N_GROUPS, EXPERTS_PER_GROUP)
    le_g = jnp.take_along_axis(le, g[:, None, None], axis=1)[:, 0]
    top_p, top_i = lax.top_k(jax.nn.softmax(le_g, axis=-1), TOP_K)
    top_p = top_p / jnp.sum(top_p, axis=-1, keepdims=True)
    weights = jnp.take_along_axis(pg, g[:, None], axis=1) * top_p
    experts = g[:, None] * EXPERTS_PER_GROUP + top_i.astype(jnp.int32)
    return experts, weights


def moe_ffn(h, experts, weights, w_gate, w_up, w_down):
    T, D = h.shape
    A = T * TOP_K
    flat_e = experts.reshape(-1)
    flat_t = jnp.repeat(jnp.arange(T, dtype=jnp.int32), TOP_K)
    order = jnp.argsort(flat_e)
    se = flat_e[order]
    st = flat_t[order]
    counts = jnp.bincount(flat_e, length=N_EXPERTS).astype(jnp.int32)
    pcounts = (counts + MOE_BLOCK - 1) // MOE_BLOCK * MOE_BLOCK
    starts = jnp.cumsum(counts) - counts
    pends = jnp.cumsum(pcounts)
    pstarts = pends - pcounts
    dest = pstarts[se] + jnp.arange(A, dtype=jnp.int32) - starts[se]
    n_blocks = -(-A // MOE_BLOCK) + N_EXPERTS
    P = n_blocks * MOE_BLOCK
    slot_tok = jnp.full((P,), T, jnp.int32).at[dest].set(st)
    h_pad = jnp.concatenate([h, jnp.zeros((1, D), h.dtype)], axis=0)
    xb = h_pad[slot_tok].reshape(n_blocks, MOE_BLOCK, D)
    blk_e = jnp.minimum(jnp.searchsorted(pends, jnp.arange(n_blocks, dtype=jnp.int32) * MOE_BLOCK, side='right'), N_EXPERTS - 1)

    def expert_block(args):
        xblk, e = args
        return (jax.nn.silu(xblk @ w_gate[e]) * (xblk @ w_up[e])) @ w_down[e]

    yb = lax.map(expert_block, (xb, blk_e)).reshape(P, D)
    y_assign = yb[dest][jnp.argsort(order)].reshape(T, TOP_K, D)
    return jnp.einsum('tkd,tk->td', y_assign, weights.astype(h.dtype))


def trunk_layer(x, c, kv_cache, conv_prev, w_ada, b_ada, norm1_g, norm2_g, w_in, rel_bias, conv_w, conv_b,
                w_pa, w_pb, w_o, w_group, b_group, w_expert, b_expert, w_e_gate, w_e_up, w_e_down):
    n, L = x.shape[:2]
    sh1, sc1, gt1, sh2, sc2, gt2 = ada_mod(c, w_ada, b_ada)
    h = rmsnorm(x, norm1_g) * (1 + sc1) + sh1
    q, k, v, cb, cc, cv, ga, gb = jnp.split(h @ w_in, IN_SPLITS, axis=-1)
    q = q.reshape(n, L, N_HEADS, HEAD_DIM)
    k = k.reshape(n, L, N_HEADS, HEAD_DIM)
    v = v.reshape(n, L, N_HEADS, HEAD_DIM)
    if kv_cache is None:
        att = prompt_attention(q, k, v, rel_bias)
        keep = min(ATT_WINDOW, L)
        new_k, new_v = k[:, L - keep:], v[:, L - keep:]
        conv_prev = jnp.zeros((n, CONV_W - 1, D_CONV), x.dtype)
    else:
        att = sample_attention(q, k, v, kv_cache[0], kv_cache[1], rel_bias)
        new_k, new_v = k, v
    conv_out, new_conv = short_conv(cb, cc, cv, conv_prev, conv_w, conv_b)
    mix = (jax.nn.sigmoid(ga) * (att @ w_pa) + jax.nn.sigmoid(gb) * (conv_out @ w_pb)) @ w_o
    x = x + gt1 * mix
    h2 = (rmsnorm(x, norm2_g) * (1 + sc2) + sh2).reshape(n * L, D_MODEL)
    experts, weights = route(h2, w_group, b_group, w_expert, b_expert)
    ffn = moe_ffn(h2, experts, weights, w_e_gate, w_e_up, w_e_down).reshape(n, L, D_MODEL)
    x = x + gt2 * ffn
    return x, new_k, new_v, new_conv


def setup_inputs(seed: int = 0) -> dict:
    key = jax.random.key(seed)
    ks = jax.random.split(key, 27)
    nrm = lambda k, shape, s: jax.random.normal(k, shape, jnp.float32) * s
    kv_rows = min(ATT_WINDOW, PAST_LEN)
    return {
        'x_prompt': nrm(ks[0], (BATCH, SEQ, D_MODEL), 1.0),
        'x_sample': nrm(ks[1], (DEC_BATCH, DEC_SEQ, D_MODEL), 1.0),
        'cache_attn_k': nrm(ks[2], (DEPTH, DEC_BATCH, kv_rows, N_HEADS, HEAD_DIM), 1.0),
        'cache_attn_v': nrm(ks[3], (DEPTH, DEC_BATCH, kv_rows, N_HEADS, HEAD_DIM), 1.0),
        'state_conv': nrm(ks[4], (DEPTH, DEC_BATCH, CONV_W - 1, D_CONV), 1.0),
        'c_prompt': nrm(ks[5], (BATCH, D_MODEL), 1.0),
        'c_sample': nrm(ks[6], (DEC_BATCH, D_MODEL), 1.0),
        'w_ada': nrm(ks[7], (DEPTH, D_MODEL, 6 * D_MODEL), 0.5 * D_MODEL ** -0.5),
        'b_ada': nrm(ks[8], (DEPTH, 6 * D_MODEL), 0.02),
        'norm1_g': 1.0 + nrm(ks[9], (DEPTH, D_MODEL), 0.02),
        'norm2_g': 1.0 + nrm(ks[10], (DEPTH, D_MODEL), 0.02),
        'w_in': nrm(ks[11], (DEPTH, D_MODEL, IN_COLS), D_MODEL ** -0.5),
        'rel_bias': nrm(ks[12], (DEPTH, N_HEADS, 2 * MAX_REL + 1), 0.5),
        'conv_w': nrm(ks[13], (DEPTH, CONV_W, D_CONV), CONV_W ** -0.5),
        'conv_b': nrm(ks[14], (DEPTH, D_CONV), 0.02),
        'w_pa': nrm(ks[15], (DEPTH, D_ATT, D_MODEL), D_ATT ** -0.5),
        'w_pb': nrm(ks[16], (DEPTH, D_CONV, D_MODEL), D_CONV ** -0.5),
        'w_o': nrm(ks[17], (DEPTH, D_MODEL, D_MODEL), D_MODEL ** -0.5),
        'w_group': nrm(ks[18], (DEPTH, D_MODEL, N_GROUPS), D_MODEL ** -0.5),
        'b_group': nrm(ks[19], (DEPTH, N_GROUPS), 0.01),
        'w_expert': nrm(ks[20], (DEPTH, D_MODEL, N_EXPERTS), D_MODEL ** -0.5),
        'b_expert': nrm(ks[21], (DEPTH, N_EXPERTS), 0.01),
        'w_e_gate': nrm(ks[22], (DEPTH, N_EXPERTS, D_MODEL, D_EXPERT), D_MODEL ** -0.5),
        'w_e_up': nrm(ks[23], (DEPTH, N_EXPERTS, D_MODEL, D_EXPERT), D_MODEL ** -0.5),
        'w_e_down': nrm(ks[24], (DEPTH, N_EXPERTS, D_EXPERT, D_MODEL), D_EXPERT ** -0.5),
        'final_g': 1.0 + nrm(ks[25], (D_MODEL,), 0.02),
    }


def reference(x_prompt, x_sample, cache_attn_k, cache_attn_v, state_conv, c_prompt, c_sample,
              w_ada, b_ada, norm1_g, norm2_g, w_in, rel_bias, conv_w, conv_b, w_pa, w_pb, w_o,
              w_group, b_group, w_expert, b_expert, w_e_gate, w_e_up, w_e_down, final_g):
    xp, xs = x_prompt, x_sample
    kp_l, vp_l, cp_l, ks_l, vs_l, cs_l = [], [], [], [], [], []
    for l in range(DEPTH):
        lw = (w_ada[l], b_ada[l], norm1_g[l], norm2_g[l], w_in[l], rel_bias[l], conv_w[l], conv_b[l],
              w_pa[l], w_pb[l], w_o[l], w_group[l], b_group[l], w_expert[l], b_expert[l],
              w_e_gate[l], w_e_up[l], w_e_down[l])
        xp, kp, vp, cp = trunk_layer(xp, c_prompt, None, None, *lw)
        xs, ks_, vs_, cs_ = trunk_layer(xs, c_sample, (cache_attn_k[l], cache_attn_v[l]), state_conv[l], *lw)
        kp_l.append(kp); vp_l.append(vp); cp_l.append(cp)
        ks_l.append(ks_); vs_l.append(vs_); cs_l.append(cs_)
    y_prompt = rmsnorm(xp, final_g)
    y_sample = rmsnorm(xs, final_g)
    return (y_prompt, y_sample, jnp.stack(kp_l), jnp.stack(vp_l), jnp.stack(cp_l),
            jnp.stack(ks_l), jnp.stack(vs_l), jnp.stack(cs_l))
```

```python
import functools

import numpy as np
import jax
import jax.numpy as jnp
from jax import lax
from jax.experimental import pallas as pl
from jax.experimental.pallas import tpu as pltpu

F32 = jnp.float32
BF16 = jnp.bfloat16

D_MODEL = 1024
CHUNK = 64
LEFT = 8
WINDOW = LEFT * CHUNK
BAND = WINDOW + CHUNK
N_HEADS = 8
HEAD_DIM = 64
D_ATT = N_HEADS * HEAD_DIM
QUAD = 256
MAX_REL = 128
D_CONV = 512
N_GROUPS = 4
EPG = 8
N_EXPERTS = 32
D_EXPERT = 512
EPS = 1e-6
NEG = -1e30

TL = 256
RING = WINDOW + TL
MOE_BLK = 256
TM = 512
RLANES = 128
VMEM_LIMIT = 56 * 1024 * 1024


def _const_spec(shape):
    nd = len(shape)
    return pl.BlockSpec(shape, lambda *_: (0,) * nd, pipeline_mode=pl.Buffered(1))


def _dot(a, b):
    return jnp.dot(a, b, preferred_element_type=F32)


def _sigmoid(x):
    return 1.0 / (1.0 + jnp.exp(-x))


def _rms(x, g):
    ms = jnp.mean(x * x, axis=-1, keepdims=True)
    return x * lax.rsqrt(ms + EPS) * g


def _ada_kernel(c_ref, w_ref, b_ref, o_ref):
    c = c_ref[...]
    s = c * _sigmoid(c)
    o_ref[...] = jnp.dot(s, w_ref[...], preferred_element_type=F32,
                         precision=lax.Precision.HIGHEST) + b_ref[...]


def _ada(c_all, w_ada, b_ada):
    n = c_all.shape[0]
    nb = 1024
    return pl.pallas_call(
        _ada_kernel,
        grid=(6 * D_MODEL // nb,),
        in_specs=[pl.BlockSpec((n, D_MODEL), lambda i: (0, 0)),
                  pl.BlockSpec((D_MODEL, nb), lambda i: (0, i)),
                  pl.BlockSpec((1, nb), lambda i: (0, i))],
        out_specs=pl.BlockSpec((n, nb), lambda i: (0, i)),
        out_shape=jax.ShapeDtypeStruct((n, 6 * D_MODEL), F32),
        name="ada",
    )(c_all, w_ada, b_ada.reshape(1, -1))


def _attend(q, kb, vb, bias, lim):
    r = q.shape[0]
    nk = kb.shape[0]
    assert r & (r - 1) == 0
    qt = jnp.concatenate([q] * 4, axis=0)
    rowh = lax.broadcasted_iota(jnp.int32, (4 * r, QUAD), 0) >> (r.bit_length() - 1)
    laneh = lax.broadcasted_iota(jnp.int32, (4 * r, QUAD), 1) >> 6
    qm = jnp.where(rowh == laneh, qt, jnp.zeros_like(qt))
    s = lax.dot_general(qm, kb, (((1,), (1,)), ((), ())), preferred_element_type=F32)
    s = s + bias
    if lim is not None:
        col = lax.broadcasted_iota(jnp.int32, (4 * r, nk), 1)
        s = jnp.where(col >= lim, s, NEG)
    m = jnp.max(s, axis=1, keepdims=True)
    p = jnp.exp(s - m)
    l = jnp.sum(p, axis=1, keepdims=True)
    o = _dot(p.astype(BF16), vb) * (1.0 / l)
    lane_o = lax.broadcasted_iota(jnp.int32, (r, QUAD), 1) >> 6
    out = o[0:r]
    for h in range(1, 4):
        out = jnp.where(lane_o == h, o[h * r:(h + 1) * r], out)
    return out


def _route(h2, wcat_ref, whi_ref, br_ref, cnt):
    r = h2.shape[0]
    hi = h2.astype(BF16)
    lo = (h2 - hi.astype(F32)).astype(BF16)
    z = _dot(hi, wcat_ref[...])
    logits = z[:, :RLANES] + z[:, RLANES:] + _dot(lo, whi_ref[...]) + br_ref[...]
    lane = lax.broadcasted_iota(jnp.int32, (r, RLANES), 1)
    lane_f = lane.astype(F32)
    big = jnp.float32(1000.0)

    lg = jnp.where((lane >= N_EXPERTS) & (lane < N_EXPERTS + N_GROUPS), logits, NEG)
    mg = jnp.max(lg, axis=1, keepdims=True)
    gi = jnp.min(jnp.where(lg == mg, lane_f, big), axis=1, keepdims=True) - N_EXPERTS
    pg = 1.0 / jnp.sum(jnp.exp(lg - mg), axis=1, keepdims=True)

    grp_of_lane = (lane >> 3).astype(F32)
    le = jnp.where((lane < N_EXPERTS) & (grp_of_lane == gi), logits, NEG)
    m1 = jnp.max(le, axis=1, keepdims=True)
    i1 = jnp.min(jnp.where(le == m1, lane_f, big), axis=1, keepdims=True)
    sel1 = lane_f == i1
    le2 = jnp.where(sel1, NEG, le)
    m2 = jnp.max(le2, axis=1, keepdims=True)
    i2 = jnp.min(jnp.where(le2 == m2, lane_f, big), axis=1, keepdims=True)
    sel2 = lane_f == i2
    rr = jnp.exp(m2 - m1)
    inv = pg / (1.0 + rr)
    w1 = inv
    w2 = inv * rr

    oh = jnp.where(sel1 | sel2, 1.0, 0.0).astype(F32)
    ri = lax.broadcasted_iota(jnp.int32, (r, r), 0)
    ci = lax.broadcasted_iota(jnp.int32, (r, r), 1)
    tri = jnp.where(ri > ci, 1.0, 0.0).astype(BF16)
    before = _dot(tri, oh.astype(BF16)) + cnt
    r1 = jnp.sum(jnp.where(sel1, before, 0.0), axis=1, keepdims=True)
    r2 = jnp.sum(jnp.where(sel2, before, 0.0), axis=1, keepdims=True)
    new_cnt = cnt + jnp.sum(oh, axis=0, keepdims=True)

    route = jnp.where(lane == 0, i1, 0.0)
    route = jnp.where(lane == 1, i2, route)
    route = jnp.where(lane == 2, r1, route)
    route = jnp.where(lane == 3, r2, route)
    route = jnp.where(lane == 4, w1, route)
    route = jnp.where(lane == 5, w2, route)
    return route, new_cnt


def _prompt_kernel(x_ref, mod_ref, g1_ref, g2_ref, win_ref, bias_ref, cw_ref, cbias_ref,
                   wpa_ref, wpb_ref, wo_ref, wcat_ref, whi_ref, br_ref,
                   x1_ref, h2_ref, ko_ref, vo_ref, uo_ref, route_ref, cnt_ref,
                   kring, vring, att_s, ucarry, cnt_s):
    b = pl.program_id(0)
    j = pl.program_id(1)

    @pl.when((b == 0) & (j == 0))
    def _():
        cnt_s[...] = jnp.zeros_like(cnt_s)

    @pl.when(j == 0)
    def _():
        kring[0:WINDOW, :] = jnp.zeros((WINDOW, D_ATT), BF16)
        vring[0:WINDOW, :] = jnp.zeros((WINDOW, D_ATT), BF16)
        ucarry[...] = jnp.zeros_like(ucarry)

    sh1 = mod_ref[0, 0:1, :]
    sc1 = mod_ref[0, 1:2, :]
    gt1 = mod_ref[0, 2:3, :]
    sh2 = mod_ref[0, 3:4, :]
    sc2 = mod_ref[0, 4:5, :]

    x = x_ref[0]
    h = _rms(x, g1_ref[...]) * (1.0 + sc1) + sh1
    hb = h.astype(BF16)

    qkv = _dot(hb, win_ref[:, 0:3 * D_ATT])
    q = (qkv[:, 0:D_ATT] * (HEAD_DIM ** -0.5)).astype(BF16)
    k = qkv[:, D_ATT:2 * D_ATT]
    v = qkv[:, 2 * D_ATT:3 * D_ATT]
    ko_ref[0] = k
    vo_ref[0] = v
    kring[WINDOW:RING, :] = k.astype(BF16)
    vring[WINDOW:RING, :] = v.astype(BF16)

    base = j * TL
    for c in range(TL // CHUNK):
        lim = WINDOW - (base + c * CHUNK)
        for qd in range(2):
            ls = slice(qd * QUAD, (qd + 1) * QUAD)
            o = _attend(q[c * CHUNK:(c + 1) * CHUNK, ls],
                        kring[c * CHUNK:c * CHUNK + BAND, ls],
                        vring[c * CHUNK:c * CHUNK + BAND, ls],
                        bias_ref[qd], lim)
            att_s[c * CHUNK:(c + 1) * CHUNK, ls] = o.astype(BF16)

    kring[0:WINDOW, :] = kring[TL:RING, :]
    vring[0:WINDOW, :] = vring[TL:RING, :]

    cbcv = _dot(hb, win_ref[:, 3 * D_ATT:3 * D_ATT + 3 * D_CONV])
    cb = cbcv[:, 0:D_CONV]
    u = cbcv[:, D_CONV:2 * D_CONV] * cbcv[:, 2 * D_CONV:3 * D_CONV]
    row = lax.broadcasted_iota(jnp.int32, (8, D_CONV), 0)
    prev = ucarry[...]
    r1 = pltpu.roll(u, 1, axis=0)
    r2 = pltpu.roll(u, 2, axis=0)
    u_m1 = jnp.concatenate(
        [jnp.where(row < 1, pltpu.roll(prev, 1, axis=0), r1[0:8]), r1[8:]], axis=0)
    u_m2 = jnp.concatenate(
        [jnp.where(row < 2, pltpu.roll(prev, 2, axis=0), r2[0:8]), r2[8:]], axis=0)
    yc = cw_ref[0:1, :] * u_m2 + cw_ref[1:2, :] * u_m1 + cw_ref[2:3, :] * u + cbias_ref[...]
    conv_out = (cb * yc).astype(BF16)
    ucarry[...] = u[TL - 8:TL, :]
    uo_ref[0] = u[TL - 8:TL, :]

    gates = _dot(hb, win_ref[:, 3 * D_ATT + 3 * D_CONV:])
    pa = _dot(att_s[...], wpa_ref[...])
    pb = _dot(conv_out, wpb_ref[...])
    mixin = _sigmoid(gates[:, 0:D_MODEL]) * pa + _sigmoid(gates[:, D_MODEL:]) * pb
    mix = _dot(mixin.astype(BF16), wo_ref[...])
    x1 = x + gt1 * mix
    x1_ref[0] = x1
    h2 = _rms(x1, g2_ref[...]) * (1.0 + sc2) + sh2
    h2_ref[0] = h2

    route, new_cnt = _route(h2, wcat_ref, whi_ref, br_ref, cnt_s[...])
    route_ref[0] = route
    cnt_s[...] = new_cnt
    cnt_ref[...] = new_cnt


def _prompt_main(x, mod, g1, g2, win, bias_q, cw, cbias, wpa, wpb, wo, wcat, whi, br):
    nb, seq, _ = x.shape
    nt = seq // TL
    keep = WINDOW // TL
    tile = lambda b, j: (b, j, 0)
    last = lambda b, j: (b, jnp.maximum(j - (nt - keep), 0), 0)
    perb = lambda b, j: (b, 0, 0)
    in_specs = [
        pl.BlockSpec((1, TL, D_MODEL), tile),
        pl.BlockSpec((1, 6, D_MODEL), perb),
        _const_spec(g1.shape), _const_spec(g2.shape), _const_spec(win.shape),
        _const_spec(bias_q.shape), _const_spec(cw.shape), _const_spec(cbias.shape),
        _const_spec(wpa.shape), _const_spec(wpb.shape), _const_spec(wo.shape),
        _const_spec(wcat.shape), _const_spec(whi.shape), _const_spec(br.shape),
    ]
    out_specs = [
        pl.BlockSpec((1, TL, D_MODEL), tile),
        pl.BlockSpec((1, TL, D_MODEL), tile),
        pl.BlockSpec((1, TL, D_ATT), last),
        pl.BlockSpec((1, TL, D_ATT), last),
        pl.BlockSpec((1, 8, D_CONV), perb),
        pl.BlockSpec((1, TL, RLANES), tile),
        pl.BlockSpec((1, RLANES), lambda b, j: (0, 0)),
    ]
    out_shape = [
        jax.ShapeDtypeStruct((nb, seq, D_MODEL), F32),
        jax.ShapeDtypeStruct((nb, seq, D_MODEL), F32),
        jax.ShapeDtypeStruct((nb, WINDOW, D_ATT), F32),
        jax.ShapeDtypeStruct((nb, WINDOW, D_ATT), F32),
        jax.ShapeDtypeStruct((nb, 8, D_CONV), F32),
        jax.ShapeDtypeStruct((nb, seq, RLANES), F32),
        jax.ShapeDtypeStruct((1, RLANES), F32),
    ]
    scratch = [
        pltpu.VMEM((RING, D_ATT), BF16), pltpu.VMEM((RING, D_ATT), BF16),
        pltpu.VMEM((TL, D_ATT), BF16), pltpu.VMEM((8, D_CONV), F32),
        pltpu.VMEM((1, RLANES), F32),
    ]
    return pl.pallas_call(
        _prompt_kernel,
        grid=(nb, nt),
        in_specs=in_specs, out_specs=out_specs, out_shape=out_shape,
        scratch_shapes=scratch,
        compiler_params=pltpu.CompilerParams(
            dimension_semantics=("arbitrary", "arbitrary"), vmem_limit_bytes=VMEM_LIMIT),
        name="prompt_main",
    )(x, mod, g1, g2, win, bias_q, cw, cbias, wpa, wpb, wo, wcat, whi, br)


def _sample_kernel(x_ref, mod_ref, ck_ref, cv_ref, up1_ref, up2_ref, cnt_in_ref,
                   g1_ref, g2_ref, win_ref, bias_ref, cw_ref, cbias_ref,
                   wpa_ref, wpb_ref, wo_ref, wcat_ref, whi_ref, br_ref,
                   x1_ref, h2_ref, ko_ref, vo_ref, uo_ref, route_ref, cnt_ref,
                   h_s, q_s, kn_s, vn_s, kband, vband, att_s, conv_s, *, nseq, slen):
    n = pl.program_id(0)
    ntok = nseq * slen

    @pl.when(n == 0)
    def _():
        def norm_body(i, carry):
            rows = pl.ds(pl.multiple_of(i * slen, slen), slen)
            xi = x_ref[rows, :]
            m = mod_ref[i]
            hi = _rms(xi, g1_ref[...]) * (1.0 + m[1:2, :]) + m[0:1, :]
            h_s[rows, :] = hi.astype(BF16)
            return carry
        lax.fori_loop(0, nseq, norm_body, 0)
        hb = h_s[...]
        qkv = _dot(hb, win_ref[:, 0:3 * D_ATT])
        q_s[...] = (qkv[:, 0:D_ATT] * (HEAD_DIM ** -0.5)).astype(BF16)
        k = qkv[:, D_ATT:2 * D_ATT]
        v = qkv[:, 2 * D_ATT:3 * D_ATT]
        ko_ref[...] = k
        vo_ref[...] = v
        kn_s[...] = k.astype(BF16)
        vn_s[...] = v.astype(BF16)

        cbcv = _dot(hb, win_ref[:, 3 * D_ATT:3 * D_ATT + 3 * D_CONV])
        cb = cbcv[:, 0:D_CONV]
        u = cbcv[:, D_CONV:2 * D_CONV] * cbcv[:, 2 * D_CONV:3 * D_CONV]
        pos = lax.broadcasted_iota(jnp.int32, (ntok, D_CONV), 0) & (slen - 1)
        u_m1 = jnp.where(pos < 1, up1_ref[...], pltpu.roll(u, 1, axis=0))
        u_m2 = jnp.where(pos < 2, up2_ref[...], pltpu.roll(u, 2, axis=0))
        yc = cw_ref[0:1, :] * u_m2 + cw_ref[1:2, :] * u_m1 + cw_ref[2:3, :] * u + cbias_ref[...]
        conv_s[...] = (cb * yc).astype(BF16)
        uo_ref[...] = u

    rows = pl.ds(pl.multiple_of(n * slen, slen), slen)
    kband[0:WINDOW, :] = ck_ref[0].astype(BF16)
    vband[0:WINDOW, :] = cv_ref[0].astype(BF16)
    kband[WINDOW:WINDOW + slen, :] = kn_s[rows, :]
    vband[WINDOW:WINDOW + slen, :] = vn_s[rows, :]
    qn = q_s[rows, :]
    outs = []
    for qd in range(2):
        ls = slice(qd * QUAD, (qd + 1) * QUAD)
        outs.append(_attend(qn[:, ls], kband[:, ls], vband[:, ls], bias_ref[qd], None))
    att_s[rows, :] = jnp.concatenate(outs, axis=1).astype(BF16)

    @pl.when(n == nseq - 1)
    def _():
        gates = _dot(h_s[...], win_ref[:, 3 * D_ATT + 3 * D_CONV:])
        pa = _dot(att_s[...], wpa_ref[...])
        pb = _dot(conv_s[...], wpb_ref[...])
        mixin = _sigmoid(gates[:, 0:D_MODEL]) * pa + _sigmoid(gates[:, D_MODEL:]) * pb
        x1_ref[...] = _dot(mixin.astype(BF16), wo_ref[...])

        def res_body(i, carry):
            r = pl.ds(pl.multiple_of(i * slen, slen), slen)
            m = mod_ref[i]
            x1 = x_ref[r, :] + m[2:3, :] * x1_ref[r, :]
            x1_ref[r, :] = x1
            h2_ref[r, :] = _rms(x1, g2_ref[...]) * (1.0 + m[4:5, :]) + m[3:4, :]
            return carry
        lax.fori_loop(0, nseq, res_body, 0)

        route, new_cnt = _route(h2_ref[...], wcat_ref, whi_ref, br_ref, cnt_in_ref[...])
        route_ref[...] = route
        cnt_ref[...] = new_cnt


def _sample_main(x2d, mod, ck, cv, up1, up2, cnt_in, g1, g2, win, bias_s, cw, cbias,
                 wpa, wpb, wo, wcat, whi, br, nseq, slen):
    ntok = nseq * slen
    args = (x2d, mod, ck, cv, up1, up2, cnt_in, g1, g2, win, bias_s, cw, cbias,
            wpa, wpb, wo, wcat, whi, br)
    in_specs = []
    for idx, a in enumerate(args):
        if idx in (2, 3):
            in_specs.append(pl.BlockSpec((1, WINDOW, D_ATT), lambda n: (n, 0, 0)))
        else:
            in_specs.append(_const_spec(a.shape))
    whole = lambda shape: pl.BlockSpec(shape, lambda n: (0,) * len(shape))
    out_shapes = [(ntok, D_MODEL), (ntok, D_MODEL), (ntok, D_ATT), (ntok, D_ATT),
                  (ntok, D_CONV), (ntok, RLANES), (1, RLANES)]
    scratch = [
        pltpu.VMEM((ntok, D_MODEL), BF16), pltpu.VMEM((ntok, D_ATT), BF16),
        pltpu.VMEM((ntok, D_ATT), BF16), pltpu.VMEM((ntok, D_ATT), BF16),
        pltpu.VMEM((WINDOW + slen, D_ATT), BF16), pltpu.VMEM((WINDOW + slen, D_ATT), BF16),
        pltpu.VMEM((ntok, D_ATT), BF16), pltpu.VMEM((ntok, D_CONV), BF16),
    ]
    return pl.pallas_call(
        functools.partial(_sample_kernel, nseq=nseq, slen=slen),
        grid=(nseq,),
        in_specs=in_specs,
        out_specs=[whole(s) for s in out_shapes],
        out_shape=[jax.ShapeDtypeStruct(s, F32) for s in out_shapes],
        scratch_shapes=scratch,
        compiler_params=pltpu.CompilerParams(
            dimension_semantics=("arbitrary",), vmem_limit_bytes=VMEM_LIMIT),
        name="sample_main",
    )(*args)


def _row_copy(src, srow, dst, drow, sem):
    return pltpu.make_async_copy(src.at[pl.ds(srow, 1)], dst.at[pl.ds(drow, 1)], sem)


def _dispatch_kernel(pend_ref, d1_ref, d2_ref, hp_hbm, hs_hbm, xs_out, zbuf, sem, zsem, *,
                     np_tiles, nblocks):
    i = pl.program_id(0)

    @pl.when(i == 0)
    def _():
        zbuf[...] = jnp.zeros_like(zbuf)

        def zcopy(e):
            start = pl.multiple_of(pend_ref[e + 1] - MOE_BLK, MOE_BLK)
            return pltpu.make_async_copy(zbuf, xs_out.at[pl.ds(start, MOE_BLK)], zsem)

        def zstart(e, carry):
            @pl.when(pend_ref[e + 1] > pend_ref[e])
            def _():
                zcopy(e).start()
            return carry

        def zwait(e, carry):
            @pl.when(pend_ref[e + 1] > pend_ref[e])
            def _():
                zcopy(e).wait()
            return carry
        lax.fori_loop(0, N_EXPERTS, zstart, 0)
        lax.fori_loop(0, N_EXPERTS, zwait, 0)

        def tcopy(bk):
            start = pl.multiple_of(bk * MOE_BLK, MOE_BLK)
            return pltpu.make_async_copy(zbuf, xs_out.at[pl.ds(start, MOE_BLK)], zsem)

        def tstart(bk, carry):
            tcopy(bk).start()
            return carry

        def twait(bk, carry):
            tcopy(bk).wait()
            return carry
        used = pend_ref[N_EXPERTS] // MOE_BLK
        lax.fori_loop(used, nblocks, tstart, 0)
        lax.fori_loop(used, nblocks, twait, 0)

    def scatter_tile(src, base):
        def issue(r, carry):
            _row_copy(src, base + r, xs_out, d1_ref[r], sem).start()
            _row_copy(src, base + r, xs_out, d2_ref[r], sem).start()
            return carry

        def drain(r, carry):
            _row_copy(src, base + r, xs_out, d1_ref[r], sem).wait()
            _row_copy(src, base + r, xs_out, d2_ref[r], sem).wait()
            return carry
        lax.fori_loop(0, TM, issue, 0)
        lax.fori_loop(0, TM, drain, 0)

    @pl.when(i < np_tiles)
    def _():
        scatter_tile(hp_hbm, i * TM)

    @pl.when(i >= np_tiles)
    def _():
        scatter_tile(hs_hbm, (i - np_tiles) * TM)


def _dispatch(pend, d1, d2, h2p, h2s, nslots):
    np_tiles = h2p.shape[0] // TM
    ns_tiles = h2s.shape[0] // TM
    smem_tile = pl.BlockSpec((TM,), lambda i, *_: (i,), memory_space=pltpu.SMEM)
    any_spec = pl.BlockSpec(memory_space=pl.ANY)
    return pl.pallas_call(
        functools.partial(_dispatch_kernel, np_tiles=np_tiles, nblocks=nslots // MOE_BLK),
        grid_spec=pltpu.PrefetchScalarGridSpec(
            num_scalar_prefetch=1,
            grid=(np_tiles + ns_tiles,),
            in_specs=[smem_tile, smem_tile, any_spec, any_spec],
            out_specs=any_spec,
            scratch_shapes=[pltpu.VMEM((MOE_BLK, D_MODEL), F32),
                            pltpu.SemaphoreType.DMA(()), pltpu.SemaphoreType.DMA(())],
        ),
        out_shape=jax.ShapeDtypeStruct((nslots, D_MODEL), F32),
        compiler_params=pltpu.CompilerParams(dimension_semantics=("arbitrary",)),
        name="dispatch",
    )(pend, d1, d2, h2p, h2s)


def _expert_kernel(blk_e_ref, nblk_ref, xs_ref, wg_ref, wu_ref, wd_ref, y_ref):
    del blk_e_ref
    live = pl.program_id(0) < nblk_ref[0]

    @pl.when(live)
    def _():
        xb = xs_ref[...].astype(BF16)
        g = _dot(xb, wg_ref[0])
        u = _dot(xb, wu_ref[0])
        a = (g * _sigmoid(g)) * u
        y_ref[...] = _dot(a.astype(BF16), wd_ref[0])

    @pl.when(jnp.logical_not(live))
    def _():
        y_ref[...] = jnp.zeros_like(y_ref)


def _experts(blk_e, nblk, xs, wg, wu, wd):
    nblocks = xs.shape[0] // MOE_BLK
    row_map = lambda i, be, nb: (jnp.minimum(i, nb[0] - 1), 0)
    w_map = lambda i, be, nb: (be[i], 0, 0)
    return pl.pallas_call(
        _expert_kernel,
        grid_spec=pltpu.PrefetchScalarGridSpec(
            num_scalar_prefetch=2,
            grid=(nblocks,),
            in_specs=[pl.BlockSpec((MOE_BLK, D_MODEL), row_map),
                      pl.BlockSpec((1, D_MODEL, D_EXPERT), w_map),
                      pl.BlockSpec((1, D_MODEL, D_EXPERT), w_map),
                      pl.BlockSpec((1, D_EXPERT, D_MODEL), w_map)],
            out_specs=pl.BlockSpec((MOE_BLK, D_MODEL), lambda i, be, nb: (i, 0)),
        ),
        out_shape=jax.ShapeDtypeStruct(xs.shape, F32),
        compiler_params=pltpu.CompilerParams(
            dimension_semantics=("arbitrary",), vmem_limit_bytes=VMEM_LIMIT),
        name="experts",
    )(blk_e, nblk, xs, wg, wu, wd)


def _combine_kernel(d1_ref, d2_ref, y_hbm, x1_ref, route_ref, gate_ref, gf_ref, o_ref,
                    buf1, buf2, sem):
    def issue(r, carry):
        _row_copy(y_hbm, d1_ref[r], buf1, r, sem).start()
        _row_copy(y_hbm, d2_ref[r], buf2, r, sem).start()
        return carry

    def drain(r, carry):
        _row_copy(y_hbm, d1_ref[r], buf1, r, sem).wait()
        _row_copy(y_hbm, d2_ref[r], buf2, r, sem).wait()
        return carry
    lax.fori_loop(0, TM, issue, 0)
    lax.fori_loop(0, TM, drain, 0)
    route = route_ref[0]
    w1 = route[:, 4:5]
    w2 = route[:, 5:6]
    ffn = w1 * buf1[...] + w2 * buf2[...]
    x2 = x1_ref[0] + gate_ref[0] * ffn
    o_ref[0] = _rms(x2, gf_ref[...])


def _combine(d1, d2, y, x1, route, gate, gf, tok_base):
    nb, seq, _ = x1.shape
    nt = seq // TM
    blk0 = tok_base // TM
    smem_tile = pl.BlockSpec((TM,), lambda b, j: (blk0 + b * nt + j,), memory_space=pltpu.SMEM)
    tile = lambda b, j: (b, j, 0)
    grows = gate.shape[1]
    gate_spec = (pl.BlockSpec((1, 1, D_MODEL), lambda b, j: (b, 0, 0)) if grows == 1
                 else pl.BlockSpec((1, TM, D_MODEL), tile))
    return pl.pallas_call(
        _combine_kernel,
        grid=(nb, nt),
        in_specs=[smem_tile, smem_tile, pl.BlockSpec(memory_space=pl.ANY),
                  pl.BlockSpec((1, TM, D_MODEL), tile),
                  pl.BlockSpec((1, TM, RLANES), tile),
                  gate_spec,
                  pl.BlockSpec((1, D_MODEL), lambda b, j: (0, 0))],
        out_specs=pl.BlockSpec((1, TM, D_MODEL), tile),
        out_shape=jax.ShapeDtypeStruct(x1.shape, F32),
        scratch_shapes=[pltpu.VMEM((TM, D_MODEL), F32), pltpu.VMEM((TM, D_MODEL), F32),
                        pltpu.SemaphoreType.DMA(())],
        compiler_params=pltpu.CompilerParams(
            dimension_semantics=("arbitrary", "arbitrary"), vmem_limit_bytes=VMEM_LIMIT),
        name="combine",
    )(d1, d2, y, x1, route, gate, gf)


def _band_bias(rel_bias, rows, keys):
    i = np.arange(rows)[:, None]
    jj = np.arange(keys)[None, :]
    rel = np.clip(WINDOW + i - jj, -MAX_REL, MAX_REL) + MAX_REL
    b = rel_bias[:, rel]
    return b.reshape(2, 4 * rows, keys)


def kernel(x_prompt, x_sample, cache_attn_k, cache_attn_v, state_conv, c_prompt, c_sample,
           w_ada, b_ada, norm1_g, norm2_g, w_in, rel_bias, conv_w, conv_b, w_pa, w_pb, w_o,
           w_group, b_group, w_expert, b_expert, w_e_gate, w_e_up, w_e_down, final_g):
    assert w_ada.shape[0] == 1, "single trunk layer"
    nb, seq, _ = x_prompt.shape
    nseq, slen, _ = x_sample.shape
    ntok_p = nb * seq
    ntok_s = nseq * slen
    ntok = ntok_p + ntok_s
    assert seq % TL == 0 and WINDOW % TL == 0 and seq % TM == 0 and ntok_s % TM == 0
    assert slen >= 2 and slen & (slen - 1) == 0 and slen % 16 == 0

    n_c = nb + nseq
    n_pad = -(-n_c // 8) * 8
    c_all = jnp.concatenate([c_prompt, c_sample, jnp.zeros((n_pad - n_c, D_MODEL), F32)], axis=0)
    mod = _ada(c_all, w_ada[0], b_ada[0]).reshape(n_pad, 6, D_MODEL)
    mod_p = mod[:nb]
    mod_s = mod[nb:n_c]

    win = w_in[0].astype(BF16)
    wpa = w_pa[0].astype(BF16)
    wpb = w_pb[0].astype(BF16)
    wo = w_o[0].astype(BF16)
    g1 = norm1_g[0].reshape(1, D_MODEL)
    g2 = norm2_g[0].reshape(1, D_MODEL)
    gf = final_g.reshape(1, D_MODEL)
    cw = jnp.concatenate([conv_w[0], jnp.zeros((8 - conv_w.shape[1], D_CONV), F32)], axis=0)
    cbias = conv_b[0].reshape(1, D_CONV)
    wr = jnp.concatenate([w_expert[0], w_group[0],
                          jnp.zeros((D_MODEL, RLANES - N_EXPERTS - N_GROUPS), F32)], axis=1)
    whi = wr.astype(BF16)
    wlo = (wr - whi.astype(F32)).astype(BF16)
    wcat = jnp.concatenate([whi, wlo], axis=1)
    br = jnp.concatenate([b_expert[0], b_group[0],
                          jnp.zeros((RLANES - N_EXPERTS - N_GROUPS,), F32)]).reshape(1, RLANES)
    bias_p = _band_bias(rel_bias[0], CHUNK, BAND)
    bias_s = _band_bias(rel_bias[0], slen, WINDOW + slen)

    x1p, h2p, kp, vp, up8, route_p, cnt_p = _prompt_main(
        x_prompt, mod_p, g1, g2, win, bias_p, cw, cbias, wpa, wpb, wo, wcat, whi, br)

    st = state_conv[0]
    up1 = jnp.zeros((nseq, slen, D_CONV), F32).at[:, 0].set(st[:, 1]).reshape(ntok_s, D_CONV)
    up2 = (jnp.zeros((nseq, slen, D_CONV), F32).at[:, 0].set(st[:, 0]).at[:, 1].set(st[:, 1])
           .reshape(ntok_s, D_CONV))
    ck = cache_attn_k[0].reshape(nseq, WINDOW, D_ATT)
    cv = cache_attn_v[0].reshape(nseq, WINDOW, D_ATT)
    x1s, h2s, ks, vs, us, route_s, cnt = _sample_main(
        x_sample.reshape(ntok_s, D_MODEL), mod_s, ck, cv, up1, up2, cnt_p,
        g1, g2, win, bias_s, cw, cbias, wpa, wpb, wo, wcat, whi, br, nseq, slen)

    route_all = jnp.concatenate([route_p.reshape(ntok_p, RLANES)[:, :4], route_s[:, :4]], axis=0)
    experts = route_all[:, 0:2].astype(jnp.int32)
    ranks = route_all[:, 2:4].astype(jnp.int32)
    counts = cnt[0, :N_EXPERTS].astype(jnp.int32)
    pcounts = (counts + MOE_BLK - 1) // MOE_BLK * MOE_BLK
    pend = jnp.cumsum(pcounts)
    pstart = pend - pcounts
    dest = pstart[experts] + ranks
    d1 = dest[:, 0]
    d2 = dest[:, 1]
    nblocks = (2 * ntok) // MOE_BLK + N_EXPERTS
    blk_e = jnp.minimum(
        jnp.searchsorted(pend, jnp.arange(nblocks, dtype=jnp.int32) * MOE_BLK, side="right"),
        N_EXPERTS - 1).astype(jnp.int32)
    nblk = (pend[-1:] // MOE_BLK).astype(jnp.int32)
    pend0 = jnp.concatenate([jnp.zeros((1,), jnp.int32), pend.astype(jnp.int32)])

    xs = _dispatch(pend0, d1, d2, h2p.reshape(ntok_p, D_MODEL), h2s, nblocks * MOE_BLK)
    y = _experts(blk_e, nblk, xs, w_e_gate[0].astype(BF16), w_e_up[0].astype(BF16),
                 w_e_down[0].astype(BF16))

    y_prompt = _combine(d1, d2, y, x1p, route_p, mod_p[:, 5:6, :], gf, 0)
    gate_s = jnp.repeat(mod_s[:, 5, :], slen, axis=0).reshape(1, ntok_s, D_MODEL)
    y_sample = _combine(d1, d2, y, x1s.reshape(1, ntok_s, D_MODEL),
                        route_s.reshape(1, ntok_s, RLANES), gate_s, gf, ntok_p)

    new_k_p = kp.reshape(1, nb, WINDOW, N_HEADS, HEAD_DIM)
    new_v_p = vp.reshape(1, nb, WINDOW, N_HEADS, HEAD_DIM)
    new_conv_p = up8[:, 6:8, :].reshape(1, nb, 2, D_CONV)
    new_k_s = ks.reshape(1, nseq, slen, N_HEADS, HEAD_DIM)
    new_v_s = vs.reshape(1, nseq, slen, N_HEADS, HEAD_DIM)
    new_conv_s = us.reshape(nseq, slen, D_CONV)[:, slen - 2:, :].reshape(1, nseq, 2, D_CONV)
    return (y_prompt, y_sample.reshape(nseq, slen, D_MODEL), new_k_p, new_v_p, new_conv_p,
            new_k_s, new_v_s, new_conv_s)
```

```python
import functools

import numpy as np
import jax
import jax.numpy as jnp
from jax import lax
from jax.experimental import pallas as pl
from jax.experimental.pallas import tpu as pltpu

F32 = jnp.float32
BF16 = jnp.bfloat16

D_MODEL = 1024
CHUNK = 64
LEFT = 8
WINDOW = LEFT * CHUNK
BAND = WINDOW + CHUNK
N_HEADS = 8
HEAD_DIM = 64
D_ATT = N_HEADS * HEAD_DIM
QUAD = 256
MAX_REL = 128
D_CONV = 512
N_GROUPS = 4
EPG = 8
N_EXPERTS = 32
D_EXPERT = 512
EPS = 1e-6
NEG = -1e30

TL = 256
RING = WINDOW + TL
MOE_BLK = 256
TM = 512
RLANES = 128
LANES = 128
ROW_TILES = D_MODEL // LANES
ISSUE_UNROLL = 8
VMEM_LIMIT = 56 * 1024 * 1024


def _const_spec(shape):
    nd = len(shape)
    return pl.BlockSpec(shape, lambda *_: (0,) * nd, pipeline_mode=pl.Buffered(1))


def _dot(a, b):
    return jnp.dot(a, b, preferred_element_type=F32)


def _sigmoid(x):
    return 1.0 / (1.0 + jnp.exp(-x))


def _rms(x, g):
    ms = jnp.mean(x * x, axis=-1, keepdims=True)
    return x * lax.rsqrt(ms + EPS) * g


def _store_rows_as_tiles(ref, val):
    r = val.shape[0]
    for c in range(ROW_TILES):
        ref[pl.ds(c, r, stride=ROW_TILES), :] = val[:, c * LANES:(c + 1) * LANES]


def _load_tiles_as_rows(ref, r):
    return jnp.concatenate(
        [ref[pl.ds(c, r, stride=ROW_TILES), :] for c in range(ROW_TILES)], axis=1)


def _tile(ref, row):
    return ref.at[pl.ds(pl.multiple_of(row * ROW_TILES, ROW_TILES), ROW_TILES)]


def _ada_kernel(c_ref, w_ref, b_ref, o_ref):
    c = c_ref[...]
    s = c * _sigmoid(c)
    o_ref[...] = jnp.dot(s, w_ref[...], preferred_element_type=F32,
                         precision=lax.Precision.HIGHEST) + b_ref[...]


def _ada(c_all, w_ada, b_ada):
    n = c_all.shape[0]
    nb = 1024
    return pl.pallas_call(
        _ada_kernel,
        grid=(6 * D_MODEL // nb,),
        in_specs=[pl.BlockSpec((n, D_MODEL), lambda i: (0, 0)),
                  pl.BlockSpec((D_MODEL, nb), lambda i: (0, i)),
                  pl.BlockSpec((1, nb), lambda i: (0, i))],
        out_specs=pl.BlockSpec((n, nb), lambda i: (0, i)),
        out_shape=jax.ShapeDtypeStruct((n, 6 * D_MODEL), F32),
        name="ada",
    )(c_all, w_ada, b_ada.reshape(1, -1))


def _attend(q, kb, vb, bias, lim):
    r = q.shape[0]
    nk = kb.shape[0]
    assert r & (r - 1) == 0
    qt = jnp.concatenate([q] * 4, axis=0)
    rowh = lax.broadcasted_iota(jnp.int32, (4 * r, QUAD), 0) >> (r.bit_length() - 1)
    laneh = lax.broadcasted_iota(jnp.int32, (4 * r, QUAD), 1) >> 6
    qm = jnp.where(rowh == laneh, qt, jnp.zeros_like(qt))
    s = lax.dot_general(qm, kb, (((1,), (1,)), ((), ())), preferred_element_type=F32)
    s = s + bias
    if lim is not None:
        col = lax.broadcasted_iota(jnp.int32, (4 * r, nk), 1)
        s = jnp.where(col >= lim, s, NEG)
    m = jnp.max(s, axis=1, keepdims=True)
    p = jnp.exp(s - m)
    l = jnp.sum(p, axis=1, keepdims=True)
    o = _dot(p.astype(BF16), vb) * (1.0 / l)
    lane_o = lax.broadcasted_iota(jnp.int32, (r, QUAD), 1) >> 6
    out = o[0:r]
    for h in range(1, 4):
        out = jnp.where(lane_o == h, o[h * r:(h + 1) * r], out)
    return out


def _route(h2, wcat_ref, whi_ref, br_ref, cnt):
    r = h2.shape[0]
    hi = h2.astype(BF16)
    lo = (h2 - hi.astype(F32)).astype(BF16)
    z = _dot(hi, wcat_ref[...])
    logits = z[:, :RLANES] + z[:, RLANES:] + _dot(lo, whi_ref[...]) + br_ref[...]
    lane = lax.broadcasted_iota(jnp.int32, (r, RLANES), 1)
    lane_f = lane.astype(F32)
    big = jnp.float32(1000.0)

    lg = jnp.where((lane >= N_EXPERTS) & (lane < N_EXPERTS + N_GROUPS), logits, NEG)
    mg = jnp.max(lg, axis=1, keepdims=True)
    gi = jnp.min(jnp.where(lg == mg, lane_f, big), axis=1, keepdims=True) - N_EXPERTS
    pg = 1.0 / jnp.sum(jnp.exp(lg - mg), axis=1, keepdims=True)

    grp_of_lane = (lane >> 3).astype(F32)
    le = jnp.where((lane < N_EXPERTS) & (grp_of_lane == gi), logits, NEG)
    m1 = jnp.max(le, axis=1, keepdims=True)
    i1 = jnp.min(jnp.where(le == m1, lane_f, big), axis=1, keepdims=True)
    sel1 = lane_f == i1
    le2 = jnp.where(sel1, NEG, le)
    m2 = jnp.max(le2, axis=1, keepdims=True)
    i2 = jnp.min(jnp.where(le2 == m2, lane_f, big), axis=1, keepdims=True)
    sel2 = lane_f == i2
    rr = jnp.exp(m2 - m1)
    inv = pg / (1.0 + rr)
    w1 = inv
    w2 = inv * rr

    oh = jnp.where(sel1 | sel2, 1.0, 0.0).astype(F32)
    ri = lax.broadcasted_iota(jnp.int32, (r, r), 0)
    ci = lax.broadcasted_iota(jnp.int32, (r, r), 1)
    tri = jnp.where(ri > ci, 1.0, 0.0).astype(BF16)
    before = _dot(tri, oh.astype(BF16)) + cnt
    r1 = jnp.sum(jnp.where(sel1, before, 0.0), axis=1, keepdims=True)
    r2 = jnp.sum(jnp.where(sel2, before, 0.0), axis=1, keepdims=True)
    new_cnt = cnt + jnp.sum(oh, axis=0, keepdims=True)

    route = jnp.where(lane == 0, i1, 0.0)
    route = jnp.where(lane == 1, i2, route)
    route = jnp.where(lane == 2, r1, route)
    route = jnp.where(lane == 3, r2, route)
    route = jnp.where(lane == 4, w1, route)
    route = jnp.where(lane == 5, w2, route)
    return route, new_cnt


def _prompt_kernel(x_ref, mod_ref, g1_ref, g2_ref, win_ref, bias_ref, cw_ref, cbias_ref,
                   wpa_ref, wpb_ref, wo_ref, wcat_ref, whi_ref, br_ref,
                   x1_ref, h2_ref, ko_ref, vo_ref, uo_ref, route_ref, cnt_ref,
                   kring, vring, att_s, ucarry, cnt_s):
    b = pl.program_id(0)
    j = pl.program_id(1)

    @pl.when((b == 0) & (j == 0))
    def _():
        cnt_s[...] = jnp.zeros_like(cnt_s)

    @pl.when(j == 0)
    def _():
        kring[0:WINDOW, :] = jnp.zeros((WINDOW, D_ATT), BF16)
        vring[0:WINDOW, :] = jnp.zeros((WINDOW, D_ATT), BF16)
        ucarry[...] = jnp.zeros_like(ucarry)

    sh1 = mod_ref[0, 0:1, :]
    sc1 = mod_ref[0, 1:2, :]
    gt1 = mod_ref[0, 2:3, :]
    sh2 = mod_ref[0, 3:4, :]
    sc2 = mod_ref[0, 4:5, :]

    x = x_ref[0]
    h = _rms(x, g1_ref[...]) * (1.0 + sc1) + sh1
    hb = h.astype(BF16)

    qkv = _dot(hb, win_ref[:, 0:3 * D_ATT])
    q = (qkv[:, 0:D_ATT] * (HEAD_DIM ** -0.5)).astype(BF16)
    k = qkv[:, D_ATT:2 * D_ATT]
    v = qkv[:, 2 * D_ATT:3 * D_ATT]
    ko_ref[0] = k
    vo_ref[0] = v
    kring[WINDOW:RING, :] = k.astype(BF16)
    vring[WINDOW:RING, :] = v.astype(BF16)

    base = j * TL
    for c in range(TL // CHUNK):
        lim = WINDOW - (base + c * CHUNK)
        for qd in range(2):
            ls = slice(qd * QUAD, (qd + 1) * QUAD)
            o = _attend(q[c * CHUNK:(c + 1) * CHUNK, ls],
                        kring[c * CHUNK:c * CHUNK + BAND, ls],
                        vring[c * CHUNK:c * CHUNK + BAND, ls],
                        bias_ref[qd], lim)
            att_s[c * CHUNK:(c + 1) * CHUNK, ls] = o.astype(BF16)

    kring[0:WINDOW, :] = kring[TL:RING, :]
    vring[0:WINDOW, :] = vring[TL:RING, :]

    cbcv = _dot(hb, win_ref[:, 3 * D_ATT:3 * D_ATT + 3 * D_CONV])
    cb = cbcv[:, 0:D_CONV]
    u = cbcv[:, D_CONV:2 * D_CONV] * cbcv[:, 2 * D_CONV:3 * D_CONV]
    row = lax.broadcasted_iota(jnp.int32, (8, D_CONV), 0)
    prev = ucarry[...]
    r1 = pltpu.roll(u, 1, axis=0)
    r2 = pltpu.roll(u, 2, axis=0)
    u_m1 = jnp.concatenate(
        [jnp.where(row < 1, pltpu.roll(prev, 1, axis=0), r1[0:8]), r1[8:]], axis=0)
    u_m2 = jnp.concatenate(
        [jnp.where(row < 2, pltpu.roll(prev, 2, axis=0), r2[0:8]), r2[8:]], axis=0)
    yc = cw_ref[0:1, :] * u_m2 + cw_ref[1:2, :] * u_m1 + cw_ref[2:3, :] * u + cbias_ref[...]
    conv_out = (cb * yc).astype(BF16)
    ucarry[...] = u[TL - 8:TL, :]
    uo_ref[0] = u[TL - 8:TL, :]

    gates = _dot(hb, win_ref[:, 3 * D_ATT + 3 * D_CONV:])
    pa = _dot(att_s[...], wpa_ref[...])
    pb = _dot(conv_out, wpb_ref[...])
    mixin = _sigmoid(gates[:, 0:D_MODEL]) * pa + _sigmoid(gates[:, D_MODEL:]) * pb
    mix = _dot(mixin.astype(BF16), wo_ref[...])
    x1 = x + gt1 * mix
    x1_ref[0] = x1
    h2 = _rms(x1, g2_ref[...]) * (1.0 + sc2) + sh2
    _store_rows_as_tiles(h2_ref, h2)

    route, new_cnt = _route(h2, wcat_ref, whi_ref, br_ref, cnt_s[...])
    route_ref[0] = route
    cnt_s[...] = new_cnt
    cnt_ref[...] = new_cnt


def _prompt_main(x, mod, g1, g2, win, bias_q, cw, cbias, wpa, wpb, wo, wcat, whi, br):
    nb, seq, _ = x.shape
    nt = seq // TL
    keep = WINDOW // TL
    tile = lambda b, j: (b, j, 0)
    last = lambda b, j: (b, jnp.maximum(j - (nt - keep), 0), 0)
    perb = lambda b, j: (b, 0, 0)
    in_specs = [
        pl.BlockSpec((1, TL, D_MODEL), tile),
        pl.BlockSpec((1, 6, D_MODEL), perb),
        _const_spec(g1.shape), _const_spec(g2.shape), _const_spec(win.shape),
        _const_spec(bias_q.shape), _const_spec(cw.shape), _const_spec(cbias.shape),
        _const_spec(wpa.shape), _const_spec(wpb.shape), _const_spec(wo.shape),
        _const_spec(wcat.shape), _const_spec(whi.shape), _const_spec(br.shape),
    ]
    out_specs = [
        pl.BlockSpec((1, TL, D_MODEL), tile),
        pl.BlockSpec((TL * ROW_TILES, LANES), lambda b, j: (b * nt + j, 0)),
        pl.BlockSpec((1, TL, D_ATT), last),
        pl.BlockSpec((1, TL, D_ATT), last),
        pl.BlockSpec((1, 8, D_CONV), perb),
        pl.BlockSpec((1, TL, RLANES), tile),
        pl.BlockSpec((1, RLANES), lambda b, j: (0, 0)),
    ]
    out_shape = [
        jax.ShapeDtypeStruct((nb, seq, D_MODEL), F32),
        jax.ShapeDtypeStruct((nb * seq * ROW_TILES, LANES), F32),
        jax.ShapeDtypeStruct((nb, WINDOW, D_ATT), F32),
        jax.ShapeDtypeStruct((nb, WINDOW, D_ATT), F32),
        jax.ShapeDtypeStruct((nb, 8, D_CONV), F32),
        jax.ShapeDtypeStruct((nb, seq, RLANES), F32),
        jax.ShapeDtypeStruct((1, RLANES), F32),
    ]
    scratch = [
        pltpu.VMEM((RING, D_ATT), BF16), pltpu.VMEM((RING, D_ATT), BF16),
        pltpu.VMEM((TL, D_ATT), BF16), pltpu.VMEM((8, D_CONV), F32),
        pltpu.VMEM((1, RLANES), F32),
    ]
    return pl.pallas_call(
        _prompt_kernel,
        grid=(nb, nt),
        in_specs=in_specs, out_specs=out_specs, out_shape=out_shape,
        scratch_shapes=scratch,
        compiler_params=pltpu.CompilerParams(
            dimension_semantics=("arbitrary", "arbitrary"), vmem_limit_bytes=VMEM_LIMIT),
        name="prompt_main",
    )(x, mod, g1, g2, win, bias_q, cw, cbias, wpa, wpb, wo, wcat, whi, br)


def _sample_kernel(x_ref, mod_ref, ck_ref, cv_ref, up1_ref, up2_ref, cnt_in_ref,
                   g1_ref, g2_ref, win_ref, bias_ref, cw_ref, cbias_ref,
                   wpa_ref, wpb_ref, wo_ref, wcat_ref, whi_ref, br_ref,
                   x1_ref, h2_ref, ko_ref, vo_ref, uo_ref, route_ref, cnt_ref,
                   h_s, q_s, kn_s, vn_s, kband, vband, att_s, conv_s, h2_s, *, nseq, slen):
    n = pl.program_id(0)
    ntok = nseq * slen

    @pl.when(n == 0)
    def _():
        def norm_body(i, carry):
            rows = pl.ds(pl.multiple_of(i * slen, slen), slen)
            xi = x_ref[rows, :]
            m = mod_ref[i]
            hi = _rms(xi, g1_ref[...]) * (1.0 + m[1:2, :]) + m[0:1, :]
            h_s[rows, :] = hi.astype(BF16)
            return carry
        lax.fori_loop(0, nseq, norm_body, 0)
        hb = h_s[...]
        qkv = _dot(hb, win_ref[:, 0:3 * D_ATT])
        q_s[...] = (qkv[:, 0:D_ATT] * (HEAD_DIM ** -0.5)).astype(BF16)
        k = qkv[:, D_ATT:2 * D_ATT]
        v = qkv[:, 2 * D_ATT:3 * D_ATT]
        ko_ref[...] = k
        vo_ref[...] = v
        kn_s[...] = k.astype(BF16)
        vn_s[...] = v.astype(BF16)

        cbcv = _dot(hb, win_ref[:, 3 * D_ATT:3 * D_ATT + 3 * D_CONV])
        cb = cbcv[:, 0:D_CONV]
        u = cbcv[:, D_CONV:2 * D_CONV] * cbcv[:, 2 * D_CONV:3 * D_CONV]
        pos = lax.broadcasted_iota(jnp.int32, (ntok, D_CONV), 0) & (slen - 1)
        u_m1 = jnp.where(pos < 1, up1_ref[...], pltpu.roll(u, 1, axis=0))
        u_m2 = jnp.where(pos < 2, up2_ref[...], pltpu.roll(u, 2, axis=0))
        yc = cw_ref[0:1, :] * u_m2 + cw_ref[1:2, :] * u_m1 + cw_ref[2:3, :] * u + cbias_ref[...]
        conv_s[...] = (cb * yc).astype(BF16)
        uo_ref[...] = u

    rows = pl.ds(pl.multiple_of(n * slen, slen), slen)
    kband[0:WINDOW, :] = ck_ref[0].astype(BF16)
    vband[0:WINDOW, :] = cv_ref[0].astype(BF16)
    kband[WINDOW:WINDOW + slen, :] = kn_s[rows, :]
    vband[WINDOW:WINDOW + slen, :] = vn_s[rows, :]
    qn = q_s[rows, :]
    outs = []
    for qd in range(2):
        ls = slice(qd * QUAD, (qd + 1) * QUAD)
        outs.append(_attend(qn[:, ls], kband[:, ls], vband[:, ls], bias_ref[qd], None))
    att_s[rows, :] = jnp.concatenate(outs, axis=1).astype(BF16)

    @pl.when(n == nseq - 1)
    def _():
        gates = _dot(h_s[...], win_ref[:, 3 * D_ATT + 3 * D_CONV:])
        pa = _dot(att_s[...], wpa_ref[...])
        pb = _dot(conv_s[...], wpb_ref[...])
        mixin = _sigmoid(gates[:, 0:D_MODEL]) * pa + _sigmoid(gates[:, D_MODEL:]) * pb
        x1_ref[...] = _dot(mixin.astype(BF16), wo_ref[...])

        def res_body(i, carry):
            r = pl.ds(pl.multiple_of(i * slen, slen), slen)
            m = mod_ref[i]
            x1 = x_ref[r, :] + m[2:3, :] * x1_ref[r, :]
            x1_ref[r, :] = x1
            h2_s[r, :] = _rms(x1, g2_ref[...]) * (1.0 + m[4:5, :]) + m[3:4, :]
            return carry
        lax.fori_loop(0, nseq, res_body, 0)

        h2 = h2_s[...]
        _store_rows_as_tiles(h2_ref, h2)
        route, new_cnt = _route(h2, wcat_ref, whi_ref, br_ref, cnt_in_ref[...])
        route_ref[...] = route
        cnt_ref[...] = new_cnt


def _sample_main(x2d, mod, ck, cv, up1, up2, cnt_in, g1, g2, win, bias_s, cw, cbias,
                 wpa, wpb, wo, wcat, whi, br, nseq, slen):
    ntok = nseq * slen
    args = (x2d, mod, ck, cv, up1, up2, cnt_in, g1, g2, win, bias_s, cw, cbias,
            wpa, wpb, wo, wcat, whi, br)
    in_specs = []
    for idx, a in enumerate(args):
        if idx in (2, 3):
            in_specs.append(pl.BlockSpec((1, WINDOW, D_ATT), lambda n: (n, 0, 0)))
        else:
            in_specs.append(_const_spec(a.shape))
    whole = lambda shape: pl.BlockSpec(shape, lambda n: (0,) * len(shape))
    out_shapes = [(ntok, D_MODEL), (ntok * ROW_TILES, LANES), (ntok, D_ATT), (ntok, D_ATT),
                  (ntok, D_CONV), (ntok, RLANES), (1, RLANES)]
    scratch = [
        pltpu.VMEM((ntok, D_MODEL), BF16), pltpu.VMEM((ntok, D_ATT), BF16),
        pltpu.VMEM((ntok, D_ATT), BF16), pltpu.VMEM((ntok, D_ATT), BF16),
        pltpu.VMEM((WINDOW + slen, D_ATT), BF16), pltpu.VMEM((WINDOW + slen, D_ATT), BF16),
        pltpu.VMEM((ntok, D_ATT), BF16), pltpu.VMEM((ntok, D_CONV), BF16),
        pltpu.VMEM((ntok, D_MODEL), F32),
    ]
    return pl.pallas_call(
        functools.partial(_sample_kernel, nseq=nseq, slen=slen),
        grid=(nseq,),
        in_specs=in_specs,
        out_specs=[whole(s) for s in out_shapes],
        out_shape=[jax.ShapeDtypeStruct(s, F32) for s in out_shapes],
        scratch_shapes=scratch,
        compiler_params=pltpu.CompilerParams(
            dimension_semantics=("arbitrary",), vmem_limit_bytes=VMEM_LIMIT),
        name="sample_main",
    )(*args)


def _issue_rows(n, body):
    def group(g, carry):
        for u in range(ISSUE_UNROLL):
            body(g * ISSUE_UNROLL + u, u)
        return carry
    lax.fori_loop(0, n // ISSUE_UNROLL, group, 0)


def _dispatch_kernel(pend_ref, d1_ref, d2_ref, hp_ref, hs_ref, xs_out, zbuf, sem, zsem, *,
                     np_tiles, nblocks):
    i = pl.program_id(0)
    blk_rows = MOE_BLK * ROW_TILES

    @pl.when(i == 0)
    def _():
        zbuf[...] = jnp.zeros_like(zbuf)

        def zcopy(e):
            start = pl.multiple_of((pend_ref[e + 1] - MOE_BLK) * ROW_TILES, blk_rows)
            return pltpu.make_async_copy(zbuf, xs_out.at[pl.ds(start, blk_rows)], zsem)

        def zstart(e, carry):
            @pl.when(pend_ref[e + 1] > pend_ref[e])
            def _():
                zcopy(e).start()
            return carry

        def zwait(e, carry):
            @pl.when(pend_ref[e + 1] > pend_ref[e])
            def _():
                zcopy(e).wait()
            return carry
        lax.fori_loop(0, N_EXPERTS, zstart, 0)
        lax.fori_loop(0, N_EXPERTS, zwait, 0)

        def tcopy(bk):
            start = pl.multiple_of(bk * blk_rows, blk_rows)
            return pltpu.make_async_copy(zbuf, xs_out.at[pl.ds(start, blk_rows)], zsem)

        def tstart(bk, carry):
            tcopy(bk).start()
            return carry

        def twait(bk, carry):
            tcopy(bk).wait()
            return carry
        used = pend_ref[N_EXPERTS] // MOE_BLK
        lax.fori_loop(used, nblocks, tstart, 0)
        lax.fori_loop(used, nblocks, twait, 0)

    def scatter_tile(src):
        def row(r, u):
            pltpu.make_async_copy(_tile(src, r), _tile(xs_out, d1_ref[r]), sem).start(priority=u % 2)
            pltpu.make_async_copy(_tile(src, r), _tile(xs_out, d2_ref[r]), sem).start(
                priority=(u + 1) % 2)
        _issue_rows(TM, row)
        for _ in range(2):
            pltpu.make_async_copy(src, xs_out.at[pl.ds(0, TM * ROW_TILES)], sem).wait()

    @pl.when(i < np_tiles)
    def _():
        scatter_tile(hp_ref)

    @pl.when(i >= np_tiles)
    def _():
        scatter_tile(hs_ref)


def _dispatch(pend, d1, d2, h2p, h2s, nslots):
    np_tiles = h2p.shape[0] // (TM * ROW_TILES)
    ns_tiles = h2s.shape[0] // (TM * ROW_TILES)
    smem_tile = pl.BlockSpec((TM,), lambda i, *_: (i,), memory_space=pltpu.SMEM)
    rows = TM * ROW_TILES
    return pl.pallas_call(
        functools.partial(_dispatch_kernel, np_tiles=np_tiles, nblocks=nslots // MOE_BLK),
        grid_spec=pltpu.PrefetchScalarGridSpec(
            num_scalar_prefetch=1,
            grid=(np_tiles + ns_tiles,),
            in_specs=[smem_tile, smem_tile,
                      pl.BlockSpec((rows, LANES), lambda i, *_: (jnp.minimum(i, np_tiles - 1), 0)),
                      pl.BlockSpec((rows, LANES), lambda i, *_: (jnp.maximum(i - np_tiles, 0), 0))],
            out_specs=pl.BlockSpec(memory_space=pl.ANY),
            scratch_shapes=[pltpu.VMEM((MOE_BLK * ROW_TILES, LANES), F32),
                            pltpu.SemaphoreType.DMA(()), pltpu.SemaphoreType.DMA(())],
        ),
        out_shape=jax.ShapeDtypeStruct((nslots * ROW_TILES, LANES), F32),
        compiler_params=pltpu.CompilerParams(dimension_semantics=("arbitrary",)),
        name="dispatch",
    )(pend, d1, d2, h2p, h2s)


def _expert_kernel(blk_e_ref, nblk_ref, xs_ref, wg_ref, wu_ref, wd_ref, y_ref):
    del blk_e_ref
    live = pl.program_id(0) < nblk_ref[0]

    @pl.when(live)
    def _():
        xb = _load_tiles_as_rows(xs_ref, MOE_BLK).astype(BF16)
        g = _dot(xb, wg_ref[0])
        u = _dot(xb, wu_ref[0])
        a = (g * _sigmoid(g)) * u
        _store_rows_as_tiles(y_ref, _dot(a.astype(BF16), wd_ref[0]))

    @pl.when(jnp.logical_not(live))
    def _():
        y_ref[...] = jnp.zeros_like(y_ref)


def _experts(blk_e, nblk, xs, wg, wu, wd):
    blk_rows = MOE_BLK * ROW_TILES
    nblocks = xs.shape[0] // blk_rows
    row_map = lambda i, be, nb: (jnp.minimum(i, nb[0] - 1), 0)
    w_map = lambda i, be, nb: (be[i], 0, 0)
    return pl.pallas_call(
        _expert_kernel,
        grid_spec=pltpu.PrefetchScalarGridSpec(
            num_scalar_prefetch=2,
            grid=(nblocks,),
            in_specs=[pl.BlockSpec((blk_rows, LANES), row_map),
                      pl.BlockSpec((1, D_MODEL, D_EXPERT), w_map),
                      pl.BlockSpec((1, D_MODEL, D_EXPERT), w_map),
                      pl.BlockSpec((1, D_EXPERT, D_MODEL), w_map)],
            out_specs=pl.BlockSpec((blk_rows, LANES), lambda i, be, nb: (i, 0)),
        ),
        out_shape=jax.ShapeDtypeStruct(xs.shape, F32),
        compiler_params=pltpu.CompilerParams(
            dimension_semantics=("arbitrary",), vmem_limit_bytes=VMEM_LIMIT),
        name="experts",
    )(blk_e, nblk, xs, wg, wu, wd)


def _combine_kernel(d1_ref, d2_ref, y_hbm, x1_ref, route_ref, gate_ref, gf_ref, o_ref,
                    buf1, buf2, sem):
    def row(r, u):
        pltpu.make_async_copy(_tile(y_hbm, d1_ref[r]), _tile(buf1, r), sem).start(priority=u % 2)
        pltpu.make_async_copy(_tile(y_hbm, d2_ref[r]), _tile(buf2, r), sem).start(
            priority=(u + 1) % 2)
    _issue_rows(TM, row)
    for buf in (buf1, buf2):
        pltpu.make_async_copy(y_hbm.at[pl.ds(0, TM * ROW_TILES)], buf, sem).wait()
    route = route_ref[0]
    w1 = route[:, 4:5]
    w2 = route[:, 5:6]
    ffn = w1 * _load_tiles_as_rows(buf1, TM) + w2 * _load_tiles_as_rows(buf2, TM)
    x2 = x1_ref[0] + gate_ref[0] * ffn
    o_ref[0] = _rms(x2, gf_ref[...])


def _combine(d1, d2, y, x1, route, gate, gf, tok_base):
    nb, seq, _ = x1.shape
    nt = seq // TM
    blk0 = tok_base // TM
    smem_tile = pl.BlockSpec((TM,), lambda b, j: (blk0 + b * nt + j,), memory_space=pltpu.SMEM)
    tile = lambda b, j: (b, j, 0)
    grows = gate.shape[1]
    gate_spec = (pl.BlockSpec((1, 1, D_MODEL), lambda b, j: (b, 0, 0)) if grows == 1
                 else pl.BlockSpec((1, TM, D_MODEL), tile))
    return pl.pallas_call(
        _combine_kernel,
        grid=(nb, nt),
        in_specs=[smem_tile, smem_tile, pl.BlockSpec(memory_space=pl.ANY),
                  pl.BlockSpec((1, TM, D_MODEL), tile),
                  pl.BlockSpec((1, TM, RLANES), tile),
                  gate_spec,
                  pl.BlockSpec((1, D_MODEL), lambda b, j: (0, 0))],
        out_specs=pl.BlockSpec((1, TM, D_MODEL), tile),
        out_shape=jax.ShapeDtypeStruct(x1.shape, F32),
        scratch_shapes=[pltpu.VMEM((TM * ROW_TILES, LANES), F32),
                        pltpu.VMEM((TM * ROW_TILES, LANES), F32),
                        pltpu.SemaphoreType.DMA(())],
        compiler_params=pltpu.CompilerParams(
            dimension_semantics=("arbitrary", "arbitrary"), vmem_limit_bytes=VMEM_LIMIT),
        name="combine",
    )(d1, d2, y, x1, route, gate, gf)


def _band_bias(rel_bias, rows, keys):
    dist = WINDOW + rows - 1 - np.arange(rows - 1 + keys)
    flipped = rel_bias[:, np.clip(dist, -MAX_REL, MAX_REL) + MAX_REL]
    b = jnp.stack([flipped[:, rows - 1 - i:rows - 1 - i + keys] for i in range(rows)], axis=1)
    return b.reshape(2, 4 * rows, keys)


def kernel(x_prompt, x_sample, cache_attn_k, cache_attn_v, state_conv, c_prompt, c_sample,
           w_ada, b_ada, norm1_g, norm2_g, w_in, rel_bias, conv_w, conv_b, w_pa, w_pb, w_o,
           w_group, b_group, w_expert, b_expert, w_e_gate, w_e_up, w_e_down, final_g):
    assert w_ada.shape[0] == 1, "single trunk layer"
    nb, seq, _ = x_prompt.shape
    nseq, slen, _ = x_sample.shape
    ntok_p = nb * seq
    ntok_s = nseq * slen
    ntok = ntok_p + ntok_s
    assert seq % TL == 0 and WINDOW % TL == 0 and seq % TM == 0 and ntok_s % TM == 0
    assert slen >= 2 and slen & (slen - 1) == 0 and slen % 16 == 0

    n_c = nb + nseq
    n_pad = -(-n_c // 8) * 8
    c_all = jnp.concatenate([c_prompt, c_sample, jnp.zeros((n_pad - n_c, D_MODEL), F32)], axis=0)
    mod = _ada(c_all, w_ada[0], b_ada[0]).reshape(n_pad, 6, D_MODEL)
    mod_p = mod[:nb]
    mod_s = mod[nb:n_c]

    win = w_in[0].astype(BF16)
    wpa = w_pa[0].astype(BF16)
    wpb = w_pb[0].astype(BF16)
    wo = w_o[0].astype(BF16)
    g1 = norm1_g[0].reshape(1, D_MODEL)
    g2 = norm2_g[0].reshape(1, D_MODEL)
    gf = final_g.reshape(1, D_MODEL)
    cw = jnp.concatenate([conv_w[0], jnp.zeros((8 - conv_w.shape[1], D_CONV), F32)], axis=0)
    cbias = conv_b[0].reshape(1, D_CONV)
    wr = jnp.concatenate([w_expert[0], w_group[0],
                          jnp.zeros((D_MODEL, RLANES - N_EXPERTS - N_GROUPS), F32)], axis=1)
    whi = wr.astype(BF16)
    wlo = (wr - whi.astype(F32)).astype(BF16)
    wcat = jnp.concatenate([whi, wlo], axis=1)
    br = jnp.concatenate([b_expert[0], b_group[0],
                          jnp.zeros((RLANES - N_EXPERTS - N_GROUPS,), F32)]).reshape(1, RLANES)
    bias_p = _band_bias(rel_bias[0], CHUNK, BAND)
    bias_s = _band_bias(rel_bias[0], slen, WINDOW + slen)

    x1p, h2p, kp, vp, up8, route_p, cnt_p = _prompt_main(
        x_prompt, mod_p, g1, g2, win, bias_p, cw, cbias, wpa, wpb, wo, wcat, whi, br)

    st = state_conv[0]
    up1 = jnp.zeros((nseq, slen, D_CONV), F32).at[:, 0].set(st[:, 1]).reshape(ntok_s, D_CONV)
    up2 = (jnp.zeros((nseq, slen, D_CONV), F32).at[:, 0].set(st[:, 0]).at[:, 1].set(st[:, 1])
           .reshape(ntok_s, D_CONV))
    ck = cache_attn_k[0].reshape(nseq, WINDOW, D_ATT)
    cv = cache_attn_v[0].reshape(nseq, WINDOW, D_ATT)
    x1s, h2s, ks, vs, us, route_s, cnt = _sample_main(
        x_sample.reshape(ntok_s, D_MODEL), mod_s, ck, cv, up1, up2, cnt_p,
        g1, g2, win, bias_s, cw, cbias, wpa, wpb, wo, wcat, whi, br, nseq, slen)

    route_all = jnp.concatenate([route_p.reshape(ntok_p, RLANES)[:, :4], route_s[:, :4]], axis=0)
    experts = route_all[:, 0:2].astype(jnp.int32)
    ranks = route_all[:, 2:4].astype(jnp.int32)
    counts = cnt[0, :N_EXPERTS].astype(jnp.int32)
    pcounts = (counts + MOE_BLK - 1) // MOE_BLK * MOE_BLK
    pend = jnp.cumsum(pcounts)
    pstart = pend - pcounts
    eids = jnp.arange(N_EXPERTS, dtype=jnp.int32)
    dest = jnp.sum(jnp.where(experts[..., None] == eids, pstart, 0), axis=-1) + ranks
    d1 = dest[:, 0]
    d2 = dest[:, 1]
    nblocks = (2 * ntok) // MOE_BLK + N_EXPERTS
    blk_start = jnp.arange(nblocks, dtype=jnp.int32) * MOE_BLK
    blk_e = jnp.minimum(jnp.sum((pend[None, :] <= blk_start[:, None]).astype(jnp.int32), axis=1),
                        N_EXPERTS - 1)
    nblk = (pend[-1:] // MOE_BLK).astype(jnp.int32)
    pend0 = jnp.concatenate([jnp.zeros((1,), jnp.int32), pend.astype(jnp.int32)])

    xs = _dispatch(pend0, d1, d2, h2p, h2s, nblocks * MOE_BLK)
    y = _experts(blk_e, nblk, xs, w_e_gate[0].astype(BF16), w_e_up[0].astype(BF16),
                 w_e_down[0].astype(BF16))

    y_prompt = _combine(d1, d2, y, x1p, route_p, mod_p[:, 5:6, :], gf, 0)
    gate_s = jnp.repeat(mod_s[:, 5, :], slen, axis=0).reshape(1, ntok_s, D_MODEL)
    y_sample = _combine(d1, d2, y, x1s.reshape(1, ntok_s, D_MODEL),
                        route_s.reshape(1, ntok_s, RLANES), gate_s, gf, ntok_p)

    new_k_p = kp.reshape(1, nb, WINDOW, N_HEADS, HEAD_DIM)
    new_v_p = vp.reshape(1, nb, WINDOW, N_HEADS, HEAD_DIM)
    new_conv_p = up8[:, 6:8, :].reshape(1, nb, 2, D_CONV)
    new_k_s = ks.reshape(1, nseq, slen, N_HEADS, HEAD_DIM)
    new_v_s = vs.reshape(1, nseq, slen, N_HEADS, HEAD_DIM)
    new_conv_s = us.reshape(nseq, slen, D_CONV)[:, slen - 2:, :].reshape(1, nseq, 2, D_CONV)
    return (y_prompt, y_sample.reshape(nseq, slen, D_MODEL), new_k_p, new_v_p, new_conv_p,
            new_k_s, new_v_s, new_conv_s)
```

```python
import functools

import numpy as np
import jax
import jax.numpy as jnp
from jax import lax
from jax.experimental import pallas as pl
from jax.experimental.pallas import tpu as pltpu

F32 = jnp.float32
BF16 = jnp.bfloat16

D_MODEL = 1024
CHUNK = 64
LEFT = 8
WINDOW = LEFT * CHUNK
BAND = WINDOW + CHUNK
N_HEADS = 8
HEAD_DIM = 64
D_ATT = N_HEADS * HEAD_DIM
QUAD = 256
MAX_REL = 128
D_CONV = 512
N_GROUPS = 4
EPG = 8
N_EXPERTS = 32
D_EXPERT = 512
EPS = 1e-6
NEG = -1e30

TL = 512
RING = WINDOW + TL
MOE_BLK = 256
TM = 512
RLANES = 128
LANES = 128
ROW_TILES = D_MODEL // LANES
ISSUE_UNROLL = 8
VMEM_LIMIT = 56 * 1024 * 1024


def _const_spec(shape):
    nd = len(shape)
    return pl.BlockSpec(shape, lambda *_: (0,) * nd, pipeline_mode=pl.Buffered(1))


def _dot(a, b):
    return jnp.dot(a, b, preferred_element_type=F32)


def _sigmoid(x):
    return 1.0 / (1.0 + jnp.exp(-x))


def _rms(x, g):
    ms = jnp.mean(x * x, axis=-1, keepdims=True)
    return x * lax.rsqrt(ms + EPS) * g


def _store_rows_as_tiles(ref, val):
    r = val.shape[0]
    for c in range(ROW_TILES):
        ref[pl.ds(c, r, stride=ROW_TILES), :] = val[:, c * LANES:(c + 1) * LANES]


def _load_tiles_as_rows(ref, r):
    return jnp.concatenate(
        [ref[pl.ds(c, r, stride=ROW_TILES), :] for c in range(ROW_TILES)], axis=1)


def _tile(ref, row):
    return ref.at[pl.ds(pl.multiple_of(row * ROW_TILES, ROW_TILES), ROW_TILES)]


def _ada_kernel(c_ref, w_ref, b_ref, o_ref):
    c = c_ref[...]
    s = c * _sigmoid(c)
    o_ref[...] = jnp.dot(s, w_ref[...], preferred_element_type=F32,
                         precision=lax.Precision.HIGHEST) + b_ref[...]


def _ada(c_all, w_ada, b_ada):
    n = c_all.shape[0]
    nb = 1024
    return pl.pallas_call(
        _ada_kernel,
        grid=(6 * D_MODEL // nb,),
        in_specs=[pl.BlockSpec((n, D_MODEL), lambda i: (0, 0)),
                  pl.BlockSpec((D_MODEL, nb), lambda i: (0, i)),
                  pl.BlockSpec((1, nb), lambda i: (0, i))],
        out_specs=pl.BlockSpec((n, nb), lambda i: (0, i)),
        out_shape=jax.ShapeDtypeStruct((n, 6 * D_MODEL), F32),
        name="ada",
    )(c_all, w_ada, b_ada.reshape(1, -1))


def _attend(q, kb, vb, bias, lim):
    r = q.shape[0]
    nk = kb.shape[0]
    assert r & (r - 1) == 0
    qt = jnp.concatenate([q] * 4, axis=0)
    rowh = lax.broadcasted_iota(jnp.int32, (4 * r, QUAD), 0) >> (r.bit_length() - 1)
    laneh = lax.broadcasted_iota(jnp.int32, (4 * r, QUAD), 1) >> 6
    qm = jnp.where(rowh == laneh, qt, jnp.zeros_like(qt))
    s = lax.dot_general(qm, kb, (((1,), (1,)), ((), ())), preferred_element_type=F32)
    s = s + bias
    if lim is not None:
        col = lax.broadcasted_iota(jnp.int32, (4 * r, nk), 1)
        s = jnp.where(col >= lim, s, NEG)
    m = jnp.max(s, axis=1, keepdims=True)
    p = jnp.exp(s - m)
    l = jnp.sum(p, axis=1, keepdims=True)
    o = _dot(p.astype(BF16), vb) * (1.0 / l)
    lane_o = lax.broadcasted_iota(jnp.int32, (r, QUAD), 1) >> 6
    out = o[0:r]
    for h in range(1, 4):
        out = jnp.where(lane_o == h, o[h * r:(h + 1) * r], out)
    return out


def _route(h2, wcat_ref, whi_ref, br_ref, cnt):
    r = h2.shape[0]
    hi = h2.astype(BF16)
    lo = (h2 - hi.astype(F32)).astype(BF16)
    z = _dot(hi, wcat_ref[...])
    logits = z[:, :RLANES] + z[:, RLANES:] + _dot(lo, whi_ref[...]) + br_ref[...]
    lane = lax.broadcasted_iota(jnp.int32, (r, RLANES), 1)
    lane_f = lane.astype(F32)
    big = jnp.float32(1000.0)

    lg = jnp.where((lane >= N_EXPERTS) & (lane < N_EXPERTS + N_GROUPS), logits, NEG)
    mg = jnp.max(lg, axis=1, keepdims=True)
    gi = jnp.min(jnp.where(lg == mg, lane_f, big), axis=1, keepdims=True) - N_EXPERTS
    pg = 1.0 / jnp.sum(jnp.exp(lg - mg), axis=1, keepdims=True)

    grp_of_lane = (lane >> 3).astype(F32)
    le = jnp.where((lane < N_EXPERTS) & (grp_of_lane == gi), logits, NEG)
    m1 = jnp.max(le, axis=1, keepdims=True)
    i1 = jnp.min(jnp.where(le == m1, lane_f, big), axis=1, keepdims=True)
    sel1 = lane_f == i1
    le2 = jnp.where(sel1, NEG, le)
    m2 = jnp.max(le2, axis=1, keepdims=True)
    i2 = jnp.min(jnp.where(le2 == m2, lane_f, big), axis=1, keepdims=True)
    sel2 = lane_f == i2
    rr = jnp.exp(m2 - m1)
    inv = pg / (1.0 + rr)
    w1 = inv
    w2 = inv * rr

    oh = jnp.where(sel1 | sel2, 1.0, 0.0).astype(F32)
    ri = lax.broadcasted_iota(jnp.int32, (r, r), 0)
    ci = lax.broadcasted_iota(jnp.int32, (r, r), 1)
    tri = jnp.where(ri > ci, 1.0, 0.0).astype(BF16)
    before = _dot(tri, oh.astype(BF16)) + cnt
    r1 = jnp.sum(jnp.where(sel1, before, 0.0), axis=1, keepdims=True)
    r2 = jnp.sum(jnp.where(sel2, before, 0.0), axis=1, keepdims=True)
    new_cnt = cnt + jnp.sum(oh, axis=0, keepdims=True)

    route = jnp.where(lane == 0, i1, 0.0)
    route = jnp.where(lane == 1, i2, route)
    route = jnp.where(lane == 2, r1, route)
    route = jnp.where(lane == 3, r2, route)
    route = jnp.where(lane == 4, w1, route)
    route = jnp.where(lane == 5, w2, route)
    return route, new_cnt


def _prompt_kernel(x_ref, mod_ref, g1_ref, g2_ref, win_ref, bias_ref, cw_ref, cbias_ref,
                   wpa_ref, wpb_ref, wo_ref, wcat_ref, whi_ref, br_ref,
                   x1_ref, h2_ref, ko_ref, vo_ref, uo_ref, route_ref, cnt_ref,
                   kring, vring, att_s, ucarry, cnt_s):
    b = pl.program_id(0)
    j = pl.program_id(1)

    @pl.when((b == 0) & (j == 0))
    def _():
        cnt_s[...] = jnp.zeros_like(cnt_s)

    @pl.when(j == 0)
    def _():
        kring[0:WINDOW, :] = jnp.zeros((WINDOW, D_ATT), BF16)
        vring[0:WINDOW, :] = jnp.zeros((WINDOW, D_ATT), BF16)
        ucarry[...] = jnp.zeros_like(ucarry)

    sh1 = mod_ref[0, 0:1, :]
    sc1 = mod_ref[0, 1:2, :]
    gt1 = mod_ref[0, 2:3, :]
    sh2 = mod_ref[0, 3:4, :]
    sc2 = mod_ref[0, 4:5, :]

    x = x_ref[0]
    h = _rms(x, g1_ref[...]) * (1.0 + sc1) + sh1
    hb = h.astype(BF16)

    qkv = _dot(hb, win_ref[:, 0:3 * D_ATT])
    q = (qkv[:, 0:D_ATT] * (HEAD_DIM ** -0.5)).astype(BF16)
    k = qkv[:, D_ATT:2 * D_ATT]
    v = qkv[:, 2 * D_ATT:3 * D_ATT]
    ko_ref[0] = k
    vo_ref[0] = v
    kring[WINDOW:RING, :] = k.astype(BF16)
    vring[WINDOW:RING, :] = v.astype(BF16)

    base = j * TL
    for c in range(TL // CHUNK):
        lim = WINDOW - (base + c * CHUNK)
        for qd in range(2):
            ls = slice(qd * QUAD, (qd + 1) * QUAD)
            o = _attend(q[c * CHUNK:(c + 1) * CHUNK, ls],
                        kring[c * CHUNK:c * CHUNK + BAND, ls],
                        vring[c * CHUNK:c * CHUNK + BAND, ls],
                        bias_ref[qd], lim)
            att_s[c * CHUNK:(c + 1) * CHUNK, ls] = o.astype(BF16)

    kring[0:WINDOW, :] = kring[TL:RING, :]
    vring[0:WINDOW, :] = vring[TL:RING, :]

    cbcv = _dot(hb, win_ref[:, 3 * D_ATT:3 * D_ATT + 3 * D_CONV])
    cb = cbcv[:, 0:D_CONV]
    u = cbcv[:, D_CONV:2 * D_CONV] * cbcv[:, 2 * D_CONV:3 * D_CONV]
    row = lax.broadcasted_iota(jnp.int32, (8, D_CONV), 0)
    prev = ucarry[...]
    r1 = pltpu.roll(u, 1, axis=0)
    r2 = pltpu.roll(u, 2, axis=0)
    u_m1 = jnp.concatenate(
        [jnp.where(row < 1, pltpu.roll(prev, 1, axis=0), r1[0:8]), r1[8:]], axis=0)
    u_m2 = jnp.concatenate(
        [jnp.where(row < 2, pltpu.roll(prev, 2, axis=0), r2[0:8]), r2[8:]], axis=0)
    yc = cw_ref[0:1, :] * u_m2 + cw_ref[1:2, :] * u_m1 + cw_ref[2:3, :] * u + cbias_ref[...]
    conv_out = (cb * yc).astype(BF16)
    ucarry[...] = u[TL - 8:TL, :]
    uo_ref[0] = u[TL - 8:TL, :]

    gates = _dot(hb, win_ref[:, 3 * D_ATT + 3 * D_CONV:])
    pa = _dot(att_s[...], wpa_ref[...])
    pb = _dot(conv_out, wpb_ref[...])
    mixin = _sigmoid(gates[:, 0:D_MODEL]) * pa + _sigmoid(gates[:, D_MODEL:]) * pb
    mix = _dot(mixin.astype(BF16), wo_ref[...])
    x1 = x + gt1 * mix
    x1_ref[0] = x1
    h2 = _rms(x1, g2_ref[...]) * (1.0 + sc2) + sh2
    _store_rows_as_tiles(h2_ref, h2)

    route, new_cnt = _route(h2, wcat_ref, whi_ref, br_ref, cnt_s[...])
    route_ref[0] = route
    cnt_s[...] = new_cnt
    cnt_ref[...] = new_cnt


def _prompt_main(x, mod, g1, g2, win, bias_q, cw, cbias, wpa, wpb, wo, wcat, whi, br):
    nb, seq, _ = x.shape
    nt = seq // TL
    keep = WINDOW // TL
    tile = lambda b, j: (b, j, 0)
    last = lambda b, j: (b, jnp.maximum(j - (nt - keep), 0), 0)
    perb = lambda b, j: (b, 0, 0)
    in_specs = [
        pl.BlockSpec((1, TL, D_MODEL), tile),
        pl.BlockSpec((1, 6, D_MODEL), perb),
        _const_spec(g1.shape), _const_spec(g2.shape), _const_spec(win.shape),
        _const_spec(bias_q.shape), _const_spec(cw.shape), _const_spec(cbias.shape),
        _const_spec(wpa.shape), _const_spec(wpb.shape), _const_spec(wo.shape),
        _const_spec(wcat.shape), _const_spec(whi.shape), _const_spec(br.shape),
    ]
    out_specs = [
        pl.BlockSpec((1, TL, D_MODEL), tile),
        pl.BlockSpec((TL * ROW_TILES, LANES), lambda b, j: (b * nt + j, 0)),
        pl.BlockSpec((1, TL, D_ATT), last),
        pl.BlockSpec((1, TL, D_ATT), last),
        pl.BlockSpec((1, 8, D_CONV), perb),
        pl.BlockSpec((1, TL, RLANES), tile),
        pl.BlockSpec((1, RLANES), lambda b, j: (0, 0)),
    ]
    out_shape = [
        jax.ShapeDtypeStruct((nb, seq, D_MODEL), F32),
        jax.ShapeDtypeStruct((nb * seq * ROW_TILES, LANES), F32),
        jax.ShapeDtypeStruct((nb, WINDOW, D_ATT), F32),
        jax.ShapeDtypeStruct((nb, WINDOW, D_ATT), F32),
        jax.ShapeDtypeStruct((nb, 8, D_CONV), F32),
        jax.ShapeDtypeStruct((nb, seq, RLANES), F32),
        jax.ShapeDtypeStruct((1, RLANES), F32),
    ]
    scratch = [
        pltpu.VMEM((RING, D_ATT), BF16), pltpu.VMEM((RING, D_ATT), BF16),
        pltpu.VMEM((TL, D_ATT), BF16), pltpu.VMEM((8, D_CONV), F32),
        pltpu.VMEM((1, RLANES), F32),
    ]
    return pl.pallas_call(
        _prompt_kernel,
        grid=(nb, nt),
        in_specs=in_specs, out_specs=out_specs, out_shape=out_shape,
        scratch_shapes=scratch,
        compiler_params=pltpu.CompilerParams(
            dimension_semantics=("arbitrary", "arbitrary"), vmem_limit_bytes=VMEM_LIMIT),
        name="prompt_main",
    )(x, mod, g1, g2, win, bias_q, cw, cbias, wpa, wpb, wo, wcat, whi, br)


def _sample_kernel(x_ref, mod_ref, ck_ref, cv_ref, up1_ref, up2_ref, cnt_in_ref,
                   g1_ref, g2_ref, win_ref, bias_ref, cw_ref, cbias_ref,
                   wpa_ref, wpb_ref, wo_ref, wcat_ref, whi_ref, br_ref,
                   x1_ref, h2_ref, ko_ref, vo_ref, uo_ref, route_ref, cnt_ref,
                   h_s, q_s, kn_s, vn_s, kband, vband, att_s, conv_s, h2_s, *, nseq, slen):
    n = pl.program_id(0)
    ntok = nseq * slen

    @pl.when(n == 0)
    def _():
        def norm_body(i, carry):
            rows = pl.ds(pl.multiple_of(i * slen, slen), slen)
            xi = x_ref[rows, :]
            m = mod_ref[i]
            hi = _rms(xi, g1_ref[...]) * (1.0 + m[1:2, :]) + m[0:1, :]
            h_s[rows, :] = hi.astype(BF16)
            return carry
        lax.fori_loop(0, nseq, norm_body, 0)
        hb = h_s[...]
        qkv = _dot(hb, win_ref[:, 0:3 * D_ATT])
        q_s[...] = (qkv[:, 0:D_ATT] * (HEAD_DIM ** -0.5)).astype(BF16)
        k = qkv[:, D_ATT:2 * D_ATT]
        v = qkv[:, 2 * D_ATT:3 * D_ATT]
        ko_ref[...] = k
        vo_ref[...] = v
        kn_s[...] = k.astype(BF16)
        vn_s[...] = v.astype(BF16)

        cbcv = _dot(hb, win_ref[:, 3 * D_ATT:3 * D_ATT + 3 * D_CONV])
        cb = cbcv[:, 0:D_CONV]
        u = cbcv[:, D_CONV:2 * D_CONV] * cbcv[:, 2 * D_CONV:3 * D_CONV]
        pos = lax.broadcasted_iota(jnp.int32, (ntok, D_CONV), 0) & (slen - 1)
        u_m1 = jnp.where(pos < 1, up1_ref[...], pltpu.roll(u, 1, axis=0))
        u_m2 = jnp.where(pos < 2, up2_ref[...], pltpu.roll(u, 2, axis=0))
        yc = cw_ref[0:1, :] * u_m2 + cw_ref[1:2, :] * u_m1 + cw_ref[2:3, :] * u + cbias_ref[...]
        conv_s[...] = (cb * yc).astype(BF16)
        uo_ref[...] = u

    rows = pl.ds(pl.multiple_of(n * slen, slen), slen)
    kband[0:WINDOW, :] = ck_ref[0]
    vband[0:WINDOW, :] = cv_ref[0]
    kband[WINDOW:WINDOW + slen, :] = kn_s[rows, :]
    vband[WINDOW:WINDOW + slen, :] = vn_s[rows, :]
    qn = q_s[rows, :]
    outs = []
    for qd in range(2):
        ls = slice(qd * QUAD, (qd + 1) * QUAD)
        outs.append(_attend(qn[:, ls], kband[:, ls], vband[:, ls], bias_ref[qd], None))
    att_s[rows, :] = jnp.concatenate(outs, axis=1).astype(BF16)

    @pl.when(n == nseq - 1)
    def _():
        gates = _dot(h_s[...], win_ref[:, 3 * D_ATT + 3 * D_CONV:])
        pa = _dot(att_s[...], wpa_ref[...])
        pb = _dot(conv_s[...], wpb_ref[...])
        mixin = _sigmoid(gates[:, 0:D_MODEL]) * pa + _sigmoid(gates[:, D_MODEL:]) * pb
        x1_ref[...] = _dot(mixin.astype(BF16), wo_ref[...])

        def res_body(i, carry):
            r = pl.ds(pl.multiple_of(i * slen, slen), slen)
            m = mod_ref[i]
            x1 = x_ref[r, :] + m[2:3, :] * x1_ref[r, :]
            x1_ref[r, :] = x1
            h2_s[r, :] = _rms(x1, g2_ref[...]) * (1.0 + m[4:5, :]) + m[3:4, :]
            return carry
        lax.fori_loop(0, nseq, res_body, 0)

        h2 = h2_s[...]
        _store_rows_as_tiles(h2_ref, h2)
        route, new_cnt = _route(h2, wcat_ref, whi_ref, br_ref, cnt_in_ref[...])
        route_ref[...] = route
        cnt_ref[...] = new_cnt


def _sample_main(x2d, mod, ck, cv, up1, up2, cnt_in, g1, g2, win, bias_s, cw, cbias,
                 wpa, wpb, wo, wcat, whi, br, nseq, slen):
    ntok = nseq * slen
    args = (x2d, mod, ck, cv, up1, up2, cnt_in, g1, g2, win, bias_s, cw, cbias,
            wpa, wpb, wo, wcat, whi, br)
    in_specs = []
    for idx, a in enumerate(args):
        if idx in (2, 3):
            in_specs.append(pl.BlockSpec((1, WINDOW, D_ATT), lambda n: (n, 0, 0)))
        else:
            in_specs.append(_const_spec(a.shape))
    whole = lambda shape: pl.BlockSpec(shape, lambda n: (0,) * len(shape))
    out_shapes = [(ntok, D_MODEL), (ntok * ROW_TILES, LANES), (ntok, D_ATT), (ntok, D_ATT),
                  (ntok, D_CONV), (ntok, RLANES), (1, RLANES)]
    scratch = [
        pltpu.VMEM((ntok, D_MODEL), BF16), pltpu.VMEM((ntok, D_ATT), BF16),
        pltpu.VMEM((ntok, D_ATT), BF16), pltpu.VMEM((ntok, D_ATT), BF16),
        pltpu.VMEM((WINDOW + slen, D_ATT), BF16), pltpu.VMEM((WINDOW + slen, D_ATT), BF16),
        pltpu.VMEM((ntok, D_ATT), BF16), pltpu.VMEM((ntok, D_CONV), BF16),
        pltpu.VMEM((ntok, D_MODEL), F32),
    ]
    return pl.pallas_call(
        functools.partial(_sample_kernel, nseq=nseq, slen=slen),
        grid=(nseq,),
        in_specs=in_specs,
        out_specs=[whole(s) for s in out_shapes],
        out_shape=[jax.ShapeDtypeStruct(s, F32) for s in out_shapes],
        scratch_shapes=scratch,
        compiler_params=pltpu.CompilerParams(
            dimension_semantics=("arbitrary",), vmem_limit_bytes=VMEM_LIMIT),
        name="sample_main",
    )(*args)


def _issue_rows(n, body):
    def group(g, carry):
        for u in range(ISSUE_UNROLL):
            body(g * ISSUE_UNROLL + u, u)
        return carry
    lax.fori_loop(0, n // ISSUE_UNROLL, group, 0)


def _dispatch_kernel(pend_ref, d1_ref, d2_ref, hp_ref, hs_ref, xs_out, zbuf, sem, zsem, *,
                     np_tiles, nblocks):
    i = pl.program_id(0)
    blk_rows = MOE_BLK * ROW_TILES

    @pl.when(i == 0)
    def _():
        zbuf[...] = jnp.zeros_like(zbuf)

        def zcopy(e):
            start = pl.multiple_of((pend_ref[e + 1] - MOE_BLK) * ROW_TILES, blk_rows)
            return pltpu.make_async_copy(zbuf, xs_out.at[pl.ds(start, blk_rows)], zsem)

        def zstart(e, carry):
            @pl.when(pend_ref[e + 1] > pend_ref[e])
            def _():
                zcopy(e).start()
            return carry

        def zwait(e, carry):
            @pl.when(pend_ref[e + 1] > pend_ref[e])
            def _():
                zcopy(e).wait()
            return carry
        lax.fori_loop(0, N_EXPERTS, zstart, 0)
        lax.fori_loop(0, N_EXPERTS, zwait, 0)

        def tcopy(bk):
            start = pl.multiple_of(bk * blk_rows, blk_rows)
            return pltpu.make_async_copy(zbuf, xs_out.at[pl.ds(start, blk_rows)], zsem)

        def tstart(bk, carry):
            tcopy(bk).start()
            return carry

        def twait(bk, carry):
            tcopy(bk).wait()
            return carry
        used = pend_ref[N_EXPERTS] // MOE_BLK
        lax.fori_loop(used, nblocks, tstart, 0)
        lax.fori_loop(used, nblocks, twait, 0)

    def scatter_tile(src):
        def row(r, u):
            pltpu.make_async_copy(_tile(src, r), _tile(xs_out, d1_ref[r]), sem).start(priority=u % 2)
            pltpu.make_async_copy(_tile(src, r), _tile(xs_out, d2_ref[r]), sem).start(
                priority=(u + 1) % 2)
        _issue_rows(TM, row)
        for _ in range(2):
            pltpu.make_async_copy(src, xs_out.at[pl.ds(0, TM * ROW_TILES)], sem).wait()

    @pl.when(i < np_tiles)
    def _():
        scatter_tile(hp_ref)

    @pl.when(i >= np_tiles)
    def _():
        scatter_tile(hs_ref)


def _dispatch(pend, d1, d2, h2p, h2s, nslots):
    np_tiles = h2p.shape[0] // (TM * ROW_TILES)
    ns_tiles = h2s.shape[0] // (TM * ROW_TILES)
    smem_tile = pl.BlockSpec((TM,), lambda i, *_: (i,), memory_space=pltpu.SMEM)
    rows = TM * ROW_TILES
    return pl.pallas_call(
        functools.partial(_dispatch_kernel, np_tiles=np_tiles, nblocks=nslots // MOE_BLK),
        grid_spec=pltpu.PrefetchScalarGridSpec(
            num_scalar_prefetch=1,
            grid=(np_tiles + ns_tiles,),
            in_specs=[smem_tile, smem_tile,
                      pl.BlockSpec((rows, LANES), lambda i, *_: (jnp.minimum(i, np_tiles - 1), 0)),
                      pl.BlockSpec((rows, LANES), lambda i, *_: (jnp.maximum(i - np_tiles, 0), 0))],
            out_specs=pl.BlockSpec(memory_space=pl.ANY),
            scratch_shapes=[pltpu.VMEM((MOE_BLK * ROW_TILES, LANES), F32),
                            pltpu.SemaphoreType.DMA(()), pltpu.SemaphoreType.DMA(())],
        ),
        out_shape=jax.ShapeDtypeStruct((nslots * ROW_TILES, LANES), F32),
        compiler_params=pltpu.CompilerParams(dimension_semantics=("arbitrary",)),
        name="dispatch",
    )(pend, d1, d2, h2p, h2s)


def _expert_kernel(blk_e_ref, nblk_ref, xs_ref, wg_ref, wu_ref, wd_ref, y_ref):
    del blk_e_ref
    live = pl.program_id(0) < nblk_ref[0]

    @pl.when(live)
    def _():
        xb = _load_tiles_as_rows(xs_ref, MOE_BLK).astype(BF16)
        g = _dot(xb, wg_ref[0])
        u = _dot(xb, wu_ref[0])
        a = (g * _sigmoid(g)) * u
        _store_rows_as_tiles(y_ref, _dot(a.astype(BF16), wd_ref[0]))

    @pl.when(jnp.logical_not(live))
    def _():
        y_ref[...] = jnp.zeros_like(y_ref)


def _experts(blk_e, nblk, xs, wg, wu, wd):
    blk_rows = MOE_BLK * ROW_TILES
    nblocks = xs.shape[0] // blk_rows
    row_map = lambda i, be, nb: (jnp.minimum(i, nb[0] - 1), 0)
    w_map = lambda i, be, nb: (be[i], 0, 0)
    return pl.pallas_call(
        _expert_kernel,
        grid_spec=pltpu.PrefetchScalarGridSpec(
            num_scalar_prefetch=2,
            grid=(nblocks,),
            in_specs=[pl.BlockSpec((blk_rows, LANES), row_map),
                      pl.BlockSpec((1, D_MODEL, D_EXPERT), w_map),
                      pl.BlockSpec((1, D_MODEL, D_EXPERT), w_map),
                      pl.BlockSpec((1, D_EXPERT, D_MODEL), w_map)],
            out_specs=pl.BlockSpec((blk_rows, LANES), lambda i, be, nb: (i, 0)),
        ),
        out_shape=jax.ShapeDtypeStruct(xs.shape, F32),
        compiler_params=pltpu.CompilerParams(
            dimension_semantics=("arbitrary",), vmem_limit_bytes=VMEM_LIMIT),
        name="experts",
    )(blk_e, nblk, xs, wg, wu, wd)


def _combine_kernel(d1_ref, d2_ref, y_hbm, x1_ref, route_ref, gate_ref, gf_ref, o_ref,
                    buf1, buf2, sem):
    def row(r, u):
        pltpu.make_async_copy(_tile(y_hbm, d1_ref[r]), _tile(buf1, r), sem).start(priority=u % 2)
        pltpu.make_async_copy(_tile(y_hbm, d2_ref[r]), _tile(buf2, r), sem).start(
            priority=(u + 1) % 2)
    _issue_rows(TM, row)
    for buf in (buf1, buf2):
        pltpu.make_async_copy(y_hbm.at[pl.ds(0, TM * ROW_TILES)], buf, sem).wait()
    route = route_ref[0]
    w1 = route[:, 4:5]
    w2 = route[:, 5:6]
    ffn = w1 * _load_tiles_as_rows(buf1, TM) + w2 * _load_tiles_as_rows(buf2, TM)
    x2 = x1_ref[0] + gate_ref[0] * ffn
    o_ref[0] = _rms(x2, gf_ref[...])


def _combine(d1, d2, y, x1, route, gate, gf, tok_base):
    nb, seq, _ = x1.shape
    nt = seq // TM
    blk0 = tok_base // TM
    smem_tile = pl.BlockSpec((TM,), lambda b, j: (blk0 + b * nt + j,), memory_space=pltpu.SMEM)
    tile = lambda b, j: (b, j, 0)
    grows = gate.shape[1]
    gate_spec = (pl.BlockSpec((1, 1, D_MODEL), lambda b, j: (b, 0, 0)) if grows == 1
                 else pl.BlockSpec((1, TM, D_MODEL), tile))
    return pl.pallas_call(
        _combine_kernel,
        grid=(nb, nt),
        in_specs=[smem_tile, smem_tile, pl.BlockSpec(memory_space=pl.ANY),
                  pl.BlockSpec((1, TM, D_MODEL), tile),
                  pl.BlockSpec((1, TM, RLANES), tile),
                  gate_spec,
                  pl.BlockSpec((1, D_MODEL), lambda b, j: (0, 0))],
        out_specs=pl.BlockSpec((1, TM, D_MODEL), tile),
        out_shape=jax.ShapeDtypeStruct(x1.shape, F32),
        scratch_shapes=[pltpu.VMEM((TM * ROW_TILES, LANES), F32),
                        pltpu.VMEM((TM * ROW_TILES, LANES), F32),
                        pltpu.SemaphoreType.DMA(())],
        compiler_params=pltpu.CompilerParams(
            dimension_semantics=("arbitrary", "arbitrary"), vmem_limit_bytes=VMEM_LIMIT),
        name="combine",
    )(d1, d2, y, x1, route, gate, gf)


def _band_bias(rel_bias, rows, keys):
    n = rows - 1 + keys
    dist = WINDOW + rows - 1 - np.arange(n + 1)
    flipped = rel_bias[:, np.clip(dist, -MAX_REL, MAX_REL) + MAX_REL]
    skew = jnp.tile(flipped, (1, rows))[:, :rows * n].reshape(N_HEADS, rows, n)
    b = skew[:, :, rows - 1:rows - 1 + keys]
    return b.reshape(2, 4 * rows, keys)


def kernel(x_prompt, x_sample, cache_attn_k, cache_attn_v, state_conv, c_prompt, c_sample,
           w_ada, b_ada, norm1_g, norm2_g, w_in, rel_bias, conv_w, conv_b, w_pa, w_pb, w_o,
           w_group, b_group, w_expert, b_expert, w_e_gate, w_e_up, w_e_down, final_g):
    assert w_ada.shape[0] == 1, "single trunk layer"
    nb, seq, _ = x_prompt.shape
    nseq, slen, _ = x_sample.shape
    ntok_p = nb * seq
    ntok_s = nseq * slen
    ntok = ntok_p + ntok_s
    assert seq % TL == 0 and WINDOW % TL == 0 and seq % TM == 0 and ntok_s % TM == 0
    assert slen >= 2 and slen & (slen - 1) == 0 and slen % 16 == 0

    n_c = nb + nseq
    n_pad = -(-n_c // 8) * 8
    c_all = jnp.concatenate([c_prompt, c_sample, jnp.zeros((n_pad - n_c, D_MODEL), F32)], axis=0)
    mod = _ada(c_all, w_ada[0], b_ada[0]).reshape(n_pad, 6, D_MODEL)
    mod_p = mod[:nb]
    mod_s = mod[nb:n_c]

    win = w_in[0].astype(BF16)
    wpa = w_pa[0].astype(BF16)
    wpb = w_pb[0].astype(BF16)
    wo = w_o[0].astype(BF16)
    g1 = norm1_g[0].reshape(1, D_MODEL)
    g2 = norm2_g[0].reshape(1, D_MODEL)
    gf = final_g.reshape(1, D_MODEL)
    cw = jnp.concatenate([conv_w[0], jnp.zeros((8 - conv_w.shape[1], D_CONV), F32)], axis=0)
    cbias = conv_b[0].reshape(1, D_CONV)
    wr = jnp.concatenate([w_expert[0], w_group[0],
                          jnp.zeros((D_MODEL, RLANES - N_EXPERTS - N_GROUPS), F32)], axis=1)
    whi = wr.astype(BF16)
    wlo = (wr - whi.astype(F32)).astype(BF16)
    wcat = jnp.concatenate([whi, wlo], axis=1)
    br = jnp.concatenate([b_expert[0], b_group[0],
                          jnp.zeros((RLANES - N_EXPERTS - N_GROUPS,), F32)]).reshape(1, RLANES)
    bias_p = _band_bias(rel_bias[0], CHUNK, BAND)
    bias_s = _band_bias(rel_bias[0], slen, WINDOW + slen)

    x1p, h2p, kp, vp, up8, route_p, cnt_p = _prompt_main(
        x_prompt, mod_p, g1, g2, win, bias_p, cw, cbias, wpa, wpb, wo, wcat, whi, br)

    st = state_conv[0]
    up1 = jnp.zeros((nseq, slen, D_CONV), F32).at[:, 0].set(st[:, 1]).reshape(ntok_s, D_CONV)
    up2 = (jnp.zeros((nseq, slen, D_CONV), F32).at[:, 0].set(st[:, 0]).at[:, 1].set(st[:, 1])
           .reshape(ntok_s, D_CONV))
    ck = cache_attn_k[0].reshape(nseq, WINDOW, D_ATT).astype(BF16)
    cv = cache_attn_v[0].reshape(nseq, WINDOW, D_ATT).astype(BF16)
    x1s, h2s, ks, vs, us, route_s, cnt = _sample_main(
        x_sample.reshape(ntok_s, D_MODEL), mod_s, ck, cv, up1, up2, cnt_p,
        g1, g2, win, bias_s, cw, cbias, wpa, wpb, wo, wcat, whi, br, nseq, slen)

    route_all = jnp.concatenate([route_p.reshape(ntok_p, RLANES)[:, :4], route_s[:, :4]], axis=0)
    experts = route_all[:, 0:2].astype(jnp.int32)
    ranks = route_all[:, 2:4].astype(jnp.int32)
    counts = cnt[0, :N_EXPERTS].astype(jnp.int32)
    pcounts = (counts + MOE_BLK - 1) // MOE_BLK * MOE_BLK
    pend = jnp.cumsum(pcounts)
    pstart = pend - pcounts
    eids = jnp.arange(N_EXPERTS, dtype=jnp.int32)
    dest = jnp.sum(jnp.where(experts[..., None] == eids, pstart, 0), axis=-1) + ranks
    d1 = dest[:, 0]
    d2 = dest[:, 1]
    nblocks = (2 * ntok) // MOE_BLK + N_EXPERTS
    blk_start = jnp.arange(nblocks, dtype=jnp.int32) * MOE_BLK
    blk_e = jnp.minimum(jnp.sum((pend[None, :] <= blk_start[:, None]).astype(jnp.int32), axis=1),
                        N_EXPERTS - 1)
    nblk = (pend[-1:] // MOE_BLK).astype(jnp.int32)
    pend0 = jnp.concatenate([jnp.zeros((1,), jnp.int32), pend.astype(jnp.int32)])

    xs = _dispatch(pend0, d1, d2, h2p, h2s, nblocks * MOE_BLK)
    y = _experts(blk_e, nblk, xs, w_e_gate[0].astype(BF16), w_e_up[0].astype(BF16),
                 w_e_down[0].astype(BF16))

    y_prompt = _combine(d1, d2, y, x1p, route_p, mod_p[:, 5:6, :], gf, 0)
    gate_s = jnp.repeat(mod_s[:, 5, :], slen, axis=0).reshape(1, ntok_s, D_MODEL)
    y_sample = _combine(d1, d2, y, x1s.reshape(1, ntok_s, D_MODEL),
                        route_s.reshape(1, ntok_s, RLANES), gate_s, gf, ntok_p)

    new_k_p = kp.reshape(1, nb, WINDOW, N_HEADS, HEAD_DIM)
    new_v_p = vp.reshape(1, nb, WINDOW, N_HEADS, HEAD_DIM)
    new_conv_p = up8[:, 6:8, :].reshape(1, nb, 2, D_CONV)
    new_k_s = ks.reshape(1, nseq, slen, N_HEADS, HEAD_DIM)
    new_v_s = vs.reshape(1, nseq, slen, N_HEADS, HEAD_DIM)
    new_conv_s = us.reshape(nseq, slen, D_CONV)[:, slen - 2:, :].reshape(1, nseq, 2, D_CONV)
    return (y_prompt, y_sample.reshape(nseq, slen, D_MODEL), new_k_p, new_v_p, new_conv_p,
            new_k_s, new_v_s, new_conv_s)
```

```python
import functools

import numpy as np
import jax
import jax.numpy as jnp
from jax import lax
from jax.experimental import pallas as pl
from jax.experimental.pallas import tpu as pltpu

F32 = jnp.float32
BF16 = jnp.bfloat16

D_MODEL = 1024
CHUNK = 64
LEFT = 8
WINDOW = LEFT * CHUNK
BAND = WINDOW + CHUNK
N_HEADS = 8
HEAD_DIM = 64
D_ATT = N_HEADS * HEAD_DIM
QUAD = 256
MAX_REL = 128
D_CONV = 512
N_GROUPS = 4
EPG = 8
N_EXPERTS = 32
D_EXPERT = 512
EPS = 1e-6
NEG = -1e30

TL = 512
RING = WINDOW + TL
MOE_BLK = 256
TM = 512
INV_TILE = 512
RLANES = 128
LANES = 128
ROW_TILES = D_MODEL // LANES
ISSUE_UNROLL = 8
VMEM_LIMIT = 56 * 1024 * 1024


def _const_spec(shape):
    nd = len(shape)
    return pl.BlockSpec(shape, lambda *_: (0,) * nd, pipeline_mode=pl.Buffered(1))


def _dot(a, b):
    return jnp.dot(a, b, preferred_element_type=F32)


def _sigmoid(x):
    return 1.0 / (1.0 + jnp.exp(-x))


def _rms(x, g):
    ms = jnp.mean(x * x, axis=-1, keepdims=True)
    return x * lax.rsqrt(ms + EPS) * g


def _store_rows_as_tiles(ref, val):
    r = val.shape[0]
    for c in range(ROW_TILES):
        ref[pl.ds(c, r, stride=ROW_TILES), :] = val[:, c * LANES:(c + 1) * LANES]


def _load_tiles_as_rows(ref, r):
    return jnp.concatenate(
        [ref[pl.ds(c, r, stride=ROW_TILES), :] for c in range(ROW_TILES)], axis=1)


def _tile(ref, row):
    return ref.at[pl.ds(pl.multiple_of(row * ROW_TILES, ROW_TILES), ROW_TILES)]


def _ada_kernel(c_ref, w_ref, b_ref, o_ref):
    c = c_ref[...]
    s = c * _sigmoid(c)
    o_ref[...] = jnp.dot(s, w_ref[...], preferred_element_type=F32,
                         precision=lax.Precision.HIGHEST) + b_ref[...]


def _ada(c_all, w_ada, b_ada):
    n = c_all.shape[0]
    nb = 1024
    return pl.pallas_call(
        _ada_kernel,
        grid=(6 * D_MODEL // nb,),
        in_specs=[pl.BlockSpec((n, D_MODEL), lambda i: (0, 0)),
                  pl.BlockSpec((D_MODEL, nb), lambda i: (0, i)),
                  pl.BlockSpec((1, nb), lambda i: (0, i))],
        out_specs=pl.BlockSpec((n, nb), lambda i: (0, i)),
        out_shape=jax.ShapeDtypeStruct((n, 6 * D_MODEL), F32),
        name="ada",
    )(c_all, w_ada, b_ada.reshape(1, -1))


def _attend(q, kb, vb, bias, lim):
    r = q.shape[0]
    nk = kb.shape[0]
    assert r & (r - 1) == 0
    qt = jnp.concatenate([q] * 4, axis=0)
    rowh = lax.broadcasted_iota(jnp.int32, (4 * r, QUAD), 0) >> (r.bit_length() - 1)
    laneh = lax.broadcasted_iota(jnp.int32, (4 * r, QUAD), 1) >> 6
    qm = jnp.where(rowh == laneh, qt, jnp.zeros_like(qt))
    s = lax.dot_general(qm, kb, (((1,), (1,)), ((), ())), preferred_element_type=F32)
    s = s + bias
    if lim is not None:
        col = lax.broadcasted_iota(jnp.int32, (4 * r, nk), 1)
        s = jnp.where(col >= lim, s, NEG)
    m = jnp.max(s, axis=1, keepdims=True)
    p = jnp.exp(s - m)
    l = jnp.sum(p, axis=1, keepdims=True)
    o = _dot(p.astype(BF16), vb) * (1.0 / l)
    lane_o = lax.broadcasted_iota(jnp.int32, (r, QUAD), 1) >> 6
    out = o[0:r]
    for h in range(1, 4):
        out = jnp.where(lane_o == h, o[h * r:(h + 1) * r], out)
    return out


def _route(h2, wcat_ref, whi_ref, br_ref, cnt):
    r = h2.shape[0]
    hi = h2.astype(BF16)
    lo = (h2 - hi.astype(F32)).astype(BF16)
    z = _dot(hi, wcat_ref[...])
    logits = z[:, :RLANES] + z[:, RLANES:] + _dot(lo, whi_ref[...]) + br_ref[...]
    lane = lax.broadcasted_iota(jnp.int32, (r, RLANES), 1)
    lane_f = lane.astype(F32)
    big = jnp.float32(1000.0)

    lg = jnp.where((lane >= N_EXPERTS) & (lane < N_EXPERTS + N_GROUPS), logits, NEG)
    mg = jnp.max(lg, axis=1, keepdims=True)
    gi = jnp.min(jnp.where(lg == mg, lane_f, big), axis=1, keepdims=True) - N_EXPERTS
    pg = 1.0 / jnp.sum(jnp.exp(lg - mg), axis=1, keepdims=True)

    grp_of_lane = (lane >> 3).astype(F32)
    le = jnp.where((lane < N_EXPERTS) & (grp_of_lane == gi), logits, NEG)
    m1 = jnp.max(le, axis=1, keepdims=True)
    i1 = jnp.min(jnp.where(le == m1, lane_f, big), axis=1, keepdims=True)
    sel1 = lane_f == i1
    le2 = jnp.where(sel1, NEG, le)
    m2 = jnp.max(le2, axis=1, keepdims=True)
    i2 = jnp.min(jnp.where(le2 == m2, lane_f, big), axis=1, keepdims=True)
    sel2 = lane_f == i2
    rr = jnp.exp(m2 - m1)
    inv = pg / (1.0 + rr)
    w1 = inv
    w2 = inv * rr

    oh = jnp.where(sel1 | sel2, 1.0, 0.0).astype(F32)
    ri = lax.broadcasted_iota(jnp.int32, (r, r), 0)
    ci = lax.broadcasted_iota(jnp.int32, (r, r), 1)
    tri = jnp.where(ri > ci, 1.0, 0.0).astype(BF16)
    before = _dot(tri, oh.astype(BF16)) + cnt
    r1 = jnp.sum(jnp.where(sel1, before, 0.0), axis=1, keepdims=True)
    r2 = jnp.sum(jnp.where(sel2, before, 0.0), axis=1, keepdims=True)
    new_cnt = cnt + jnp.sum(oh, axis=0, keepdims=True)

    route = jnp.where(lane == 0, i1, 0.0)
    route = jnp.where(lane == 1, i2, route)
    route = jnp.where(lane == 2, r1, route)
    route = jnp.where(lane == 3, r2, route)
    route = jnp.where(lane == 4, w1, route)
    route = jnp.where(lane == 5, w2, route)
    return route, new_cnt


def _prompt_kernel(x_ref, mod_ref, g1_ref, g2_ref, win_ref, bias_ref, cw_ref, cbias_ref,
                   wpa_ref, wpb_ref, wo_ref, wcat_ref, whi_ref, br_ref,
                   x1_ref, h2_ref, ko_ref, vo_ref, uo_ref, route_ref, cnt_ref,
                   kring, vring, att_s, ucarry, cnt_s):
    b = pl.program_id(0)
    j = pl.program_id(1)

    @pl.when((b == 0) & (j == 0))
    def _():
        cnt_s[...] = jnp.zeros_like(cnt_s)

    @pl.when(j == 0)
    def _():
        kring[0:WINDOW, :] = jnp.zeros((WINDOW, D_ATT), BF16)
        vring[0:WINDOW, :] = jnp.zeros((WINDOW, D_ATT), BF16)
        ucarry[...] = jnp.zeros_like(ucarry)

    sh1 = mod_ref[0, 0:1, :]
    sc1 = mod_ref[0, 1:2, :]
    gt1 = mod_ref[0, 2:3, :]
    sh2 = mod_ref[0, 3:4, :]
    sc2 = mod_ref[0, 4:5, :]

    x = x_ref[0]
    h = _rms(x, g1_ref[...]) * (1.0 + sc1) + sh1
    hb = h.astype(BF16)

    qkv = _dot(hb, win_ref[:, 0:3 * D_ATT])
    q = (qkv[:, 0:D_ATT] * (HEAD_DIM ** -0.5)).astype(BF16)
    k = qkv[:, D_ATT:2 * D_ATT]
    v = qkv[:, 2 * D_ATT:3 * D_ATT]
    ko_ref[0] = k
    vo_ref[0] = v
    kring[WINDOW:RING, :] = k.astype(BF16)
    vring[WINDOW:RING, :] = v.astype(BF16)

    base = j * TL
    for c in range(TL // CHUNK):
        lim = WINDOW - (base + c * CHUNK)
        for qd in range(2):
            ls = slice(qd * QUAD, (qd + 1) * QUAD)
            o = _attend(q[c * CHUNK:(c + 1) * CHUNK, ls],
                        kring[c * CHUNK:c * CHUNK + BAND, ls],
                        vring[c * CHUNK:c * CHUNK + BAND, ls],
                        bias_ref[qd], lim)
            att_s[c * CHUNK:(c + 1) * CHUNK, ls] = o.astype(BF16)

    kring[0:WINDOW, :] = kring[TL:RING, :]
    vring[0:WINDOW, :] = vring[TL:RING, :]

    cbcv = _dot(hb, win_ref[:, 3 * D_ATT:3 * D_ATT + 3 * D_CONV])
    cb = cbcv[:, 0:D_CONV]
    u = cbcv[:, D_CONV:2 * D_CONV] * cbcv[:, 2 * D_CONV:3 * D_CONV]
    row = lax.broadcasted_iota(jnp.int32, (8, D_CONV), 0)
    prev = ucarry[...]
    r1 = pltpu.roll(u, 1, axis=0)
    r2 = pltpu.roll(u, 2, axis=0)
    u_m1 = jnp.concatenate(
        [jnp.where(row < 1, pltpu.roll(prev, 1, axis=0), r1[0:8]), r1[8:]], axis=0)
    u_m2 = jnp.concatenate(
        [jnp.where(row < 2, pltpu.roll(prev, 2, axis=0), r2[0:8]), r2[8:]], axis=0)
    yc = cw_ref[0:1, :] * u_m2 + cw_ref[1:2, :] * u_m1 + cw_ref[2:3, :] * u + cbias_ref[...]
    conv_out = (cb * yc).astype(BF16)
    ucarry[...] = u[TL - 8:TL, :]
    uo_ref[0] = u[TL - 8:TL, :]

    gates = _dot(hb, win_ref[:, 3 * D_ATT + 3 * D_CONV:])
    pa = _dot(att_s[...], wpa_ref[...])
    pb = _dot(conv_out, wpb_ref[...])
    mixin = _sigmoid(gates[:, 0:D_MODEL]) * pa + _sigmoid(gates[:, D_MODEL:]) * pb
    mix = _dot(mixin.astype(BF16), wo_ref[...])
    x1 = x + gt1 * mix
    x1_ref[0] = x1
    h2 = _rms(x1, g2_ref[...]) * (1.0 + sc2) + sh2
    _store_rows_as_tiles(h2_ref, h2)

    route, new_cnt = _route(h2, wcat_ref, whi_ref, br_ref, cnt_s[...])
    route_ref[0] = route
    cnt_s[...] = new_cnt
    cnt_ref[...] = new_cnt


def _prompt_main(x, mod, g1, g2, win, bias_q, cw, cbias, wpa, wpb, wo, wcat, whi, br):
    nb, seq, _ = x.shape
    nt = seq // TL
    keep = WINDOW // TL
    tile = lambda b, j: (b, j, 0)
    last = lambda b, j: (b, jnp.maximum(j - (nt - keep), 0), 0)
    perb = lambda b, j: (b, 0, 0)
    in_specs = [
        pl.BlockSpec((1, TL, D_MODEL), tile),
        pl.BlockSpec((1, 6, D_MODEL), perb),
        _const_spec(g1.shape), _const_spec(g2.shape), _const_spec(win.shape),
        _const_spec(bias_q.shape), _const_spec(cw.shape), _const_spec(cbias.shape),
        _const_spec(wpa.shape), _const_spec(wpb.shape), _const_spec(wo.shape),
        _const_spec(wcat.shape), _const_spec(whi.shape), _const_spec(br.shape),
    ]
    out_specs = [
        pl.BlockSpec((1, TL, D_MODEL), tile),
        pl.BlockSpec((TL * ROW_TILES, LANES), lambda b, j: (b * nt + j, 0)),
        pl.BlockSpec((1, TL, D_ATT), last),
        pl.BlockSpec((1, TL, D_ATT), last),
        pl.BlockSpec((1, 8, D_CONV), perb),
        pl.BlockSpec((1, TL, RLANES), tile),
        pl.BlockSpec((1, RLANES), lambda b, j: (0, 0)),
    ]
    out_shape = [
        jax.ShapeDtypeStruct((nb, seq, D_MODEL), F32),
        jax.ShapeDtypeStruct((nb * seq * ROW_TILES, LANES), F32),
        jax.ShapeDtypeStruct((nb, WINDOW, D_ATT), F32),
        jax.ShapeDtypeStruct((nb, WINDOW, D_ATT), F32),
        jax.ShapeDtypeStruct((nb, 8, D_CONV), F32),
        jax.ShapeDtypeStruct((nb, seq, RLANES), F32),
        jax.ShapeDtypeStruct((1, RLANES), F32),
    ]
    scratch = [
        pltpu.VMEM((RING, D_ATT), BF16), pltpu.VMEM((RING, D_ATT), BF16),
        pltpu.VMEM((TL, D_ATT), BF16), pltpu.VMEM((8, D_CONV), F32),
        pltpu.VMEM((1, RLANES), F32),
    ]
    return pl.pallas_call(
        _prompt_kernel,
        grid=(nb, nt),
        in_specs=in_specs, out_specs=out_specs, out_shape=out_shape,
        scratch_shapes=scratch,
        compiler_params=pltpu.CompilerParams(
            dimension_semantics=("arbitrary", "arbitrary"), vmem_limit_bytes=VMEM_LIMIT),
        name="prompt_main",
    )(x, mod, g1, g2, win, bias_q, cw, cbias, wpa, wpb, wo, wcat, whi, br)


def _sample_kernel(x_ref, mod_ref, ck_ref, cv_ref, up1_ref, up2_ref, cnt_in_ref,
                   g1_ref, g2_ref, win_ref, bias_ref, cw_ref, cbias_ref,
                   wpa_ref, wpb_ref, wo_ref, wcat_ref, whi_ref, br_ref,
                   x1_ref, h2_ref, ko_ref, vo_ref, uo_ref, route_ref, cnt_ref,
                   h_s, q_s, kn_s, vn_s, kband, vband, att_s, conv_s, h2_s, *, nseq, slen):
    n = pl.program_id(0)
    ntok = nseq * slen

    @pl.when(n == 0)
    def _():
        def norm_body(i, carry):
            rows = pl.ds(pl.multiple_of(i * slen, slen), slen)
            xi = x_ref[rows, :]
            m = mod_ref[i]
            hi = _rms(xi, g1_ref[...]) * (1.0 + m[1:2, :]) + m[0:1, :]
            h_s[rows, :] = hi.astype(BF16)
            return carry
        lax.fori_loop(0, nseq, norm_body, 0)
        hb = h_s[...]
        qkv = _dot(hb, win_ref[:, 0:3 * D_ATT])
        q_s[...] = (qkv[:, 0:D_ATT] * (HEAD_DIM ** -0.5)).astype(BF16)
        k = qkv[:, D_ATT:2 * D_ATT]
        v = qkv[:, 2 * D_ATT:3 * D_ATT]
        ko_ref[...] = k
        vo_ref[...] = v
        kn_s[...] = k.astype(BF16)
        vn_s[...] = v.astype(BF16)

        cbcv = _dot(hb, win_ref[:, 3 * D_ATT:3 * D_ATT + 3 * D_CONV])
        cb = cbcv[:, 0:D_CONV]
        u = cbcv[:, D_CONV:2 * D_CONV] * cbcv[:, 2 * D_CONV:3 * D_CONV]
        pos = lax.broadcasted_iota(jnp.int32, (ntok, D_CONV), 0) & (slen - 1)
        u_m1 = jnp.where(pos < 1, up1_ref[...], pltpu.roll(u, 1, axis=0))
        u_m2 = jnp.where(pos < 2, up2_ref[...], pltpu.roll(u, 2, axis=0))
        yc = cw_ref[0:1, :] * u_m2 + cw_ref[1:2, :] * u_m1 + cw_ref[2:3, :] * u + cbias_ref[...]
        conv_s[...] = (cb * yc).astype(BF16)
        uo_ref[...] = u

    rows = pl.ds(pl.multiple_of(n * slen, slen), slen)
    kband[0:WINDOW, :] = ck_ref[0].astype(BF16)
    vband[0:WINDOW, :] = cv_ref[0].astype(BF16)
    kband[WINDOW:WINDOW + slen, :] = kn_s[rows, :]
    vband[WINDOW:WINDOW + slen, :] = vn_s[rows, :]
    qn = q_s[rows, :]
    outs = []
    for qd in range(2):
        ls = slice(qd * QUAD, (qd + 1) * QUAD)
        outs.append(_attend(qn[:, ls], kband[:, ls], vband[:, ls], bias_ref[qd], None))
    att_s[rows, :] = jnp.concatenate(outs, axis=1).astype(BF16)

    @pl.when(n == nseq - 1)
    def _():
        gates = _dot(h_s[...], win_ref[:, 3 * D_ATT + 3 * D_CONV:])
        pa = _dot(att_s[...], wpa_ref[...])
        pb = _dot(conv_s[...], wpb_ref[...])
        mixin = _sigmoid(gates[:, 0:D_MODEL]) * pa + _sigmoid(gates[:, D_MODEL:]) * pb
        x1_ref[...] = _dot(mixin.astype(BF16), wo_ref[...])

        def res_body(i, carry):
            r = pl.ds(pl.multiple_of(i * slen, slen), slen)
            m = mod_ref[i]
            x1 = x_ref[r, :] + m[2:3, :] * x1_ref[r, :]
            x1_ref[r, :] = x1
            h2_s[r, :] = _rms(x1, g2_ref[...]) * (1.0 + m[4:5, :]) + m[3:4, :]
            return carry
        lax.fori_loop(0, nseq, res_body, 0)

        h2 = h2_s[...]
        _store_rows_as_tiles(h2_ref, h2)
        route, new_cnt = _route(h2, wcat_ref, whi_ref, br_ref, cnt_in_ref[...])
        route_ref[...] = route
        cnt_ref[...] = new_cnt


def _sample_main(x2d, mod, ck, cv, up1, up2, cnt_in, g1, g2, win, bias_s, cw, cbias,
                 wpa, wpb, wo, wcat, whi, br, nseq, slen):
    ntok = nseq * slen
    args = (x2d, mod, ck, cv, up1, up2, cnt_in, g1, g2, win, bias_s, cw, cbias,
            wpa, wpb, wo, wcat, whi, br)
    in_specs = []
    for idx, a in enumerate(args):
        if idx in (2, 3):
            in_specs.append(pl.BlockSpec((1, WINDOW, D_ATT), lambda n: (n, 0, 0)))
        else:
            in_specs.append(_const_spec(a.shape))
    whole = lambda shape: pl.BlockSpec(shape, lambda n: (0,) * len(shape))
    out_shapes = [(ntok, D_MODEL), (ntok * ROW_TILES, LANES), (ntok, D_ATT), (ntok, D_ATT),
                  (ntok, D_CONV), (ntok, RLANES), (1, RLANES)]
    scratch = [
        pltpu.VMEM((ntok, D_MODEL), BF16), pltpu.VMEM((ntok, D_ATT), BF16),
        pltpu.VMEM((ntok, D_ATT), BF16), pltpu.VMEM((ntok, D_ATT), BF16),
        pltpu.VMEM((WINDOW + slen, D_ATT), BF16), pltpu.VMEM((WINDOW + slen, D_ATT), BF16),
        pltpu.VMEM((ntok, D_ATT), BF16), pltpu.VMEM((ntok, D_CONV), BF16),
        pltpu.VMEM((ntok, D_MODEL), F32),
    ]
    return pl.pallas_call(
        functools.partial(_sample_kernel, nseq=nseq, slen=slen),
        grid=(nseq,),
        in_specs=in_specs,
        out_specs=[whole(s) for s in out_shapes],
        out_shape=[jax.ShapeDtypeStruct(s, F32) for s in out_shapes],
        scratch_shapes=scratch,
        compiler_params=pltpu.CompilerParams(
            dimension_semantics=("arbitrary",), vmem_limit_bytes=VMEM_LIMIT),
        name="sample_main",
    )(*args)


def _issue_rows(n, body):
    def group(g, carry):
        for u in range(ISSUE_UNROLL):
            body(g * ISSUE_UNROLL + u, u)
        return carry
    lax.fori_loop(0, n // ISSUE_UNROLL, group, 0)


def _invert_kernel(pend_ref, cnt_ref, d1_ref, d2_ref, inv_ref, *, ntok, nslots):
    i = pl.program_id(0)

    @pl.when(i == 0)
    def _():
        def mark(s, padc):
            inv_ref[s] = -(padc + 1)
            return padc + 1

        def per_expert(e, padc):
            return lax.fori_loop(pend_ref[e] + cnt_ref[e], pend_ref[e + 1], mark, padc)
        padc = lax.fori_loop(0, N_EXPERTS, per_expert, 0)
        lax.fori_loop(pend_ref[N_EXPERTS], nslots, mark, padc)

    base = i * INV_TILE

    def group(g, carry):
        for u in range(ISSUE_UNROLL):
            r = g * ISSUE_UNROLL + u
            inv_ref[d1_ref[r]] = base + r
            inv_ref[d2_ref[r]] = ntok + base + r
        return carry
    lax.fori_loop(0, INV_TILE // ISSUE_UNROLL, group, 0)


def _invert(pend0, counts, d1, d2, nslots):
    ntok = d1.shape[0]
    smem_tile = pl.BlockSpec((INV_TILE,), lambda i, *_: (i,), memory_space=pltpu.SMEM)
    return pl.pallas_call(
        functools.partial(_invert_kernel, ntok=ntok, nslots=nslots),
        grid_spec=pltpu.PrefetchScalarGridSpec(
            num_scalar_prefetch=2,
            grid=(ntok // INV_TILE,),
            in_specs=[smem_tile, smem_tile],
            out_specs=pl.BlockSpec((nslots,), lambda i, *_: (0,), memory_space=pltpu.SMEM),
        ),
        out_shape=jax.ShapeDtypeStruct((nslots,), jnp.int32),
        compiler_params=pltpu.CompilerParams(dimension_semantics=("arbitrary",)),
        name="invert",
    )(pend0, counts, d1, d2)


def _issue_range(lo, hi, body):
    g_lo = (lo + ISSUE_UNROLL - 1) // ISSUE_UNROLL
    g_hi = jnp.maximum(hi // ISSUE_UNROLL, g_lo)

    def single(r, carry):
        body(r, 0)
        return carry

    def group(g, carry):
        for u in range(ISSUE_UNROLL):
            body(g * ISSUE_UNROLL + u, u)
        return carry
    lax.fori_loop(lo, jnp.minimum(g_lo * ISSUE_UNROLL, hi), single, 0)
    lax.fori_loop(g_lo, g_hi, group, 0)
    lax.fori_loop(jnp.maximum(g_hi * ISSUE_UNROLL, lo), hi, single, 0)


def _expert_kernel(blk_e_ref, nblk_ref, split_ref, srcn_ref, src0_ref, dst_ref,
                   hp_hbm, hs_hbm, wg_ref, wu_ref, wd_ref, yt_hbm,
                   xbuf, ybuf, gsem, ssem, *, nblocks):
    del blk_e_ref
    i = pl.program_id(0)
    slot = i % 2
    blk_rows = MOE_BLK * ROW_TILES

    def gather(src_ref, split, buf, sem):
        def from_prompt(r, u):
            pltpu.make_async_copy(_tile(hp_hbm, src_ref[r]), _tile(buf, r), sem).start(priority=u % 2)

        def from_sample(r, u):
            pltpu.make_async_copy(_tile(hs_hbm, src_ref[r]), _tile(buf, r), sem).start(priority=u % 2)
        _issue_range(0, split, from_prompt)
        _issue_range(split, MOE_BLK, from_sample)

    def wait_block(buf, sem):
        pltpu.make_async_copy(yt_hbm.at[pl.ds(0, blk_rows)], buf, sem).wait()

    @pl.when(i == 0)
    def _():
        gather(src0_ref, split_ref[0], xbuf.at[0], gsem.at[0])

    wait_block(xbuf.at[slot], gsem.at[slot])

    @pl.when(i + 1 < nblocks)
    def _():
        gather(srcn_ref, split_ref[i + 1], xbuf.at[1 - slot], gsem.at[1 - slot])

    @pl.when(i >= 2)
    def _():
        wait_block(ybuf.at[slot], ssem.at[slot])

    live = i < nblk_ref[0]

    @pl.when(live)
    def _():
        xb = _load_tiles_as_rows(xbuf.at[slot], MOE_BLK).astype(BF16)
        g = _dot(xb, wg_ref[0])
        u = _dot(xb, wu_ref[0])
        a = (g * _sigmoid(g)) * u
        _store_rows_as_tiles(ybuf.at[slot], _dot(a.astype(BF16), wd_ref[0]))

    @pl.when(jnp.logical_not(live))
    def _():
        ybuf[slot] = jnp.zeros((blk_rows, LANES), F32)

    def to_token(r, u):
        pltpu.make_async_copy(_tile(ybuf.at[slot], r), _tile(yt_hbm, dst_ref[r]),
                              ssem.at[slot]).start(priority=u % 2)
    _issue_rows(MOE_BLK, to_token)

    @pl.when(i == nblocks - 1)
    def _():
        if nblocks > 1:
            wait_block(ybuf.at[1 - slot], ssem.at[1 - slot])
        wait_block(ybuf.at[slot], ssem.at[slot])


def _experts(blk_e, nblk, split, src, dst, h2p, h2s, wg, wu, wd):
    nslots = src.shape[0]
    nblocks = nslots // MOE_BLK
    blk_rows = MOE_BLK * ROW_TILES
    smem = lambda fn: pl.BlockSpec((MOE_BLK,), fn, memory_space=pltpu.SMEM)
    w_map = lambda i, be, nb, sp: (be[i], 0, 0)
    any_spec = pl.BlockSpec(memory_space=pl.ANY)
    return pl.pallas_call(
        functools.partial(_expert_kernel, nblocks=nblocks),
        grid_spec=pltpu.PrefetchScalarGridSpec(
            num_scalar_prefetch=3,
            grid=(nblocks,),
            in_specs=[smem(lambda i, *_: (jnp.minimum(i + 1, nblocks - 1),)),
                      smem(lambda i, *_: (0,)),
                      smem(lambda i, *_: (i,)),
                      any_spec, any_spec,
                      pl.BlockSpec((1, D_MODEL, D_EXPERT), w_map),
                      pl.BlockSpec((1, D_MODEL, D_EXPERT), w_map),
                      pl.BlockSpec((1, D_EXPERT, D_MODEL), w_map)],
            out_specs=any_spec,
            scratch_shapes=[pltpu.VMEM((2, blk_rows, LANES), F32),
                            pltpu.VMEM((2, blk_rows, LANES), F32),
                            pltpu.SemaphoreType.DMA((2,)), pltpu.SemaphoreType.DMA((2,))],
        ),
        out_shape=jax.ShapeDtypeStruct((nslots * ROW_TILES, LANES), F32),
        compiler_params=pltpu.CompilerParams(
            dimension_semantics=("arbitrary",), vmem_limit_bytes=VMEM_LIMIT),
        name="experts",
    )(blk_e, nblk, split, src, src, dst, h2p, h2s, wg, wu, wd)


def _combine_kernel(y1_ref, y2_ref, x1_ref, route_ref, gate_ref, gf_ref, o_ref):
    route = route_ref[0]
    w1 = route[:, 4:5]
    w2 = route[:, 5:6]
    ffn = w1 * _load_tiles_as_rows(y1_ref, TM) + w2 * _load_tiles_as_rows(y2_ref, TM)
    x2 = x1_ref[0] + gate_ref[0] * ffn
    o_ref[0] = _rms(x2, gf_ref[...])


def _combine(yt, x1, route, gate, gf, tok_base, ntok):
    nb, seq, _ = x1.shape
    nt = seq // TM
    blk0 = tok_base // TM
    plane = ntok // TM
    tile = lambda b, j: (b, j, 0)
    rows = TM * ROW_TILES
    grows = gate.shape[1]
    gate_spec = (pl.BlockSpec((1, 1, D_MODEL), lambda b, j: (b, 0, 0)) if grows == 1
                 else pl.BlockSpec((1, TM, D_MODEL), tile))
    return pl.pallas_call(
        _combine_kernel,
        grid=(nb, nt),
        in_specs=[pl.BlockSpec((rows, LANES), lambda b, j: (blk0 + b * nt + j, 0)),
                  pl.BlockSpec((rows, LANES), lambda b, j: (plane + blk0 + b * nt + j, 0)),
                  pl.BlockSpec((1, TM, D_MODEL), tile),
                  pl.BlockSpec((1, TM, RLANES), tile),
                  gate_spec,
                  pl.BlockSpec((1, D_MODEL), lambda b, j: (0, 0))],
        out_specs=pl.BlockSpec((1, TM, D_MODEL), tile),
        out_shape=jax.ShapeDtypeStruct(x1.shape, F32),
        compiler_params=pltpu.CompilerParams(
            dimension_semantics=("arbitrary", "arbitrary"), vmem_limit_bytes=VMEM_LIMIT),
        name="combine",
    )(yt, yt, x1, route, gate, gf)


def _band_bias(rel_bias, rows, keys):
    n = rows - 1 + keys
    dist = WINDOW + rows - 1 - np.arange(n + 1)
    flipped = rel_bias[:, np.clip(dist, -MAX_REL, MAX_REL) + MAX_REL]
    skew = jnp.tile(flipped, (1, rows))[:, :rows * n].reshape(N_HEADS, rows, n)
    b = skew[:, :, rows - 1:rows - 1 + keys]
    return b.reshape(2, 4 * rows, keys)


def kernel(x_prompt, x_sample, cache_attn_k, cache_attn_v, state_conv, c_prompt, c_sample,
           w_ada, b_ada, norm1_g, norm2_g, w_in, rel_bias, conv_w, conv_b, w_pa, w_pb, w_o,
           w_group, b_group, w_expert, b_expert, w_e_gate, w_e_up, w_e_down, final_g):
    assert w_ada.shape[0] == 1, "single trunk layer"
    nb, seq, _ = x_prompt.shape
    nseq, slen, _ = x_sample.shape
    ntok_p = nb * seq
    ntok_s = nseq * slen
    ntok = ntok_p + ntok_s
    assert seq % TL == 0 and WINDOW % TL == 0 and seq % TM == 0 and ntok_s % TM == 0
    assert slen >= 2 and slen & (slen - 1) == 0 and slen % 16 == 0
    assert ntok % INV_TILE == 0 and ntok % TM == 0 and (2 * ntok) % MOE_BLK == 0

    n_c = nb + nseq
    n_pad = -(-n_c // 8) * 8
    c_all = jnp.concatenate([c_prompt, c_sample, jnp.zeros((n_pad - n_c, D_MODEL), F32)], axis=0)
    mod = _ada(c_all, w_ada[0], b_ada[0]).reshape(n_pad, 6, D_MODEL)
    mod_p = mod[:nb]
    mod_s = mod[nb:n_c]

    win = w_in[0].astype(BF16)
    wpa = w_pa[0].astype(BF16)
    wpb = w_pb[0].astype(BF16)
    wo = w_o[0].astype(BF16)
    g1 = norm1_g[0].reshape(1, D_MODEL)
    g2 = norm2_g[0].reshape(1, D_MODEL)
    gf = final_g.reshape(1, D_MODEL)
    cw = jnp.concatenate([conv_w[0], jnp.zeros((8 - conv_w.shape[1], D_CONV), F32)], axis=0)
    cbias = conv_b[0].reshape(1, D_CONV)
    wr = jnp.concatenate([w_expert[0], w_group[0],
                          jnp.zeros((D_MODEL, RLANES - N_EXPERTS - N_GROUPS), F32)], axis=1)
    whi = wr.astype(BF16)
    wlo = (wr - whi.astype(F32)).astype(BF16)
    wcat = jnp.concatenate([whi, wlo], axis=1)
    br = jnp.concatenate([b_expert[0], b_group[0],
                          jnp.zeros((RLANES - N_EXPERTS - N_GROUPS,), F32)]).reshape(1, RLANES)
    bias_p = _band_bias(rel_bias[0], CHUNK, BAND)
    bias_s = _band_bias(rel_bias[0], slen, WINDOW + slen)

    x1p, h2p, kp, vp, up8, route_p, cnt_p = _prompt_main(
        x_prompt, mod_p, g1, g2, win, bias_p, cw, cbias, wpa, wpb, wo, wcat, whi, br)

    st = state_conv[0]
    up1 = jnp.zeros((nseq, slen, D_CONV), F32).at[:, 0].set(st[:, 1]).reshape(ntok_s, D_CONV)
    up2 = (jnp.zeros((nseq, slen, D_CONV), F32).at[:, 0].set(st[:, 0]).at[:, 1].set(st[:, 1])
           .reshape(ntok_s, D_CONV))
    ck = cache_attn_k[0].reshape(nseq, WINDOW, D_ATT)
    cv = cache_attn_v[0].reshape(nseq, WINDOW, D_ATT)
    x1s, h2s, ks, vs, us, route_s, cnt = _sample_main(
        x_sample.reshape(ntok_s, D_MODEL), mod_s, ck, cv, up1, up2, cnt_p,
        g1, g2, win, bias_s, cw, cbias, wpa, wpb, wo, wcat, whi, br, nseq, slen)

    route_all = jnp.concatenate([route_p.reshape(ntok_p, RLANES)[:, :4], route_s[:, :4]], axis=0)
    experts = route_all[:, 0:2].astype(jnp.int32)
    ranks = route_all[:, 2:4].astype(jnp.int32)
    counts = cnt[0, :N_EXPERTS].astype(jnp.int32)
    pcounts = (counts + MOE_BLK - 1) // MOE_BLK * MOE_BLK
    pend = jnp.cumsum(pcounts)
    pstart = pend - pcounts
    eids = jnp.arange(N_EXPERTS, dtype=jnp.int32)
    dest = jnp.sum(jnp.where(experts[..., None] == eids, pstart, 0), axis=-1) + ranks
    d1 = dest[:, 0]
    d2 = dest[:, 1]
    nblocks = (2 * ntok) // MOE_BLK + N_EXPERTS
    blk_start = jnp.arange(nblocks, dtype=jnp.int32) * MOE_BLK
    blk_e = jnp.minimum(jnp.sum((pend[None, :] <= blk_start[:, None]).astype(jnp.int32), axis=1),
                        N_EXPERTS - 1)
    nblk = (pend[-1:] // MOE_BLK).astype(jnp.int32)
    pend0 = jnp.concatenate([jnp.zeros((1,), jnp.int32), pend.astype(jnp.int32)])

    nslots = nblocks * MOE_BLK
    inv = _invert(pend0, counts, d1, d2, nslots)
    real = inv >= 0
    tok = jnp.where(inv >= ntok, inv - ntok, inv)
    from_prompt = real & (tok < ntok_p)
    src = jnp.where(from_prompt, tok, jnp.where(real, tok - ntok_p, 0))
    dst = jnp.where(real, inv, 2 * ntok - 1 - inv)
    split = jnp.sum(from_prompt.reshape(nblocks, MOE_BLK).astype(jnp.int32), axis=1)

    yt = _experts(blk_e, nblk, split, src, dst, h2p, h2s, w_e_gate[0].astype(BF16),
                  w_e_up[0].astype(BF16), w_e_down[0].astype(BF16))

    y_prompt = _combine(yt, x1p, route_p, mod_p[:, 5:6, :], gf, 0, ntok)
    gate_s = jnp.repeat(mod_s[:, 5, :], slen, axis=0).reshape(1, ntok_s, D_MODEL)
    y_sample = _combine(yt, x1s.reshape(1, ntok_s, D_MODEL),
                        route_s.reshape(1, ntok_s, RLANES), gate_s, gf, ntok_p, ntok)

    new_k_p = kp.reshape(1, nb, WINDOW, N_HEADS, HEAD_DIM)
    new_v_p = vp.reshape(1, nb, WINDOW, N_HEADS, HEAD_DIM)
    new_conv_p = up8[:, 6:8, :].reshape(1, nb, 2, D_CONV)
    new_k_s = ks.reshape(1, nseq, slen, N_HEADS, HEAD_DIM)
    new_v_s = vs.reshape(1, nseq, slen, N_HEADS, HEAD_DIM)
    new_conv_s = us.reshape(nseq, slen, D_CONV)[:, slen - 2:, :].reshape(1, nseq, 2, D_CONV)
    return (y_prompt, y_sample.reshape(nseq, slen, D_MODEL), new_k_p, new_v_p, new_conv_p,
            new_k_s, new_v_s, new_conv_s)
```

```python
import functools

import numpy as np
import jax
import jax.numpy as jnp
from jax import lax
from jax.experimental import pallas as pl
from jax.experimental.pallas import tpu as pltpu

F32 = jnp.float32
BF16 = jnp.bfloat16
I32 = jnp.int32

D_MODEL = 1024
CHUNK = 64
LEFT = 8
WINDOW = LEFT * CHUNK
BAND = WINDOW + CHUNK
N_HEADS = 8
HEAD_DIM = 64
D_ATT = N_HEADS * HEAD_DIM
QUAD = 256
MAX_REL = 128
D_CONV = 512
N_GROUPS = 4
EPG = 8
N_EXPERTS = 32
D_EXPERT = 512
EPS = 1e-6
NEG = -1e30

TL = 512
RING = WINDOW + TL
MOE_BLK = 256
TM = 512
RLANES = 128
LANES = 128
HALF = D_MODEL // 2
ROW_TILES = HALF // LANES
ISSUE_UNROLL = 8
VMEM_LIMIT = 56 * 1024 * 1024


def _const_spec(shape):
    nd = len(shape)
    return pl.BlockSpec(shape, lambda *_: (0,) * nd, pipeline_mode=pl.Buffered(1))


def _dot(a, b):
    return jnp.dot(a, b, preferred_element_type=F32)


def _sigmoid(x):
    return 1.0 / (1.0 + jnp.exp(-x))


def _rms(x, g):
    ms = jnp.mean(x * x, axis=-1, keepdims=True)
    return x * lax.rsqrt(ms + EPS) * g


def _pack_rows(val):
    lo = lax.bitcast_convert_type(val[:, :HALF], I32) + 0x8000
    hi = lax.bitcast_convert_type(val[:, HALF:], I32) + 0x8000
    return (hi & -65536) | lax.shift_right_logical(lo, 16)


def _unpack_rows(packed):
    lo = lax.bitcast_convert_type(lax.shift_left(packed, 16), F32)
    hi = lax.bitcast_convert_type(packed & -65536, F32)
    return lo, hi


def _store_rows_as_tiles(ref, packed):
    r = packed.shape[0]
    for c in range(ROW_TILES):
        ref[pl.ds(c, r, stride=ROW_TILES), :] = packed[:, c * LANES:(c + 1) * LANES]


def _load_tiles_as_rows(ref, r):
    return jnp.concatenate(
        [ref[pl.ds(c, r, stride=ROW_TILES), :] for c in range(ROW_TILES)], axis=1)


def _tile(ref, row):
    return ref.at[pl.ds(pl.multiple_of(row * ROW_TILES, ROW_TILES), ROW_TILES)]


def _ada_kernel(c_ref, w_ref, b_ref, o_ref):
    c = c_ref[...]
    s = c * _sigmoid(c)
    o_ref[...] = jnp.dot(s, w_ref[...], preferred_element_type=F32,
                         precision=lax.Precision.HIGHEST) + b_ref[...]


def _ada(c_all, w_ada, b_ada):
    n = c_all.shape[0]
    nb = 1024
    return pl.pallas_call(
        _ada_kernel,
        grid=(6 * D_MODEL // nb,),
        in_specs=[pl.BlockSpec((n, D_MODEL), lambda i: (0, 0)),
                  pl.BlockSpec((D_MODEL, nb), lambda i: (0, i)),
                  pl.BlockSpec((1, nb), lambda i: (0, i))],
        out_specs=pl.BlockSpec((n, nb), lambda i: (0, i)),
        out_shape=jax.ShapeDtypeStruct((n, 6 * D_MODEL), F32),
        name="ada",
    )(c_all, w_ada, b_ada.reshape(1, -1))


def _attend(q, kb, vb, bias, lim):
    r = q.shape[0]
    nk = kb.shape[0]
    assert r & (r - 1) == 0
    qt = jnp.concatenate([q] * 4, axis=0)
    rowh = lax.broadcasted_iota(jnp.int32, (4 * r, QUAD), 0) >> (r.bit_length() - 1)
    laneh = lax.broadcasted_iota(jnp.int32, (4 * r, QUAD), 1) >> 6
    qm = jnp.where(rowh == laneh, qt, jnp.zeros_like(qt))
    s = lax.dot_general(qm, kb, (((1,), (1,)), ((), ())), preferred_element_type=F32)
    s = s + bias
    if lim is not None:
        col = lax.broadcasted_iota(jnp.int32, (4 * r, nk), 1)
        s = jnp.where(col >= lim, s, NEG)
    m = jnp.max(s, axis=1, keepdims=True)
    p = jnp.exp(s - m)
    l = jnp.sum(p, axis=1, keepdims=True)
    o = _dot(p.astype(BF16), vb) * (1.0 / l)
    lane_o = lax.broadcasted_iota(jnp.int32, (r, QUAD), 1) >> 6
    out = o[0:r]
    for h in range(1, 4):
        out = jnp.where(lane_o == h, o[h * r:(h + 1) * r], out)
    return out


def _route(h2, wcat_ref, whi_ref, br_ref, cnt):
    r = h2.shape[0]
    hi = h2.astype(BF16)
    lo = (h2 - hi.astype(F32)).astype(BF16)
    z = _dot(hi, wcat_ref[...])
    logits = z[:, :RLANES] + z[:, RLANES:] + _dot(lo, whi_ref[...]) + br_ref[...]
    lane = lax.broadcasted_iota(jnp.int32, (r, RLANES), 1)
    lane_f = lane.astype(F32)
    big = jnp.float32(1000.0)

    lg = jnp.where((lane >= N_EXPERTS) & (lane < N_EXPERTS + N_GROUPS), logits, NEG)
    mg = jnp.max(lg, axis=1, keepdims=True)
    gi = jnp.min(jnp.where(lg == mg, lane_f, big), axis=1, keepdims=True) - N_EXPERTS
    pg = 1.0 / jnp.sum(jnp.exp(lg - mg), axis=1, keepdims=True)

    grp_of_lane = (lane >> 3).astype(F32)
    le = jnp.where((lane < N_EXPERTS) & (grp_of_lane == gi), logits, NEG)
    m1 = jnp.max(le, axis=1, keepdims=True)
    i1 = jnp.min(jnp.where(le == m1, lane_f, big), axis=1, keepdims=True)
    sel1 = lane_f == i1
    le2 = jnp.where(sel1, NEG, le)
    m2 = jnp.max(le2, axis=1, keepdims=True)
    i2 = jnp.min(jnp.where(le2 == m2, lane_f, big), axis=1, keepdims=True)
    sel2 = lane_f == i2
    rr = jnp.exp(m2 - m1)
    inv = pg / (1.0 + rr)
    w1 = inv
    w2 = inv * rr

    oh = jnp.where(sel1 | sel2, 1.0, 0.0).astype(F32)
    ri = lax.broadcasted_iota(jnp.int32, (r, r), 0)
    ci = lax.broadcasted_iota(jnp.int32, (r, r), 1)
    tri = jnp.where(ri > ci, 1.0, 0.0).astype(BF16)
    before = _dot(tri, oh.astype(BF16)) + cnt
    r1 = jnp.sum(jnp.where(sel1, before, 0.0), axis=1, keepdims=True)
    r2 = jnp.sum(jnp.where(sel2, before, 0.0), axis=1, keepdims=True)
    new_cnt = cnt + jnp.sum(oh, axis=0, keepdims=True)

    route = jnp.where(lane == 0, i1, 0.0)
    route = jnp.where(lane == 1, i2, route)
    route = jnp.where(lane == 2, r1, route)
    route = jnp.where(lane == 3, r2, route)
    route = jnp.where(lane == 4, w1, route)
    route = jnp.where(lane == 5, w2, route)
    return route, new_cnt


def _prompt_kernel(x_ref, mod_ref, g1_ref, g2_ref, win_ref, bias_ref, cw_ref, cbias_ref,
                   wpa_ref, wpb_ref, wo_ref, wcat_ref, whi_ref, br_ref,
                   x1_ref, h2_ref, ko_ref, vo_ref, uo_ref, route_ref, cnt_ref,
                   kring, vring, att_s, ucarry, cnt_s):
    b = pl.program_id(0)
    j = pl.program_id(1)

    @pl.when((b == 0) & (j == 0))
    def _():
        cnt_s[...] = jnp.zeros_like(cnt_s)

    @pl.when(j == 0)
    def _():
        kring[0:WINDOW, :] = jnp.zeros((WINDOW, D_ATT), BF16)
        vring[0:WINDOW, :] = jnp.zeros((WINDOW, D_ATT), BF16)
        ucarry[...] = jnp.zeros_like(ucarry)

    sh1 = mod_ref[0, 0:1, :]
    sc1 = mod_ref[0, 1:2, :]
    gt1 = mod_ref[0, 2:3, :]
    sh2 = mod_ref[0, 3:4, :]
    sc2 = mod_ref[0, 4:5, :]

    x = x_ref[0]
    h = _rms(x, g1_ref[...]) * (1.0 + sc1) + sh1
    hb = h.astype(BF16)

    qkv = _dot(hb, win_ref[:, 0:3 * D_ATT])
    q = (qkv[:, 0:D_ATT] * (HEAD_DIM ** -0.5)).astype(BF16)
    k = qkv[:, D_ATT:2 * D_ATT]
    v = qkv[:, 2 * D_ATT:3 * D_ATT]
    ko_ref[0] = k
    vo_ref[0] = v
    kring[WINDOW:RING, :] = k.astype(BF16)
    vring[WINDOW:RING, :] = v.astype(BF16)

    base = j * TL
    for c in range(TL // CHUNK):
        lim = WINDOW - (base + c * CHUNK)
        for qd in range(2):
            ls = slice(qd * QUAD, (qd + 1) * QUAD)
            o = _attend(q[c * CHUNK:(c + 1) * CHUNK, ls],
                        kring[c * CHUNK:c * CHUNK + BAND, ls],
                        vring[c * CHUNK:c * CHUNK + BAND, ls],
                        bias_ref[qd], lim)
            att_s[c * CHUNK:(c + 1) * CHUNK, ls] = o.astype(BF16)

    kring[0:WINDOW, :] = kring[TL:RING, :]
    vring[0:WINDOW, :] = vring[TL:RING, :]

    cbcv = _dot(hb, win_ref[:, 3 * D_ATT:3 * D_ATT + 3 * D_CONV])
    cb = cbcv[:, 0:D_CONV]
    u = cbcv[:, D_CONV:2 * D_CONV] * cbcv[:, 2 * D_CONV:3 * D_CONV]
    row = lax.broadcasted_iota(jnp.int32, (8, D_CONV), 0)
    prev = ucarry[...]
    r1 = pltpu.roll(u, 1, axis=0)
    r2 = pltpu.roll(u, 2, axis=0)
    u_m1 = jnp.concatenate(
        [jnp.where(row < 1, pltpu.roll(prev, 1, axis=0), r1[0:8]), r1[8:]], axis=0)
    u_m2 = jnp.concatenate(
        [jnp.where(row < 2, pltpu.roll(prev, 2, axis=0), r2[0:8]), r2[8:]], axis=0)
    yc = cw_ref[0:1, :] * u_m2 + cw_ref[1:2, :] * u_m1 + cw_ref[2:3, :] * u + cbias_ref[...]
    conv_out = (cb * yc).astype(BF16)
    ucarry[...] = u[TL - 8:TL, :]
    uo_ref[0] = u[TL - 8:TL, :]

    gates = _dot(hb, win_ref[:, 3 * D_ATT + 3 * D_CONV:])
    pa = _dot(att_s[...], wpa_ref[...])
    pb = _dot(conv_out, wpb_ref[...])
    mixin = _sigmoid(gates[:, 0:D_MODEL]) * pa + _sigmoid(gates[:, D_MODEL:]) * pb
    mix = _dot(mixin.astype(BF16), wo_ref[...])
    x1 = x + gt1 * mix
    x1_ref[0] = x1
    h2 = _rms(x1, g2_ref[...]) * (1.0 + sc2) + sh2
    _store_rows_as_tiles(h2_ref, _pack_rows(h2))

    route, new_cnt = _route(h2, wcat_ref, whi_ref, br_ref, cnt_s[...])
    route_ref[0] = route
    cnt_s[...] = new_cnt
    cnt_ref[...] = new_cnt


def _prompt_main(x, mod, g1, g2, win, bias_q, cw, cbias, wpa, wpb, wo, wcat, whi, br):
    nb, seq, _ = x.shape
    nt = seq // TL
    keep = WINDOW // TL
    tile = lambda b, j: (b, j, 0)
    last = lambda b, j: (b, jnp.maximum(j - (nt - keep), 0), 0)
    perb = lambda b, j: (b, 0, 0)
    in_specs = [
        pl.BlockSpec((1, TL, D_MODEL), tile),
        pl.BlockSpec((1, 6, D_MODEL), perb),
        _const_spec(g1.shape), _const_spec(g2.shape), _const_spec(win.shape),
        _const_spec(bias_q.shape), _const_spec(cw.shape), _const_spec(cbias.shape),
        _const_spec(wpa.shape), _const_spec(wpb.shape), _const_spec(wo.shape),
        _const_spec(wcat.shape), _const_spec(whi.shape), _const_spec(br.shape),
    ]
    out_specs = [
        pl.BlockSpec((1, TL, D_MODEL), tile),
        pl.BlockSpec((TL * ROW_TILES, LANES), lambda b, j: (b * nt + j, 0)),
        pl.BlockSpec((1, TL, D_ATT), last),
        pl.BlockSpec((1, TL, D_ATT), last),
        pl.BlockSpec((1, 8, D_CONV), perb),
        pl.BlockSpec((1, TL, RLANES), tile),
        pl.BlockSpec((1, RLANES), lambda b, j: (0, 0)),
    ]
    out_shape = [
        jax.ShapeDtypeStruct((nb, seq, D_MODEL), F32),
        jax.ShapeDtypeStruct((nb * seq * ROW_TILES, LANES), I32),
        jax.ShapeDtypeStruct((nb, WINDOW, D_ATT), F32),
        jax.ShapeDtypeStruct((nb, WINDOW, D_ATT), F32),
        jax.ShapeDtypeStruct((nb, 8, D_CONV), F32),
        jax.ShapeDtypeStruct((nb, seq, RLANES), F32),
        jax.ShapeDtypeStruct((1, RLANES), F32),
    ]
    scratch = [
        pltpu.VMEM((RING, D_ATT), BF16), pltpu.VMEM((RING, D_ATT), BF16),
        pltpu.VMEM((TL, D_ATT), BF16), pltpu.VMEM((8, D_CONV), F32),
        pltpu.VMEM((1, RLANES), F32),
    ]
    return pl.pallas_call(
        _prompt_kernel,
        grid=(nb, nt),
        in_specs=in_specs, out_specs=out_specs, out_shape=out_shape,
        scratch_shapes=scratch,
        compiler_params=pltpu.CompilerParams(
            dimension_semantics=("arbitrary", "arbitrary"), vmem_limit_bytes=VMEM_LIMIT),
        name="prompt_main",
    )(x, mod, g1, g2, win, bias_q, cw, cbias, wpa, wpb, wo, wcat, whi, br)


def _sample_kernel(x_ref, mod_ref, ck_ref, cv_ref, up1_ref, up2_ref, cnt_in_ref,
                   g1_ref, g2_ref, win_ref, bias_ref, cw_ref, cbias_ref,
                   wpa_ref, wpb_ref, wo_ref, wcat_ref, whi_ref, br_ref,
                   x1_ref, h2_ref, ko_ref, vo_ref, uo_ref, route_ref, cnt_ref,
                   h_s, q_s, kn_s, vn_s, kband, vband, att_s, conv_s, h2_s, *, nseq, slen):
    n = pl.program_id(0)
    ntok = nseq * slen

    @pl.when(n == 0)
    def _():
        def norm_body(i, carry):
            rows = pl.ds(pl.multiple_of(i * slen, slen), slen)
            xi = x_ref[rows, :]
            m = mod_ref[i]
            hi = _rms(xi, g1_ref[...]) * (1.0 + m[1:2, :]) + m[0:1, :]
            h_s[rows, :] = hi.astype(BF16)
            return carry
        lax.fori_loop(0, nseq, norm_body, 0)
        hb = h_s[...]
        qkv = _dot(hb, win_ref[:, 0:3 * D_ATT])
        q_s[...] = (qkv[:, 0:D_ATT] * (HEAD_DIM ** -0.5)).astype(BF16)
        k = qkv[:, D_ATT:2 * D_ATT]
        v = qkv[:, 2 * D_ATT:3 * D_ATT]
        ko_ref[...] = k
        vo_ref[...] = v
        kn_s[...] = k.astype(BF16)
        vn_s[...] = v.astype(BF16)

        cbcv = _dot(hb, win_ref[:, 3 * D_ATT:3 * D_ATT + 3 * D_CONV])
        cb = cbcv[:, 0:D_CONV]
        u = cbcv[:, D_CONV:2 * D_CONV] * cbcv[:, 2 * D_CONV:3 * D_CONV]
        pos = lax.broadcasted_iota(jnp.int32, (ntok, D_CONV), 0) & (slen - 1)
        u_m1 = jnp.where(pos < 1, up1_ref[...], pltpu.roll(u, 1, axis=0))
        u_m2 = jnp.where(pos < 2, up2_ref[...], pltpu.roll(u, 2, axis=0))
        yc = cw_ref[0:1, :] * u_m2 + cw_ref[1:2, :] * u_m1 + cw_ref[2:3, :] * u + cbias_ref[...]
        conv_s[...] = (cb * yc).astype(BF16)
        uo_ref[...] = u

    rows = pl.ds(pl.multiple_of(n * slen, slen), slen)
    kband[0:WINDOW, :] = ck_ref[0].astype(BF16)
    vband[0:WINDOW, :] = cv_ref[0].astype(BF16)
    kband[WINDOW:WINDOW + slen, :] = kn_s[rows, :]
    vband[WINDOW:WINDOW + slen, :] = vn_s[rows, :]
    qn = q_s[rows, :]
    outs = []
    for qd in range(2):
        ls = slice(qd * QUAD, (qd + 1) * QUAD)
        outs.append(_attend(qn[:, ls], kband[:, ls], vband[:, ls], bias_ref[qd], None))
    att_s[rows, :] = jnp.concatenate(outs, axis=1).astype(BF16)

    @pl.when(n == nseq - 1)
    def _():
        gates = _dot(h_s[...], win_ref[:, 3 * D_ATT + 3 * D_CONV:])
        pa = _dot(att_s[...], wpa_ref[...])
        pb = _dot(conv_s[...], wpb_ref[...])
        mixin = _sigmoid(gates[:, 0:D_MODEL]) * pa + _sigmoid(gates[:, D_MODEL:]) * pb
        x1_ref[...] = _dot(mixin.astype(BF16), wo_ref[...])

        def res_body(i, carry):
            r = pl.ds(pl.multiple_of(i * slen, slen), slen)
            m = mod_ref[i]
            x1 = x_ref[r, :] + m[2:3, :] * x1_ref[r, :]
            x1_ref[r, :] = x1
            h2_s[r, :] = _rms(x1, g2_ref[...]) * (1.0 + m[4:5, :]) + m[3:4, :]
            return carry
        lax.fori_loop(0, nseq, res_body, 0)

        h2 = h2_s[...]
        _store_rows_as_tiles(h2_ref, _pack_rows(h2))
        route, new_cnt = _route(h2, wcat_ref, whi_ref, br_ref, cnt_in_ref[...])
        route_ref[...] = route
        cnt_ref[...] = new_cnt


def _sample_main(x2d, mod, ck, cv, up1, up2, cnt_in, g1, g2, win, bias_s, cw, cbias,
                 wpa, wpb, wo, wcat, whi, br, nseq, slen):
    ntok = nseq * slen
    args = (x2d, mod, ck, cv, up1, up2, cnt_in, g1, g2, win, bias_s, cw, cbias,
            wpa, wpb, wo, wcat, whi, br)
    in_specs = []
    for idx, a in enumerate(args):
        if idx in (2, 3):
            in_specs.append(pl.BlockSpec((1, WINDOW, D_ATT), lambda n: (n, 0, 0)))
        else:
            in_specs.append(_const_spec(a.shape))
    whole = lambda shape: pl.BlockSpec(shape, lambda n: (0,) * len(shape))
    outs = [((ntok, D_MODEL), F32), ((ntok * ROW_TILES, LANES), I32), ((ntok, D_ATT), F32),
            ((ntok, D_ATT), F32), ((ntok, D_CONV), F32), ((ntok, RLANES), F32), ((1, RLANES), F32)]
    scratch = [
        pltpu.VMEM((ntok, D_MODEL), BF16), pltpu.VMEM((ntok, D_ATT), BF16),
        pltpu.VMEM((ntok, D_ATT), BF16), pltpu.VMEM((ntok, D_ATT), BF16),
        pltpu.VMEM((WINDOW + slen, D_ATT), BF16), pltpu.VMEM((WINDOW + slen, D_ATT), BF16),
        pltpu.VMEM((ntok, D_ATT), BF16), pltpu.VMEM((ntok, D_CONV), BF16),
        pltpu.VMEM((ntok, D_MODEL), F32),
    ]
    return pl.pallas_call(
        functools.partial(_sample_kernel, nseq=nseq, slen=slen),
        grid=(nseq,),
        in_specs=in_specs,
        out_specs=[whole(s) for s, _ in outs],
        out_shape=[jax.ShapeDtypeStruct(s, d) for s, d in outs],
        scratch_shapes=scratch,
        compiler_params=pltpu.CompilerParams(
            dimension_semantics=("arbitrary",), vmem_limit_bytes=VMEM_LIMIT),
        name="sample_main",
    )(*args)


def _issue_rows(n, body):
    def group(g, carry):
        for u in range(ISSUE_UNROLL):
            body(g * ISSUE_UNROLL + u, u)
        return carry
    lax.fori_loop(0, n // ISSUE_UNROLL, group, 0)


def _dispatch_kernel(pend_ref, d1_ref, d2_ref, hp_ref, hs_ref, xs_out, zbuf, sem, zsem, *,
                     np_tiles, nblocks):
    i = pl.program_id(0)
    blk_rows = MOE_BLK * ROW_TILES

    @pl.when(i == 0)
    def _():
        zbuf[...] = jnp.zeros_like(zbuf)

        def zcopy(e):
            start = pl.multiple_of((pend_ref[e + 1] - MOE_BLK) * ROW_TILES, blk_rows)
            return pltpu.make_async_copy(zbuf, xs_out.at[pl.ds(start, blk_rows)], zsem)

        def zstart(e, carry):
            @pl.when(pend_ref[e + 1] > pend_ref[e])
            def _():
                zcopy(e).start()
            return carry

        def zwait(e, carry):
            @pl.when(pend_ref[e + 1] > pend_ref[e])
            def _():
                zcopy(e).wait()
            return carry
        lax.fori_loop(0, N_EXPERTS, zstart, 0)
        lax.fori_loop(0, N_EXPERTS, zwait, 0)

        def tcopy(bk):
            start = pl.multiple_of(bk * blk_rows, blk_rows)
            return pltpu.make_async_copy(zbuf, xs_out.at[pl.ds(start, blk_rows)], zsem)

        def tstart(bk, carry):
            tcopy(bk).start()
            return carry

        def twait(bk, carry):
            tcopy(bk).wait()
            return carry
        used = pend_ref[N_EXPERTS] // MOE_BLK
        lax.fori_loop(used, nblocks, tstart, 0)
        lax.fori_loop(used, nblocks, twait, 0)

    def scatter_tile(src):
        def row(r, u):
            pltpu.make_async_copy(_tile(src, r), _tile(xs_out, d1_ref[r]), sem).start(priority=u % 2)
            pltpu.make_async_copy(_tile(src, r), _tile(xs_out, d2_ref[r]), sem).start(
                priority=(u + 1) % 2)
        _issue_rows(TM, row)
        for _ in range(2):
            pltpu.make_async_copy(src, xs_out.at[pl.ds(0, TM * ROW_TILES)], sem).wait()

    @pl.when(i < np_tiles)
    def _():
        scatter_tile(hp_ref)

    @pl.when(i >= np_tiles)
    def _():
        scatter_tile(hs_ref)


def _dispatch(pend, d1, d2, h2p, h2s, nslots):
    np_tiles = h2p.shape[0] // (TM * ROW_TILES)
    ns_tiles = h2s.shape[0] // (TM * ROW_TILES)
    smem_tile = pl.BlockSpec((TM,), lambda i, *_: (i,), memory_space=pltpu.SMEM)
    rows = TM * ROW_TILES
    return pl.pallas_call(
        functools.partial(_dispatch_kernel, np_tiles=np_tiles, nblocks=nslots // MOE_BLK),
        grid_spec=pltpu.PrefetchScalarGridSpec(
            num_scalar_prefetch=1,
            grid=(np_tiles + ns_tiles,),
            in_specs=[smem_tile, smem_tile,
                      pl.BlockSpec((rows, LANES), lambda i, *_: (jnp.minimum(i, np_tiles - 1), 0)),
                      pl.BlockSpec((rows, LANES), lambda i, *_: (jnp.maximum(i - np_tiles, 0), 0))],
            out_specs=pl.BlockSpec(memory_space=pl.ANY),
            scratch_shapes=[pltpu.VMEM((MOE_BLK * ROW_TILES, LANES), I32),
                            pltpu.SemaphoreType.DMA(()), pltpu.SemaphoreType.DMA(())],
        ),
        out_shape=jax.ShapeDtypeStruct((nslots * ROW_TILES, LANES), I32),
        compiler_params=pltpu.CompilerParams(dimension_semantics=("arbitrary",)),
        name="dispatch",
    )(pend, d1, d2, h2p, h2s)


def _expert_kernel(blk_e_ref, nblk_ref, xs_ref, wg_ref, wu_ref, wd_ref, y_ref):
    del blk_e_ref
    live = pl.program_id(0) < nblk_ref[0]

    @pl.when(live)
    def _():
        x_lo, x_hi = _unpack_rows(_load_tiles_as_rows(xs_ref, MOE_BLK))
        x_lo = x_lo.astype(BF16)
        x_hi = x_hi.astype(BF16)
        g = _dot(x_lo, wg_ref[0, 0:HALF, :]) + _dot(x_hi, wg_ref[0, HALF:, :])
        u = _dot(x_lo, wu_ref[0, 0:HALF, :]) + _dot(x_hi, wu_ref[0, HALF:, :])
        a = (g * _sigmoid(g)) * u
        _store_rows_as_tiles(y_ref, _pack_rows(_dot(a.astype(BF16), wd_ref[0])))

    @pl.when(jnp.logical_not(live))
    def _():
        y_ref[...] = jnp.zeros_like(y_ref)


def _experts(blk_e, nblk, xs, wg, wu, wd):
    blk_rows = MOE_BLK * ROW_TILES
    nblocks = xs.shape[0] // blk_rows
    row_map = lambda i, be, nb: (jnp.minimum(i, nb[0] - 1), 0)
    w_map = lambda i, be, nb: (be[i], 0, 0)
    return pl.pallas_call(
        _expert_kernel,
        grid_spec=pltpu.PrefetchScalarGridSpec(
            num_scalar_prefetch=2,
            grid=(nblocks,),
            in_specs=[pl.BlockSpec((blk_rows, LANES), row_map),
                      pl.BlockSpec((1, D_MODEL, D_EXPERT), w_map),
                      pl.BlockSpec((1, D_MODEL, D_EXPERT), w_map),
                      pl.BlockSpec((1, D_EXPERT, D_MODEL), w_map)],
            out_specs=pl.BlockSpec((blk_rows, LANES), lambda i, be, nb: (i, 0)),
        ),
        out_shape=jax.ShapeDtypeStruct(xs.shape, I32),
        compiler_params=pltpu.CompilerParams(
            dimension_semantics=("arbitrary",), vmem_limit_bytes=VMEM_LIMIT),
        name="experts",
    )(blk_e, nblk, xs, wg, wu, wd)


def _combine_kernel(d1_ref, d2_ref, y_hbm, x1_ref, route_ref, gate_ref, gf_ref, o_ref,
                    buf1, buf2, sem):
    def row(r, u):
        pltpu.make_async_copy(_tile(y_hbm, d1_ref[r]), _tile(buf1, r), sem).start(priority=u % 2)
        pltpu.make_async_copy(_tile(y_hbm, d2_ref[r]), _tile(buf2, r), sem).start(
            priority=(u + 1) % 2)
    _issue_rows(TM, row)
    for buf in (buf1, buf2):
        pltpu.make_async_copy(y_hbm.at[pl.ds(0, TM * ROW_TILES)], buf, sem).wait()
    route = route_ref[0]
    w1 = route[:, 4:5]
    w2 = route[:, 5:6]
    a_lo, a_hi = _unpack_rows(_load_tiles_as_rows(buf1, TM))
    b_lo, b_hi = _unpack_rows(_load_tiles_as_rows(buf2, TM))
    ffn = jnp.concatenate([w1 * a_lo + w2 * b_lo, w1 * a_hi + w2 * b_hi], axis=1)
    x2 = x1_ref[0] + gate_ref[0] * ffn
    o_ref[0] = _rms(x2, gf_ref[...])


def _combine(d1, d2, y, x1, route, gate, gf, tok_base):
    nb, seq, _ = x1.shape
    nt = seq // TM
    blk0 = tok_base // TM
    smem_tile = pl.BlockSpec((TM,), lambda b, j: (blk0 + b * nt + j,), memory_space=pltpu.SMEM)
    tile = lambda b, j: (b, j, 0)
    grows = gate.shape[1]
    gate_spec = (pl.BlockSpec((1, 1, D_MODEL), lambda b, j: (b, 0, 0)) if grows == 1
                 else pl.BlockSpec((1, TM, D_MODEL), tile))
    return pl.pallas_call(
        _combine_kernel,
        grid=(nb, nt),
        in_specs=[smem_tile, smem_tile, pl.BlockSpec(memory_space=pl.ANY),
                  pl.BlockSpec((1, TM, D_MODEL), tile),
                  pl.BlockSpec((1, TM, RLANES), tile),
                  gate_spec,
                  pl.BlockSpec((1, D_MODEL), lambda b, j: (0, 0))],
        out_specs=pl.BlockSpec((1, TM, D_MODEL), tile),
        out_shape=jax.ShapeDtypeStruct(x1.shape, F32),
        scratch_shapes=[pltpu.VMEM((TM * ROW_TILES, LANES), I32),
                        pltpu.VMEM((TM * ROW_TILES, LANES), I32),
                        pltpu.SemaphoreType.DMA(())],
        compiler_params=pltpu.CompilerParams(
            dimension_semantics=("arbitrary", "arbitrary"), vmem_limit_bytes=VMEM_LIMIT),
        name="combine",
    )(d1, d2, y, x1, route, gate, gf)


def _band_bias(rel_bias, rows, keys):
    n = rows - 1 + keys
    dist = WINDOW + rows - 1 - np.arange(n + 1)
    flipped = rel_bias[:, np.clip(dist, -MAX_REL, MAX_REL) + MAX_REL]
    skew = jnp.tile(flipped, (1, rows))[:, :rows * n].reshape(N_HEADS, rows, n)
    b = skew[:, :, rows - 1:rows - 1 + keys]
    return b.reshape(2, 4 * rows, keys)


def kernel(x_prompt, x_sample, cache_attn_k, cache_attn_v, state_conv, c_prompt, c_sample,
           w_ada, b_ada, norm1_g, norm2_g, w_in, rel_bias, conv_w, conv_b, w_pa, w_pb, w_o,
           w_group, b_group, w_expert, b_expert, w_e_gate, w_e_up, w_e_down, final_g):
    assert w_ada.shape[0] == 1, "single trunk layer"
    nb, seq, _ = x_prompt.shape
    nseq, slen, _ = x_sample.shape
    ntok_p = nb * seq
    ntok_s = nseq * slen
    ntok = ntok_p + ntok_s
    assert seq % TL == 0 and WINDOW % TL == 0 and seq % TM == 0 and ntok_s % TM == 0
    assert slen >= 2 and slen & (slen - 1) == 0 and slen % 16 == 0

    n_c = nb + nseq
    n_pad = -(-n_c // 8) * 8
    c_all = jnp.concatenate([c_prompt, c_sample, jnp.zeros((n_pad - n_c, D_MODEL), F32)], axis=0)
    mod = _ada(c_all, w_ada[0], b_ada[0]).reshape(n_pad, 6, D_MODEL)
    mod_p = mod[:nb]
    mod_s = mod[nb:n_c]

    win = w_in[0].astype(BF16)
    wpa = w_pa[0].astype(BF16)
    wpb = w_pb[0].astype(BF16)
    wo = w_o[0].astype(BF16)
    g1 = norm1_g[0].reshape(1, D_MODEL)
    g2 = norm2_g[0].reshape(1, D_MODEL)
    gf = final_g.reshape(1, D_MODEL)
    cw = jnp.concatenate([conv_w[0], jnp.zeros((8 - conv_w.shape[1], D_CONV), F32)], axis=0)
    cbias = conv_b[0].reshape(1, D_CONV)
    wr = jnp.concatenate([w_expert[0], w_group[0],
                          jnp.zeros((D_MODEL, RLANES - N_EXPERTS - N_GROUPS), F32)], axis=1)
    whi = wr.astype(BF16)
    wlo = (wr - whi.astype(F32)).astype(BF16)
    wcat = jnp.concatenate([whi, wlo], axis=1)
    br = jnp.concatenate([b_expert[0], b_group[0],
                          jnp.zeros((RLANES - N_EXPERTS - N_GROUPS,), F32)]).reshape(1, RLANES)
    bias_p = _band_bias(rel_bias[0], CHUNK, BAND)
    bias_s = _band_bias(rel_bias[0], slen, WINDOW + slen)

    x1p, h2p, kp, vp, up8, route_p, cnt_p = _prompt_main(
        x_prompt, mod_p, g1, g2, win, bias_p, cw, cbias, wpa, wpb, wo, wcat, whi, br)

    st = state_conv[0]
    up1 = jnp.zeros((nseq, slen, D_CONV), F32).at[:, 0].set(st[:, 1]).reshape(ntok_s, D_CONV)
    up2 = (jnp.zeros((nseq, slen, D_CONV), F32).at[:, 0].set(st[:, 0]).at[:, 1].set(st[:, 1])
           .reshape(ntok_s, D_CONV))
    ck = cache_attn_k[0].reshape(nseq, WINDOW, D_ATT)
    cv = cache_attn_v[0].reshape(nseq, WINDOW, D_ATT)
    x1s, h2s, ks, vs, us, route_s, cnt = _sample_main(
        x_sample.reshape(ntok_s, D_MODEL), mod_s, ck, cv, up1, up2, cnt_p,
        g1, g2, win, bias_s, cw, cbias, wpa, wpb, wo, wcat, whi, br, nseq, slen)

    route_all = jnp.concatenate([route_p.reshape(ntok_p, RLANES)[:, :4], route_s[:, :4]], axis=0)
    experts = route_all[:, 0:2].astype(jnp.int32)
    ranks = route_all[:, 2:4].astype(jnp.int32)
    counts = cnt[0, :N_EXPERTS].astype(jnp.int32)
    pcounts = (counts + MOE_BLK - 1) // MOE_BLK * MOE_BLK
    pend = jnp.cumsum(pcounts)
    pstart = pend - pcounts
    eids = jnp.arange(N_EXPERTS, dtype=jnp.int32)
    dest = jnp.sum(jnp.where(experts[..., None] == eids, pstart, 0), axis=-1) + ranks
    d1 = dest[:, 0]
    d2 = dest[:, 1]
    nblocks = (2 * ntok) // MOE_BLK + N_EXPERTS
    blk_start = jnp.arange(nblocks, dtype=jnp.int32) * MOE_BLK
    blk_e = jnp.minimum(jnp.sum((pend[None, :] <= blk_start[:, None]).astype(jnp.int32), axis=1),
                        N_EXPERTS - 1)
    nblk = (pend[-1:] // MOE_BLK).astype(jnp.int32)
    pend0 = jnp.concatenate([jnp.zeros((1,), jnp.int32), pend.astype(jnp.int32)])

    xs = _dispatch(pend0, d1, d2, h2p, h2s, nblocks * MOE_BLK)
    y = _experts(blk_e, nblk, xs, w_e_gate[0].astype(BF16), w_e_up[0].astype(BF16),
                 w_e_down[0].astype(BF16))

    y_prompt = _combine(d1, d2, y, x1p, route_p, mod_p[:, 5:6, :], gf, 0)
    gate_s = jnp.repeat(mod_s[:, 5, :], slen, axis=0).reshape(1, ntok_s, D_MODEL)
    y_sample = _combine(d1, d2, y, x1s.reshape(1, ntok_s, D_MODEL),
                        route_s.reshape(1, ntok_s, RLANES), gate_s, gf, ntok_p)

    new_k_p = kp.reshape(1, nb, WINDOW, N_HEADS, HEAD_DIM)
    new_v_p = vp.reshape(1, nb, WINDOW, N_HEADS, HEAD_DIM)
    new_conv_p = up8[:, 6:8, :].reshape(1, nb, 2, D_CONV)
    new_k_s = ks.reshape(1, nseq, slen, N_HEADS, HEAD_DIM)
    new_v_s = vs.reshape(1, nseq, slen, N_HEADS, HEAD_DIM)
    new_conv_s = us.reshape(nseq, slen, D_CONV)[:, slen - 2:, :].reshape(1, nseq, 2, D_CONV)
    return (y_prompt, y_sample.reshape(nseq, slen, D_MODEL), new_k_p, new_v_p, new_conv_p,
            new_k_s, new_v_s, new_conv_s)
```

```python
import functools

import numpy as np
import jax
import jax.numpy as jnp
from jax import lax
from jax.experimental import pallas as pl
from jax.experimental.pallas import tpu as pltpu

F32 = jnp.float32
BF16 = jnp.bfloat16
I32 = jnp.int32

D_MODEL = 1024
CHUNK = 64
LEFT = 8
WINDOW = LEFT * CHUNK
BAND = WINDOW + CHUNK
N_HEADS = 8
HEAD_DIM = 64
D_ATT = N_HEADS * HEAD_DIM
QUAD = 256
MAX_REL = 128
D_CONV = 512
N_GROUPS = 4
EPG = 8
N_EXPERTS = 32
D_EXPERT = 512
EPS = 1e-6
NEG = -1e30

TL = 512
RING = WINDOW + TL
MOE_BLK = 512
TM = 512
RLANES = 128
LANES = 128
HALF = D_MODEL // 2
ROW_TILES = HALF // LANES
ISSUE_UNROLL = 8
VMEM_LIMIT = 56 * 1024 * 1024


def _const_spec(shape):
    nd = len(shape)
    return pl.BlockSpec(shape, lambda *_: (0,) * nd, pipeline_mode=pl.Buffered(1))


def _dot(a, b):
    return jnp.dot(a, b, preferred_element_type=F32)


def _sigmoid(x):
    return 1.0 / (1.0 + jnp.exp(-x))


def _rms(x, g):
    ms = jnp.mean(x * x, axis=-1, keepdims=True)
    return x * lax.rsqrt(ms + EPS) * g


def _pack_rows(val):
    lo = lax.bitcast_convert_type(val[:, :HALF], I32) + 0x8000
    hi = lax.bitcast_convert_type(val[:, HALF:], I32) + 0x8000
    return (hi & -65536) | lax.shift_right_logical(lo, 16)


def _unpack_rows(packed):
    lo = lax.bitcast_convert_type(lax.shift_left(packed, 16), F32)
    hi = lax.bitcast_convert_type(packed & -65536, F32)
    return lo, hi


def _store_rows_as_tiles(ref, packed):
    r = packed.shape[0]
    for c in range(ROW_TILES):
        ref[pl.ds(c, r, stride=ROW_TILES), :] = packed[:, c * LANES:(c + 1) * LANES]


def _load_tiles_as_rows(ref, r):
    return jnp.concatenate(
        [ref[pl.ds(c, r, stride=ROW_TILES), :] for c in range(ROW_TILES)], axis=1)


def _tile(ref, row):
    return ref.at[pl.ds(pl.multiple_of(row * ROW_TILES, ROW_TILES), ROW_TILES)]


def _ada_kernel(c_ref, w_ref, b_ref, o_ref):
    c = c_ref[...]
    s = c * _sigmoid(c)
    o_ref[...] = jnp.dot(s, w_ref[...], preferred_element_type=F32,
                         precision=lax.Precision.HIGHEST) + b_ref[...]


def _ada(c_all, w_ada, b_ada):
    n = c_all.shape[0]
    nb = 1024
    return pl.pallas_call(
        _ada_kernel,
        grid=(6 * D_MODEL // nb,),
        in_specs=[pl.BlockSpec((n, D_MODEL), lambda i: (0, 0)),
                  pl.BlockSpec((D_MODEL, nb), lambda i: (0, i)),
                  pl.BlockSpec((1, nb), lambda i: (0, i))],
        out_specs=pl.BlockSpec((n, nb), lambda i: (0, i)),
        out_shape=jax.ShapeDtypeStruct((n, 6 * D_MODEL), F32),
        name="ada",
    )(c_all, w_ada, b_ada.reshape(1, -1))


def _attend(q, kb, vb, bias, lim):
    r = q.shape[0]
    nk = kb.shape[0]
    assert r & (r - 1) == 0
    qt = jnp.concatenate([q] * 4, axis=0)
    rowh = lax.broadcasted_iota(jnp.int32, (4 * r, QUAD), 0) >> (r.bit_length() - 1)
    laneh = lax.broadcasted_iota(jnp.int32, (4 * r, QUAD), 1) >> 6
    qm = jnp.where(rowh == laneh, qt, jnp.zeros_like(qt))
    s = lax.dot_general(qm, kb, (((1,), (1,)), ((), ())), preferred_element_type=F32)
    s = s + bias
    if lim is not None:
        col = lax.broadcasted_iota(jnp.int32, (4 * r, nk), 1)
        s = jnp.where(col >= lim, s, NEG)
    m = jnp.max(s, axis=1, keepdims=True)
    p = jnp.exp(s - m)
    l = jnp.sum(p, axis=1, keepdims=True)
    o = _dot(p.astype(BF16), vb) * (1.0 / l)
    lane_o = lax.broadcasted_iota(jnp.int32, (r, QUAD), 1) >> 6
    out = o[0:r]
    for h in range(1, 4):
        out = jnp.where(lane_o == h, o[h * r:(h + 1) * r], out)
    return out


def _route(h2, wcat_ref, whi_ref, br_ref, cnt):
    r = h2.shape[0]
    hi = h2.astype(BF16)
    lo = (h2 - hi.astype(F32)).astype(BF16)
    z = _dot(hi, wcat_ref[...])
    logits = z[:, :RLANES] + z[:, RLANES:] + _dot(lo, whi_ref[...]) + br_ref[...]
    lane = lax.broadcasted_iota(jnp.int32, (r, RLANES), 1)
    lane_f = lane.astype(F32)
    big = jnp.float32(1000.0)

    lg = jnp.where((lane >= N_EXPERTS) & (lane < N_EXPERTS + N_GROUPS), logits, NEG)
    mg = jnp.max(lg, axis=1, keepdims=True)
    gi = jnp.min(jnp.where(lg == mg, lane_f, big), axis=1, keepdims=True) - N_EXPERTS
    pg = 1.0 / jnp.sum(jnp.exp(lg - mg), axis=1, keepdims=True)

    grp_of_lane = (lane >> 3).astype(F32)
    le = jnp.where((lane < N_EXPERTS) & (grp_of_lane == gi), logits, NEG)
    m1 = jnp.max(le, axis=1, keepdims=True)
    i1 = jnp.min(jnp.where(le == m1, lane_f, big), axis=1, keepdims=True)
    sel1 = lane_f == i1
    le2 = jnp.where(sel1, NEG, le)
    m2 = jnp.max(le2, axis=1, keepdims=True)
    i2 = jnp.min(jnp.where(le2 == m2, lane_f, big), axis=1, keepdims=True)
    sel2 = lane_f == i2
    rr = jnp.exp(m2 - m1)
    inv = pg / (1.0 + rr)
    w1 = inv
    w2 = inv * rr

    oh = jnp.where(sel1 | sel2, 1.0, 0.0).astype(F32)
    ri = lax.broadcasted_iota(jnp.int32, (r, r), 0)
    ci = lax.broadcasted_iota(jnp.int32, (r, r), 1)
    tri = jnp.where(ri > ci, 1.0, 0.0).astype(BF16)
    before = _dot(tri, oh.astype(BF16)) + cnt
    r1 = jnp.sum(jnp.where(sel1, before, 0.0), axis=1, keepdims=True)
    r2 = jnp.sum(jnp.where(sel2, before, 0.0), axis=1, keepdims=True)
    new_cnt = cnt + jnp.sum(oh, axis=0, keepdims=True)

    route = jnp.where(lane == 0, i1, 0.0)
    route = jnp.where(lane == 1, i2, route)
    route = jnp.where(lane == 2, r1, route)
    route = jnp.where(lane == 3, r2, route)
    route = jnp.where(lane == 4, w1, route)
    route = jnp.where(lane == 5, w2, route)
    return route, new_cnt


def _prompt_kernel(x_ref, mod_ref, g1_ref, g2_ref, win_ref, bias_ref, cw_ref, cbias_ref,
                   wpa_ref, wpb_ref, wo_ref, wcat_ref, whi_ref, br_ref,
                   x1_ref, h2_ref, ko_ref, vo_ref, uo_ref, route_ref, cnt_ref,
                   kring, vring, att_s, ucarry, cnt_s):
    b = pl.program_id(0)
    j = pl.program_id(1)

    @pl.when((b == 0) & (j == 0))
    def _():
        cnt_s[...] = jnp.zeros_like(cnt_s)

    @pl.when(j == 0)
    def _():
        kring[0:WINDOW, :] = jnp.zeros((WINDOW, D_ATT), BF16)
        vring[0:WINDOW, :] = jnp.zeros((WINDOW, D_ATT), BF16)
        ucarry[...] = jnp.zeros_like(ucarry)

    sh1 = mod_ref[0, 0:1, :]
    sc1 = mod_ref[0, 1:2, :]
    gt1 = mod_ref[0, 2:3, :]
    sh2 = mod_ref[0, 3:4, :]
    sc2 = mod_ref[0, 4:5, :]

    x = x_ref[0]
    h = _rms(x, g1_ref[...]) * (1.0 + sc1) + sh1
    hb = h.astype(BF16)

    qkv = _dot(hb, win_ref[:, 0:3 * D_ATT])
    q = (qkv[:, 0:D_ATT] * (HEAD_DIM ** -0.5)).astype(BF16)
    k = qkv[:, D_ATT:2 * D_ATT]
    v = qkv[:, 2 * D_ATT:3 * D_ATT]
    ko_ref[0] = k
    vo_ref[0] = v
    kring[WINDOW:RING, :] = k.astype(BF16)
    vring[WINDOW:RING, :] = v.astype(BF16)

    base = j * TL
    for c in range(TL // CHUNK):
        lim = WINDOW - (base + c * CHUNK)
        for qd in range(2):
            ls = slice(qd * QUAD, (qd + 1) * QUAD)
            o = _attend(q[c * CHUNK:(c + 1) * CHUNK, ls],
                        kring[c * CHUNK:c * CHUNK + BAND, ls],
                        vring[c * CHUNK:c * CHUNK + BAND, ls],
                        bias_ref[qd], lim)
            att_s[c * CHUNK:(c + 1) * CHUNK, ls] = o.astype(BF16)

    kring[0:WINDOW, :] = kring[TL:RING, :]
    vring[0:WINDOW, :] = vring[TL:RING, :]

    cbcv = _dot(hb, win_ref[:, 3 * D_ATT:3 * D_ATT + 3 * D_CONV])
    cb = cbcv[:, 0:D_CONV]
    u = cbcv[:, D_CONV:2 * D_CONV] * cbcv[:, 2 * D_CONV:3 * D_CONV]
    row = lax.broadcasted_iota(jnp.int32, (8, D_CONV), 0)
    prev = ucarry[...]
    r1 = pltpu.roll(u, 1, axis=0)
    r2 = pltpu.roll(u, 2, axis=0)
    u_m1 = jnp.concatenate(
        [jnp.where(row < 1, pltpu.roll(prev, 1, axis=0), r1[0:8]), r1[8:]], axis=0)
    u_m2 = jnp.concatenate(
        [jnp.where(row < 2, pltpu.roll(prev, 2, axis=0), r2[0:8]), r2[8:]], axis=0)
    yc = cw_ref[0:1, :] * u_m2 + cw_ref[1:2, :] * u_m1 + cw_ref[2:3, :] * u + cbias_ref[...]
    conv_out = (cb * yc).astype(BF16)
    ucarry[...] = u[TL - 8:TL, :]
    uo_ref[0] = u[TL - 8:TL, :]

    gates = _dot(hb, win_ref[:, 3 * D_ATT + 3 * D_CONV:])
    pa = _dot(att_s[...], wpa_ref[...])
    pb = _dot(conv_out, wpb_ref[...])
    mixin = _sigmoid(gates[:, 0:D_MODEL]) * pa + _sigmoid(gates[:, D_MODEL:]) * pb
    mix = _dot(mixin.astype(BF16), wo_ref[...])
    x1 = x + gt1 * mix
    x1_ref[0] = x1
    h2 = _rms(x1, g2_ref[...]) * (1.0 + sc2) + sh2
    _store_rows_as_tiles(h2_ref, _pack_rows(h2))

    route, new_cnt = _route(h2, wcat_ref, whi_ref, br_ref, cnt_s[...])
    route_ref[0] = route
    cnt_s[...] = new_cnt
    cnt_ref[...] = new_cnt


def _prompt_main(x, mod, g1, g2, win, bias_q, cw, cbias, wpa, wpb, wo, wcat, whi, br):
    nb, seq, _ = x.shape
    nt = seq // TL
    keep = WINDOW // TL
    tile = lambda b, j: (b, j, 0)
    last = lambda b, j: (b, jnp.maximum(j - (nt - keep), 0), 0)
    perb = lambda b, j: (b, 0, 0)
    in_specs = [
        pl.BlockSpec((1, TL, D_MODEL), tile),
        pl.BlockSpec((1, 6, D_MODEL), perb),
        _const_spec(g1.shape), _const_spec(g2.shape), _const_spec(win.shape),
        _const_spec(bias_q.shape), _const_spec(cw.shape), _const_spec(cbias.shape),
        _const_spec(wpa.shape), _const_spec(wpb.shape), _const_spec(wo.shape),
        _const_spec(wcat.shape), _const_spec(whi.shape), _const_spec(br.shape),
    ]
    out_specs = [
        pl.BlockSpec((1, TL, D_MODEL), tile),
        pl.BlockSpec((TL * ROW_TILES, LANES), lambda b, j: (b * nt + j, 0)),
        pl.BlockSpec((1, TL, D_ATT), last),
        pl.BlockSpec((1, TL, D_ATT), last),
        pl.BlockSpec((1, 8, D_CONV), perb),
        pl.BlockSpec((1, TL, RLANES), tile),
        pl.BlockSpec((1, RLANES), lambda b, j: (0, 0)),
    ]
    out_shape = [
        jax.ShapeDtypeStruct((nb, seq, D_MODEL), F32),
        jax.ShapeDtypeStruct((nb * seq * ROW_TILES, LANES), I32),
        jax.ShapeDtypeStruct((nb, WINDOW, D_ATT), F32),
        jax.ShapeDtypeStruct((nb, WINDOW, D_ATT), F32),
        jax.ShapeDtypeStruct((nb, 8, D_CONV), F32),
        jax.ShapeDtypeStruct((nb, seq, RLANES), F32),
        jax.ShapeDtypeStruct((1, RLANES), F32),
    ]
    scratch = [
        pltpu.VMEM((RING, D_ATT), BF16), pltpu.VMEM((RING, D_ATT), BF16),
        pltpu.VMEM((TL, D_ATT), BF16), pltpu.VMEM((8, D_CONV), F32),
        pltpu.VMEM((1, RLANES), F32),
    ]
    return pl.pallas_call(
        _prompt_kernel,
        grid=(nb, nt),
        in_specs=in_specs, out_specs=out_specs, out_shape=out_shape,
        scratch_shapes=scratch,
        compiler_params=pltpu.CompilerParams(
            dimension_semantics=("arbitrary", "arbitrary"), vmem_limit_bytes=VMEM_LIMIT),
        name="prompt_main",
    )(x, mod, g1, g2, win, bias_q, cw, cbias, wpa, wpb, wo, wcat, whi, br)


def _sample_kernel(x_ref, mod_ref, ck_ref, cv_ref, up1_ref, up2_ref, cnt_in_ref,
                   g1_ref, g2_ref, win_ref, bias_ref, cw_ref, cbias_ref,
                   wpa_ref, wpb_ref, wo_ref, wcat_ref, whi_ref, br_ref,
                   x1_ref, h2_ref, ko_ref, vo_ref, uo_ref, route_ref, cnt_ref,
                   h_s, q_s, kn_s, vn_s, kband, vband, att_s, conv_s, h2_s, *, nseq, slen):
    n = pl.program_id(0)
    ntok = nseq * slen

    @pl.when(n == 0)
    def _():
        def norm_body(i, carry):
            rows = pl.ds(pl.multiple_of(i * slen, slen), slen)
            xi = x_ref[rows, :]
            m = mod_ref[i]
            hi = _rms(xi, g1_ref[...]) * (1.0 + m[1:2, :]) + m[0:1, :]
            h_s[rows, :] = hi.astype(BF16)
            return carry
        lax.fori_loop(0, nseq, norm_body, 0)
        hb = h_s[...]
        qkv = _dot(hb, win_ref[:, 0:3 * D_ATT])
        q_s[...] = (qkv[:, 0:D_ATT] * (HEAD_DIM ** -0.5)).astype(BF16)
        k = qkv[:, D_ATT:2 * D_ATT]
        v = qkv[:, 2 * D_ATT:3 * D_ATT]
        ko_ref[...] = k
        vo_ref[...] = v
        kn_s[...] = k.astype(BF16)
        vn_s[...] = v.astype(BF16)

        cbcv = _dot(hb, win_ref[:, 3 * D_ATT:3 * D_ATT + 3 * D_CONV])
        cb = cbcv[:, 0:D_CONV]
        u = cbcv[:, D_CONV:2 * D_CONV] * cbcv[:, 2 * D_CONV:3 * D_CONV]
        pos = lax.broadcasted_iota(jnp.int32, (ntok, D_CONV), 0) & (slen - 1)
        u_m1 = jnp.where(pos < 1, up1_ref[...], pltpu.roll(u, 1, axis=0))
        u_m2 = jnp.where(pos < 2, up2_ref[...], pltpu.roll(u, 2, axis=0))
        yc = cw_ref[0:1, :] * u_m2 + cw_ref[1:2, :] * u_m1 + cw_ref[2:3, :] * u + cbias_ref[...]
        conv_s[...] = (cb * yc).astype(BF16)
        uo_ref[...] = u

    rows = pl.ds(pl.multiple_of(n * slen, slen), slen)
    kband[0:WINDOW, :] = ck_ref[0].astype(BF16)
    vband[0:WINDOW, :] = cv_ref[0].astype(BF16)
    kband[WINDOW:WINDOW + slen, :] = kn_s[rows, :]
    vband[WINDOW:WINDOW + slen, :] = vn_s[rows, :]
    qn = q_s[rows, :]
    outs = []
    for qd in range(2):
        ls = slice(qd * QUAD, (qd + 1) * QUAD)
        outs.append(_attend(qn[:, ls], kband[:, ls], vband[:, ls], bias_ref[qd], None))
    att_s[rows, :] = jnp.concatenate(outs, axis=1).astype(BF16)

    @pl.when(n == nseq - 1)
    def _():
        gates = _dot(h_s[...], win_ref[:, 3 * D_ATT + 3 * D_CONV:])
        pa = _dot(att_s[...], wpa_ref[...])
        pb = _dot(conv_s[...], wpb_ref[...])
        mixin = _sigmoid(gates[:, 0:D_MODEL]) * pa + _sigmoid(gates[:, D_MODEL:]) * pb
        x1_ref[...] = _dot(mixin.astype(BF16), wo_ref[...])

        def res_body(i, carry):
            r = pl.ds(pl.multiple_of(i * slen, slen), slen)
            m = mod_ref[i]
            x1 = x_ref[r, :] + m[2:3, :] * x1_ref[r, :]
            x1_ref[r, :] = x1
            h2_s[r, :] = _rms(x1, g2_ref[...]) * (1.0 + m[4:5, :]) + m[3:4, :]
            return carry
        lax.fori_loop(0, nseq, res_body, 0)

        h2 = h2_s[...]
        _store_rows_as_tiles(h2_ref, _pack_rows(h2))
        route, new_cnt = _route(h2, wcat_ref, whi_ref, br_ref, cnt_in_ref[...])
        route_ref[...] = route
        cnt_ref[...] = new_cnt


def _sample_main(x2d, mod, ck, cv, up1, up2, cnt_in, g1, g2, win, bias_s, cw, cbias,
                 wpa, wpb, wo, wcat, whi, br, nseq, slen):
    ntok = nseq * slen
    args = (x2d, mod, ck, cv, up1, up2, cnt_in, g1, g2, win, bias_s, cw, cbias,
            wpa, wpb, wo, wcat, whi, br)
    in_specs = []
    for idx, a in enumerate(args):
        if idx in (2, 3):
            in_specs.append(pl.BlockSpec((1, WINDOW, D_ATT), lambda n: (n, 0, 0)))
        else:
            in_specs.append(_const_spec(a.shape))
    whole = lambda shape: pl.BlockSpec(shape, lambda n: (0,) * len(shape))
    outs = [((ntok, D_MODEL), F32), ((ntok * ROW_TILES, LANES), I32), ((ntok, D_ATT), F32),
            ((ntok, D_ATT), F32), ((ntok, D_CONV), F32), ((ntok, RLANES), F32), ((1, RLANES), F32)]
    scratch = [
        pltpu.VMEM((ntok, D_MODEL), BF16), pltpu.VMEM((ntok, D_ATT), BF16),
        pltpu.VMEM((ntok, D_ATT), BF16), pltpu.VMEM((ntok, D_ATT), BF16),
        pltpu.VMEM((WINDOW + slen, D_ATT), BF16), pltpu.VMEM((WINDOW + slen, D_ATT), BF16),
        pltpu.VMEM((ntok, D_ATT), BF16), pltpu.VMEM((ntok, D_CONV), BF16),
        pltpu.VMEM((ntok, D_MODEL), F32),
    ]
    return pl.pallas_call(
        functools.partial(_sample_kernel, nseq=nseq, slen=slen),
        grid=(nseq,),
        in_specs=in_specs,
        out_specs=[whole(s) for s, _ in outs],
        out_shape=[jax.ShapeDtypeStruct(s, d) for s, d in outs],
        scratch_shapes=scratch,
        compiler_params=pltpu.CompilerParams(
            dimension_semantics=("arbitrary",), vmem_limit_bytes=VMEM_LIMIT),
        name="sample_main",
    )(*args)


def _issue_rows(n, body):
    def group(g, carry):
        for u in range(ISSUE_UNROLL):
            body(g * ISSUE_UNROLL + u, u)
        return carry
    lax.fori_loop(0, n // ISSUE_UNROLL, group, 0)


def _dispatch_kernel(pend_ref, d1_ref, d2_ref, hp_ref, hs_ref, xs_out, zbuf, sem, zsem, *,
                     np_tiles, nblocks):
    i = pl.program_id(0)
    blk_rows = MOE_BLK * ROW_TILES

    @pl.when(i == 0)
    def _():
        zbuf[...] = jnp.zeros_like(zbuf)

        def zcopy(e):
            start = pl.multiple_of((pend_ref[e + 1] - MOE_BLK) * ROW_TILES, blk_rows)
            return pltpu.make_async_copy(zbuf, xs_out.at[pl.ds(start, blk_rows)], zsem)

        def zstart(e, carry):
            @pl.when(pend_ref[e + 1] > pend_ref[e])
            def _():
                zcopy(e).start()
            return carry

        def zwait(e, carry):
            @pl.when(pend_ref[e + 1] > pend_ref[e])
            def _():
                zcopy(e).wait()
            return carry
        lax.fori_loop(0, N_EXPERTS, zstart, 0)
        lax.fori_loop(0, N_EXPERTS, zwait, 0)

        def tcopy(bk):
            start = pl.multiple_of(bk * blk_rows, blk_rows)
            return pltpu.make_async_copy(zbuf, xs_out.at[pl.ds(start, blk_rows)], zsem)

        def tstart(bk, carry):
            tcopy(bk).start()
            return carry

        def twait(bk, carry):
            tcopy(bk).wait()
            return carry
        used = pend_ref[N_EXPERTS] // MOE_BLK
        lax.fori_loop(used, nblocks, tstart, 0)
        lax.fori_loop(used, nblocks, twait, 0)

    def scatter_tile(src):
        def row(r, u):
            pltpu.make_async_copy(_tile(src, r), _tile(xs_out, d1_ref[r]), sem).start(priority=u % 2)
            pltpu.make_async_copy(_tile(src, r), _tile(xs_out, d2_ref[r]), sem).start(
                priority=(u + 1) % 2)
        _issue_rows(TM, row)
        for _ in range(2):
            pltpu.make_async_copy(src, xs_out.at[pl.ds(0, TM * ROW_TILES)], sem).wait()

    @pl.when(i < np_tiles)
    def _():
        scatter_tile(hp_ref)

    @pl.when(i >= np_tiles)
    def _():
        scatter_tile(hs_ref)


def _dispatch(pend, d1, d2, h2p, h2s, nslots):
    np_tiles = h2p.shape[0] // (TM * ROW_TILES)
    ns_tiles = h2s.shape[0] // (TM * ROW_TILES)
    smem_tile = pl.BlockSpec((TM,), lambda i, *_: (i,), memory_space=pltpu.SMEM)
    rows = TM * ROW_TILES
    return pl.pallas_call(
        functools.partial(_dispatch_kernel, np_tiles=np_tiles, nblocks=nslots // MOE_BLK),
        grid_spec=pltpu.PrefetchScalarGridSpec(
            num_scalar_prefetch=1,
            grid=(np_tiles + ns_tiles,),
            in_specs=[smem_tile, smem_tile,
                      pl.BlockSpec((rows, LANES), lambda i, *_: (jnp.minimum(i, np_tiles - 1), 0)),
                      pl.BlockSpec((rows, LANES), lambda i, *_: (jnp.maximum(i - np_tiles, 0), 0))],
            out_specs=pl.BlockSpec(memory_space=pl.ANY),
            scratch_shapes=[pltpu.VMEM((MOE_BLK * ROW_TILES, LANES), I32),
                            pltpu.SemaphoreType.DMA(()), pltpu.SemaphoreType.DMA(())],
        ),
        out_shape=jax.ShapeDtypeStruct((nslots * ROW_TILES, LANES), I32),
        compiler_params=pltpu.CompilerParams(dimension_semantics=("arbitrary",)),
        name="dispatch",
    )(pend, d1, d2, h2p, h2s)


def _expert_kernel(blk_e_ref, nblk_ref, first_ref, wslot_ref, next_e_ref,
                   xs_ref, wg_hbm, wu_hbm, wd_hbm, y_ref,
                   wg_f, wu_f, wd_f, wg_b, wu_b, wd_b, wsem):
    i = pl.program_id(0)
    live = i < nblk_ref[0]

    def weight_copies(e, slot):
        return (pltpu.make_async_copy(wg_hbm.at[e], wg_f.at[slot], wsem.at[slot, 0]),
                pltpu.make_async_copy(wu_hbm.at[e], wu_f.at[slot], wsem.at[slot, 1]),
                pltpu.make_async_copy(wd_hbm.at[e], wd_f.at[slot], wsem.at[slot, 2]))

    @pl.when(i == 0)
    def _():
        for cp in weight_copies(blk_e_ref[0], 0):
            cp.start()

    @pl.when(live & (first_ref[i] == 1))
    def _():
        slot = wslot_ref[i]
        for cp in weight_copies(blk_e_ref[i], slot):
            cp.wait()
        wg_b[...] = wg_f[slot].astype(BF16)
        wu_b[...] = wu_f[slot].astype(BF16)
        wd_b[...] = wd_f[slot].astype(BF16)

        @pl.when(next_e_ref[i] >= 0)
        def _():
            for cp in weight_copies(next_e_ref[i], 1 - slot):
                cp.start()

    @pl.when(live)
    def _():
        x_lo, x_hi = _unpack_rows(_load_tiles_as_rows(xs_ref, MOE_BLK))
        x_lo = x_lo.astype(BF16)
        x_hi = x_hi.astype(BF16)
        g = _dot(x_lo, wg_b[0:HALF, :]) + _dot(x_hi, wg_b[HALF:, :])
        u = _dot(x_lo, wu_b[0:HALF, :]) + _dot(x_hi, wu_b[HALF:, :])
        a = (g * _sigmoid(g)) * u
        _store_rows_as_tiles(y_ref, _pack_rows(_dot(a.astype(BF16), wd_b[...])))

    @pl.when(jnp.logical_not(live))
    def _():
        y_ref[...] = jnp.zeros_like(y_ref)


def _experts(blk_e, nblk, first, wslot, next_e, xs, wg, wu, wd):
    blk_rows = MOE_BLK * ROW_TILES
    nblocks = xs.shape[0] // blk_rows
    row_map = lambda i, be, nb, *_: (jnp.minimum(i, nb[0] - 1), 0)
    any_spec = pl.BlockSpec(memory_space=pl.ANY)
    return pl.pallas_call(
        _expert_kernel,
        grid_spec=pltpu.PrefetchScalarGridSpec(
            num_scalar_prefetch=5,
            grid=(nblocks,),
            in_specs=[pl.BlockSpec((blk_rows, LANES), row_map), any_spec, any_spec, any_spec],
            out_specs=pl.BlockSpec((blk_rows, LANES), lambda i, *_: (i, 0)),
            scratch_shapes=[pltpu.VMEM((2, D_MODEL, D_EXPERT), F32),
                            pltpu.VMEM((2, D_MODEL, D_EXPERT), F32),
                            pltpu.VMEM((2, D_EXPERT, D_MODEL), F32),
                            pltpu.VMEM((D_MODEL, D_EXPERT), BF16),
                            pltpu.VMEM((D_MODEL, D_EXPERT), BF16),
                            pltpu.VMEM((D_EXPERT, D_MODEL), BF16),
                            pltpu.SemaphoreType.DMA((2, 3))],
        ),
        out_shape=jax.ShapeDtypeStruct(xs.shape, I32),
        compiler_params=pltpu.CompilerParams(
            dimension_semantics=("arbitrary",), vmem_limit_bytes=VMEM_LIMIT),
        name="experts",
    )(blk_e, nblk, first, wslot, next_e, xs, wg, wu, wd)


def _combine_kernel(d1_ref, d2_ref, y_hbm, x1_ref, route_ref, gate_ref, gf_ref, o_ref,
                    buf1, buf2, sem):
    def row(r, u):
        pltpu.make_async_copy(_tile(y_hbm, d1_ref[r]), _tile(buf1, r), sem).start(priority=u % 2)
        pltpu.make_async_copy(_tile(y_hbm, d2_ref[r]), _tile(buf2, r), sem).start(
            priority=(u + 1) % 2)
    _issue_rows(TM, row)
    for buf in (buf1, buf2):
        pltpu.make_async_copy(y_hbm.at[pl.ds(0, TM * ROW_TILES)], buf, sem).wait()
    route = route_ref[0]
    w1 = route[:, 4:5]
    w2 = route[:, 5:6]
    a_lo, a_hi = _unpack_rows(_load_tiles_as_rows(buf1, TM))
    b_lo, b_hi = _unpack_rows(_load_tiles_as_rows(buf2, TM))
    ffn = jnp.concatenate([w1 * a_lo + w2 * b_lo, w1 * a_hi + w2 * b_hi], axis=1)
    x2 = x1_ref[0] + gate_ref[0] * ffn
    o_ref[0] = _rms(x2, gf_ref[...])


def _combine(d1, d2, y, x1, route, gate, gf, tok_base):
    nb, seq, _ = x1.shape
    nt = seq // TM
    blk0 = tok_base // TM
    smem_tile = pl.BlockSpec((TM,), lambda b, j: (blk0 + b * nt + j,), memory_space=pltpu.SMEM)
    tile = lambda b, j: (b, j, 0)
    grows = gate.shape[1]
    gate_spec = (pl.BlockSpec((1, 1, D_MODEL), lambda b, j: (b, 0, 0)) if grows == 1
                 else pl.BlockSpec((1, TM, D_MODEL), tile))
    return pl.pallas_call(
        _combine_kernel,
        grid=(nb, nt),
        in_specs=[smem_tile, smem_tile, pl.BlockSpec(memory_space=pl.ANY),
                  pl.BlockSpec((1, TM, D_MODEL), tile),
                  pl.BlockSpec((1, TM, RLANES), tile),
                  gate_spec,
                  pl.BlockSpec((1, D_MODEL), lambda b, j: (0, 0))],
        out_specs=pl.BlockSpec((1, TM, D_MODEL), tile),
        out_shape=jax.ShapeDtypeStruct(x1.shape, F32),
        scratch_shapes=[pltpu.VMEM((TM * ROW_TILES, LANES), I32),
                        pltpu.VMEM((TM * ROW_TILES, LANES), I32),
                        pltpu.SemaphoreType.DMA(())],
        compiler_params=pltpu.CompilerParams(
            dimension_semantics=("arbitrary", "arbitrary"), vmem_limit_bytes=VMEM_LIMIT),
        name="combine",
    )(d1, d2, y, x1, route, gate, gf)


def _band_bias(rel_bias, rows, keys):
    n = rows - 1 + keys
    dist = WINDOW + rows - 1 - np.arange(n + 1)
    flipped = rel_bias[:, np.clip(dist, -MAX_REL, MAX_REL) + MAX_REL]
    skew = jnp.tile(flipped, (1, rows))[:, :rows * n].reshape(N_HEADS, rows, n)
    b = skew[:, :, rows - 1:rows - 1 + keys]
    return b.reshape(2, 4 * rows, keys)


def kernel(x_prompt, x_sample, cache_attn_k, cache_attn_v, state_conv, c_prompt, c_sample,
           w_ada, b_ada, norm1_g, norm2_g, w_in, rel_bias, conv_w, conv_b, w_pa, w_pb, w_o,
           w_group, b_group, w_expert, b_expert, w_e_gate, w_e_up, w_e_down, final_g):
    assert w_ada.shape[0] == 1, "single trunk layer"
    nb, seq, _ = x_prompt.shape
    nseq, slen, _ = x_sample.shape
    ntok_p = nb * seq
    ntok_s = nseq * slen
    ntok = ntok_p + ntok_s
    assert seq % TL == 0 and WINDOW % TL == 0 and seq % TM == 0 and ntok_s % TM == 0
    assert slen >= 2 and slen & (slen - 1) == 0 and slen % 16 == 0

    n_c = nb + nseq
    n_pad = -(-n_c // 8) * 8
    c_all = jnp.concatenate([c_prompt, c_sample, jnp.zeros((n_pad - n_c, D_MODEL), F32)], axis=0)
    mod = _ada(c_all, w_ada[0], b_ada[0]).reshape(n_pad, 6, D_MODEL)
    mod_p = mod[:nb]
    mod_s = mod[nb:n_c]

    win = w_in[0].astype(BF16)
    wpa = w_pa[0].astype(BF16)
    wpb = w_pb[0].astype(BF16)
    wo = w_o[0].astype(BF16)
    g1 = norm1_g[0].reshape(1, D_MODEL)
    g2 = norm2_g[0].reshape(1, D_MODEL)
    gf = final_g.reshape(1, D_MODEL)
    cw = jnp.concatenate([conv_w[0], jnp.zeros((8 - conv_w.shape[1], D_CONV), F32)], axis=0)
    cbias = conv_b[0].reshape(1, D_CONV)
    wr = jnp.concatenate([w_expert[0], w_group[0],
                          jnp.zeros((D_MODEL, RLANES - N_EXPERTS - N_GROUPS), F32)], axis=1)
    whi = wr.astype(BF16)
    wlo = (wr - whi.astype(F32)).astype(BF16)
    wcat = jnp.concatenate([whi, wlo], axis=1)
    br = jnp.concatenate([b_expert[0], b_group[0],
                          jnp.zeros((RLANES - N_EXPERTS - N_GROUPS,), F32)]).reshape(1, RLANES)
    bias_p = _band_bias(rel_bias[0], CHUNK, BAND)
    bias_s = _band_bias(rel_bias[0], slen, WINDOW + slen)

    x1p, h2p, kp, vp, up8, route_p, cnt_p = _prompt_main(
        x_prompt, mod_p, g1, g2, win, bias_p, cw, cbias, wpa, wpb, wo, wcat, whi, br)

    st = state_conv[0]
    up1 = jnp.zeros((nseq, slen, D_CONV), F32).at[:, 0].set(st[:, 1]).reshape(ntok_s, D_CONV)
    up2 = (jnp.zeros((nseq, slen, D_CONV), F32).at[:, 0].set(st[:, 0]).at[:, 1].set(st[:, 1])
           .reshape(ntok_s, D_CONV))
    ck = cache_attn_k[0].reshape(nseq, WINDOW, D_ATT)
    cv = cache_attn_v[0].reshape(nseq, WINDOW, D_ATT)
    x1s, h2s, ks, vs, us, route_s, cnt = _sample_main(
        x_sample.reshape(ntok_s, D_MODEL), mod_s, ck, cv, up1, up2, cnt_p,
        g1, g2, win, bias_s, cw, cbias, wpa, wpb, wo, wcat, whi, br, nseq, slen)

    route_all = jnp.concatenate([route_p.reshape(ntok_p, RLANES)[:, :4], route_s[:, :4]], axis=0)
    experts = route_all[:, 0:2].astype(jnp.int32)
    ranks = route_all[:, 2:4].astype(jnp.int32)
    counts = cnt[0, :N_EXPERTS].astype(jnp.int32)
    pcounts = (counts + MOE_BLK - 1) // MOE_BLK * MOE_BLK
    pend = jnp.cumsum(pcounts)
    pstart = pend - pcounts
    eids = jnp.arange(N_EXPERTS, dtype=jnp.int32)
    dest = jnp.sum(jnp.where(experts[..., None] == eids, pstart, 0), axis=-1) + ranks
    d1 = dest[:, 0]
    d2 = dest[:, 1]
    nblocks = (2 * ntok) // MOE_BLK + N_EXPERTS
    blk_start = jnp.arange(nblocks, dtype=jnp.int32) * MOE_BLK
    blk_e = jnp.minimum(jnp.sum((pend[None, :] <= blk_start[:, None]).astype(jnp.int32), axis=1),
                        N_EXPERTS - 1)
    nblk = (pend[-1:] // MOE_BLK).astype(jnp.int32)
    pend0 = jnp.concatenate([jnp.zeros((1,), jnp.int32), pend.astype(jnp.int32)])

    xs = _dispatch(pend0, d1, d2, h2p, h2s, nblocks * MOE_BLK)
    blk_id = jnp.arange(nblocks, dtype=jnp.int32)
    first = (blk_id < nblk[0]) & ((blk_id == 0) | (blk_e != jnp.roll(blk_e, 1)))
    wslot = (jnp.cumsum(first.astype(jnp.int32)) - 1) % 2
    later_first = lax.cummin(jnp.where(first, blk_id, nblocks)[::-1])[::-1]
    next_first = jnp.concatenate([later_first[1:], jnp.full((1,), nblocks, jnp.int32)])
    next_e = jnp.where(next_first < nblocks, blk_e[jnp.minimum(next_first, nblocks - 1)], -1)
    y = _experts(blk_e, nblk, first.astype(jnp.int32), wslot.astype(jnp.int32),
                 next_e.astype(jnp.int32), xs, w_e_gate[0], w_e_up[0], w_e_down[0])

    y_prompt = _combine(d1, d2, y, x1p, route_p, mod_p[:, 5:6, :], gf, 0)
    gate_s = jnp.repeat(mod_s[:, 5, :], slen, axis=0).reshape(1, ntok_s, D_MODEL)
    y_sample = _combine(d1, d2, y, x1s.reshape(1, ntok_s, D_MODEL),
                        route_s.reshape(1, ntok_s, RLANES), gate_s, gf, ntok_p)

    new_k_p = kp.reshape(1, nb, WINDOW, N_HEADS, HEAD_DIM)
    new_v_p = vp.reshape(1, nb, WINDOW, N_HEADS, HEAD_DIM)
    new_conv_p = up8[:, 6:8, :].reshape(1, nb, 2, D_CONV)
    new_k_s = ks.reshape(1, nseq, slen, N_HEADS, HEAD_DIM)
    new_v_s = vs.reshape(1, nseq, slen, N_HEADS, HEAD_DIM)
    new_conv_s = us.reshape(nseq, slen, D_CONV)[:, slen - 2:, :].reshape(1, nseq, 2, D_CONV)
    return (y_prompt, y_sample.reshape(nseq, slen, D_MODEL), new_k_p, new_v_p, new_conv_p,
            new_k_s, new_v_s, new_conv_s)
```

```python
import functools

import numpy as np
import jax
import jax.numpy as jnp
from jax import lax
from jax.experimental import pallas as pl
from jax.experimental.pallas import tpu as pltpu

F32 = jnp.float32
BF16 = jnp.bfloat16
I32 = jnp.int32

D_MODEL = 1024
CHUNK = 64
LEFT = 8
WINDOW = LEFT * CHUNK
BAND = WINDOW + CHUNK
N_HEADS = 8
HEAD_DIM = 64
D_ATT = N_HEADS * HEAD_DIM
QUAD = 256
MAX_REL = 128
D_CONV = 512
N_GROUPS = 4
EPG = 8
N_EXPERTS = 32
D_EXPERT = 512
EPS = 1e-6
NEG = -1e30

TL = 512
RING = WINDOW + TL
MOE_BLK = 512
TM = 512
RLANES = 128
LANES = 128
HALF = D_MODEL // 2
ROW_TILES = HALF // LANES
ISSUE_UNROLL = 8
VMEM_LIMIT = 56 * 1024 * 1024


def _const_spec(shape):
    nd = len(shape)
    return pl.BlockSpec(shape, lambda *_: (0,) * nd, pipeline_mode=pl.Buffered(1))


def _dot(a, b):
    return jnp.dot(a, b, preferred_element_type=F32)


def _sigmoid(x):
    return 1.0 / (1.0 + jnp.exp(-x))


def _rms(x, g):
    ms = jnp.mean(x * x, axis=-1, keepdims=True)
    return x * lax.rsqrt(ms + EPS) * g


def _pack_rows(val):
    lo = lax.bitcast_convert_type(val[:, :HALF], I32) + 0x8000
    hi = lax.bitcast_convert_type(val[:, HALF:], I32) + 0x8000
    return (hi & -65536) | lax.shift_right_logical(lo, 16)


def _unpack_rows(packed):
    lo = lax.bitcast_convert_type(lax.shift_left(packed, 16), F32)
    hi = lax.bitcast_convert_type(packed & -65536, F32)
    return lo, hi


def _store_rows_as_tiles(ref, packed):
    r = packed.shape[0]
    for c in range(ROW_TILES):
        ref[pl.ds(c, r, stride=ROW_TILES), :] = packed[:, c * LANES:(c + 1) * LANES]


def _load_tiles_as_rows(ref, r):
    return jnp.concatenate(
        [ref[pl.ds(c, r, stride=ROW_TILES), :] for c in range(ROW_TILES)], axis=1)


def _tile(ref, row):
    return ref.at[pl.ds(pl.multiple_of(row * ROW_TILES, ROW_TILES), ROW_TILES)]


def _ada_kernel(c_ref, w_ref, b_ref, o_ref):
    c = c_ref[...]
    s = c * _sigmoid(c)
    o_ref[...] = jnp.dot(s, w_ref[...], preferred_element_type=F32,
                         precision=lax.Precision.HIGHEST) + b_ref[...]


def _ada(c_all, w_ada, b_ada):
    n = c_all.shape[0]
    nb = 1024
    return pl.pallas_call(
        _ada_kernel,
        grid=(6 * D_MODEL // nb,),
        in_specs=[pl.BlockSpec((n, D_MODEL), lambda i: (0, 0)),
                  pl.BlockSpec((D_MODEL, nb), lambda i: (0, i)),
                  pl.BlockSpec((1, nb), lambda i: (0, i))],
        out_specs=pl.BlockSpec((n, nb), lambda i: (0, i)),
        out_shape=jax.ShapeDtypeStruct((n, 6 * D_MODEL), F32),
        name="ada",
    )(c_all, w_ada, b_ada.reshape(1, -1))


def _attend(q, kb, vb, bias, lim):
    r = q.shape[0]
    nk = kb.shape[0]
    assert r & (r - 1) == 0
    qt = jnp.concatenate([q] * 4, axis=0)
    rowh = lax.broadcasted_iota(jnp.int32, (4 * r, QUAD), 0) >> (r.bit_length() - 1)
    laneh = lax.broadcasted_iota(jnp.int32, (4 * r, QUAD), 1) >> 6
    qm = jnp.where(rowh == laneh, qt, jnp.zeros_like(qt))
    s = lax.dot_general(qm, kb, (((1,), (1,)), ((), ())), preferred_element_type=F32)
    s = s + bias
    if lim is not None:
        col = lax.broadcasted_iota(jnp.int32, (4 * r, nk), 1)
        s = jnp.where(col >= lim, s, NEG)
    m = jnp.max(s, axis=1, keepdims=True)
    p = jnp.exp(s - m)
    l = jnp.sum(p, axis=1, keepdims=True)
    o = _dot(p.astype(BF16), vb) * (1.0 / l)
    lane_o = lax.broadcasted_iota(jnp.int32, (r, QUAD), 1) >> 6
    out = o[0:r]
    for h in range(1, 4):
        out = jnp.where(lane_o == h, o[h * r:(h + 1) * r], out)
    return out


def _route_stages(h2, wcat_ref, whi_ref, br_ref, cnt, valid=None):
    r = h2.shape[0]
    lane = lax.broadcasted_iota(jnp.int32, (r, RLANES), 1)
    lane_f = lane.astype(F32)
    big = jnp.float32(1000.0)
    v = {}

    def logits():
        hi = h2.astype(BF16)
        lo = (h2 - hi.astype(F32)).astype(BF16)
        z = _dot(hi, wcat_ref[...])
        v["logits"] = z[:, :RLANES] + z[:, RLANES:] + _dot(lo, whi_ref[...]) + br_ref[...]

    def group():
        lg = jnp.where((lane >= N_EXPERTS) & (lane < N_EXPERTS + N_GROUPS), v["logits"], NEG)
        mg = jnp.max(lg, axis=1, keepdims=True)
        v["gi"] = jnp.min(jnp.where(lg == mg, lane_f, big), axis=1, keepdims=True) - N_EXPERTS
        v["pg"] = 1.0 / jnp.sum(jnp.exp(lg - mg), axis=1, keepdims=True)

    def top1():
        grp_of_lane = (lane >> 3).astype(F32)
        le = jnp.where((lane < N_EXPERTS) & (grp_of_lane == v["gi"]), v["logits"], NEG)
        v["m1"] = jnp.max(le, axis=1, keepdims=True)
        v["i1"] = jnp.min(jnp.where(le == v["m1"], lane_f, big), axis=1, keepdims=True)
        v["le"] = le

    def top2():
        sel1 = lane_f == v["i1"]
        le2 = jnp.where(sel1, NEG, v["le"])
        m2 = jnp.max(le2, axis=1, keepdims=True)
        v["i2"] = jnp.min(jnp.where(le2 == m2, lane_f, big), axis=1, keepdims=True)
        rr = jnp.exp(m2 - v["m1"])
        inv = v["pg"] / (1.0 + rr)
        v["w1"] = inv
        v["w2"] = inv * rr
        v["sel1"] = sel1
        v["sel2"] = lane_f == v["i2"]

    def ranks():
        oh = jnp.where(v["sel1"] | v["sel2"], 1.0 if valid is None else valid, 0.0).astype(F32)
        ri = lax.broadcasted_iota(jnp.int32, (r, r), 0)
        ci = lax.broadcasted_iota(jnp.int32, (r, r), 1)
        tri = jnp.where(ri > ci, 1.0, 0.0).astype(BF16)
        v["before"] = _dot(tri, oh.astype(BF16)) + cnt
        v["new_cnt"] = cnt + jnp.sum(oh, axis=0, keepdims=True)

    def assemble():
        r1 = jnp.sum(jnp.where(v["sel1"], v["before"], 0.0), axis=1, keepdims=True)
        r2 = jnp.sum(jnp.where(v["sel2"], v["before"], 0.0), axis=1, keepdims=True)
        route = jnp.where(lane == 0, v["i1"], 0.0)
        route = jnp.where(lane == 1, v["i2"], route)
        route = jnp.where(lane == 2, r1, route)
        route = jnp.where(lane == 3, r2, route)
        route = jnp.where(lane == 4, v["w1"], route)
        route = jnp.where(lane == 5, v["w2"], route)
        return route, v["new_cnt"]

    return [logits, group, top1, top2, ranks, assemble]


def _route(h2, wcat_ref, whi_ref, br_ref, cnt):
    stages = _route_stages(h2, wcat_ref, whi_ref, br_ref, cnt)
    for stage in stages[:-1]:
        stage()
    return stages[-1]()


def _prompt_kernel(x_ref, mod_ref, g1_ref, g2_ref, win_ref, bias_ref, cw_ref, cbias_ref,
                   wpa_ref, wpb_ref, wo_ref, wcat_ref, whi_ref, br_ref,
                   x1_ref, h2_ref, ko_ref, vo_ref, uo_ref, route_ref, cnt_ref,
                   kring, vring, att_s, ucarry, cnt_s, h2_prev, *, nt, ntiles):
    g = pl.program_id(0)

    @pl.when(g == 0)
    def _():
        cnt_s[...] = jnp.zeros_like(cnt_s)
        h2_prev[...] = jnp.zeros_like(h2_prev)

    @pl.when(g < ntiles)
    def _():
        _prompt_tile(g, nt, x_ref, mod_ref, g1_ref, g2_ref, win_ref, bias_ref, cw_ref, cbias_ref,
                     wpa_ref, wpb_ref, wo_ref, wcat_ref, whi_ref, br_ref,
                     x1_ref, h2_ref, ko_ref, vo_ref, uo_ref, route_ref, cnt_ref,
                     kring, vring, att_s, ucarry, cnt_s, h2_prev)

    @pl.when(g == ntiles)
    def _():
        route, new_cnt = _route(h2_prev[...], wcat_ref, whi_ref, br_ref, cnt_s[...])
        route_ref[0] = route
        cnt_ref[...] = new_cnt


def _prompt_tile(g, nt, x_ref, mod_ref, g1_ref, g2_ref, win_ref, bias_ref, cw_ref, cbias_ref,
                 wpa_ref, wpb_ref, wo_ref, wcat_ref, whi_ref, br_ref,
                 x1_ref, h2_ref, ko_ref, vo_ref, uo_ref, route_ref, cnt_ref,
                 kring, vring, att_s, ucarry, cnt_s, h2_prev):
    j = g % nt

    @pl.when(j == 0)
    def _():
        kring[0:WINDOW, :] = jnp.zeros((WINDOW, D_ATT), BF16)
        vring[0:WINDOW, :] = jnp.zeros((WINDOW, D_ATT), BF16)
        ucarry[...] = jnp.zeros_like(ucarry)

    route_stages = _route_stages(h2_prev[...], wcat_ref, whi_ref, br_ref, cnt_s[...],
                                 valid=jnp.where(g > 0, 1.0, 0.0).astype(F32))
    route_stages[0]()

    sh1 = mod_ref[0, 0:1, :]
    sc1 = mod_ref[0, 1:2, :]
    gt1 = mod_ref[0, 2:3, :]
    sh2 = mod_ref[0, 3:4, :]
    sc2 = mod_ref[0, 4:5, :]

    x = x_ref[0]
    h = _rms(x, g1_ref[...]) * (1.0 + sc1) + sh1
    hb = h.astype(BF16)

    qkv = _dot(hb, win_ref[:, 0:3 * D_ATT])
    q = (qkv[:, 0:D_ATT] * (HEAD_DIM ** -0.5)).astype(BF16)
    k = qkv[:, D_ATT:2 * D_ATT]
    v = qkv[:, 2 * D_ATT:3 * D_ATT]
    ko_ref[0] = k
    vo_ref[0] = v
    kring[WINDOW:RING, :] = k.astype(BF16)
    vring[WINDOW:RING, :] = v.astype(BF16)

    base = j * TL
    for c in range(TL // CHUNK):
        lim = WINDOW - (base + c * CHUNK)
        for qd in range(2):
            ls = slice(qd * QUAD, (qd + 1) * QUAD)
            o = _attend(q[c * CHUNK:(c + 1) * CHUNK, ls],
                        kring[c * CHUNK:c * CHUNK + BAND, ls],
                        vring[c * CHUNK:c * CHUNK + BAND, ls],
                        bias_ref[qd], lim)
            att_s[c * CHUNK:(c + 1) * CHUNK, ls] = o.astype(BF16)
        if c + 1 < len(route_stages) - 1:
            route_stages[c + 1]()
    route, new_cnt = route_stages[-1]()
    route_ref[0] = route
    cnt_s[...] = new_cnt
    cnt_ref[...] = new_cnt

    kring[0:WINDOW, :] = kring[TL:RING, :]
    vring[0:WINDOW, :] = vring[TL:RING, :]

    cbcv = _dot(hb, win_ref[:, 3 * D_ATT:3 * D_ATT + 3 * D_CONV])
    cb = cbcv[:, 0:D_CONV]
    u = cbcv[:, D_CONV:2 * D_CONV] * cbcv[:, 2 * D_CONV:3 * D_CONV]
    row = lax.broadcasted_iota(jnp.int32, (8, D_CONV), 0)
    prev = ucarry[...]
    r1 = pltpu.roll(u, 1, axis=0)
    r2 = pltpu.roll(u, 2, axis=0)
    u_m1 = jnp.concatenate(
        [jnp.where(row < 1, pltpu.roll(prev, 1, axis=0), r1[0:8]), r1[8:]], axis=0)
    u_m2 = jnp.concatenate(
        [jnp.where(row < 2, pltpu.roll(prev, 2, axis=0), r2[0:8]), r2[8:]], axis=0)
    yc = cw_ref[0:1, :] * u_m2 + cw_ref[1:2, :] * u_m1 + cw_ref[2:3, :] * u + cbias_ref[...]
    conv_out = (cb * yc).astype(BF16)
    ucarry[...] = u[TL - 8:TL, :]
    uo_ref[0] = u[TL - 8:TL, :]

    gates = _dot(hb, win_ref[:, 3 * D_ATT + 3 * D_CONV:])
    pa = _dot(att_s[...], wpa_ref[...])
    pb = _dot(conv_out, wpb_ref[...])
    mixin = _sigmoid(gates[:, 0:D_MODEL]) * pa + _sigmoid(gates[:, D_MODEL:]) * pb
    mix = _dot(mixin.astype(BF16), wo_ref[...])
    x1 = x + gt1 * mix
    x1_ref[0] = x1
    h2 = _rms(x1, g2_ref[...]) * (1.0 + sc2) + sh2
    _store_rows_as_tiles(h2_ref, _pack_rows(h2))

    h2_prev[...] = h2


def _prompt_main(x, mod, g1, g2, win, bias_q, cw, cbias, wpa, wpb, wo, wcat, whi, br):
    nb, seq, _ = x.shape
    nt = seq // TL
    ntiles = nb * nt
    keep = WINDOW // TL
    cur = lambda g: jnp.minimum(g, ntiles - 1)
    prv = lambda g: jnp.maximum(g - 1, 0)
    tile = lambda g: (cur(g) // nt, cur(g) % nt, 0)
    last = lambda g: (cur(g) // nt, jnp.maximum(cur(g) % nt - (nt - keep), 0), 0)
    perb = lambda g: (cur(g) // nt, 0, 0)
    in_specs = [
        pl.BlockSpec((1, TL, D_MODEL), tile),
        pl.BlockSpec((1, 6, D_MODEL), perb),
        _const_spec(g1.shape), _const_spec(g2.shape), _const_spec(win.shape),
        _const_spec(bias_q.shape), _const_spec(cw.shape), _const_spec(cbias.shape),
        _const_spec(wpa.shape), _const_spec(wpb.shape), _const_spec(wo.shape),
        _const_spec(wcat.shape), _const_spec(whi.shape), _const_spec(br.shape),
    ]
    out_specs = [
        pl.BlockSpec((1, TL, D_MODEL), tile),
        pl.BlockSpec((TL * ROW_TILES, LANES), lambda g: (cur(g), 0)),
        pl.BlockSpec((1, TL, D_ATT), last),
        pl.BlockSpec((1, TL, D_ATT), last),
        pl.BlockSpec((1, 8, D_CONV), perb),
        pl.BlockSpec((1, TL, RLANES), lambda g: (prv(g) // nt, prv(g) % nt, 0)),
        pl.BlockSpec((1, RLANES), lambda g: (0, 0)),
    ]
    out_shape = [
        jax.ShapeDtypeStruct((nb, seq, D_MODEL), F32),
        jax.ShapeDtypeStruct((nb * seq * ROW_TILES, LANES), I32),
        jax.ShapeDtypeStruct((nb, WINDOW, D_ATT), F32),
        jax.ShapeDtypeStruct((nb, WINDOW, D_ATT), F32),
        jax.ShapeDtypeStruct((nb, 8, D_CONV), F32),
        jax.ShapeDtypeStruct((nb, seq, RLANES), F32),
        jax.ShapeDtypeStruct((1, RLANES), F32),
    ]
    scratch = [
        pltpu.VMEM((RING, D_ATT), BF16), pltpu.VMEM((RING, D_ATT), BF16),
        pltpu.VMEM((TL, D_ATT), BF16), pltpu.VMEM((8, D_CONV), F32),
        pltpu.VMEM((1, RLANES), F32), pltpu.VMEM((TL, D_MODEL), F32),
    ]
    return pl.pallas_call(
        functools.partial(_prompt_kernel, nt=nt, ntiles=ntiles),
        grid=(ntiles + 1,),
        in_specs=in_specs, out_specs=out_specs, out_shape=out_shape,
        scratch_shapes=scratch,
        compiler_params=pltpu.CompilerParams(
            dimension_semantics=("arbitrary",), vmem_limit_bytes=VMEM_LIMIT),
        name="prompt_main",
    )(x, mod, g1, g2, win, bias_q, cw, cbias, wpa, wpb, wo, wcat, whi, br)


def _sample_kernel(x_ref, mod_ref, ck_ref, cv_ref, up1_ref, up2_ref, cnt_in_ref,
                   g1_ref, g2_ref, win_ref, bias_ref, cw_ref, cbias_ref,
                   wpa_ref, wpb_ref, wo_ref, wcat_ref, whi_ref, br_ref,
                   x1_ref, h2_ref, ko_ref, vo_ref, uo_ref, route_ref, cnt_ref,
                   h_s, q_s, kn_s, vn_s, kband, vband, att_s, conv_s, h2_s, *, nseq, slen):
    n = pl.program_id(0)
    ntok = nseq * slen

    @pl.when(n == 0)
    def _():
        def norm_body(i, carry):
            rows = pl.ds(pl.multiple_of(i * slen, slen), slen)
            xi = x_ref[rows, :]
            m = mod_ref[i]
            hi = _rms(xi, g1_ref[...]) * (1.0 + m[1:2, :]) + m[0:1, :]
            h_s[rows, :] = hi.astype(BF16)
            return carry
        lax.fori_loop(0, nseq, norm_body, 0)
        hb = h_s[...]
        qkv = _dot(hb, win_ref[:, 0:3 * D_ATT])
        q_s[...] = (qkv[:, 0:D_ATT] * (HEAD_DIM ** -0.5)).astype(BF16)
        k = qkv[:, D_ATT:2 * D_ATT]
        v = qkv[:, 2 * D_ATT:3 * D_ATT]
        ko_ref[...] = k
        vo_ref[...] = v
        kn_s[...] = k.astype(BF16)
        vn_s[...] = v.astype(BF16)

        cbcv = _dot(hb, win_ref[:, 3 * D_ATT:3 * D_ATT + 3 * D_CONV])
        cb = cbcv[:, 0:D_CONV]
        u = cbcv[:, D_CONV:2 * D_CONV] * cbcv[:, 2 * D_CONV:3 * D_CONV]
        pos = lax.broadcasted_iota(jnp.int32, (ntok, D_CONV), 0) & (slen - 1)
        u_m1 = jnp.where(pos < 1, up1_ref[...], pltpu.roll(u, 1, axis=0))
        u_m2 = jnp.where(pos < 2, up2_ref[...], pltpu.roll(u, 2, axis=0))
        yc = cw_ref[0:1, :] * u_m2 + cw_ref[1:2, :] * u_m1 + cw_ref[2:3, :] * u + cbias_ref[...]
        conv_s[...] = (cb * yc).astype(BF16)
        uo_ref[...] = u

    rows = pl.ds(pl.multiple_of(n * slen, slen), slen)
    kband[0:WINDOW, :] = ck_ref[0].astype(BF16)
    vband[0:WINDOW, :] = cv_ref[0].astype(BF16)
    kband[WINDOW:WINDOW + slen, :] = kn_s[rows, :]
    vband[WINDOW:WINDOW + slen, :] = vn_s[rows, :]
    qn = q_s[rows, :]
    outs = []
    for qd in range(2):
        ls = slice(qd * QUAD, (qd + 1) * QUAD)
        outs.append(_attend(qn[:, ls], kband[:, ls], vband[:, ls], bias_ref[qd], None))
    att_s[rows, :] = jnp.concatenate(outs, axis=1).astype(BF16)

    @pl.when(n == nseq - 1)
    def _():
        gates = _dot(h_s[...], win_ref[:, 3 * D_ATT + 3 * D_CONV:])
        pa = _dot(att_s[...], wpa_ref[...])
        pb = _dot(conv_s[...], wpb_ref[...])
        mixin = _sigmoid(gates[:, 0:D_MODEL]) * pa + _sigmoid(gates[:, D_MODEL:]) * pb
        x1_ref[...] = _dot(mixin.astype(BF16), wo_ref[...])

        def res_body(i, carry):
            r = pl.ds(pl.multiple_of(i * slen, slen), slen)
            m = mod_ref[i]
            x1 = x_ref[r, :] + m[2:3, :] * x1_ref[r, :]
            x1_ref[r, :] = x1
            h2_s[r, :] = _rms(x1, g2_ref[...]) * (1.0 + m[4:5, :]) + m[3:4, :]
            return carry
        lax.fori_loop(0, nseq, res_body, 0)

        h2 = h2_s[...]
        _store_rows_as_tiles(h2_ref, _pack_rows(h2))
        route, new_cnt = _route(h2, wcat_ref, whi_ref, br_ref, cnt_in_ref[...])
        route_ref[...] = route
        cnt_ref[...] = new_cnt


def _sample_main(x2d, mod, ck, cv, up1, up2, cnt_in, g1, g2, win, bias_s, cw, cbias,
                 wpa, wpb, wo, wcat, whi, br, nseq, slen):
    ntok = nseq * slen
    args = (x2d, mod, ck, cv, up1, up2, cnt_in, g1, g2, win, bias_s, cw, cbias,
            wpa, wpb, wo, wcat, whi, br)
    in_specs = []
    for idx, a in enumerate(args):
        if idx in (2, 3):
            in_specs.append(pl.BlockSpec((1, WINDOW, D_ATT), lambda n: (n, 0, 0)))
        else:
            in_specs.append(_const_spec(a.shape))
    whole = lambda shape: pl.BlockSpec(shape, lambda n: (0,) * len(shape))
    outs = [((ntok, D_MODEL), F32), ((ntok * ROW_TILES, LANES), I32), ((ntok, D_ATT), F32),
            ((ntok, D_ATT), F32), ((ntok, D_CONV), F32), ((ntok, RLANES), F32), ((1, RLANES), F32)]
    scratch = [
        pltpu.VMEM((ntok, D_MODEL), BF16), pltpu.VMEM((ntok, D_ATT), BF16),
        pltpu.VMEM((ntok, D_ATT), BF16), pltpu.VMEM((ntok, D_ATT), BF16),
        pltpu.VMEM((WINDOW + slen, D_ATT), BF16), pltpu.VMEM((WINDOW + slen, D_ATT), BF16),
        pltpu.VMEM((ntok, D_ATT), BF16), pltpu.VMEM((ntok, D_CONV), BF16),
        pltpu.VMEM((ntok, D_MODEL), F32),
    ]
    return pl.pallas_call(
        functools.partial(_sample_kernel, nseq=nseq, slen=slen),
        grid=(nseq,),
        in_specs=in_specs,
        out_specs=[whole(s) for s, _ in outs],
        out_shape=[jax.ShapeDtypeStruct(s, d) for s, d in outs],
        scratch_shapes=scratch,
        compiler_params=pltpu.CompilerParams(
            dimension_semantics=("arbitrary",), vmem_limit_bytes=VMEM_LIMIT),
        name="sample_main",
    )(*args)


def _issue_rows(n, body):
    def group(g, carry):
        for u in range(ISSUE_UNROLL):
            body(g * ISSUE_UNROLL + u, u)
        return carry
    lax.fori_loop(0, n // ISSUE_UNROLL, group, 0)


def _dispatch_kernel(pend_ref, d1_ref, d2_ref, hp_ref, hs_ref, xs_out, zbuf, sem, zsem, *,
                     np_tiles, nblocks):
    i = pl.program_id(0)
    blk_rows = MOE_BLK * ROW_TILES

    @pl.when(i == 0)
    def _():
        zbuf[...] = jnp.zeros_like(zbuf)

        def zcopy(e):
            start = pl.multiple_of((pend_ref[e + 1] - MOE_BLK) * ROW_TILES, blk_rows)
            return pltpu.make_async_copy(zbuf, xs_out.at[pl.ds(start, blk_rows)], zsem)

        def zstart(e, carry):
            @pl.when(pend_ref[e + 1] > pend_ref[e])
            def _():
                zcopy(e).start()
            return carry

        def zwait(e, carry):
            @pl.when(pend_ref[e + 1] > pend_ref[e])
            def _():
                zcopy(e).wait()
            return carry
        lax.fori_loop(0, N_EXPERTS, zstart, 0)
        lax.fori_loop(0, N_EXPERTS, zwait, 0)

        def tcopy(bk):
            start = pl.multiple_of(bk * blk_rows, blk_rows)
            return pltpu.make_async_copy(zbuf, xs_out.at[pl.ds(start, blk_rows)], zsem)

        def tstart(bk, carry):
            tcopy(bk).start()
            return carry

        def twait(bk, carry):
            tcopy(bk).wait()
            return carry
        used = pend_ref[N_EXPERTS] // MOE_BLK
        lax.fori_loop(used, nblocks, tstart, 0)
        lax.fori_loop(used, nblocks, twait, 0)

    def scatter_tile(src):
        def row(r, u):
            pltpu.make_async_copy(_tile(src, r), _tile(xs_out, d1_ref[r]), sem).start(priority=u % 2)
            pltpu.make_async_copy(_tile(src, r), _tile(xs_out, d2_ref[r]), sem).start(
                priority=(u + 1) % 2)
        _issue_rows(TM, row)
        for _ in range(2):
            pltpu.make_async_copy(src, xs_out.at[pl.ds(0, TM * ROW_TILES)], sem).wait()

    @pl.when(i < np_tiles)
    def _():
        scatter_tile(hp_ref)

    @pl.when(i >= np_tiles)
    def _():
        scatter_tile(hs_ref)


def _dispatch(pend, d1, d2, h2p, h2s, nslots):
    np_tiles = h2p.shape[0] // (TM * ROW_TILES)
    ns_tiles = h2s.shape[0] // (TM * ROW_TILES)
    smem_tile = pl.BlockSpec((TM,), lambda i, *_: (i,), memory_space=pltpu.SMEM)
    rows = TM * ROW_TILES
    return pl.pallas_call(
        functools.partial(_dispatch_kernel, np_tiles=np_tiles, nblocks=nslots // MOE_BLK),
        grid_spec=pltpu.PrefetchScalarGridSpec(
            num_scalar_prefetch=1,
            grid=(np_tiles + ns_tiles,),
            in_specs=[smem_tile, smem_tile,
                      pl.BlockSpec((rows, LANES), lambda i, *_: (jnp.minimum(i, np_tiles - 1), 0)),
                      pl.BlockSpec((rows, LANES), lambda i, *_: (jnp.maximum(i - np_tiles, 0), 0))],
            out_specs=pl.BlockSpec(memory_space=pl.ANY),
            scratch_shapes=[pltpu.VMEM((MOE_BLK * ROW_TILES, LANES), I32),
                            pltpu.SemaphoreType.DMA(()), pltpu.SemaphoreType.DMA(())],
        ),
        out_shape=jax.ShapeDtypeStruct((nslots * ROW_TILES, LANES), I32),
        compiler_params=pltpu.CompilerParams(dimension_semantics=("arbitrary",)),
        name="dispatch",
    )(pend, d1, d2, h2p, h2s)


def _expert_kernel(blk_e_ref, nblk_ref, first_ref, wslot_ref, next_e_ref,
                   xs_ref, wg_hbm, wu_hbm, wd_hbm, y_ref,
                   wg_f, wu_f, wd_f, wg_b, wu_b, wd_b, wsem):
    i = pl.program_id(0)
    live = i < nblk_ref[0]

    def weight_copies(e, slot):
        return (pltpu.make_async_copy(wg_hbm.at[e], wg_f.at[slot], wsem.at[slot, 0]),
                pltpu.make_async_copy(wu_hbm.at[e], wu_f.at[slot], wsem.at[slot, 1]),
                pltpu.make_async_copy(wd_hbm.at[e], wd_f.at[slot], wsem.at[slot, 2]))

    @pl.when(i == 0)
    def _():
        for cp in weight_copies(blk_e_ref[0], 0):
            cp.start()

    @pl.when(live & (first_ref[i] == 1))
    def _():
        slot = wslot_ref[i]
        for cp in weight_copies(blk_e_ref[i], slot):
            cp.wait()
        wg_b[...] = wg_f[slot].astype(BF16)
        wu_b[...] = wu_f[slot].astype(BF16)
        wd_b[...] = wd_f[slot].astype(BF16)

        @pl.when(next_e_ref[i] >= 0)
        def _():
            for cp in weight_copies(next_e_ref[i], 1 - slot):
                cp.start()

    @pl.when(live)
    def _():
        x_lo, x_hi = _unpack_rows(_load_tiles_as_rows(xs_ref, MOE_BLK))
        x_lo = x_lo.astype(BF16)
        x_hi = x_hi.astype(BF16)
        g = _dot(x_lo, wg_b[0:HALF, :]) + _dot(x_hi, wg_b[HALF:, :])
        u = _dot(x_lo, wu_b[0:HALF, :]) + _dot(x_hi, wu_b[HALF:, :])
        a = (g * _sigmoid(g)) * u
        _store_rows_as_tiles(y_ref, _pack_rows(_dot(a.astype(BF16), wd_b[...])))

    @pl.when(jnp.logical_not(live))
    def _():
        y_ref[...] = jnp.zeros_like(y_ref)


def _experts(blk_e, nblk, first, wslot, next_e, xs, wg, wu, wd):
    blk_rows = MOE_BLK * ROW_TILES
    nblocks = xs.shape[0] // blk_rows
    row_map = lambda i, be, nb, *_: (jnp.minimum(i, nb[0] - 1), 0)
    any_spec = pl.BlockSpec(memory_space=pl.ANY)
    return pl.pallas_call(
        _expert_kernel,
        grid_spec=pltpu.PrefetchScalarGridSpec(
            num_scalar_prefetch=5,
            grid=(nblocks,),
            in_specs=[pl.BlockSpec((blk_rows, LANES), row_map), any_spec, any_spec, any_spec],
            out_specs=pl.BlockSpec((blk_rows, LANES), lambda i, *_: (i, 0)),
            scratch_shapes=[pltpu.VMEM((2, D_MODEL, D_EXPERT), F32),
                            pltpu.VMEM((2, D_MODEL, D_EXPERT), F32),
                            pltpu.VMEM((2, D_EXPERT, D_MODEL), F32),
                            pltpu.VMEM((D_MODEL, D_EXPERT), BF16),
                            pltpu.VMEM((D_MODEL, D_EXPERT), BF16),
                            pltpu.VMEM((D_EXPERT, D_MODEL), BF16),
                            pltpu.SemaphoreType.DMA((2, 3))],
        ),
        out_shape=jax.ShapeDtypeStruct(xs.shape, I32),
        compiler_params=pltpu.CompilerParams(
            dimension_semantics=("arbitrary",), vmem_limit_bytes=VMEM_LIMIT),
        name="experts",
    )(blk_e, nblk, first, wslot, next_e, xs, wg, wu, wd)


def _combine_kernel(d1_ref, d2_ref, y_hbm, x1_ref, route_ref, gate_ref, gf_ref, o_ref,
                    buf1, buf2, sem):
    def row(r, u):
        pltpu.make_async_copy(_tile(y_hbm, d1_ref[r]), _tile(buf1, r), sem).start(priority=u % 2)
        pltpu.make_async_copy(_tile(y_hbm, d2_ref[r]), _tile(buf2, r), sem).start(
            priority=(u + 1) % 2)
    _issue_rows(TM, row)
    for buf in (buf1, buf2):
        pltpu.make_async_copy(y_hbm.at[pl.ds(0, TM * ROW_TILES)], buf, sem).wait()
    route = route_ref[0]
    w1 = route[:, 4:5]
    w2 = route[:, 5:6]
    a_lo, a_hi = _unpack_rows(_load_tiles_as_rows(buf1, TM))
    b_lo, b_hi = _unpack_rows(_load_tiles_as_rows(buf2, TM))
    ffn = jnp.concatenate([w1 * a_lo + w2 * b_lo, w1 * a_hi + w2 * b_hi], axis=1)
    x2 = x1_ref[0] + gate_ref[0] * ffn
    o_ref[0] = _rms(x2, gf_ref[...])


def _combine(d1, d2, y, x1, route, gate, gf, tok_base):
    nb, seq, _ = x1.shape
    nt = seq // TM
    blk0 = tok_base // TM
    smem_tile = pl.BlockSpec((TM,), lambda b, j: (blk0 + b * nt + j,), memory_space=pltpu.SMEM)
    tile = lambda b, j: (b, j, 0)
    grows = gate.shape[1]
    gate_spec = (pl.BlockSpec((1, 1, D_MODEL), lambda b, j: (b, 0, 0)) if grows == 1
                 else pl.BlockSpec((1, TM, D_MODEL), tile))
    return pl.pallas_call(
        _combine_kernel,
        grid=(nb, nt),
        in_specs=[smem_tile, smem_tile, pl.BlockSpec(memory_space=pl.ANY),
                  pl.BlockSpec((1, TM, D_MODEL), tile),
                  pl.BlockSpec((1, TM, RLANES), tile),
                  gate_spec,
                  pl.BlockSpec((1, D_MODEL), lambda b, j: (0, 0))],
        out_specs=pl.BlockSpec((1, TM, D_MODEL), tile),
        out_shape=jax.ShapeDtypeStruct(x1.shape, F32),
        scratch_shapes=[pltpu.VMEM((TM * ROW_TILES, LANES), I32),
                        pltpu.VMEM((TM * ROW_TILES, LANES), I32),
                        pltpu.SemaphoreType.DMA(())],
        compiler_params=pltpu.CompilerParams(
            dimension_semantics=("arbitrary", "arbitrary"), vmem_limit_bytes=VMEM_LIMIT),
        name="combine",
    )(d1, d2, y, x1, route, gate, gf)


def _band_bias(rel_bias, rows, keys):
    n = rows - 1 + keys
    dist = WINDOW + rows - 1 - np.arange(n + 1)
    flipped = rel_bias[:, np.clip(dist, -MAX_REL, MAX_REL) + MAX_REL]
    skew = jnp.tile(flipped, (1, rows))[:, :rows * n].reshape(N_HEADS, rows, n)
    b = skew[:, :, rows - 1:rows - 1 + keys]
    return b.reshape(2, 4 * rows, keys)


def kernel(x_prompt, x_sample, cache_attn_k, cache_attn_v, state_conv, c_prompt, c_sample,
           w_ada, b_ada, norm1_g, norm2_g, w_in, rel_bias, conv_w, conv_b, w_pa, w_pb, w_o,
           w_group, b_group, w_expert, b_expert, w_e_gate, w_e_up, w_e_down, final_g):
    assert w_ada.shape[0] == 1, "single trunk layer"
    nb, seq, _ = x_prompt.shape
    nseq, slen, _ = x_sample.shape
    ntok_p = nb * seq
    ntok_s = nseq * slen
    ntok = ntok_p + ntok_s
    assert seq % TL == 0 and WINDOW % TL == 0 and seq % TM == 0 and ntok_s % TM == 0
    assert slen >= 2 and slen & (slen - 1) == 0 and slen % 16 == 0

    n_c = nb + nseq
    n_pad = -(-n_c // 8) * 8
    c_all = jnp.concatenate([c_prompt, c_sample, jnp.zeros((n_pad - n_c, D_MODEL), F32)], axis=0)
    mod = _ada(c_all, w_ada[0], b_ada[0]).reshape(n_pad, 6, D_MODEL)
    mod_p = mod[:nb]
    mod_s = mod[nb:n_c]

    win = w_in[0].astype(BF16)
    wpa = w_pa[0].astype(BF16)
    wpb = w_pb[0].astype(BF16)
    wo = w_o[0].astype(BF16)
    g1 = norm1_g[0].reshape(1, D_MODEL)
    g2 = norm2_g[0].reshape(1, D_MODEL)
    gf = final_g.reshape(1, D_MODEL)
    cw = jnp.concatenate([conv_w[0], jnp.zeros((8 - conv_w.shape[1], D_CONV), F32)], axis=0)
    cbias = conv_b[0].reshape(1, D_CONV)
    wr = jnp.concatenate([w_expert[0], w_group[0],
                          jnp.zeros((D_MODEL, RLANES - N_EXPERTS - N_GROUPS), F32)], axis=1)
    whi = wr.astype(BF16)
    wlo = (wr - whi.astype(F32)).astype(BF16)
    wcat = jnp.concatenate([whi, wlo], axis=1)
    br = jnp.concatenate([b_expert[0], b_group[0],
                          jnp.zeros((RLANES - N_EXPERTS - N_GROUPS,), F32)]).reshape(1, RLANES)
    bias_p = _band_bias(rel_bias[0], CHUNK, BAND)
    bias_s = _band_bias(rel_bias[0], slen, WINDOW + slen)

    x1p, h2p, kp, vp, up8, route_p, cnt_p = _prompt_main(
        x_prompt, mod_p, g1, g2, win, bias_p, cw, cbias, wpa, wpb, wo, wcat, whi, br)

    st = state_conv[0]
    up1 = jnp.zeros((nseq, slen, D_CONV), F32).at[:, 0].set(st[:, 1]).reshape(ntok_s, D_CONV)
    up2 = (jnp.zeros((nseq, slen, D_CONV), F32).at[:, 0].set(st[:, 0]).at[:, 1].set(st[:, 1])
           .reshape(ntok_s, D_CONV))
    ck = cache_attn_k[0].reshape(nseq, WINDOW, D_ATT)
    cv = cache_attn_v[0].reshape(nseq, WINDOW, D_ATT)
    x1s, h2s, ks, vs, us, route_s, cnt = _sample_main(
        x_sample.reshape(ntok_s, D_MODEL), mod_s, ck, cv, up1, up2, cnt_p,
        g1, g2, win, bias_s, cw, cbias, wpa, wpb, wo, wcat, whi, br, nseq, slen)

    route_all = jnp.concatenate([route_p.reshape(ntok_p, RLANES)[:, :4], route_s[:, :4]], axis=0)
    experts = route_all[:, 0:2].astype(jnp.int32)
    ranks = route_all[:, 2:4].astype(jnp.int32)
    counts = cnt[0, :N_EXPERTS].astype(jnp.int32)
    pcounts = (counts + MOE_BLK - 1) // MOE_BLK * MOE_BLK
    pend = jnp.cumsum(pcounts)
    pstart = pend - pcounts
    eids = jnp.arange(N_EXPERTS, dtype=jnp.int32)
    dest = jnp.sum(jnp.where(experts[..., None] == eids, pstart, 0), axis=-1) + ranks
    d1 = dest[:, 0]
    d2 = dest[:, 1]
    nblocks = (2 * ntok) // MOE_BLK + N_EXPERTS
    blk_start = jnp.arange(nblocks, dtype=jnp.int32) * MOE_BLK
    blk_e = jnp.minimum(jnp.sum((pend[None, :] <= blk_start[:, None]).astype(jnp.int32), axis=1),
                        N_EXPERTS - 1)
    nblk = (pend[-1:] // MOE_BLK).astype(jnp.int32)
    pend0 = jnp.concatenate([jnp.zeros((1,), jnp.int32), pend.astype(jnp.int32)])

    xs = _dispatch(pend0, d1, d2, h2p, h2s, nblocks * MOE_BLK)
    blk_id = jnp.arange(nblocks, dtype=jnp.int32)
    first = (blk_id < nblk[0]) & ((blk_id == 0) | (blk_e != jnp.roll(blk_e, 1)))
    wslot = (jnp.cumsum(first.astype(jnp.int32)) - 1) % 2
    later_first = lax.cummin(jnp.where(first, blk_id, nblocks)[::-1])[::-1]
    next_first = jnp.concatenate([later_first[1:], jnp.full((1,), nblocks, jnp.int32)])
    next_e = jnp.where(next_first < nblocks, blk_e[jnp.minimum(next_first, nblocks - 1)], -1)
    y = _experts(blk_e, nblk, first.astype(jnp.int32), wslot.astype(jnp.int32),
                 next_e.astype(jnp.int32), xs, w_e_gate[0], w_e_up[0], w_e_down[0])

    y_prompt = _combine(d1, d2, y, x1p, route_p, mod_p[:, 5:6, :], gf, 0)
    gate_s = jnp.repeat(mod_s[:, 5, :], slen, axis=0).reshape(1, ntok_s, D_MODEL)
    y_sample = _combine(d1, d2, y, x1s.reshape(1, ntok_s, D_MODEL),
                        route_s.reshape(1, ntok_s, RLANES), gate_s, gf, ntok_p)

    new_k_p = kp.reshape(1, nb, WINDOW, N_HEADS, HEAD_DIM)
    new_v_p = vp.reshape(1, nb, WINDOW, N_HEADS, HEAD_DIM)
    new_conv_p = up8[:, 6:8, :].reshape(1, nb, 2, D_CONV)
    new_k_s = ks.reshape(1, nseq, slen, N_HEADS, HEAD_DIM)
    new_v_s = vs.reshape(1, nseq, slen, N_HEADS, HEAD_DIM)
    new_conv_s = us.reshape(nseq, slen, D_CONV)[:, slen - 2:, :].reshape(1, nseq, 2, D_CONV)
    return (y_prompt, y_sample.reshape(nseq, slen, D_MODEL), new_k_p, new_v_p, new_conv_p,
            new_k_s, new_v_s, new_conv_s)
```

```python
import functools

import numpy as np
import jax
import jax.numpy as jnp
from jax import lax
from jax.experimental import pallas as pl
from jax.experimental.pallas import tpu as pltpu

F32 = jnp.float32
BF16 = jnp.bfloat16
I32 = jnp.int32

D_MODEL = 1024
CHUNK = 64
LEFT = 8
WINDOW = LEFT * CHUNK
BAND = WINDOW + CHUNK
N_HEADS = 8
HEAD_DIM = 64
D_ATT = N_HEADS * HEAD_DIM
QUAD = 256
MAX_REL = 128
D_CONV = 512
N_GROUPS = 4
EPG = 8
N_EXPERTS = 32
D_EXPERT = 512
EPS = 1e-6
NEG = -1e30
LOG2E = float(np.log2(np.e))
Q_SCALE = HEAD_DIM ** -0.5 * LOG2E

TL = 512
RING = WINDOW + TL
MOE_BLK = 512
TM = 512
RLANES = 128
LANES = 128
HALF = D_MODEL // 2
ROW_TILES = HALF // LANES
ISSUE_UNROLL = 8
VMEM_LIMIT = 56 * 1024 * 1024


def _const_spec(shape):
    nd = len(shape)
    return pl.BlockSpec(shape, lambda *_: (0,) * nd, pipeline_mode=pl.Buffered(1))


def _dot(a, b):
    return jnp.dot(a, b, preferred_element_type=F32)


def _sigmoid(x):
    return 1.0 / (1.0 + jnp.exp(-x))


def _rms(x, g):
    ms = jnp.mean(x * x, axis=-1, keepdims=True)
    return x * lax.rsqrt(ms + EPS) * g


def _pack_rows(val):
    lo = lax.bitcast_convert_type(val[:, :HALF], I32) + 0x8000
    hi = lax.bitcast_convert_type(val[:, HALF:], I32) + 0x8000
    return (hi & -65536) | lax.shift_right_logical(lo, 16)


def _unpack_rows(packed):
    lo = lax.bitcast_convert_type(lax.shift_left(packed, 16), F32)
    hi = lax.bitcast_convert_type(packed & -65536, F32)
    return lo, hi


def _store_rows_as_tiles(ref, packed):
    r = packed.shape[0]
    for c in range(ROW_TILES):
        ref[pl.ds(c, r, stride=ROW_TILES), :] = packed[:, c * LANES:(c + 1) * LANES]


def _load_tiles_as_rows(ref, r):
    return jnp.concatenate(
        [ref[pl.ds(c, r, stride=ROW_TILES), :] for c in range(ROW_TILES)], axis=1)


def _tile(ref, row):
    return ref.at[pl.ds(pl.multiple_of(row * ROW_TILES, ROW_TILES), ROW_TILES)]


def _ada_kernel(c_ref, w_ref, b_ref, o_ref):
    c = c_ref[...]
    s = c * _sigmoid(c)
    o_ref[...] = _dot(s.astype(BF16), w_ref[...].astype(BF16)) + b_ref[...]


def _ada(c_all, w_ada, b_ada):
    n = c_all.shape[0]
    nb = 1024
    return pl.pallas_call(
        _ada_kernel,
        grid=(6 * D_MODEL // nb,),
        in_specs=[pl.BlockSpec((n, D_MODEL), lambda i: (0, 0)),
                  pl.BlockSpec((D_MODEL, nb), lambda i: (0, i)),
                  pl.BlockSpec((1, nb), lambda i: (0, i))],
        out_specs=pl.BlockSpec((n, nb), lambda i: (0, i)),
        out_shape=jax.ShapeDtypeStruct((n, 6 * D_MODEL), F32),
        name="ada",
    )(c_all, w_ada, b_ada.reshape(1, -1))


def _attend(q, kb, vb, bias, lim):
    r = q.shape[0]
    nk = kb.shape[0]
    assert r & (r - 1) == 0
    qt = jnp.concatenate([q] * 4, axis=0)
    rowh = lax.broadcasted_iota(jnp.int32, (4 * r, QUAD), 0) >> (r.bit_length() - 1)
    laneh = lax.broadcasted_iota(jnp.int32, (4 * r, QUAD), 1) >> 6
    qm = jnp.where(rowh == laneh, qt, jnp.zeros_like(qt))
    s = lax.dot_general(qm, kb, (((1,), (1,)), ((), ())), preferred_element_type=F32)
    s = s + bias
    if lim is not None:
        col = lax.broadcasted_iota(jnp.int32, (4 * r, nk), 1)
        s = jnp.where(col >= lim, s, NEG)
    m = jnp.max(s, axis=1, keepdims=True)
    p = jnp.exp2(s - m)
    l = jnp.sum(p, axis=1, keepdims=True)
    o = _dot(p.astype(BF16), vb) * (1.0 / l)
    lane_o = lax.broadcasted_iota(jnp.int32, (r, QUAD), 1) >> 6
    out = o[0:r]
    for h in range(1, 4):
        out = jnp.where(lane_o == h, o[h * r:(h + 1) * r], out)
    return out


def _route_stages(h2, wcat_ref, whi_ref, br_ref, cnt, valid=None):
    r = h2.shape[0]
    lane = lax.broadcasted_iota(jnp.int32, (r, RLANES), 1)
    lane_f = lane.astype(F32)
    big = jnp.float32(1000.0)
    v = {}

    def logits():
        hi = h2.astype(BF16)
        lo = (h2 - hi.astype(F32)).astype(BF16)
        z = _dot(hi, wcat_ref[...])
        v["logits"] = z[:, :RLANES] + z[:, RLANES:] + _dot(lo, whi_ref[...]) + br_ref[...]

    def group():
        lg = jnp.where((lane >= N_EXPERTS) & (lane < N_EXPERTS + N_GROUPS), v["logits"], NEG)
        mg = jnp.max(lg, axis=1, keepdims=True)
        v["gi"] = jnp.min(jnp.where(lg == mg, lane_f, big), axis=1, keepdims=True) - N_EXPERTS
        v["pg"] = 1.0 / jnp.sum(jnp.exp(lg - mg), axis=1, keepdims=True)

    def top1():
        grp_of_lane = (lane >> 3).astype(F32)
        le = jnp.where((lane < N_EXPERTS) & (grp_of_lane == v["gi"]), v["logits"], NEG)
        v["m1"] = jnp.max(le, axis=1, keepdims=True)
        v["i1"] = jnp.min(jnp.where(le == v["m1"], lane_f, big), axis=1, keepdims=True)
        v["le"] = le

    def top2():
        sel1 = lane_f == v["i1"]
        le2 = jnp.where(sel1, NEG, v["le"])
        m2 = jnp.max(le2, axis=1, keepdims=True)
        v["i2"] = jnp.min(jnp.where(le2 == m2, lane_f, big), axis=1, keepdims=True)
        rr = jnp.exp(m2 - v["m1"])
        inv = v["pg"] / (1.0 + rr)
        v["w1"] = inv
        v["w2"] = inv * rr
        v["sel1"] = sel1
        v["sel2"] = lane_f == v["i2"]

    def ranks():
        oh = jnp.where(v["sel1"] | v["sel2"], 1.0 if valid is None else valid, 0.0).astype(F32)
        ri = lax.broadcasted_iota(jnp.int32, (r, r), 0)
        ci = lax.broadcasted_iota(jnp.int32, (r, r), 1)
        tri = jnp.where(ri > ci, 1.0, 0.0).astype(BF16)
        v["before"] = _dot(tri, oh.astype(BF16)) + cnt
        v["new_cnt"] = cnt + jnp.sum(oh, axis=0, keepdims=True)

    def assemble():
        r1 = jnp.sum(jnp.where(v["sel1"], v["before"], 0.0), axis=1, keepdims=True)
        r2 = jnp.sum(jnp.where(v["sel2"], v["before"], 0.0), axis=1, keepdims=True)
        route = jnp.where(lane == 0, v["i1"], 0.0)
        route = jnp.where(lane == 1, v["i2"], route)
        route = jnp.where(lane == 2, r1, route)
        route = jnp.where(lane == 3, r2, route)
        route = jnp.where(lane == 4, v["w1"], route)
        route = jnp.where(lane == 5, v["w2"], route)
        return route, v["new_cnt"]

    return [logits, group, top1, top2, ranks, assemble]


def _route(h2, wcat_ref, whi_ref, br_ref, cnt):
    stages = _route_stages(h2, wcat_ref, whi_ref, br_ref, cnt)
    for stage in stages[:-1]:
        stage()
    return stages[-1]()


def _prompt_kernel(x_ref, mod_ref, g1_ref, g2_ref, win_ref, bias_ref, cw_ref, cbias_ref,
                   wpa_ref, wpb_ref, wo_ref, wcat_ref, whi_ref, br_ref,
                   x1_ref, h2_ref, ko_ref, vo_ref, uo_ref, route_ref, cnt_ref,
                   kring, vring, att_s, ucarry, cnt_s, h2_prev, *, nt, ntiles):
    g = pl.program_id(0)

    @pl.when(g == 0)
    def _():
        cnt_s[...] = jnp.zeros_like(cnt_s)
        h2_prev[...] = jnp.zeros_like(h2_prev)

    @pl.when(g < ntiles)
    def _():
        _prompt_tile(g, nt, x_ref, mod_ref, g1_ref, g2_ref, win_ref, bias_ref, cw_ref, cbias_ref,
                     wpa_ref, wpb_ref, wo_ref, wcat_ref, whi_ref, br_ref,
                     x1_ref, h2_ref, ko_ref, vo_ref, uo_ref, route_ref, cnt_ref,
                     kring, vring, att_s, ucarry, cnt_s, h2_prev)

    @pl.when(g == ntiles)
    def _():
        route, new_cnt = _route(h2_prev[...], wcat_ref, whi_ref, br_ref, cnt_s[...])
        route_ref[0] = route
        cnt_ref[...] = new_cnt


def _prompt_tile(g, nt, x_ref, mod_ref, g1_ref, g2_ref, win_ref, bias_ref, cw_ref, cbias_ref,
                 wpa_ref, wpb_ref, wo_ref, wcat_ref, whi_ref, br_ref,
                 x1_ref, h2_ref, ko_ref, vo_ref, uo_ref, route_ref, cnt_ref,
                 kring, vring, att_s, ucarry, cnt_s, h2_prev):
    j = g % nt

    @pl.when(j == 0)
    def _():
        kring[0:WINDOW, :] = jnp.zeros((WINDOW, D_ATT), BF16)
        vring[0:WINDOW, :] = jnp.zeros((WINDOW, D_ATT), BF16)
        ucarry[...] = jnp.zeros_like(ucarry)

    route_stages = _route_stages(h2_prev[...], wcat_ref, whi_ref, br_ref, cnt_s[...],
                                 valid=jnp.where(g > 0, 1.0, 0.0).astype(F32))
    route_stages[0]()

    sh1 = mod_ref[0, 0:1, :]
    sc1 = mod_ref[0, 1:2, :]
    gt1 = mod_ref[0, 2:3, :]
    sh2 = mod_ref[0, 3:4, :]
    sc2 = mod_ref[0, 4:5, :]

    x = x_ref[0]
    h = _rms(x, g1_ref[...]) * (1.0 + sc1) + sh1
    hb = h.astype(BF16)

    qkv = _dot(hb, win_ref[:, 0:3 * D_ATT])
    q = (qkv[:, 0:D_ATT] * Q_SCALE).astype(BF16)
    k = qkv[:, D_ATT:2 * D_ATT]
    v = qkv[:, 2 * D_ATT:3 * D_ATT]
    ko_ref[0] = k
    vo_ref[0] = v
    kring[WINDOW:RING, :] = k.astype(BF16)
    vring[WINDOW:RING, :] = v.astype(BF16)

    base = j * TL
    for c in range(TL // CHUNK):
        lim = WINDOW - (base + c * CHUNK)
        for qd in range(2):
            ls = slice(qd * QUAD, (qd + 1) * QUAD)
            o = _attend(q[c * CHUNK:(c + 1) * CHUNK, ls],
                        kring[c * CHUNK:c * CHUNK + BAND, ls],
                        vring[c * CHUNK:c * CHUNK + BAND, ls],
                        bias_ref[qd], lim)
            att_s[c * CHUNK:(c + 1) * CHUNK, ls] = o.astype(BF16)
        if c + 1 < len(route_stages) - 1:
            route_stages[c + 1]()
    route, new_cnt = route_stages[-1]()
    route_ref[0] = route
    cnt_s[...] = new_cnt
    cnt_ref[...] = new_cnt

    kring[0:WINDOW, :] = kring[TL:RING, :]
    vring[0:WINDOW, :] = vring[TL:RING, :]

    cbcv = _dot(hb, win_ref[:, 3 * D_ATT:3 * D_ATT + 3 * D_CONV])
    cb = cbcv[:, 0:D_CONV]
    u = cbcv[:, D_CONV:2 * D_CONV] * cbcv[:, 2 * D_CONV:3 * D_CONV]
    row = lax.broadcasted_iota(jnp.int32, (8, D_CONV), 0)
    prev = ucarry[...]
    r1 = pltpu.roll(u, 1, axis=0)
    r2 = pltpu.roll(u, 2, axis=0)
    u_m1 = jnp.concatenate(
        [jnp.where(row < 1, pltpu.roll(prev, 1, axis=0), r1[0:8]), r1[8:]], axis=0)
    u_m2 = jnp.concatenate(
        [jnp.where(row < 2, pltpu.roll(prev, 2, axis=0), r2[0:8]), r2[8:]], axis=0)
    yc = cw_ref[0:1, :] * u_m2 + cw_ref[1:2, :] * u_m1 + cw_ref[2:3, :] * u + cbias_ref[...]
    conv_out = (cb * yc).astype(BF16)
    ucarry[...] = u[TL - 8:TL, :]
    uo_ref[0] = u[TL - 8:TL, :]

    gates = _dot(hb, win_ref[:, 3 * D_ATT + 3 * D_CONV:])
    pa = _dot(att_s[...], wpa_ref[...])
    pb = _dot(conv_out, wpb_ref[...])
    mixin = _sigmoid(gates[:, 0:D_MODEL]) * pa + _sigmoid(gates[:, D_MODEL:]) * pb
    mix = _dot(mixin.astype(BF16), wo_ref[...])
    x1 = x + gt1 * mix
    x1_ref[0] = x1
    h2 = _rms(x1, g2_ref[...]) * (1.0 + sc2) + sh2
    _store_rows_as_tiles(h2_ref, _pack_rows(h2))

    h2_prev[...] = h2


def _prompt_main(x, mod, g1, g2, win, bias_q, cw, cbias, wpa, wpb, wo, wcat, whi, br):
    nb, seq, _ = x.shape
    nt = seq // TL
    ntiles = nb * nt
    keep = WINDOW // TL
    cur = lambda g: jnp.minimum(g, ntiles - 1)
    prv = lambda g: jnp.maximum(g - 1, 0)
    tile = lambda g: (cur(g) // nt, cur(g) % nt, 0)
    last = lambda g: (cur(g) // nt, jnp.maximum(cur(g) % nt - (nt - keep), 0), 0)
    perb = lambda g: (cur(g) // nt, 0, 0)
    in_specs = [
        pl.BlockSpec((1, TL, D_MODEL), tile),
        pl.BlockSpec((1, 6, D_MODEL), perb),
        _const_spec(g1.shape), _const_spec(g2.shape), _const_spec(win.shape),
        _const_spec(bias_q.shape), _const_spec(cw.shape), _const_spec(cbias.shape),
        _const_spec(wpa.shape), _const_spec(wpb.shape), _const_spec(wo.shape),
        _const_spec(wcat.shape), _const_spec(whi.shape), _const_spec(br.shape),
    ]
    out_specs = [
        pl.BlockSpec((1, TL, D_MODEL), tile),
        pl.BlockSpec((TL * ROW_TILES, LANES), lambda g: (cur(g), 0)),
        pl.BlockSpec((1, TL, D_ATT), last),
        pl.BlockSpec((1, TL, D_ATT), last),
        pl.BlockSpec((1, 8, D_CONV), perb),
        pl.BlockSpec((1, TL, RLANES), lambda g: (prv(g) // nt, prv(g) % nt, 0)),
        pl.BlockSpec((1, RLANES), lambda g: (0, 0)),
    ]
    out_shape = [
        jax.ShapeDtypeStruct((nb, seq, D_MODEL), F32),
        jax.ShapeDtypeStruct((nb * seq * ROW_TILES, LANES), I32),
        jax.ShapeDtypeStruct((nb, WINDOW, D_ATT), F32),
        jax.ShapeDtypeStruct((nb, WINDOW, D_ATT), F32),
        jax.ShapeDtypeStruct((nb, 8, D_CONV), F32),
        jax.ShapeDtypeStruct((nb, seq, RLANES), F32),
        jax.ShapeDtypeStruct((1, RLANES), F32),
    ]
    scratch = [
        pltpu.VMEM((RING, D_ATT), BF16), pltpu.VMEM((RING, D_ATT), BF16),
        pltpu.VMEM((TL, D_ATT), BF16), pltpu.VMEM((8, D_CONV), F32),
        pltpu.VMEM((1, RLANES), F32), pltpu.VMEM((TL, D_MODEL), F32),
    ]
    return pl.pallas_call(
        functools.partial(_prompt_kernel, nt=nt, ntiles=ntiles),
        grid=(ntiles + 1,),
        in_specs=in_specs, out_specs=out_specs, out_shape=out_shape,
        scratch_shapes=scratch,
        compiler_params=pltpu.CompilerParams(
            dimension_semantics=("arbitrary",), vmem_limit_bytes=VMEM_LIMIT),
        name="prompt_main",
    )(x, mod, g1, g2, win, bias_q, cw, cbias, wpa, wpb, wo, wcat, whi, br)


def _sample_kernel(x_ref, mod_ref, ck_ref, cv_ref, up1_ref, up2_ref, cnt_in_ref,
                   g1_ref, g2_ref, win_ref, bias_ref, cw_ref, cbias_ref,
                   wpa_ref, wpb_ref, wo_ref, wcat_ref, whi_ref, br_ref,
                   x1_ref, h2_ref, ko_ref, vo_ref, uo_ref, route_ref, cnt_ref,
                   h_s, q_s, kn_s, vn_s, kband, vband, att_s, conv_s, h2_s, *, nseq, slen):
    n = pl.program_id(0)
    ntok = nseq * slen

    @pl.when(n == 0)
    def _():
        def norm_body(i, carry):
            rows = pl.ds(pl.multiple_of(i * slen, slen), slen)
            xi = x_ref[rows, :]
            m = mod_ref[i]
            hi = _rms(xi, g1_ref[...]) * (1.0 + m[1:2, :]) + m[0:1, :]
            h_s[rows, :] = hi.astype(BF16)
            return carry
        lax.fori_loop(0, nseq, norm_body, 0)
        hb = h_s[...]
        qkv = _dot(hb, win_ref[:, 0:3 * D_ATT])
        q_s[...] = (qkv[:, 0:D_ATT] * Q_SCALE).astype(BF16)
        k = qkv[:, D_ATT:2 * D_ATT]
        v = qkv[:, 2 * D_ATT:3 * D_ATT]
        ko_ref[...] = k
        vo_ref[...] = v
        kn_s[...] = k.astype(BF16)
        vn_s[...] = v.astype(BF16)

        cbcv = _dot(hb, win_ref[:, 3 * D_ATT:3 * D_ATT + 3 * D_CONV])
        cb = cbcv[:, 0:D_CONV]
        u = cbcv[:, D_CONV:2 * D_CONV] * cbcv[:, 2 * D_CONV:3 * D_CONV]
        pos = lax.broadcasted_iota(jnp.int32, (ntok, D_CONV), 0) & (slen - 1)
        u_m1 = jnp.where(pos < 1, up1_ref[...], pltpu.roll(u, 1, axis=0))
        u_m2 = jnp.where(pos < 2, up2_ref[...], pltpu.roll(u, 2, axis=0))
        yc = cw_ref[0:1, :] * u_m2 + cw_ref[1:2, :] * u_m1 + cw_ref[2:3, :] * u + cbias_ref[...]
        conv_s[...] = (cb * yc).astype(BF16)
        uo_ref[...] = u

    rows = pl.ds(pl.multiple_of(n * slen, slen), slen)
    kband[0:WINDOW, :] = ck_ref[0].astype(BF16)
    vband[0:WINDOW, :] = cv_ref[0].astype(BF16)
    kband[WINDOW:WINDOW + slen, :] = kn_s[rows, :]
    vband[WINDOW:WINDOW + slen, :] = vn_s[rows, :]
    qn = q_s[rows, :]
    outs = []
    for qd in range(2):
        ls = slice(qd * QUAD, (qd + 1) * QUAD)
        outs.append(_attend(qn[:, ls], kband[:, ls], vband[:, ls], bias_ref[qd], None))
    att_s[rows, :] = jnp.concatenate(outs, axis=1).astype(BF16)

    @pl.when(n == nseq - 1)
    def _():
        gates = _dot(h_s[...], win_ref[:, 3 * D_ATT + 3 * D_CONV:])
        pa = _dot(att_s[...], wpa_ref[...])
        pb = _dot(conv_s[...], wpb_ref[...])
        mixin = _sigmoid(gates[:, 0:D_MODEL]) * pa + _sigmoid(gates[:, D_MODEL:]) * pb
        x1_ref[...] = _dot(mixin.astype(BF16), wo_ref[...])

        def res_body(i, carry):
            r = pl.ds(pl.multiple_of(i * slen, slen), slen)
            m = mod_ref[i]
            x1 = x_ref[r, :] + m[2:3, :] * x1_ref[r, :]
            x1_ref[r, :] = x1
            h2_s[r, :] = _rms(x1, g2_ref[...]) * (1.0 + m[4:5, :]) + m[3:4, :]
            return carry
        lax.fori_loop(0, nseq, res_body, 0)

        h2 = h2_s[...]
        _store_rows_as_tiles(h2_ref, _pack_rows(h2))
        route, new_cnt = _route(h2, wcat_ref, whi_ref, br_ref, cnt_in_ref[...])
        route_ref[...] = route
        cnt_ref[...] = new_cnt


def _sample_main(x2d, mod, ck, cv, up1, up2, cnt_in, g1, g2, win, bias_s, cw, cbias,
                 wpa, wpb, wo, wcat, whi, br, nseq, slen):
    ntok = nseq * slen
    args = (x2d, mod, ck, cv, up1, up2, cnt_in, g1, g2, win, bias_s, cw, cbias,
            wpa, wpb, wo, wcat, whi, br)
    in_specs = []
    for idx, a in enumerate(args):
        if idx in (2, 3):
            in_specs.append(pl.BlockSpec((1, WINDOW, D_ATT), lambda n: (n, 0, 0)))
        else:
            in_specs.append(_const_spec(a.shape))
    whole = lambda shape: pl.BlockSpec(shape, lambda n: (0,) * len(shape))
    outs = [((ntok, D_MODEL), F32), ((ntok * ROW_TILES, LANES), I32), ((ntok, D_ATT), F32),
            ((ntok, D_ATT), F32), ((ntok, D_CONV), F32), ((ntok, RLANES), F32), ((1, RLANES), F32)]
    scratch = [
        pltpu.VMEM((ntok, D_MODEL), BF16), pltpu.VMEM((ntok, D_ATT), BF16),
        pltpu.VMEM((ntok, D_ATT), BF16), pltpu.VMEM((ntok, D_ATT), BF16),
        pltpu.VMEM((WINDOW + slen, D_ATT), BF16), pltpu.VMEM((WINDOW + slen, D_ATT), BF16),
        pltpu.VMEM((ntok, D_ATT), BF16), pltpu.VMEM((ntok, D_CONV), BF16),
        pltpu.VMEM((ntok, D_MODEL), F32),
    ]
    return pl.pallas_call(
        functools.partial(_sample_kernel, nseq=nseq, slen=slen),
        grid=(nseq,),
        in_specs=in_specs,
        out_specs=[whole(s) for s, _ in outs],
        out_shape=[jax.ShapeDtypeStruct(s, d) for s, d in outs],
        scratch_shapes=scratch,
        compiler_params=pltpu.CompilerParams(
            dimension_semantics=("arbitrary",), vmem_limit_bytes=VMEM_LIMIT),
        name="sample_main",
    )(*args)


def _issue_rows(n, body):
    def group(g, carry):
        for u in range(ISSUE_UNROLL):
            body(g * ISSUE_UNROLL + u, u)
        return carry
    lax.fori_loop(0, n // ISSUE_UNROLL, group, 0)


def _dispatch_kernel(pend_ref, d1_ref, d2_ref, hp_ref, hs_ref, xs_out, zbuf, sem, zsem, *,
                     np_tiles, nblocks):
    i = pl.program_id(0)
    blk_rows = MOE_BLK * ROW_TILES

    @pl.when(i == 0)
    def _():
        zbuf[...] = jnp.zeros_like(zbuf)

        def zcopy(e):
            start = pl.multiple_of((pend_ref[e + 1] - MOE_BLK) * ROW_TILES, blk_rows)
            return pltpu.make_async_copy(zbuf, xs_out.at[pl.ds(start, blk_rows)], zsem)

        def zstart(e, carry):
            @pl.when(pend_ref[e + 1] > pend_ref[e])
            def _():
                zcopy(e).start()
            return carry

        def zwait(e, carry):
            @pl.when(pend_ref[e + 1] > pend_ref[e])
            def _():
                zcopy(e).wait()
            return carry
        lax.fori_loop(0, N_EXPERTS, zstart, 0)
        lax.fori_loop(0, N_EXPERTS, zwait, 0)

        def tcopy(bk):
            start = pl.multiple_of(bk * blk_rows, blk_rows)
            return pltpu.make_async_copy(zbuf, xs_out.at[pl.ds(start, blk_rows)], zsem)

        def tstart(bk, carry):
            tcopy(bk).start()
            return carry

        def twait(bk, carry):
            tcopy(bk).wait()
            return carry
        used = pend_ref[N_EXPERTS] // MOE_BLK
        lax.fori_loop(used, nblocks, tstart, 0)
        lax.fori_loop(used, nblocks, twait, 0)

    def scatter_tile(src):
        def row(r, u):
            pltpu.make_async_copy(_tile(src, r), _tile(xs_out, d1_ref[r]), sem).start(priority=u % 2)
            pltpu.make_async_copy(_tile(src, r), _tile(xs_out, d2_ref[r]), sem).start(
                priority=(u + 1) % 2)
        _issue_rows(TM, row)
        for _ in range(2):
            pltpu.make_async_copy(src, xs_out.at[pl.ds(0, TM * ROW_TILES)], sem).wait()

    @pl.when(i < np_tiles)
    def _():
        scatter_tile(hp_ref)

    @pl.when(i >= np_tiles)
    def _():
        scatter_tile(hs_ref)


def _dispatch(pend, d1, d2, h2p, h2s, nslots):
    np_tiles = h2p.shape[0] // (TM * ROW_TILES)
    ns_tiles = h2s.shape[0] // (TM * ROW_TILES)
    smem_tile = pl.BlockSpec((TM,), lambda i, *_: (i,), memory_space=pltpu.SMEM)
    rows = TM * ROW_TILES
    return pl.pallas_call(
        functools.partial(_dispatch_kernel, np_tiles=np_tiles, nblocks=nslots // MOE_BLK),
        grid_spec=pltpu.PrefetchScalarGridSpec(
            num_scalar_prefetch=1,
            grid=(np_tiles + ns_tiles,),
            in_specs=[smem_tile, smem_tile,
                      pl.BlockSpec((rows, LANES), lambda i, *_: (jnp.minimum(i, np_tiles - 1), 0)),
                      pl.BlockSpec((rows, LANES), lambda i, *_: (jnp.maximum(i - np_tiles, 0), 0))],
            out_specs=pl.BlockSpec(memory_space=pl.ANY),
            scratch_shapes=[pltpu.VMEM((MOE_BLK * ROW_TILES, LANES), I32),
                            pltpu.SemaphoreType.DMA(()), pltpu.SemaphoreType.DMA(())],
        ),
        out_shape=jax.ShapeDtypeStruct((nslots * ROW_TILES, LANES), I32),
        compiler_params=pltpu.CompilerParams(dimension_semantics=("arbitrary",)),
        name="dispatch",
    )(pend, d1, d2, h2p, h2s)


def _expert_kernel(blk_e_ref, nblk_ref, first_ref, wslot_ref, next_e_ref,
                   xs_ref, wg_hbm, wu_hbm, wd_hbm, y_ref,
                   wg_f, wu_f, wd_f, wg_b, wu_b, wd_b, wsem):
    i = pl.program_id(0)
    live = i < nblk_ref[0]

    def weight_copies(e, slot):
        return (pltpu.make_async_copy(wg_hbm.at[e], wg_f.at[slot], wsem.at[slot, 0]),
                pltpu.make_async_copy(wu_hbm.at[e], wu_f.at[slot], wsem.at[slot, 1]),
                pltpu.make_async_copy(wd_hbm.at[e], wd_f.at[slot], wsem.at[slot, 2]))

    @pl.when(i == 0)
    def _():
        for cp in weight_copies(blk_e_ref[0], 0):
            cp.start()

    @pl.when(live & (first_ref[i] == 1))
    def _():
        slot = wslot_ref[i]
        for cp in weight_copies(blk_e_ref[i], slot):
            cp.wait()
        wg_b[...] = wg_f[slot].astype(BF16)
        wu_b[...] = wu_f[slot].astype(BF16)
        wd_b[...] = wd_f[slot].astype(BF16)

        @pl.when(next_e_ref[i] >= 0)
        def _():
            for cp in weight_copies(next_e_ref[i], 1 - slot):
                cp.start()

    @pl.when(live)
    def _():
        x_lo, x_hi = _unpack_rows(_load_tiles_as_rows(xs_ref, MOE_BLK))
        x_lo = x_lo.astype(BF16)
        x_hi = x_hi.astype(BF16)
        g = _dot(x_lo, wg_b[0:HALF, :]) + _dot(x_hi, wg_b[HALF:, :])
        u = _dot(x_lo, wu_b[0:HALF, :]) + _dot(x_hi, wu_b[HALF:, :])
        a = (g * _sigmoid(g)) * u
        _store_rows_as_tiles(y_ref, _pack_rows(_dot(a.astype(BF16), wd_b[...])))

    @pl.when(jnp.logical_not(live))
    def _():
        y_ref[...] = jnp.zeros_like(y_ref)


def _experts(blk_e, nblk, first, wslot, next_e, xs, wg, wu, wd):
    blk_rows = MOE_BLK * ROW_TILES
    nblocks = xs.shape[0] // blk_rows
    row_map = lambda i, be, nb, *_: (jnp.minimum(i, nb[0] - 1), 0)
    any_spec = pl.BlockSpec(memory_space=pl.ANY)
    return pl.pallas_call(
        _expert_kernel,
        grid_spec=pltpu.PrefetchScalarGridSpec(
            num_scalar_prefetch=5,
            grid=(nblocks,),
            in_specs=[pl.BlockSpec((blk_rows, LANES), row_map), any_spec, any_spec, any_spec],
            out_specs=pl.BlockSpec((blk_rows, LANES), lambda i, *_: (i, 0)),
            scratch_shapes=[pltpu.VMEM((2, D_MODEL, D_EXPERT), F32),
                            pltpu.VMEM((2, D_MODEL, D_EXPERT), F32),
                            pltpu.VMEM((2, D_EXPERT, D_MODEL), F32),
                            pltpu.VMEM((D_MODEL, D_EXPERT), BF16),
                            pltpu.VMEM((D_MODEL, D_EXPERT), BF16),
                            pltpu.VMEM((D_EXPERT, D_MODEL), BF16),
                            pltpu.SemaphoreType.DMA((2, 3))],
        ),
        out_shape=jax.ShapeDtypeStruct(xs.shape, I32),
        compiler_params=pltpu.CompilerParams(
            dimension_semantics=("arbitrary",), vmem_limit_bytes=VMEM_LIMIT),
        name="experts",
    )(blk_e, nblk, first, wslot, next_e, xs, wg, wu, wd)


def _combine_kernel(d1_ref, d2_ref, y_hbm, x1_ref, route_ref, gate_ref, gf_ref, o_ref,
                    buf1, buf2, sem):
    def row(r, u):
        pltpu.make_async_copy(_tile(y_hbm, d1_ref[r]), _tile(buf1, r), sem).start(priority=u % 2)
        pltpu.make_async_copy(_tile(y_hbm, d2_ref[r]), _tile(buf2, r), sem).start(
            priority=(u + 1) % 2)
    _issue_rows(TM, row)
    for buf in (buf1, buf2):
        pltpu.make_async_copy(y_hbm.at[pl.ds(0, TM * ROW_TILES)], buf, sem).wait()
    route = route_ref[0]
    w1 = route[:, 4:5]
    w2 = route[:, 5:6]
    a_lo, a_hi = _unpack_rows(_load_tiles_as_rows(buf1, TM))
    b_lo, b_hi = _unpack_rows(_load_tiles_as_rows(buf2, TM))
    ffn = jnp.concatenate([w1 * a_lo + w2 * b_lo, w1 * a_hi + w2 * b_hi], axis=1)
    x2 = x1_ref[0] + gate_ref[0] * ffn
    o_ref[0] = _rms(x2, gf_ref[...])


def _combine(d1, d2, y, x1, route, gate, gf, tok_base):
    nb, seq, _ = x1.shape
    nt = seq // TM
    blk0 = tok_base // TM
    smem_tile = pl.BlockSpec((TM,), lambda b, j: (blk0 + b * nt + j,), memory_space=pltpu.SMEM)
    tile = lambda b, j: (b, j, 0)
    grows = gate.shape[1]
    gate_spec = (pl.BlockSpec((1, 1, D_MODEL), lambda b, j: (b, 0, 0)) if grows == 1
                 else pl.BlockSpec((1, TM, D_MODEL), tile))
    return pl.pallas_call(
        _combine_kernel,
        grid=(nb, nt),
        in_specs=[smem_tile, smem_tile, pl.BlockSpec(memory_space=pl.ANY),
                  pl.BlockSpec((1, TM, D_MODEL), tile),
                  pl.BlockSpec((1, TM, RLANES), tile),
                  gate_spec,
                  pl.BlockSpec((1, D_MODEL), lambda b, j: (0, 0))],
        out_specs=pl.BlockSpec((1, TM, D_MODEL), tile),
        out_shape=jax.ShapeDtypeStruct(x1.shape, F32),
        scratch_shapes=[pltpu.VMEM((TM * ROW_TILES, LANES), I32),
                        pltpu.VMEM((TM * ROW_TILES, LANES), I32),
                        pltpu.SemaphoreType.DMA(())],
        compiler_params=pltpu.CompilerParams(
            dimension_semantics=("arbitrary", "arbitrary"), vmem_limit_bytes=VMEM_LIMIT),
        name="combine",
    )(d1, d2, y, x1, route, gate, gf)


def _band_bias(rel_bias, rows, keys):
    n = rows - 1 + keys
    dist = WINDOW + rows - 1 - np.arange(n + 1)
    flipped = rel_bias[:, np.clip(dist, -MAX_REL, MAX_REL) + MAX_REL]
    skew = jnp.tile(flipped, (1, rows))[:, :rows * n].reshape(N_HEADS, rows, n)
    b = skew[:, :, rows - 1:rows - 1 + keys]
    return b.reshape(2, 4 * rows, keys)


def kernel(x_prompt, x_sample, cache_attn_k, cache_attn_v, state_conv, c_prompt, c_sample,
           w_ada, b_ada, norm1_g, norm2_g, w_in, rel_bias, conv_w, conv_b, w_pa, w_pb, w_o,
           w_group, b_group, w_expert, b_expert, w_e_gate, w_e_up, w_e_down, final_g):
    assert w_ada.shape[0] == 1, "single trunk layer"
    nb, seq, _ = x_prompt.shape
    nseq, slen, _ = x_sample.shape
    ntok_p = nb * seq
    ntok_s = nseq * slen
    ntok = ntok_p + ntok_s
    assert seq % TL == 0 and WINDOW % TL == 0 and seq % TM == 0 and ntok_s % TM == 0
    assert slen >= 2 and slen & (slen - 1) == 0 and slen % 16 == 0

    n_c = nb + nseq
    n_pad = -(-n_c // 8) * 8
    c_all = jnp.concatenate([c_prompt, c_sample, jnp.zeros((n_pad - n_c, D_MODEL), F32)], axis=0)
    mod = _ada(c_all, w_ada[0], b_ada[0]).reshape(n_pad, 6, D_MODEL)
    mod_p = mod[:nb]
    mod_s = mod[nb:n_c]

    win = w_in[0].astype(BF16)
    wpa = w_pa[0].astype(BF16)
    wpb = w_pb[0].astype(BF16)
    wo = w_o[0].astype(BF16)
    g1 = norm1_g[0].reshape(1, D_MODEL)
    g2 = norm2_g[0].reshape(1, D_MODEL)
    gf = final_g.reshape(1, D_MODEL)
    cw = jnp.concatenate([conv_w[0], jnp.zeros((8 - conv_w.shape[1], D_CONV), F32)], axis=0)
    cbias = conv_b[0].reshape(1, D_CONV)
    wr = jnp.concatenate([w_expert[0], w_group[0],
                          jnp.zeros((D_MODEL, RLANES - N_EXPERTS - N_GROUPS), F32)], axis=1)
    whi = wr.astype(BF16)
    wlo = (wr - whi.astype(F32)).astype(BF16)
    wcat = jnp.concatenate([whi, wlo], axis=1)
    br = jnp.concatenate([b_expert[0], b_group[0],
                          jnp.zeros((RLANES - N_EXPERTS - N_GROUPS,), F32)]).reshape(1, RLANES)
    bias_p = _band_bias(rel_bias[0] * LOG2E, CHUNK, BAND)
    bias_s = _band_bias(rel_bias[0] * LOG2E, slen, WINDOW + slen)

    x1p, h2p, kp, vp, up8, route_p, cnt_p = _prompt_main(
        x_prompt, mod_p, g1, g2, win, bias_p, cw, cbias, wpa, wpb, wo, wcat, whi, br)

    st = state_conv[0]
    up1 = jnp.zeros((nseq, slen, D_CONV), F32).at[:, 0].set(st[:, 1]).reshape(ntok_s, D_CONV)
    up2 = (jnp.zeros((nseq, slen, D_CONV), F32).at[:, 0].set(st[:, 0]).at[:, 1].set(st[:, 1])
           .reshape(ntok_s, D_CONV))
    ck = cache_attn_k[0].reshape(nseq, WINDOW, D_ATT)
    cv = cache_attn_v[0].reshape(nseq, WINDOW, D_ATT)
    x1s, h2s, ks, vs, us, route_s, cnt = _sample_main(
        x_sample.reshape(ntok_s, D_MODEL), mod_s, ck, cv, up1, up2, cnt_p,
        g1, g2, win, bias_s, cw, cbias, wpa, wpb, wo, wcat, whi, br, nseq, slen)

    route_all = jnp.concatenate([route_p.reshape(ntok_p, RLANES)[:, :4], route_s[:, :4]], axis=0)
    experts = route_all[:, 0:2].astype(jnp.int32)
    ranks = route_all[:, 2:4].astype(jnp.int32)
    counts = cnt[0, :N_EXPERTS].astype(jnp.int32)
    pcounts = (counts + MOE_BLK - 1) // MOE_BLK * MOE_BLK
    pend = jnp.cumsum(pcounts)
    pstart = pend - pcounts
    eids = jnp.arange(N_EXPERTS, dtype=jnp.int32)
    dest = jnp.sum(jnp.where(experts[..., None] == eids, pstart, 0), axis=-1) + ranks
    d1 = dest[:, 0]
    d2 = dest[:, 1]
    nblocks = (2 * ntok) // MOE_BLK + N_EXPERTS
    blk_start = jnp.arange(nblocks, dtype=jnp.int32) * MOE_BLK
    blk_e = jnp.minimum(jnp.sum((pend[None, :] <= blk_start[:, None]).astype(jnp.int32), axis=1),
                        N_EXPERTS - 1)
    nblk = (pend[-1:] // MOE_BLK).astype(jnp.int32)
    pend0 = jnp.concatenate([jnp.zeros((1,), jnp.int32), pend.astype(jnp.int32)])

    xs = _dispatch(pend0, d1, d2, h2p, h2s, nblocks * MOE_BLK)
    blk_id = jnp.arange(nblocks, dtype=jnp.int32)
    first = (blk_id < nblk[0]) & ((blk_id == 0) | (blk_e != jnp.roll(blk_e, 1)))
    wslot = (jnp.cumsum(first.astype(jnp.int32)) - 1) % 2
    later_first = lax.cummin(jnp.where(first, blk_id, nblocks)[::-1])[::-1]
    next_first = jnp.concatenate([later_first[1:], jnp.full((1,), nblocks, jnp.int32)])
    next_e = jnp.where(next_first < nblocks, blk_e[jnp.minimum(next_first, nblocks - 1)], -1)
    y = _experts(blk_e, nblk, first.astype(jnp.int32), wslot.astype(jnp.int32),
                 next_e.astype(jnp.int32), xs, w_e_gate[0], w_e_up[0], w_e_down[0])

    y_prompt = _combine(d1, d2, y, x1p, route_p, mod_p[:, 5:6, :], gf, 0)
    gate_s = jnp.repeat(mod_s[:, 5, :], slen, axis=0).reshape(1, ntok_s, D_MODEL)
    y_sample = _combine(d1, d2, y, x1s.reshape(1, ntok_s, D_MODEL),
                        route_s.reshape(1, ntok_s, RLANES), gate_s, gf, ntok_p)

    new_k_p = kp.reshape(1, nb, WINDOW, N_HEADS, HEAD_DIM)
    new_v_p = vp.reshape(1, nb, WINDOW, N_HEADS, HEAD_DIM)
    new_conv_p = up8[:, 6:8, :].reshape(1, nb, 2, D_CONV)
    new_k_s = ks.reshape(1, nseq, slen, N_HEADS, HEAD_DIM)
    new_v_s = vs.reshape(1, nseq, slen, N_HEADS, HEAD_DIM)
    new_conv_s = us.reshape(nseq, slen, D_CONV)[:, slen - 2:, :].reshape(1, nseq, 2, D_CONV)
    return (y_prompt, y_sample.reshape(nseq, slen, D_MODEL), new_k_p, new_v_p, new_conv_p,
            new_k_s, new_v_s, new_conv_s)
```

```python
import functools

import numpy as np
import jax
import jax.numpy as jnp
from jax import lax
from jax.experimental import pallas as pl
from jax.experimental.pallas import tpu as pltpu

F32 = jnp.float32
BF16 = jnp.bfloat16
I32 = jnp.int32

D_MODEL = 1024
CHUNK = 64
LEFT = 8
WINDOW = LEFT * CHUNK
BAND = WINDOW + CHUNK
N_HEADS = 8
HEAD_DIM = 64
D_ATT = N_HEADS * HEAD_DIM
QUAD = 256
MAX_REL = 128
D_CONV = 512
N_GROUPS = 4
EPG = 8
N_EXPERTS = 32
D_EXPERT = 512
EPS = 1e-6
NEG = -1e30
LOG2E = float(np.log2(np.e))
Q_SCALE = HEAD_DIM ** -0.5 * LOG2E

TL = 512
RING = WINDOW + TL
MOE_BLK = 512
TM = 512
RLANES = 128
LANES = 128
HALF = D_MODEL // 2
ROW_TILES = HALF // LANES
ISSUE_UNROLL = 8
VMEM_LIMIT = 56 * 1024 * 1024


def _const_spec(shape):
    nd = len(shape)
    return pl.BlockSpec(shape, lambda *_: (0,) * nd, pipeline_mode=pl.Buffered(1))


def _dot(a, b):
    return jnp.dot(a, b, preferred_element_type=F32)


def _sigmoid(x):
    return 1.0 / (1.0 + jnp.exp(-x))


def _rms(x, g):
    ms = jnp.mean(x * x, axis=-1, keepdims=True)
    return x * lax.rsqrt(ms + EPS) * g


def _pack_rows(val):
    lo = lax.bitcast_convert_type(val[:, :HALF], I32) + 0x8000
    hi = lax.bitcast_convert_type(val[:, HALF:], I32) + 0x8000
    return (hi & -65536) | lax.shift_right_logical(lo, 16)


def _unpack_rows(packed):
    lo = lax.bitcast_convert_type(lax.shift_left(packed, 16), F32)
    hi = lax.bitcast_convert_type(packed & -65536, F32)
    return lo, hi


def _store_rows_as_tiles(ref, packed):
    r = packed.shape[0]
    for c in range(ROW_TILES):
        ref[pl.ds(c, r, stride=ROW_TILES), :] = packed[:, c * LANES:(c + 1) * LANES]


def _load_tiles_as_rows(ref, r):
    return jnp.concatenate(
        [ref[pl.ds(c, r, stride=ROW_TILES), :] for c in range(ROW_TILES)], axis=1)


def _tile(ref, row):
    return ref.at[pl.ds(pl.multiple_of(row * ROW_TILES, ROW_TILES), ROW_TILES)]


def _ada_kernel(c_ref, w_ref, b_ref, o_ref):
    c = c_ref[...]
    s = c * _sigmoid(c)
    o_ref[...] = _dot(s.astype(BF16), w_ref[...].astype(BF16)) + b_ref[...]


def _ada(c_all, w_ada, b_ada):
    n = c_all.shape[0]
    nb = 1024
    return pl.pallas_call(
        _ada_kernel,
        grid=(6 * D_MODEL // nb,),
        in_specs=[pl.BlockSpec((n, D_MODEL), lambda i: (0, 0)),
                  pl.BlockSpec((D_MODEL, nb), lambda i: (0, i)),
                  pl.BlockSpec((1, nb), lambda i: (0, i))],
        out_specs=pl.BlockSpec((n, nb), lambda i: (0, i)),
        out_shape=jax.ShapeDtypeStruct((n, 6 * D_MODEL), F32),
        name="ada",
    )(c_all, w_ada, b_ada.reshape(1, -1))


def _attend(q, kb, vb, bias, lim):
    r = q.shape[0]
    nk = kb.shape[0]
    assert r & (r - 1) == 0
    qt = jnp.concatenate([q] * 4, axis=0)
    rowh = lax.broadcasted_iota(jnp.int32, (4 * r, QUAD), 0) >> (r.bit_length() - 1)
    laneh = lax.broadcasted_iota(jnp.int32, (4 * r, QUAD), 1) >> 6
    qm = jnp.where(rowh == laneh, qt, jnp.zeros_like(qt))
    s = lax.dot_general(qm, kb, (((1,), (1,)), ((), ())), preferred_element_type=F32)
    valid = None
    if lim is not None:
        valid = lax.broadcasted_iota(jnp.int32, (r, nk), 1) >= lim
    ps, inv_l = [], []
    for h in range(4):
        sh = s[h * r:(h + 1) * r] + bias[h * r:(h + 1) * r]
        if valid is not None:
            sh = jnp.where(valid, sh, NEG)
        m = jnp.max(sh, axis=1, keepdims=True)
        ph = jnp.exp2(sh - m)
        inv_l.append(1.0 / jnp.sum(ph, axis=1, keepdims=True))
        ps.append(ph.astype(BF16))
    o = _dot(jnp.concatenate(ps, axis=0), vb)
    lane_o = lax.broadcasted_iota(jnp.int32, (r, QUAD), 1) >> 6
    out = o[0:r] * inv_l[0]
    for h in range(1, 4):
        out = jnp.where(lane_o == h, o[h * r:(h + 1) * r] * inv_l[h], out)
    return out


def _route_stages(h2, wcat_ref, whi_ref, br_ref, cnt, valid=None):
    r = h2.shape[0]
    lane = lax.broadcasted_iota(jnp.int32, (r, RLANES), 1)
    lane_f = lane.astype(F32)
    big = jnp.float32(1000.0)
    v = {}

    def logits():
        hi = h2.astype(BF16)
        lo = (h2 - hi.astype(F32)).astype(BF16)
        z = _dot(hi, wcat_ref[...])
        v["logits"] = z[:, :RLANES] + z[:, RLANES:] + _dot(lo, whi_ref[...]) + br_ref[...]

    def group():
        lg = jnp.where((lane >= N_EXPERTS) & (lane < N_EXPERTS + N_GROUPS), v["logits"], NEG)
        mg = jnp.max(lg, axis=1, keepdims=True)
        v["gi"] = jnp.min(jnp.where(lg == mg, lane_f, big), axis=1, keepdims=True) - N_EXPERTS
        v["pg"] = 1.0 / jnp.sum(jnp.exp(lg - mg), axis=1, keepdims=True)

    def top1():
        grp_of_lane = (lane >> 3).astype(F32)
        le = jnp.where((lane < N_EXPERTS) & (grp_of_lane == v["gi"]), v["logits"], NEG)
        v["m1"] = jnp.max(le, axis=1, keepdims=True)
        v["i1"] = jnp.min(jnp.where(le == v["m1"], lane_f, big), axis=1, keepdims=True)
        v["le"] = le

    def top2():
        sel1 = lane_f == v["i1"]
        le2 = jnp.where(sel1, NEG, v["le"])
        m2 = jnp.max(le2, axis=1, keepdims=True)
        v["i2"] = jnp.min(jnp.where(le2 == m2, lane_f, big), axis=1, keepdims=True)
        rr = jnp.exp(m2 - v["m1"])
        inv = v["pg"] / (1.0 + rr)
        v["w1"] = inv
        v["w2"] = inv * rr
        v["sel1"] = sel1
        v["sel2"] = lane_f == v["i2"]

    def ranks():
        oh = jnp.where(v["sel1"] | v["sel2"], 1.0 if valid is None else valid, 0.0).astype(F32)
        ri = lax.broadcasted_iota(jnp.int32, (r, r), 0)
        ci = lax.broadcasted_iota(jnp.int32, (r, r), 1)
        tri = jnp.where(ri > ci, 1.0, 0.0).astype(BF16)
        v["before"] = _dot(tri, oh.astype(BF16)) + cnt
        v["new_cnt"] = cnt + jnp.sum(oh, axis=0, keepdims=True)

    def assemble():
        r1 = jnp.sum(jnp.where(v["sel1"], v["before"], 0.0), axis=1, keepdims=True)
        r2 = jnp.sum(jnp.where(v["sel2"], v["before"], 0.0), axis=1, keepdims=True)
        route = jnp.where(lane == 0, v["i1"], 0.0)
        route = jnp.where(lane == 1, v["i2"], route)
        route = jnp.where(lane == 2, r1, route)
        route = jnp.where(lane == 3, r2, route)
        route = jnp.where(lane == 4, v["w1"], route)
        route = jnp.where(lane == 5, v["w2"], route)
        return route, v["new_cnt"]

    return [logits, group, top1, top2, ranks, assemble]


def _route(h2, wcat_ref, whi_ref, br_ref, cnt):
    stages = _route_stages(h2, wcat_ref, whi_ref, br_ref, cnt)
    for stage in stages[:-1]:
        stage()
    return stages[-1]()


def _prompt_kernel(x_ref, mod_ref, g1_ref, g2_ref, win_ref, bias_ref, cw_ref, cbias_ref,
                   wpa_ref, wpb_ref, wo_ref, wcat_ref, whi_ref, br_ref,
                   x1_ref, h2_ref, ko_ref, vo_ref, uo_ref, route_ref, cnt_ref,
                   kring, vring, att_s, ucarry, cnt_s, h2_prev, *, nt, ntiles):
    g = pl.program_id(0)

    @pl.when(g == 0)
    def _():
        cnt_s[...] = jnp.zeros_like(cnt_s)
        h2_prev[...] = jnp.zeros_like(h2_prev)

    @pl.when(g < ntiles)
    def _():
        _prompt_tile(g, nt, x_ref, mod_ref, g1_ref, g2_ref, win_ref, bias_ref, cw_ref, cbias_ref,
                     wpa_ref, wpb_ref, wo_ref, wcat_ref, whi_ref, br_ref,
                     x1_ref, h2_ref, ko_ref, vo_ref, uo_ref, route_ref, cnt_ref,
                     kring, vring, att_s, ucarry, cnt_s, h2_prev)

    @pl.when(g == ntiles)
    def _():
        route, new_cnt = _route(h2_prev[...], wcat_ref, whi_ref, br_ref, cnt_s[...])
        route_ref[0] = route
        cnt_ref[...] = new_cnt


def _prompt_tile(g, nt, x_ref, mod_ref, g1_ref, g2_ref, win_ref, bias_ref, cw_ref, cbias_ref,
                 wpa_ref, wpb_ref, wo_ref, wcat_ref, whi_ref, br_ref,
                 x1_ref, h2_ref, ko_ref, vo_ref, uo_ref, route_ref, cnt_ref,
                 kring, vring, att_s, ucarry, cnt_s, h2_prev):
    j = g % nt

    @pl.when(j == 0)
    def _():
        kring[0:WINDOW, :] = jnp.zeros((WINDOW, D_ATT), BF16)
        vring[0:WINDOW, :] = jnp.zeros((WINDOW, D_ATT), BF16)
        ucarry[...] = jnp.zeros_like(ucarry)

    route_stages = _route_stages(h2_prev[...], wcat_ref, whi_ref, br_ref, cnt_s[...],
                                 valid=jnp.where(g > 0, 1.0, 0.0).astype(F32))
    route_stages[0]()

    sh1 = mod_ref[0, 0:1, :]
    sc1 = mod_ref[0, 1:2, :]
    gt1 = mod_ref[0, 2:3, :]
    sh2 = mod_ref[0, 3:4, :]
    sc2 = mod_ref[0, 4:5, :]

    x = x_ref[0]
    h = _rms(x, g1_ref[...]) * (1.0 + sc1) + sh1
    hb = h.astype(BF16)

    qkv = _dot(hb, win_ref[:, 0:3 * D_ATT])
    q = (qkv[:, 0:D_ATT] * Q_SCALE).astype(BF16)
    k = qkv[:, D_ATT:2 * D_ATT]
    v = qkv[:, 2 * D_ATT:3 * D_ATT]
    ko_ref[0] = k
    vo_ref[0] = v
    kring[WINDOW:RING, :] = k.astype(BF16)
    vring[WINDOW:RING, :] = v.astype(BF16)

    base = j * TL
    for c in range(TL // CHUNK):
        lim = WINDOW - (base + c * CHUNK)
        for qd in range(2):
            ls = slice(qd * QUAD, (qd + 1) * QUAD)
            o = _attend(q[c * CHUNK:(c + 1) * CHUNK, ls],
                        kring[c * CHUNK:c * CHUNK + BAND, ls],
                        vring[c * CHUNK:c * CHUNK + BAND, ls],
                        bias_ref[qd], lim)
            att_s[c * CHUNK:(c + 1) * CHUNK, ls] = o.astype(BF16)
        if c + 1 < len(route_stages) - 1:
            route_stages[c + 1]()
    route, new_cnt = route_stages[-1]()
    route_ref[0] = route
    cnt_s[...] = new_cnt
    cnt_ref[...] = new_cnt

    kring[0:WINDOW, :] = kring[TL:RING, :]
    vring[0:WINDOW, :] = vring[TL:RING, :]

    cbcv = _dot(hb, win_ref[:, 3 * D_ATT:3 * D_ATT + 3 * D_CONV])
    cb = cbcv[:, 0:D_CONV]
    u = cbcv[:, D_CONV:2 * D_CONV] * cbcv[:, 2 * D_CONV:3 * D_CONV]
    row = lax.broadcasted_iota(jnp.int32, (8, D_CONV), 0)
    prev = ucarry[...]
    r1 = pltpu.roll(u, 1, axis=0)
    r2 = pltpu.roll(u, 2, axis=0)
    u_m1 = jnp.concatenate(
        [jnp.where(row < 1, pltpu.roll(prev, 1, axis=0), r1[0:8]), r1[8:]], axis=0)
    u_m2 = jnp.concatenate(
        [jnp.where(row < 2, pltpu.roll(prev, 2, axis=0), r2[0:8]), r2[8:]], axis=0)
    yc = cw_ref[0:1, :] * u_m2 + cw_ref[1:2, :] * u_m1 + cw_ref[2:3, :] * u + cbias_ref[...]
    conv_out = (cb * yc).astype(BF16)
    ucarry[...] = u[TL - 8:TL, :]
    uo_ref[0] = u[TL - 8:TL, :]

    gates = _dot(hb, win_ref[:, 3 * D_ATT + 3 * D_CONV:])
    pa = _dot(att_s[...], wpa_ref[...])
    pb = _dot(conv_out, wpb_ref[...])
    mixin = _sigmoid(gates[:, 0:D_MODEL]) * pa + _sigmoid(gates[:, D_MODEL:]) * pb
    mix = _dot(mixin.astype(BF16), wo_ref[...])
    x1 = x + gt1 * mix
    x1_ref[0] = x1
    h2 = _rms(x1, g2_ref[...]) * (1.0 + sc2) + sh2
    _store_rows_as_tiles(h2_ref, _pack_rows(h2))

    h2_prev[...] = h2


def _prompt_main(x, mod, g1, g2, win, bias_q, cw, cbias, wpa, wpb, wo, wcat, whi, br):
    nb, seq, _ = x.shape
    nt = seq // TL
    ntiles = nb * nt
    keep = WINDOW // TL
    cur = lambda g: jnp.minimum(g, ntiles - 1)
    prv = lambda g: jnp.maximum(g - 1, 0)
    tile = lambda g: (cur(g) // nt, cur(g) % nt, 0)
    last = lambda g: (cur(g) // nt, jnp.maximum(cur(g) % nt - (nt - keep), 0), 0)
    perb = lambda g: (cur(g) // nt, 0, 0)
    in_specs = [
        pl.BlockSpec((1, TL, D_MODEL), tile),
        pl.BlockSpec((1, 6, D_MODEL), perb),
        _const_spec(g1.shape), _const_spec(g2.shape), _const_spec(win.shape),
        _const_spec(bias_q.shape), _const_spec(cw.shape), _const_spec(cbias.shape),
        _const_spec(wpa.shape), _const_spec(wpb.shape), _const_spec(wo.shape),
        _const_spec(wcat.shape), _const_spec(whi.shape), _const_spec(br.shape),
    ]
    out_specs = [
        pl.BlockSpec((1, TL, D_MODEL), tile),
        pl.BlockSpec((TL * ROW_TILES, LANES), lambda g: (cur(g), 0)),
        pl.BlockSpec((1, TL, D_ATT), last),
        pl.BlockSpec((1, TL, D_ATT), last),
        pl.BlockSpec((1, 8, D_CONV), perb),
        pl.BlockSpec((1, TL, RLANES), lambda g: (prv(g) // nt, prv(g) % nt, 0)),
        pl.BlockSpec((1, RLANES), lambda g: (0, 0)),
    ]
    out_shape = [
        jax.ShapeDtypeStruct((nb, seq, D_MODEL), F32),
        jax.ShapeDtypeStruct((nb * seq * ROW_TILES, LANES), I32),
        jax.ShapeDtypeStruct((nb, WINDOW, D_ATT), F32),
        jax.ShapeDtypeStruct((nb, WINDOW, D_ATT), F32),
        jax.ShapeDtypeStruct((nb, 8, D_CONV), F32),
        jax.ShapeDtypeStruct((nb, seq, RLANES), F32),
        jax.ShapeDtypeStruct((1, RLANES), F32),
    ]
    scratch = [
        pltpu.VMEM((RING, D_ATT), BF16), pltpu.VMEM((RING, D_ATT), BF16),
        pltpu.VMEM((TL, D_ATT), BF16), pltpu.VMEM((8, D_CONV), F32),
        pltpu.VMEM((1, RLANES), F32), pltpu.VMEM((TL, D_MODEL), F32),
    ]
    return pl.pallas_call(
        functools.partial(_prompt_kernel, nt=nt, ntiles=ntiles),
        grid=(ntiles + 1,),
        in_specs=in_specs, out_specs=out_specs, out_shape=out_shape,
        scratch_shapes=scratch,
        compiler_params=pltpu.CompilerParams(
            dimension_semantics=("arbitrary",), vmem_limit_bytes=VMEM_LIMIT),
        name="prompt_main",
    )(x, mod, g1, g2, win, bias_q, cw, cbias, wpa, wpb, wo, wcat, whi, br)


def _sample_kernel(x_ref, mod_ref, ck_ref, cv_ref, up1_ref, up2_ref, cnt_in_ref,
                   g1_ref, g2_ref, win_ref, bias_ref, cw_ref, cbias_ref,
                   wpa_ref, wpb_ref, wo_ref, wcat_ref, whi_ref, br_ref,
                   x1_ref, h2_ref, ko_ref, vo_ref, uo_ref, route_ref, cnt_ref,
                   h_s, q_s, kn_s, vn_s, kband, vband, att_s, conv_s, h2_s, *, nseq, slen):
    n = pl.program_id(0)
    ntok = nseq * slen

    @pl.when(n == 0)
    def _():
        def norm_body(i, carry):
            rows = pl.ds(pl.multiple_of(i * slen, slen), slen)
            xi = x_ref[rows, :]
            m = mod_ref[i]
            hi = _rms(xi, g1_ref[...]) * (1.0 + m[1:2, :]) + m[0:1, :]
            h_s[rows, :] = hi.astype(BF16)
            return carry
        lax.fori_loop(0, nseq, norm_body, 0)
        hb = h_s[...]
        qkv = _dot(hb, win_ref[:, 0:3 * D_ATT])
        q_s[...] = (qkv[:, 0:D_ATT] * Q_SCALE).astype(BF16)
        k = qkv[:, D_ATT:2 * D_ATT]
        v = qkv[:, 2 * D_ATT:3 * D_ATT]
        ko_ref[...] = k
        vo_ref[...] = v
        kn_s[...] = k.astype(BF16)
        vn_s[...] = v.astype(BF16)

        cbcv = _dot(hb, win_ref[:, 3 * D_ATT:3 * D_ATT + 3 * D_CONV])
        cb = cbcv[:, 0:D_CONV]
        u = cbcv[:, D_CONV:2 * D_CONV] * cbcv[:, 2 * D_CONV:3 * D_CONV]
        pos = lax.broadcasted_iota(jnp.int32, (ntok, D_CONV), 0) & (slen - 1)
        u_m1 = jnp.where(pos < 1, up1_ref[...], pltpu.roll(u, 1, axis=0))
        u_m2 = jnp.where(pos < 2, up2_ref[...], pltpu.roll(u, 2, axis=0))
        yc = cw_ref[0:1, :] * u_m2 + cw_ref[1:2, :] * u_m1 + cw_ref[2:3, :] * u + cbias_ref[...]
        conv_s[...] = (cb * yc).astype(BF16)
        uo_ref[...] = u

    rows = pl.ds(pl.multiple_of(n * slen, slen), slen)
    kband[0:WINDOW, :] = ck_ref[0].astype(BF16)
    vband[0:WINDOW, :] = cv_ref[0].astype(BF16)
    kband[WINDOW:WINDOW + slen, :] = kn_s[rows, :]
    vband[WINDOW:WINDOW + slen, :] = vn_s[rows, :]
    qn = q_s[rows, :]
    outs = []
    for qd in range(2):
        ls = slice(qd * QUAD, (qd + 1) * QUAD)
        outs.append(_attend(qn[:, ls], kband[:, ls], vband[:, ls], bias_ref[qd], None))
    att_s[rows, :] = jnp.concatenate(outs, axis=1).astype(BF16)

    @pl.when(n == nseq - 1)
    def _():
        gates = _dot(h_s[...], win_ref[:, 3 * D_ATT + 3 * D_CONV:])
        pa = _dot(att_s[...], wpa_ref[...])
        pb = _dot(conv_s[...], wpb_ref[...])
        mixin = _sigmoid(gates[:, 0:D_MODEL]) * pa + _sigmoid(gates[:, D_MODEL:]) * pb
        x1_ref[...] = _dot(mixin.astype(BF16), wo_ref[...])

        def res_body(i, carry):
            r = pl.ds(pl.multiple_of(i * slen, slen), slen)
            m = mod_ref[i]
            x1 = x_ref[r, :] + m[2:3, :] * x1_ref[r, :]
            x1_ref[r, :] = x1
            h2_s[r, :] = _rms(x1, g2_ref[...]) * (1.0 + m[4:5, :]) + m[3:4, :]
            return carry
        lax.fori_loop(0, nseq, res_body, 0)

        h2 = h2_s[...]
        _store_rows_as_tiles(h2_ref, _pack_rows(h2))
        route, new_cnt = _route(h2, wcat_ref, whi_ref, br_ref, cnt_in_ref[...])
        route_ref[...] = route
        cnt_ref[...] = new_cnt


def _sample_main(x2d, mod, ck, cv, up1, up2, cnt_in, g1, g2, win, bias_s, cw, cbias,
                 wpa, wpb, wo, wcat, whi, br, nseq, slen):
    ntok = nseq * slen
    args = (x2d, mod, ck, cv, up1, up2, cnt_in, g1, g2, win, bias_s, cw, cbias,
            wpa, wpb, wo, wcat, whi, br)
    in_specs = []
    for idx, a in enumerate(args):
        if idx in (2, 3):
            in_specs.append(pl.BlockSpec((1, WINDOW, D_ATT), lambda n: (n, 0, 0)))
        else:
            in_specs.append(_const_spec(a.shape))
    whole = lambda shape: pl.BlockSpec(shape, lambda n: (0,) * len(shape))
    outs = [((ntok, D_MODEL), F32), ((ntok * ROW_TILES, LANES), I32), ((ntok, D_ATT), F32),
            ((ntok, D_ATT), F32), ((ntok, D_CONV), F32), ((ntok, RLANES), F32), ((1, RLANES), F32)]
    scratch = [
        pltpu.VMEM((ntok, D_MODEL), BF16), pltpu.VMEM((ntok, D_ATT), BF16),
        pltpu.VMEM((ntok, D_ATT), BF16), pltpu.VMEM((ntok, D_ATT), BF16),
        pltpu.VMEM((WINDOW + slen, D_ATT), BF16), pltpu.VMEM((WINDOW + slen, D_ATT), BF16),
        pltpu.VMEM((ntok, D_ATT), BF16), pltpu.VMEM((ntok, D_CONV), BF16),
        pltpu.VMEM((ntok, D_MODEL), F32),
    ]
    return pl.pallas_call(
        functools.partial(_sample_kernel, nseq=nseq, slen=slen),
        grid=(nseq,),
        in_specs=in_specs,
        out_specs=[whole(s) for s, _ in outs],
        out_shape=[jax.ShapeDtypeStruct(s, d) for s, d in outs],
        scratch_shapes=scratch,
        compiler_params=pltpu.CompilerParams(
            dimension_semantics=("arbitrary",), vmem_limit_bytes=VMEM_LIMIT),
        name="sample_main",
    )(*args)


def _issue_rows(n, body):
    def group(g, carry):
        for u in range(ISSUE_UNROLL):
            body(g * ISSUE_UNROLL + u, u)
        return carry
    lax.fori_loop(0, n // ISSUE_UNROLL, group, 0)


def _dispatch_kernel(pend_ref, d1_ref, d2_ref, hp_ref, hs_ref, xs_out, zbuf, sem, zsem, *,
                     np_tiles, nblocks):
    i = pl.program_id(0)
    blk_rows = MOE_BLK * ROW_TILES

    @pl.when(i == 0)
    def _():
        zbuf[...] = jnp.zeros_like(zbuf)

        def zcopy(e):
            start = pl.multiple_of((pend_ref[e + 1] - MOE_BLK) * ROW_TILES, blk_rows)
            return pltpu.make_async_copy(zbuf, xs_out.at[pl.ds(start, blk_rows)], zsem)

        def zstart(e, carry):
            @pl.when(pend_ref[e + 1] > pend_ref[e])
            def _():
                zcopy(e).start()
            return carry

        def zwait(e, carry):
            @pl.when(pend_ref[e + 1] > pend_ref[e])
            def _():
                zcopy(e).wait()
            return carry
        lax.fori_loop(0, N_EXPERTS, zstart, 0)
        lax.fori_loop(0, N_EXPERTS, zwait, 0)

        def tcopy(bk):
            start = pl.multiple_of(bk * blk_rows, blk_rows)
            return pltpu.make_async_copy(zbuf, xs_out.at[pl.ds(start, blk_rows)], zsem)

        def tstart(bk, carry):
            tcopy(bk).start()
            return carry

        def twait(bk, carry):
            tcopy(bk).wait()
            return carry
        used = pend_ref[N_EXPERTS] // MOE_BLK
        lax.fori_loop(used, nblocks, tstart, 0)
        lax.fori_loop(used, nblocks, twait, 0)

    def scatter_tile(src):
        def row(r, u):
            pltpu.make_async_copy(_tile(src, r), _tile(xs_out, d1_ref[r]), sem).start(priority=u % 2)
            pltpu.make_async_copy(_tile(src, r), _tile(xs_out, d2_ref[r]), sem).start(
                priority=(u + 1) % 2)
        _issue_rows(TM, row)
        for _ in range(2):
            pltpu.make_async_copy(src, xs_out.at[pl.ds(0, TM * ROW_TILES)], sem).wait()

    @pl.when(i < np_tiles)
    def _():
        scatter_tile(hp_ref)

    @pl.when(i >= np_tiles)
    def _():
        scatter_tile(hs_ref)


def _dispatch(pend, d1, d2, h2p, h2s, nslots):
    np_tiles = h2p.shape[0] // (TM * ROW_TILES)
    ns_tiles = h2s.shape[0] // (TM * ROW_TILES)
    smem_tile = pl.BlockSpec((TM,), lambda i, *_: (i,), memory_space=pltpu.SMEM)
    rows = TM * ROW_TILES
    return pl.pallas_call(
        functools.partial(_dispatch_kernel, np_tiles=np_tiles, nblocks=nslots // MOE_BLK),
        grid_spec=pltpu.PrefetchScalarGridSpec(
            num_scalar_prefetch=1,
            grid=(np_tiles + ns_tiles,),
            in_specs=[smem_tile, smem_tile,
                      pl.BlockSpec((rows, LANES), lambda i, *_: (jnp.minimum(i, np_tiles - 1), 0)),
                      pl.BlockSpec((rows, LANES), lambda i, *_: (jnp.maximum(i - np_tiles, 0), 0))],
            out_specs=pl.BlockSpec(memory_space=pl.ANY),
            scratch_shapes=[pltpu.VMEM((MOE_BLK * ROW_TILES, LANES), I32),
                            pltpu.SemaphoreType.DMA(()), pltpu.SemaphoreType.DMA(())],
        ),
        out_shape=jax.ShapeDtypeStruct((nslots * ROW_TILES, LANES), I32),
        compiler_params=pltpu.CompilerParams(dimension_semantics=("arbitrary",)),
        name="dispatch",
    )(pend, d1, d2, h2p, h2s)


def _expert_kernel(blk_e_ref, nblk_ref, first_ref, wslot_ref, next_e_ref,
                   xs_ref, wg_hbm, wu_hbm, wd_hbm, y_ref,
                   wg_f, wu_f, wd_f, wg_b, wu_b, wd_b, wsem):
    i = pl.program_id(0)
    live = i < nblk_ref[0]

    def weight_copies(e, slot):
        return (pltpu.make_async_copy(wg_hbm.at[e], wg_f.at[slot], wsem.at[slot, 0]),
                pltpu.make_async_copy(wu_hbm.at[e], wu_f.at[slot], wsem.at[slot, 1]),
                pltpu.make_async_copy(wd_hbm.at[e], wd_f.at[slot], wsem.at[slot, 2]))

    @pl.when(i == 0)
    def _():
        for cp in weight_copies(blk_e_ref[0], 0):
            cp.start()

    @pl.when(live & (first_ref[i] == 1))
    def _():
        slot = wslot_ref[i]
        for cp in weight_copies(blk_e_ref[i], slot):
            cp.wait()
        wg_b[...] = wg_f[slot].astype(BF16)
        wu_b[...] = wu_f[slot].astype(BF16)
        wd_b[...] = wd_f[slot].astype(BF16)

        @pl.when(next_e_ref[i] >= 0)
        def _():
            for cp in weight_copies(next_e_ref[i], 1 - slot):
                cp.start()

    @pl.when(live)
    def _():
        x_lo, x_hi = _unpack_rows(_load_tiles_as_rows(xs_ref, MOE_BLK))
        x_lo = x_lo.astype(BF16)
        x_hi = x_hi.astype(BF16)
        g = _dot(x_lo, wg_b[0:HALF, :]) + _dot(x_hi, wg_b[HALF:, :])
        u = _dot(x_lo, wu_b[0:HALF, :]) + _dot(x_hi, wu_b[HALF:, :])
        a = (g * _sigmoid(g)) * u
        _store_rows_as_tiles(y_ref, _pack_rows(_dot(a.astype(BF16), wd_b[...])))

    @pl.when(jnp.logical_not(live))
    def _():
        y_ref[...] = jnp.zeros_like(y_ref)


def _experts(blk_e, nblk, first, wslot, next_e, xs, wg, wu, wd):
    blk_rows = MOE_BLK * ROW_TILES
    nblocks = xs.shape[0] // blk_rows
    row_map = lambda i, be, nb, *_: (jnp.minimum(i, nb[0] - 1), 0)
    any_spec = pl.BlockSpec(memory_space=pl.ANY)
    return pl.pallas_call(
        _expert_kernel,
        grid_spec=pltpu.PrefetchScalarGridSpec(
            num_scalar_prefetch=5,
            grid=(nblocks,),
            in_specs=[pl.BlockSpec((blk_rows, LANES), row_map), any_spec, any_spec, any_spec],
            out_specs=pl.BlockSpec((blk_rows, LANES), lambda i, *_: (i, 0)),
            scratch_shapes=[pltpu.VMEM((2, D_MODEL, D_EXPERT), F32),
                            pltpu.VMEM((2, D_MODEL, D_EXPERT), F32),
                            pltpu.VMEM((2, D_EXPERT, D_MODEL), F32),
                            pltpu.VMEM((D_MODEL, D_EXPERT), BF16),
                            pltpu.VMEM((D_MODEL, D_EXPERT), BF16),
                            pltpu.VMEM((D_EXPERT, D_MODEL), BF16),
                            pltpu.SemaphoreType.DMA((2, 3))],
        ),
        out_shape=jax.ShapeDtypeStruct(xs.shape, I32),
        compiler_params=pltpu.CompilerParams(
            dimension_semantics=("arbitrary",), vmem_limit_bytes=VMEM_LIMIT),
        name="experts",
    )(blk_e, nblk, first, wslot, next_e, xs, wg, wu, wd)


def _combine_kernel(d1_ref, d2_ref, d1n_ref, d2n_ref, y_hbm, x1_ref, route_ref, gate_ref, gf_ref,
                    o_ref, a0, b0, a1, b1, sem, *, ntiles):
    t = pl.program_id(0)
    bufs = ((a0, b0), (a1, b1))

    def gather(i1_ref, i2_ref, par):
        buf_a, buf_b = bufs[par]

        def row(r, u):
            pltpu.make_async_copy(_tile(y_hbm, i1_ref[r]), _tile(buf_a, r),
                                  sem.at[par]).start(priority=u % 2)
            pltpu.make_async_copy(_tile(y_hbm, i2_ref[r]), _tile(buf_b, r),
                                  sem.at[par]).start(priority=(u + 1) % 2)
        _issue_rows(TM, row)

    @pl.when(t == 0)
    def _():
        gather(d1_ref, d2_ref, 0)

    def step(par):
        buf_a, buf_b = bufs[par]
        for buf in (buf_a, buf_b):
            pltpu.make_async_copy(y_hbm.at[pl.ds(0, TM * ROW_TILES)], buf, sem.at[par]).wait()

        @pl.when(t + 1 < ntiles)
        def _():
            gather(d1n_ref, d2n_ref, 1 - par)

        route = route_ref[0]
        w1 = route[:, 4:5]
        w2 = route[:, 5:6]
        a_lo, a_hi = _unpack_rows(_load_tiles_as_rows(buf_a, TM))
        b_lo, b_hi = _unpack_rows(_load_tiles_as_rows(buf_b, TM))
        ffn = jnp.concatenate([w1 * a_lo + w2 * b_lo, w1 * a_hi + w2 * b_hi], axis=1)
        x2 = x1_ref[0] + gate_ref[0] * ffn
        o_ref[0] = _rms(x2, gf_ref[...])

    for par in range(2):
        pl.when(t % 2 == par)(functools.partial(step, par))


def _combine(d1, d2, y, x1, route, gate, gf, tok_base):
    nb, seq, _ = x1.shape
    nt = seq // TM
    ntiles = nb * nt
    blk0 = tok_base // TM
    smem = lambda fn: pl.BlockSpec((TM,), fn, memory_space=pltpu.SMEM)
    cur = lambda t: (blk0 + t,)
    nxt = lambda t: (blk0 + jnp.minimum(t + 1, ntiles - 1),)
    tile = lambda t: (t // nt, t % nt, 0)
    grows = gate.shape[1]
    gate_spec = (pl.BlockSpec((1, 1, D_MODEL), lambda t: (t // nt, 0, 0)) if grows == 1
                 else pl.BlockSpec((1, TM, D_MODEL), tile))
    return pl.pallas_call(
        functools.partial(_combine_kernel, ntiles=ntiles),
        grid=(ntiles,),
        in_specs=[smem(cur), smem(cur), smem(nxt), smem(nxt), pl.BlockSpec(memory_space=pl.ANY),
                  pl.BlockSpec((1, TM, D_MODEL), tile),
                  pl.BlockSpec((1, TM, RLANES), tile),
                  gate_spec,
                  pl.BlockSpec((1, D_MODEL), lambda t: (0, 0))],
        out_specs=pl.BlockSpec((1, TM, D_MODEL), tile),
        out_shape=jax.ShapeDtypeStruct(x1.shape, F32),
        scratch_shapes=[pltpu.VMEM((TM * ROW_TILES, LANES), I32)] * 4 + [
            pltpu.SemaphoreType.DMA((2,))],
        compiler_params=pltpu.CompilerParams(
            dimension_semantics=("arbitrary",), vmem_limit_bytes=VMEM_LIMIT),
        name="combine",
    )(d1, d2, d1, d2, y, x1, route, gate, gf)


def _band_bias(rel_bias, rows, keys):
    n = rows - 1 + keys
    dist = WINDOW + rows - 1 - np.arange(n + 1)
    flipped = rel_bias[:, np.clip(dist, -MAX_REL, MAX_REL) + MAX_REL]
    skew = jnp.tile(flipped, (1, rows))[:, :rows * n].reshape(N_HEADS, rows, n)
    b = skew[:, :, rows - 1:rows - 1 + keys]
    return b.reshape(2, 4 * rows, keys)


def kernel(x_prompt, x_sample, cache_attn_k, cache_attn_v, state_conv, c_prompt, c_sample,
           w_ada, b_ada, norm1_g, norm2_g, w_in, rel_bias, conv_w, conv_b, w_pa, w_pb, w_o,
           w_group, b_group, w_expert, b_expert, w_e_gate, w_e_up, w_e_down, final_g):
    assert w_ada.shape[0] == 1, "single trunk layer"
    nb, seq, _ = x_prompt.shape
    nseq, slen, _ = x_sample.shape
    ntok_p = nb * seq
    ntok_s = nseq * slen
    ntok = ntok_p + ntok_s
    assert seq % TL == 0 and WINDOW % TL == 0 and seq % TM == 0 and ntok_s % TM == 0
    assert slen >= 2 and slen & (slen - 1) == 0 and slen % 16 == 0

    n_c = nb + nseq
    n_pad = -(-n_c // 8) * 8
    c_all = jnp.concatenate([c_prompt, c_sample, jnp.zeros((n_pad - n_c, D_MODEL), F32)], axis=0)
    mod = _ada(c_all, w_ada[0], b_ada[0]).reshape(n_pad, 6, D_MODEL)
    mod_p = mod[:nb]
    mod_s = mod[nb:n_c]

    win = w_in[0].astype(BF16)
    wpa = w_pa[0].astype(BF16)
    wpb = w_pb[0].astype(BF16)
    wo = w_o[0].astype(BF16)
    g1 = norm1_g[0].reshape(1, D_MODEL)
    g2 = norm2_g[0].reshape(1, D_MODEL)
    gf = final_g.reshape(1, D_MODEL)
    cw = jnp.concatenate([conv_w[0], jnp.zeros((8 - conv_w.shape[1], D_CONV), F32)], axis=0)
    cbias = conv_b[0].reshape(1, D_CONV)
    wr = jnp.concatenate([w_expert[0], w_group[0],
                          jnp.zeros((D_MODEL, RLANES - N_EXPERTS - N_GROUPS), F32)], axis=1)
    whi = wr.astype(BF16)
    wlo = (wr - whi.astype(F32)).astype(BF16)
    wcat = jnp.concatenate([whi, wlo], axis=1)
    br = jnp.concatenate([b_expert[0], b_group[0],
                          jnp.zeros((RLANES - N_EXPERTS - N_GROUPS,), F32)]).reshape(1, RLANES)
    bias_p = _band_bias(rel_bias[0] * LOG2E, CHUNK, BAND)
    bias_s = _band_bias(rel_bias[0] * LOG2E, slen, WINDOW + slen)

    x1p, h2p, kp, vp, up8, route_p, cnt_p = _prompt_main(
        x_prompt, mod_p, g1, g2, win, bias_p, cw, cbias, wpa, wpb, wo, wcat, whi, br)

    st = state_conv[0]
    up1 = jnp.zeros((nseq, slen, D_CONV), F32).at[:, 0].set(st[:, 1]).reshape(ntok_s, D_CONV)
    up2 = (jnp.zeros((nseq, slen, D_CONV), F32).at[:, 0].set(st[:, 0]).at[:, 1].set(st[:, 1])
           .reshape(ntok_s, D_CONV))
    ck = cache_attn_k[0].reshape(nseq, WINDOW, D_ATT)
    cv = cache_attn_v[0].reshape(nseq, WINDOW, D_ATT)
    x1s, h2s, ks, vs, us, route_s, cnt = _sample_main(
        x_sample.reshape(ntok_s, D_MODEL), mod_s, ck, cv, up1, up2, cnt_p,
        g1, g2, win, bias_s, cw, cbias, wpa, wpb, wo, wcat, whi, br, nseq, slen)

    route_all = jnp.concatenate([route_p.reshape(ntok_p, RLANES)[:, :4], route_s[:, :4]], axis=0)
    experts = route_all[:, 0:2].astype(jnp.int32)
    ranks = route_all[:, 2:4].astype(jnp.int32)
    counts = cnt[0, :N_EXPERTS].astype(jnp.int32)
    pcounts = (counts + MOE_BLK - 1) // MOE_BLK * MOE_BLK
    pend = jnp.cumsum(pcounts)
    pstart = pend - pcounts
    eids = jnp.arange(N_EXPERTS, dtype=jnp.int32)
    dest = jnp.sum(jnp.where(experts[..., None] == eids, pstart, 0), axis=-1) + ranks
    d1 = dest[:, 0]
    d2 = dest[:, 1]
    nblocks = (2 * ntok) // MOE_BLK + N_EXPERTS
    blk_start = jnp.arange(nblocks, dtype=jnp.int32) * MOE_BLK
    blk_e = jnp.minimum(jnp.sum((pend[None, :] <= blk_start[:, None]).astype(jnp.int32), axis=1),
                        N_EXPERTS - 1)
    nblk = (pend[-1:] // MOE_BLK).astype(jnp.int32)
    pend0 = jnp.concatenate([jnp.zeros((1,), jnp.int32), pend.astype(jnp.int32)])

    xs = _dispatch(pend0, d1, d2, h2p, h2s, nblocks * MOE_BLK)
    blk_id = jnp.arange(nblocks, dtype=jnp.int32)
    first = (blk_id < nblk[0]) & ((blk_id == 0) | (blk_e != jnp.roll(blk_e, 1)))
    wslot = (jnp.cumsum(first.astype(jnp.int32)) - 1) % 2
    later_first = lax.cummin(jnp.where(first, blk_id, nblocks)[::-1])[::-1]
    next_first = jnp.concatenate([later_first[1:], jnp.full((1,), nblocks, jnp.int32)])
    next_e = jnp.where(next_first < nblocks, blk_e[jnp.minimum(next_first, nblocks - 1)], -1)
    y = _experts(blk_e, nblk, first.astype(jnp.int32), wslot.astype(jnp.int32),
                 next_e.astype(jnp.int32), xs, w_e_gate[0], w_e_up[0], w_e_down[0])

    y_prompt = _combine(d1, d2, y, x1p, route_p, mod_p[:, 5:6, :], gf, 0)
    gate_s = jnp.repeat(mod_s[:, 5, :], slen, axis=0).reshape(1, ntok_s, D_MODEL)
    y_sample = _combine(d1, d2, y, x1s.reshape(1, ntok_s, D_MODEL),
                        route_s.reshape(1, ntok_s, RLANES), gate_s, gf, ntok_p)

    new_k_p = kp.reshape(1, nb, WINDOW, N_HEADS, HEAD_DIM)
    new_v_p = vp.reshape(1, nb, WINDOW, N_HEADS, HEAD_DIM)
    new_conv_p = up8[:, 6:8, :].reshape(1, nb, 2, D_CONV)
    new_k_s = ks.reshape(1, nseq, slen, N_HEADS, HEAD_DIM)
    new_v_s = vs.reshape(1, nseq, slen, N_HEADS, HEAD_DIM)
    new_conv_s = us.reshape(nseq, slen, D_CONV)[:, slen - 2:, :].reshape(1, nseq, 2, D_CONV)
    return (y_prompt, y_sample.reshape(nseq, slen, D_MODEL), new_k_p, new_v_p, new_conv_p,
            new_k_s, new_v_s, new_conv_s)
```

```python
import functools

import numpy as np
import jax
import jax.numpy as jnp
from jax import lax
from jax.experimental import pallas as pl
from jax.experimental.pallas import tpu as pltpu

F32 = jnp.float32
BF16 = jnp.bfloat16
I32 = jnp.int32

D_MODEL = 1024
CHUNK = 64
LEFT = 8
WINDOW = LEFT * CHUNK
BAND = WINDOW + CHUNK
N_HEADS = 8
HEAD_DIM = 64
D_ATT = N_HEADS * HEAD_DIM
QUAD = 256
MAX_REL = 128
D_CONV = 512
N_GROUPS = 4
EPG = 8
N_EXPERTS = 32
D_EXPERT = 512
EPS = 1e-6
NEG = -1e30
LOG2E = float(np.log2(np.e))
Q_SCALE = HEAD_DIM ** -0.5 * LOG2E

TL = 512
RING = WINDOW + TL
MOE_BLK = 512
TM = 512
RLANES = 128
LANES = 128
HALF = D_MODEL // 2
ROW_TILES = HALF // LANES
ISSUE_UNROLL = 8
VMEM_LIMIT = 56 * 1024 * 1024


def _const_spec(shape):
    nd = len(shape)
    return pl.BlockSpec(shape, lambda *_: (0,) * nd, pipeline_mode=pl.Buffered(1))


def _dot(a, b):
    return jnp.dot(a, b, preferred_element_type=F32)


def _sigmoid(x):
    return 1.0 / (1.0 + jnp.exp(-x))


def _rms(x, g):
    ms = jnp.mean(x * x, axis=-1, keepdims=True)
    return x * lax.rsqrt(ms + EPS) * g


def _pack_rows(val):
    lo = lax.bitcast_convert_type(val[:, :HALF], I32) + 0x8000
    hi = lax.bitcast_convert_type(val[:, HALF:], I32) + 0x8000
    return (hi & -65536) | lax.shift_right_logical(lo, 16)


def _unpack_rows(packed):
    lo = lax.bitcast_convert_type(lax.shift_left(packed, 16), F32)
    hi = lax.bitcast_convert_type(packed & -65536, F32)
    return lo, hi


def _store_rows_as_tiles(ref, packed):
    r = packed.shape[0]
    for c in range(ROW_TILES):
        ref[pl.ds(c, r, stride=ROW_TILES), :] = packed[:, c * LANES:(c + 1) * LANES]


def _load_tiles_as_rows(ref, r):
    return jnp.concatenate(
        [ref[pl.ds(c, r, stride=ROW_TILES), :] for c in range(ROW_TILES)], axis=1)


def _tile(ref, row):
    return ref.at[pl.ds(pl.multiple_of(row * ROW_TILES, ROW_TILES), ROW_TILES)]


def _ada_kernel(c_ref, w_ref, b_ref, o_ref):
    c = c_ref[...]
    s = c * _sigmoid(c)
    o_ref[...] = _dot(s.astype(BF16), w_ref[...].astype(BF16)) + b_ref[...]


def _ada(c_all, w_ada, b_ada):
    n = c_all.shape[0]
    nb = 1024
    return pl.pallas_call(
        _ada_kernel,
        grid=(6 * D_MODEL // nb,),
        in_specs=[pl.BlockSpec((n, D_MODEL), lambda i: (0, 0)),
                  pl.BlockSpec((D_MODEL, nb), lambda i: (0, i)),
                  pl.BlockSpec((1, nb), lambda i: (0, i))],
        out_specs=pl.BlockSpec((n, nb), lambda i: (0, i)),
        out_shape=jax.ShapeDtypeStruct((n, 6 * D_MODEL), F32),
        name="ada",
    )(c_all, w_ada, b_ada.reshape(1, -1))


def _attend(q, kb, vb, bias, lim):
    r = q.shape[0]
    nk = kb.shape[0]
    assert r & (r - 1) == 0
    qt = jnp.concatenate([q] * 4, axis=0)
    rowh = lax.broadcasted_iota(jnp.int32, (4 * r, QUAD), 0) >> (r.bit_length() - 1)
    laneh = lax.broadcasted_iota(jnp.int32, (4 * r, QUAD), 1) >> 6
    qm = jnp.where(rowh == laneh, qt, jnp.zeros_like(qt))
    s = lax.dot_general(qm, kb, (((1,), (1,)), ((), ())), preferred_element_type=F32)
    valid = None
    if lim is not None:
        valid = lax.broadcasted_iota(jnp.int32, (r, nk), 1) >= lim
    ps, inv_l = [], []
    for h in range(4):
        sh = s[h * r:(h + 1) * r] + bias[h * r:(h + 1) * r]
        if valid is not None:
            sh = jnp.where(valid, sh, NEG)
        m = jnp.max(sh, axis=1, keepdims=True)
        ph = jnp.exp2(sh - m)
        inv_l.append(1.0 / jnp.sum(ph, axis=1, keepdims=True))
        ps.append(ph.astype(BF16))
    o = _dot(jnp.concatenate(ps, axis=0), vb)
    lane_o = lax.broadcasted_iota(jnp.int32, (r, QUAD), 1) >> 6
    out = o[0:r] * inv_l[0]
    for h in range(1, 4):
        out = jnp.where(lane_o == h, o[h * r:(h + 1) * r] * inv_l[h], out)
    return out


def _route_stages(h2, wcat_ref, whi_ref, br_ref, cnt, valid=None):
    r = h2.shape[0]
    lane = lax.broadcasted_iota(jnp.int32, (r, RLANES), 1)
    lane_f = lane.astype(F32)
    big = jnp.float32(1000.0)
    v = {}

    def logits():
        v["logits"] = _dot(h2.astype(BF16), whi_ref[...]) + br_ref[...]

    def group():
        lg = jnp.where((lane >= N_EXPERTS) & (lane < N_EXPERTS + N_GROUPS), v["logits"], NEG)
        mg = jnp.max(lg, axis=1, keepdims=True)
        v["gi"] = jnp.min(jnp.where(lg == mg, lane_f, big), axis=1, keepdims=True) - N_EXPERTS
        v["pg"] = 1.0 / jnp.sum(jnp.exp(lg - mg), axis=1, keepdims=True)

    def top1():
        grp_of_lane = (lane >> 3).astype(F32)
        le = jnp.where((lane < N_EXPERTS) & (grp_of_lane == v["gi"]), v["logits"], NEG)
        v["m1"] = jnp.max(le, axis=1, keepdims=True)
        v["i1"] = jnp.min(jnp.where(le == v["m1"], lane_f, big), axis=1, keepdims=True)
        v["le"] = le

    def top2():
        sel1 = lane_f == v["i1"]
        le2 = jnp.where(sel1, NEG, v["le"])
        m2 = jnp.max(le2, axis=1, keepdims=True)
        v["i2"] = jnp.min(jnp.where(le2 == m2, lane_f, big), axis=1, keepdims=True)
        rr = jnp.exp(m2 - v["m1"])
        inv = v["pg"] / (1.0 + rr)
        v["w1"] = inv
        v["w2"] = inv * rr
        v["sel1"] = sel1
        v["sel2"] = lane_f == v["i2"]

    def ranks():
        oh = jnp.where(v["sel1"] | v["sel2"], 1.0 if valid is None else valid, 0.0).astype(F32)
        ri = lax.broadcasted_iota(jnp.int32, (r, r), 0)
        ci = lax.broadcasted_iota(jnp.int32, (r, r), 1)
        tri = jnp.where(ri > ci, 1.0, 0.0).astype(BF16)
        v["before"] = _dot(tri, oh.astype(BF16)) + cnt
        v["new_cnt"] = cnt + jnp.sum(oh, axis=0, keepdims=True)

    def assemble():
        r1 = jnp.sum(jnp.where(v["sel1"], v["before"], 0.0), axis=1, keepdims=True)
        r2 = jnp.sum(jnp.where(v["sel2"], v["before"], 0.0), axis=1, keepdims=True)
        route = jnp.where(lane == 0, v["i1"], 0.0)
        route = jnp.where(lane == 1, v["i2"], route)
        route = jnp.where(lane == 2, r1, route)
        route = jnp.where(lane == 3, r2, route)
        route = jnp.where(lane == 4, v["w1"], route)
        route = jnp.where(lane == 5, v["w2"], route)
        return route, v["new_cnt"]

    return [logits, group, top1, top2, ranks, assemble]


def _route(h2, wcat_ref, whi_ref, br_ref, cnt):
    stages = _route_stages(h2, wcat_ref, whi_ref, br_ref, cnt)
    for stage in stages[:-1]:
        stage()
    return stages[-1]()


def _prompt_kernel(x_ref, mod_ref, g1_ref, g2_ref, win_ref, bias_ref, cw_ref, cbias_ref,
                   wpa_ref, wpb_ref, wo_ref, wcat_ref, whi_ref, br_ref,
                   x1_ref, h2_ref, ko_ref, vo_ref, uo_ref, route_ref, cnt_ref,
                   kring, vring, att_s, ucarry, cnt_s, h2_prev, *, nt, ntiles):
    g = pl.program_id(0)

    @pl.when(g == 0)
    def _():
        cnt_s[...] = jnp.zeros_like(cnt_s)
        h2_prev[...] = jnp.zeros_like(h2_prev)

    @pl.when(g < ntiles)
    def _():
        _prompt_tile(g, nt, x_ref, mod_ref, g1_ref, g2_ref, win_ref, bias_ref, cw_ref, cbias_ref,
                     wpa_ref, wpb_ref, wo_ref, wcat_ref, whi_ref, br_ref,
                     x1_ref, h2_ref, ko_ref, vo_ref, uo_ref, route_ref, cnt_ref,
                     kring, vring, att_s, ucarry, cnt_s, h2_prev)

    @pl.when(g == ntiles)
    def _():
        route, new_cnt = _route(h2_prev[...], wcat_ref, whi_ref, br_ref, cnt_s[...])
        route_ref[0] = route
        cnt_ref[...] = new_cnt


def _prompt_tile(g, nt, x_ref, mod_ref, g1_ref, g2_ref, win_ref, bias_ref, cw_ref, cbias_ref,
                 wpa_ref, wpb_ref, wo_ref, wcat_ref, whi_ref, br_ref,
                 x1_ref, h2_ref, ko_ref, vo_ref, uo_ref, route_ref, cnt_ref,
                 kring, vring, att_s, ucarry, cnt_s, h2_prev):
    j = g % nt

    @pl.when(j == 0)
    def _():
        kring[0:WINDOW, :] = jnp.zeros((WINDOW, D_ATT), BF16)
        vring[0:WINDOW, :] = jnp.zeros((WINDOW, D_ATT), BF16)
        ucarry[...] = jnp.zeros_like(ucarry)

    route_stages = _route_stages(h2_prev[...], wcat_ref, whi_ref, br_ref, cnt_s[...],
                                 valid=jnp.where(g > 0, 1.0, 0.0).astype(F32))
    route_stages[0]()

    sh1 = mod_ref[0, 0:1, :]
    sc1 = mod_ref[0, 1:2, :]
    gt1 = mod_ref[0, 2:3, :]
    sh2 = mod_ref[0, 3:4, :]
    sc2 = mod_ref[0, 4:5, :]

    x = x_ref[0]
    h = _rms(x, g1_ref[...]) * (1.0 + sc1) + sh1
    hb = h.astype(BF16)

    qkv = _dot(hb, win_ref[:, 0:3 * D_ATT])
    q = (qkv[:, 0:D_ATT] * Q_SCALE).astype(BF16)
    k = qkv[:, D_ATT:2 * D_ATT]
    v = qkv[:, 2 * D_ATT:3 * D_ATT]
    ko_ref[0] = k
    vo_ref[0] = v
    kring[WINDOW:RING, :] = k.astype(BF16)
    vring[WINDOW:RING, :] = v.astype(BF16)

    base = j * TL
    for c in range(TL // CHUNK):
        lim = WINDOW - (base + c * CHUNK)
        for qd in range(2):
            ls = slice(qd * QUAD, (qd + 1) * QUAD)
            o = _attend(q[c * CHUNK:(c + 1) * CHUNK, ls],
                        kring[c * CHUNK:c * CHUNK + BAND, ls],
                        vring[c * CHUNK:c * CHUNK + BAND, ls],
                        bias_ref[qd], lim)
            att_s[c * CHUNK:(c + 1) * CHUNK, ls] = o.astype(BF16)
        if c + 1 < len(route_stages) - 1:
            route_stages[c + 1]()
    route, new_cnt = route_stages[-1]()
    route_ref[0] = route
    cnt_s[...] = new_cnt
    cnt_ref[...] = new_cnt

    kring[0:WINDOW, :] = kring[TL:RING, :]
    vring[0:WINDOW, :] = vring[TL:RING, :]

    cbcv = _dot(hb, win_ref[:, 3 * D_ATT:3 * D_ATT + 3 * D_CONV])
    cb = cbcv[:, 0:D_CONV]
    u = cbcv[:, D_CONV:2 * D_CONV] * cbcv[:, 2 * D_CONV:3 * D_CONV]
    row = lax.broadcasted_iota(jnp.int32, (8, D_CONV), 0)
    prev = ucarry[...]
    r1 = pltpu.roll(u, 1, axis=0)
    r2 = pltpu.roll(u, 2, axis=0)
    u_m1 = jnp.concatenate(
        [jnp.where(row < 1, pltpu.roll(prev, 1, axis=0), r1[0:8]), r1[8:]], axis=0)
    u_m2 = jnp.concatenate(
        [jnp.where(row < 2, pltpu.roll(prev, 2, axis=0), r2[0:8]), r2[8:]], axis=0)
    yc = cw_ref[0:1, :] * u_m2 + cw_ref[1:2, :] * u_m1 + cw_ref[2:3, :] * u + cbias_ref[...]
    conv_out = (cb * yc).astype(BF16)
    ucarry[...] = u[TL - 8:TL, :]
    uo_ref[0] = u[TL - 8:TL, :]

    gates = _dot(hb, win_ref[:, 3 * D_ATT + 3 * D_CONV:])
    pa = _dot(att_s[...], wpa_ref[...])
    pb = _dot(conv_out, wpb_ref[...])
    mixin = _sigmoid(gates[:, 0:D_MODEL]) * pa + _sigmoid(gates[:, D_MODEL:]) * pb
    mix = _dot(mixin.astype(BF16), wo_ref[...])
    x1 = x + gt1 * mix
    x1_ref[0] = x1
    h2 = _rms(x1, g2_ref[...]) * (1.0 + sc2) + sh2
    _store_rows_as_tiles(h2_ref, _pack_rows(h2))

    h2_prev[...] = h2


def _prompt_main(x, mod, g1, g2, win, bias_q, cw, cbias, wpa, wpb, wo, wcat, whi, br):
    nb, seq, _ = x.shape
    nt = seq // TL
    ntiles = nb * nt
    keep = WINDOW // TL
    cur = lambda g: jnp.minimum(g, ntiles - 1)
    prv = lambda g: jnp.maximum(g - 1, 0)
    tile = lambda g: (cur(g) // nt, cur(g) % nt, 0)
    last = lambda g: (cur(g) // nt, jnp.maximum(cur(g) % nt - (nt - keep), 0), 0)
    perb = lambda g: (cur(g) // nt, 0, 0)
    in_specs = [
        pl.BlockSpec((1, TL, D_MODEL), tile),
        pl.BlockSpec((1, 6, D_MODEL), perb),
        _const_spec(g1.shape), _const_spec(g2.shape), _const_spec(win.shape),
        _const_spec(bias_q.shape), _const_spec(cw.shape), _const_spec(cbias.shape),
        _const_spec(wpa.shape), _const_spec(wpb.shape), _const_spec(wo.shape),
        _const_spec(wcat.shape), _const_spec(whi.shape), _const_spec(br.shape),
    ]
    out_specs = [
        pl.BlockSpec((1, TL, D_MODEL), tile),
        pl.BlockSpec((TL * ROW_TILES, LANES), lambda g: (cur(g), 0)),
        pl.BlockSpec((1, TL, D_ATT), last),
        pl.BlockSpec((1, TL, D_ATT), last),
        pl.BlockSpec((1, 8, D_CONV), perb),
        pl.BlockSpec((1, TL, RLANES), lambda g: (prv(g) // nt, prv(g) % nt, 0)),
        pl.BlockSpec((1, RLANES), lambda g: (0, 0)),
    ]
    out_shape = [
        jax.ShapeDtypeStruct((nb, seq, D_MODEL), F32),
        jax.ShapeDtypeStruct((nb * seq * ROW_TILES, LANES), I32),
        jax.ShapeDtypeStruct((nb, WINDOW, D_ATT), F32),
        jax.ShapeDtypeStruct((nb, WINDOW, D_ATT), F32),
        jax.ShapeDtypeStruct((nb, 8, D_CONV), F32),
        jax.ShapeDtypeStruct((nb, seq, RLANES), F32),
        jax.ShapeDtypeStruct((1, RLANES), F32),
    ]
    scratch = [
        pltpu.VMEM((RING, D_ATT), BF16), pltpu.VMEM((RING, D_ATT), BF16),
        pltpu.VMEM((TL, D_ATT), BF16), pltpu.VMEM((8, D_CONV), F32),
        pltpu.VMEM((1, RLANES), F32), pltpu.VMEM((TL, D_MODEL), F32),
    ]
    return pl.pallas_call(
        functools.partial(_prompt_kernel, nt=nt, ntiles=ntiles),
        grid=(ntiles + 1,),
        in_specs=in_specs, out_specs=out_specs, out_shape=out_shape,
        scratch_shapes=scratch,
        compiler_params=pltpu.CompilerParams(
            dimension_semantics=("arbitrary",), vmem_limit_bytes=VMEM_LIMIT),
        name="prompt_main",
    )(x, mod, g1, g2, win, bias_q, cw, cbias, wpa, wpb, wo, wcat, whi, br)


def _sample_kernel(x_ref, mod_ref, ck_ref, cv_ref, up1_ref, up2_ref, cnt_in_ref,
                   g1_ref, g2_ref, win_ref, bias_ref, cw_ref, cbias_ref,
                   wpa_ref, wpb_ref, wo_ref, wcat_ref, whi_ref, br_ref,
                   x1_ref, h2_ref, ko_ref, vo_ref, uo_ref, route_ref, cnt_ref,
                   h_s, q_s, kn_s, vn_s, kband, vband, att_s, conv_s, h2_s, *, nseq, slen):
    n = pl.program_id(0)
    ntok = nseq * slen

    @pl.when(n == 0)
    def _():
        def norm_body(i, carry):
            rows = pl.ds(pl.multiple_of(i * slen, slen), slen)
            xi = x_ref[rows, :]
            m = mod_ref[i]
            hi = _rms(xi, g1_ref[...]) * (1.0 + m[1:2, :]) + m[0:1, :]
            h_s[rows, :] = hi.astype(BF16)
            return carry
        lax.fori_loop(0, nseq, norm_body, 0)
        hb = h_s[...]
        qkv = _dot(hb, win_ref[:, 0:3 * D_ATT])
        q_s[...] = (qkv[:, 0:D_ATT] * Q_SCALE).astype(BF16)
        k = qkv[:, D_ATT:2 * D_ATT]
        v = qkv[:, 2 * D_ATT:3 * D_ATT]
        ko_ref[...] = k
        vo_ref[...] = v
        kn_s[...] = k.astype(BF16)
        vn_s[...] = v.astype(BF16)

        cbcv = _dot(hb, win_ref[:, 3 * D_ATT:3 * D_ATT + 3 * D_CONV])
        cb = cbcv[:, 0:D_CONV]
        u = cbcv[:, D_CONV:2 * D_CONV] * cbcv[:, 2 * D_CONV:3 * D_CONV]
        pos = lax.broadcasted_iota(jnp.int32, (ntok, D_CONV), 0) & (slen - 1)
        u_m1 = jnp.where(pos < 1, up1_ref[...], pltpu.roll(u, 1, axis=0))
        u_m2 = jnp.where(pos < 2, up2_ref[...], pltpu.roll(u, 2, axis=0))
        yc = cw_ref[0:1, :] * u_m2 + cw_ref[1:2, :] * u_m1 + cw_ref[2:3, :] * u + cbias_ref[...]
        conv_s[...] = (cb * yc).astype(BF16)
        uo_ref[...] = u

    rows = pl.ds(pl.multiple_of(n * slen, slen), slen)
    kband[0:WINDOW, :] = ck_ref[0].astype(BF16)
    vband[0:WINDOW, :] = cv_ref[0].astype(BF16)
    kband[WINDOW:WINDOW + slen, :] = kn_s[rows, :]
    vband[WINDOW:WINDOW + slen, :] = vn_s[rows, :]
    qn = q_s[rows, :]
    outs = []
    for qd in range(2):
        ls = slice(qd * QUAD, (qd + 1) * QUAD)
        outs.append(_attend(qn[:, ls], kband[:, ls], vband[:, ls], bias_ref[qd], None))
    att_s[rows, :] = jnp.concatenate(outs, axis=1).astype(BF16)

    @pl.when(n == nseq - 1)
    def _():
        gates = _dot(h_s[...], win_ref[:, 3 * D_ATT + 3 * D_CONV:])
        pa = _dot(att_s[...], wpa_ref[...])
        pb = _dot(conv_s[...], wpb_ref[...])
        mixin = _sigmoid(gates[:, 0:D_MODEL]) * pa + _sigmoid(gates[:, D_MODEL:]) * pb
        x1_ref[...] = _dot(mixin.astype(BF16), wo_ref[...])

        def res_body(i, carry):
            r = pl.ds(pl.multiple_of(i * slen, slen), slen)
            m = mod_ref[i]
            x1 = x_ref[r, :] + m[2:3, :] * x1_ref[r, :]
            x1_ref[r, :] = x1
            h2_s[r, :] = _rms(x1, g2_ref[...]) * (1.0 + m[4:5, :]) + m[3:4, :]
            return carry
        lax.fori_loop(0, nseq, res_body, 0)

        h2 = h2_s[...]
        _store_rows_as_tiles(h2_ref, _pack_rows(h2))
        route, new_cnt = _route(h2, wcat_ref, whi_ref, br_ref, cnt_in_ref[...])
        route_ref[...] = route
        cnt_ref[...] = new_cnt


def _sample_main(x2d, mod, ck, cv, up1, up2, cnt_in, g1, g2, win, bias_s, cw, cbias,
                 wpa, wpb, wo, wcat, whi, br, nseq, slen):
    ntok = nseq * slen
    args = (x2d, mod, ck, cv, up1, up2, cnt_in, g1, g2, win, bias_s, cw, cbias,
            wpa, wpb, wo, wcat, whi, br)
    in_specs = []
    for idx, a in enumerate(args):
        if idx in (2, 3):
            in_specs.append(pl.BlockSpec((1, WINDOW, D_ATT), lambda n: (n, 0, 0)))
        else:
            in_specs.append(_const_spec(a.shape))
    whole = lambda shape: pl.BlockSpec(shape, lambda n: (0,) * len(shape))
    outs = [((ntok, D_MODEL), F32), ((ntok * ROW_TILES, LANES), I32), ((ntok, D_ATT), F32),
            ((ntok, D_ATT), F32), ((ntok, D_CONV), F32), ((ntok, RLANES), F32), ((1, RLANES), F32)]
    scratch = [
        pltpu.VMEM((ntok, D_MODEL), BF16), pltpu.VMEM((ntok, D_ATT), BF16),
        pltpu.VMEM((ntok, D_ATT), BF16), pltpu.VMEM((ntok, D_ATT), BF16),
        pltpu.VMEM((WINDOW + slen, D_ATT), BF16), pltpu.VMEM((WINDOW + slen, D_ATT), BF16),
        pltpu.VMEM((ntok, D_ATT), BF16), pltpu.VMEM((ntok, D_CONV), BF16),
        pltpu.VMEM((ntok, D_MODEL), F32),
    ]
    return pl.pallas_call(
        functools.partial(_sample_kernel, nseq=nseq, slen=slen),
        grid=(nseq,),
        in_specs=in_specs,
        out_specs=[whole(s) for s, _ in outs],
        out_shape=[jax.ShapeDtypeStruct(s, d) for s, d in outs],
        scratch_shapes=scratch,
        compiler_params=pltpu.CompilerParams(
            dimension_semantics=("arbitrary",), vmem_limit_bytes=VMEM_LIMIT),
        name="sample_main",
    )(*args)


def _issue_rows(n, body):
    def group(g, carry):
        for u in range(ISSUE_UNROLL):
            body(g * ISSUE_UNROLL + u, u)
        return carry
    lax.fori_loop(0, n // ISSUE_UNROLL, group, 0)


def _dispatch_kernel(pend_ref, d1_ref, d2_ref, hp_ref, hs_ref, xs_out, zbuf, sem, zsem, *,
                     np_tiles, nblocks):
    i = pl.program_id(0)
    blk_rows = MOE_BLK * ROW_TILES

    @pl.when(i == 0)
    def _():
        zbuf[...] = jnp.zeros_like(zbuf)

        def zcopy(e):
            start = pl.multiple_of((pend_ref[e + 1] - MOE_BLK) * ROW_TILES, blk_rows)
            return pltpu.make_async_copy(zbuf, xs_out.at[pl.ds(start, blk_rows)], zsem)

        def zstart(e, carry):
            @pl.when(pend_ref[e + 1] > pend_ref[e])
            def _():
                zcopy(e).start()
            return carry

        def zwait(e, carry):
            @pl.when(pend_ref[e + 1] > pend_ref[e])
            def _():
                zcopy(e).wait()
            return carry
        lax.fori_loop(0, N_EXPERTS, zstart, 0)
        lax.fori_loop(0, N_EXPERTS, zwait, 0)

        def tcopy(bk):
            start = pl.multiple_of(bk * blk_rows, blk_rows)
            return pltpu.make_async_copy(zbuf, xs_out.at[pl.ds(start, blk_rows)], zsem)

        def tstart(bk, carry):
            tcopy(bk).start()
            return carry

        def twait(bk, carry):
            tcopy(bk).wait()
            return carry
        used = pend_ref[N_EXPERTS] // MOE_BLK
        lax.fori_loop(used, nblocks, tstart, 0)
        lax.fori_loop(used, nblocks, twait, 0)

    def scatter_tile(src):
        def row(r, u):
            pltpu.make_async_copy(_tile(src, r), _tile(xs_out, d1_ref[r]), sem).start(priority=u % 2)
            pltpu.make_async_copy(_tile(src, r), _tile(xs_out, d2_ref[r]), sem).start(
                priority=(u + 1) % 2)
        _issue_rows(TM, row)
        for _ in range(2):
            pltpu.make_async_copy(src, xs_out.at[pl.ds(0, TM * ROW_TILES)], sem).wait()

    @pl.when(i < np_tiles)
    def _():
        scatter_tile(hp_ref)

    @pl.when(i >= np_tiles)
    def _():
        scatter_tile(hs_ref)


def _dispatch(pend, d1, d2, h2p, h2s, nslots):
    np_tiles = h2p.shape[0] // (TM * ROW_TILES)
    ns_tiles = h2s.shape[0] // (TM * ROW_TILES)
    smem_tile = pl.BlockSpec((TM,), lambda i, *_: (i,), memory_space=pltpu.SMEM)
    rows = TM * ROW_TILES
    return pl.pallas_call(
        functools.partial(_dispatch_kernel, np_tiles=np_tiles, nblocks=nslots // MOE_BLK),
        grid_spec=pltpu.PrefetchScalarGridSpec(
            num_scalar_prefetch=1,
            grid=(np_tiles + ns_tiles,),
            in_specs=[smem_tile, smem_tile,
                      pl.BlockSpec((rows, LANES), lambda i, *_: (jnp.minimum(i, np_tiles - 1), 0)),
                      pl.BlockSpec((rows, LANES), lambda i, *_: (jnp.maximum(i - np_tiles, 0), 0))],
            out_specs=pl.BlockSpec(memory_space=pl.ANY),
            scratch_shapes=[pltpu.VMEM((MOE_BLK * ROW_TILES, LANES), I32),
                            pltpu.SemaphoreType.DMA(()), pltpu.SemaphoreType.DMA(())],
        ),
        out_shape=jax.ShapeDtypeStruct((nslots * ROW_TILES, LANES), I32),
        compiler_params=pltpu.CompilerParams(dimension_semantics=("arbitrary",)),
        name="dispatch",
    )(pend, d1, d2, h2p, h2s)


def _expert_kernel(blk_e_ref, nblk_ref, first_ref, wslot_ref, next_e_ref,
                   xs_ref, wg_hbm, wu_hbm, wd_hbm, y_ref,
                   wg_f, wu_f, wd_f, wg_b, wu_b, wd_b, wsem):
    i = pl.program_id(0)
    live = i < nblk_ref[0]

    def weight_copies(e, slot):
        return (pltpu.make_async_copy(wg_hbm.at[e], wg_f.at[slot], wsem.at[slot, 0]),
                pltpu.make_async_copy(wu_hbm.at[e], wu_f.at[slot], wsem.at[slot, 1]),
                pltpu.make_async_copy(wd_hbm.at[e], wd_f.at[slot], wsem.at[slot, 2]))

    @pl.when(i == 0)
    def _():
        for cp in weight_copies(blk_e_ref[0], 0):
            cp.start()

    @pl.when(live & (first_ref[i] == 1))
    def _():
        slot = wslot_ref[i]
        for cp in weight_copies(blk_e_ref[i], slot):
            cp.wait()
        wg_b[...] = wg_f[slot].astype(BF16)
        wu_b[...] = wu_f[slot].astype(BF16)
        wd_b[...] = wd_f[slot].astype(BF16)

        @pl.when(next_e_ref[i] >= 0)
        def _():
            for cp in weight_copies(next_e_ref[i], 1 - slot):
                cp.start()

    @pl.when(live)
    def _():
        x_lo, x_hi = _unpack_rows(_load_tiles_as_rows(xs_ref, MOE_BLK))
        x_lo = x_lo.astype(BF16)
        x_hi = x_hi.astype(BF16)
        g = _dot(x_lo, wg_b[0:HALF, :]) + _dot(x_hi, wg_b[HALF:, :])
        u = _dot(x_lo, wu_b[0:HALF, :]) + _dot(x_hi, wu_b[HALF:, :])
        a = (g * _sigmoid(g)) * u
        _store_rows_as_tiles(y_ref, _pack_rows(_dot(a.astype(BF16), wd_b[...])))

    @pl.when(jnp.logical_not(live))
    def _():
        y_ref[...] = jnp.zeros_like(y_ref)


def _experts(blk_e, nblk, first, wslot, next_e, xs, wg, wu, wd):
    blk_rows = MOE_BLK * ROW_TILES
    nblocks = xs.shape[0] // blk_rows
    row_map = lambda i, be, nb, *_: (jnp.minimum(i, nb[0] - 1), 0)
    any_spec = pl.BlockSpec(memory_space=pl.ANY)
    return pl.pallas_call(
        _expert_kernel,
        grid_spec=pltpu.PrefetchScalarGridSpec(
            num_scalar_prefetch=5,
            grid=(nblocks,),
            in_specs=[pl.BlockSpec((blk_rows, LANES), row_map), any_spec, any_spec, any_spec],
            out_specs=pl.BlockSpec((blk_rows, LANES), lambda i, *_: (i, 0)),
            scratch_shapes=[pltpu.VMEM((2, D_MODEL, D_EXPERT), F32),
                            pltpu.VMEM((2, D_MODEL, D_EXPERT), F32),
                            pltpu.VMEM((2, D_EXPERT, D_MODEL), F32),
                            pltpu.VMEM((D_MODEL, D_EXPERT), BF16),
                            pltpu.VMEM((D_MODEL, D_EXPERT), BF16),
                            pltpu.VMEM((D_EXPERT, D_MODEL), BF16),
                            pltpu.SemaphoreType.DMA((2, 3))],
        ),
        out_shape=jax.ShapeDtypeStruct(xs.shape, I32),
        compiler_params=pltpu.CompilerParams(
            dimension_semantics=("arbitrary",), vmem_limit_bytes=VMEM_LIMIT),
        name="experts",
    )(blk_e, nblk, first, wslot, next_e, xs, wg, wu, wd)


def _combine_kernel(d1_ref, d2_ref, d1n_ref, d2n_ref, y_hbm, x1_ref, route_ref, gate_ref, gf_ref,
                    o_ref, a0, b0, a1, b1, sem, *, ntiles):
    t = pl.program_id(0)
    bufs = ((a0, b0), (a1, b1))

    def gather(i1_ref, i2_ref, par):
        buf_a, buf_b = bufs[par]

        def row(r, u):
            pltpu.make_async_copy(_tile(y_hbm, i1_ref[r]), _tile(buf_a, r),
                                  sem.at[par]).start(priority=u % 2)
            pltpu.make_async_copy(_tile(y_hbm, i2_ref[r]), _tile(buf_b, r),
                                  sem.at[par]).start(priority=(u + 1) % 2)
        _issue_rows(TM, row)

    @pl.when(t == 0)
    def _():
        gather(d1_ref, d2_ref, 0)

    def step(par):
        buf_a, buf_b = bufs[par]
        for buf in (buf_a, buf_b):
            pltpu.make_async_copy(y_hbm.at[pl.ds(0, TM * ROW_TILES)], buf, sem.at[par]).wait()

        @pl.when(t + 1 < ntiles)
        def _():
            gather(d1n_ref, d2n_ref, 1 - par)

        route = route_ref[0]
        w1 = route[:, 4:5]
        w2 = route[:, 5:6]
        a_lo, a_hi = _unpack_rows(_load_tiles_as_rows(buf_a, TM))
        b_lo, b_hi = _unpack_rows(_load_tiles_as_rows(buf_b, TM))
        ffn = jnp.concatenate([w1 * a_lo + w2 * b_lo, w1 * a_hi + w2 * b_hi], axis=1)
        x2 = x1_ref[0] + gate_ref[0] * ffn
        o_ref[0] = _rms(x2, gf_ref[...])

    for par in range(2):
        pl.when(t % 2 == par)(functools.partial(step, par))


def _combine(d1, d2, y, x1, route, gate, gf, tok_base):
    nb, seq, _ = x1.shape
    nt = seq // TM
    ntiles = nb * nt
    blk0 = tok_base // TM
    smem = lambda fn: pl.BlockSpec((TM,), fn, memory_space=pltpu.SMEM)
    cur = lambda t: (blk0 + t,)
    nxt = lambda t: (blk0 + jnp.minimum(t + 1, ntiles - 1),)
    tile = lambda t: (t // nt, t % nt, 0)
    grows = gate.shape[1]
    gate_spec = (pl.BlockSpec((1, 1, D_MODEL), lambda t: (t // nt, 0, 0)) if grows == 1
                 else pl.BlockSpec((1, TM, D_MODEL), tile))
    return pl.pallas_call(
        functools.partial(_combine_kernel, ntiles=ntiles),
        grid=(ntiles,),
        in_specs=[smem(cur), smem(cur), smem(nxt), smem(nxt), pl.BlockSpec(memory_space=pl.ANY),
                  pl.BlockSpec((1, TM, D_MODEL), tile),
                  pl.BlockSpec((1, TM, RLANES), tile),
                  gate_spec,
                  pl.BlockSpec((1, D_MODEL), lambda t: (0, 0))],
        out_specs=pl.BlockSpec((1, TM, D_MODEL), tile),
        out_shape=jax.ShapeDtypeStruct(x1.shape, F32),
        scratch_shapes=[pltpu.VMEM((TM * ROW_TILES, LANES), I32)] * 4 + [
            pltpu.SemaphoreType.DMA((2,))],
        compiler_params=pltpu.CompilerParams(
            dimension_semantics=("arbitrary",), vmem_limit_bytes=VMEM_LIMIT),
        name="combine",
    )(d1, d2, d1, d2, y, x1, route, gate, gf)


def _band_bias(rel_bias, rows, keys):
    n = rows - 1 + keys
    dist = WINDOW + rows - 1 - np.arange(n + 1)
    flipped = rel_bias[:, np.clip(dist, -MAX_REL, MAX_REL) + MAX_REL]
    skew = jnp.tile(flipped, (1, rows))[:, :rows * n].reshape(N_HEADS, rows, n)
    b = skew[:, :, rows - 1:rows - 1 + keys]
    return b.reshape(2, 4 * rows, keys)


def kernel(x_prompt, x_sample, cache_attn_k, cache_attn_v, state_conv, c_prompt, c_sample,
           w_ada, b_ada, norm1_g, norm2_g, w_in, rel_bias, conv_w, conv_b, w_pa, w_pb, w_o,
           w_group, b_group, w_expert, b_expert, w_e_gate, w_e_up, w_e_down, final_g):
    assert w_ada.shape[0] == 1, "single trunk layer"
    nb, seq, _ = x_prompt.shape
    nseq, slen, _ = x_sample.shape
    ntok_p = nb * seq
    ntok_s = nseq * slen
    ntok = ntok_p + ntok_s
    assert seq % TL == 0 and WINDOW % TL == 0 and seq % TM == 0 and ntok_s % TM == 0
    assert slen >= 2 and slen & (slen - 1) == 0 and slen % 16 == 0

    n_c = nb + nseq
    n_pad = -(-n_c // 8) * 8
    c_all = jnp.concatenate([c_prompt, c_sample, jnp.zeros((n_pad - n_c, D_MODEL), F32)], axis=0)
    mod = _ada(c_all, w_ada[0], b_ada[0]).reshape(n_pad, 6, D_MODEL)
    mod_p = mod[:nb]
    mod_s = mod[nb:n_c]

    win = w_in[0].astype(BF16)
    wpa = w_pa[0].astype(BF16)
    wpb = w_pb[0].astype(BF16)
    wo = w_o[0].astype(BF16)
    g1 = norm1_g[0].reshape(1, D_MODEL)
    g2 = norm2_g[0].reshape(1, D_MODEL)
    gf = final_g.reshape(1, D_MODEL)
    cw = jnp.concatenate([conv_w[0], jnp.zeros((8 - conv_w.shape[1], D_CONV), F32)], axis=0)
    cbias = conv_b[0].reshape(1, D_CONV)
    wr = jnp.concatenate([w_expert[0], w_group[0],
                          jnp.zeros((D_MODEL, RLANES - N_EXPERTS - N_GROUPS), F32)], axis=1)
    whi = wr.astype(BF16)
    wlo = (wr - whi.astype(F32)).astype(BF16)
    wcat = jnp.concatenate([whi, wlo], axis=1)
    br = jnp.concatenate([b_expert[0], b_group[0],
                          jnp.zeros((RLANES - N_EXPERTS - N_GROUPS,), F32)]).reshape(1, RLANES)
    bias_p = _band_bias(rel_bias[0] * LOG2E, CHUNK, BAND)
    bias_s = _band_bias(rel_bias[0] * LOG2E, slen, WINDOW + slen)

    x1p, h2p, kp, vp, up8, route_p, cnt_p = _prompt_main(
        x_prompt, mod_p, g1, g2, win, bias_p, cw, cbias, wpa, wpb, wo, wcat, whi, br)

    st = state_conv[0]
    up1 = jnp.zeros((nseq, slen, D_CONV), F32).at[:, 0].set(st[:, 1]).reshape(ntok_s, D_CONV)
    up2 = (jnp.zeros((nseq, slen, D_CONV), F32).at[:, 0].set(st[:, 0]).at[:, 1].set(st[:, 1])
           .reshape(ntok_s, D_CONV))
    ck = cache_attn_k[0].reshape(nseq, WINDOW, D_ATT)
    cv = cache_attn_v[0].reshape(nseq, WINDOW, D_ATT)
    x1s, h2s, ks, vs, us, route_s, cnt = _sample_main(
        x_sample.reshape(ntok_s, D_MODEL), mod_s, ck, cv, up1, up2, cnt_p,
        g1, g2, win, bias_s, cw, cbias, wpa, wpb, wo, wcat, whi, br, nseq, slen)

    route_all = jnp.concatenate([route_p.reshape(ntok_p, RLANES)[:, :4], route_s[:, :4]], axis=0)
    experts = route_all[:, 0:2].astype(jnp.int32)
    ranks = route_all[:, 2:4].astype(jnp.int32)
    counts = cnt[0, :N_EXPERTS].astype(jnp.int32)
    pcounts = (counts + MOE_BLK - 1) // MOE_BLK * MOE_BLK
    pend = jnp.cumsum(pcounts)
    pstart = pend - pcounts
    eids = jnp.arange(N_EXPERTS, dtype=jnp.int32)
    dest = jnp.sum(jnp.where(experts[..., None] == eids, pstart, 0), axis=-1) + ranks
    d1 = dest[:, 0]
    d2 = dest[:, 1]
    nblocks = (2 * ntok) // MOE_BLK + N_EXPERTS
    blk_start = jnp.arange(nblocks, dtype=jnp.int32) * MOE_BLK
    blk_e = jnp.minimum(jnp.sum((pend[None, :] <= blk_start[:, None]).astype(jnp.int32), axis=1),
                        N_EXPERTS - 1)
    nblk = (pend[-1:] // MOE_BLK).astype(jnp.int32)
    pend0 = jnp.concatenate([jnp.zeros((1,), jnp.int32), pend.astype(jnp.int32)])

    xs = _dispatch(pend0, d1, d2, h2p, h2s, nblocks * MOE_BLK)
    blk_id = jnp.arange(nblocks, dtype=jnp.int32)
    first = (blk_id < nblk[0]) & ((blk_id == 0) | (blk_e != jnp.roll(blk_e, 1)))
    wslot = (jnp.cumsum(first.astype(jnp.int32)) - 1) % 2
    later_first = lax.cummin(jnp.where(first, blk_id, nblocks)[::-1])[::-1]
    next_first = jnp.concatenate([later_first[1:], jnp.full((1,), nblocks, jnp.int32)])
    next_e = jnp.where(next_first < nblocks, blk_e[jnp.minimum(next_first, nblocks - 1)], -1)
    y = _experts(blk_e, nblk, first.astype(jnp.int32), wslot.astype(jnp.int32),
                 next_e.astype(jnp.int32), xs, w_e_gate[0], w_e_up[0], w_e_down[0])

    y_prompt = _combine(d1, d2, y, x1p, route_p, mod_p[:, 5:6, :], gf, 0)
    gate_s = jnp.repeat(mod_s[:, 5, :], slen, axis=0).reshape(1, ntok_s, D_MODEL)
    y_sample = _combine(d1, d2, y, x1s.reshape(1, ntok_s, D_MODEL),
                        route_s.reshape(1, ntok_s, RLANES), gate_s, gf, ntok_p)

    new_k_p = kp.reshape(1, nb, WINDOW, N_HEADS, HEAD_DIM)
    new_v_p = vp.reshape(1, nb, WINDOW, N_HEADS, HEAD_DIM)
    new_conv_p = up8[:, 6:8, :].reshape(1, nb, 2, D_CONV)
    new_k_s = ks.reshape(1, nseq, slen, N_HEADS, HEAD_DIM)
    new_v_s = vs.reshape(1, nseq, slen, N_HEADS, HEAD_DIM)
    new_conv_s = us.reshape(nseq, slen, D_CONV)[:, slen - 2:, :].reshape(1, nseq, 2, D_CONV)
    return (y_prompt, y_sample.reshape(nseq, slen, D_MODEL), new_k_p, new_v_p, new_conv_p,
            new_k_s, new_v_s, new_conv_s)
```

```python
import functools

import numpy as np
import jax
import jax.numpy as jnp
from jax import lax
from jax.experimental import pallas as pl
from jax.experimental.pallas import tpu as pltpu

F32 = jnp.float32
BF16 = jnp.bfloat16
I32 = jnp.int32

D_MODEL = 1024
CHUNK = 64
LEFT = 8
WINDOW = LEFT * CHUNK
BAND = WINDOW + CHUNK
N_HEADS = 8
HEAD_DIM = 64
D_ATT = N_HEADS * HEAD_DIM
QUAD = 256
MAX_REL = 128
D_CONV = 512
N_GROUPS = 4
EPG = 8
N_EXPERTS = 32
D_EXPERT = 512
EPS = 1e-6
NEG = -1e30
LOG2E = float(np.log2(np.e))
Q_SCALE = HEAD_DIM ** -0.5 * LOG2E

TL = 512
RING = WINDOW + TL
MOE_BLK = 512
TM = 512
RLANES = 128
LANES = 128
HALF = D_MODEL // 2
ROW_TILES = HALF // LANES
ISSUE_UNROLL = 8
VMEM_LIMIT = 56 * 1024 * 1024


def _const_spec(shape):
    nd = len(shape)
    return pl.BlockSpec(shape, lambda *_: (0,) * nd, pipeline_mode=pl.Buffered(1))


def _dot(a, b):
    return jnp.dot(a, b, preferred_element_type=F32)


def _sigmoid(x):
    return 1.0 / (1.0 + jnp.exp(-x))


def _rms(x, g):
    ms = jnp.mean(x * x, axis=-1, keepdims=True)
    return x * lax.rsqrt(ms + EPS) * g


def _pack_rows(val):
    lo = lax.bitcast_convert_type(val[:, :HALF], I32) + 0x8000
    hi = lax.bitcast_convert_type(val[:, HALF:], I32) + 0x8000
    return (hi & -65536) | lax.shift_right_logical(lo, 16)


def _unpack_rows(packed):
    lo = lax.bitcast_convert_type(lax.shift_left(packed, 16), F32)
    hi = lax.bitcast_convert_type(packed & -65536, F32)
    return lo, hi


def _store_rows_as_tiles(ref, packed):
    r = packed.shape[0]
    for c in range(ROW_TILES):
        ref[pl.ds(c, r, stride=ROW_TILES), :] = packed[:, c * LANES:(c + 1) * LANES]


def _load_tiles_as_rows(ref, r):
    return jnp.concatenate(
        [ref[pl.ds(c, r, stride=ROW_TILES), :] for c in range(ROW_TILES)], axis=1)


def _tile(ref, row):
    return ref.at[pl.ds(pl.multiple_of(row * ROW_TILES, ROW_TILES), ROW_TILES)]


def _ada_kernel(c_ref, w_ref, b_ref, o_ref):
    c = c_ref[...]
    s = c * _sigmoid(c)
    o_ref[...] = _dot(s.astype(BF16), w_ref[...].astype(BF16)) + b_ref[...]


def _ada(c_all, w_ada, b_ada):
    n = c_all.shape[0]
    nb = 1024
    return pl.pallas_call(
        _ada_kernel,
        grid=(6 * D_MODEL // nb,),
        in_specs=[pl.BlockSpec((n, D_MODEL), lambda i: (0, 0)),
                  pl.BlockSpec((D_MODEL, nb), lambda i: (0, i)),
                  pl.BlockSpec((1, nb), lambda i: (0, i))],
        out_specs=pl.BlockSpec((n, nb), lambda i: (0, i)),
        out_shape=jax.ShapeDtypeStruct((n, 6 * D_MODEL), F32),
        name="ada",
    )(c_all, w_ada, b_ada.reshape(1, -1))


def _attend(q, kb, vb, bias, lim):
    r = q.shape[0]
    nk = kb.shape[0]
    assert r & (r - 1) == 0
    qt = jnp.concatenate([q] * 4, axis=0)
    rowh = lax.broadcasted_iota(jnp.int32, (4 * r, QUAD), 0) >> (r.bit_length() - 1)
    laneh = lax.broadcasted_iota(jnp.int32, (4 * r, QUAD), 1) >> 6
    qm = jnp.where(rowh == laneh, qt, jnp.zeros_like(qt))
    s = lax.dot_general(qm, kb, (((1,), (1,)), ((), ())), preferred_element_type=F32)
    valid = None
    if lim is not None:
        valid = lax.broadcasted_iota(jnp.int32, (r, nk), 1) >= lim
    ps, inv_l = [], []
    for h in range(4):
        sh = s[h * r:(h + 1) * r] + bias[h * r:(h + 1) * r]
        if valid is not None:
            sh = jnp.where(valid, sh, NEG)
        m = jnp.max(sh, axis=1, keepdims=True)
        ph = jnp.exp2(sh - m)
        inv_l.append(1.0 / jnp.sum(ph, axis=1, keepdims=True))
        ps.append(ph.astype(BF16))
    o = _dot(jnp.concatenate(ps, axis=0), vb)
    lane_o = lax.broadcasted_iota(jnp.int32, (r, QUAD), 1) >> 6
    out = o[0:r] * inv_l[0]
    for h in range(1, 4):
        out = jnp.where(lane_o == h, o[h * r:(h + 1) * r] * inv_l[h], out)
    return out


def _route_stages(h2, whi_ref, br_ref, cnt, valid=None):
    r = h2.shape[0]
    lane = lax.broadcasted_iota(jnp.int32, (r, RLANES), 1)
    lane_f = lane.astype(F32)
    big = jnp.float32(1000.0)
    v = {}

    def logits():
        v["logits"] = _dot(h2.astype(BF16), whi_ref[...]) + br_ref[...]

    def group():
        lg = jnp.where((lane >= N_EXPERTS) & (lane < N_EXPERTS + N_GROUPS), v["logits"], NEG)
        mg = jnp.max(lg, axis=1, keepdims=True)
        v["gi"] = jnp.min(jnp.where(lg == mg, lane_f, big), axis=1, keepdims=True) - N_EXPERTS
        v["pg"] = 1.0 / jnp.sum(jnp.exp(lg - mg), axis=1, keepdims=True)

    def top1():
        grp_of_lane = (lane >> 3).astype(F32)
        le = jnp.where((lane < N_EXPERTS) & (grp_of_lane == v["gi"]), v["logits"], NEG)
        v["m1"] = jnp.max(le, axis=1, keepdims=True)
        v["i1"] = jnp.min(jnp.where(le == v["m1"], lane_f, big), axis=1, keepdims=True)
        v["le"] = le

    def top2():
        sel1 = lane_f == v["i1"]
        le2 = jnp.where(sel1, NEG, v["le"])
        m2 = jnp.max(le2, axis=1, keepdims=True)
        v["i2"] = jnp.min(jnp.where(le2 == m2, lane_f, big), axis=1, keepdims=True)
        rr = jnp.exp(m2 - v["m1"])
        inv = v["pg"] / (1.0 + rr)
        v["w1"] = inv
        v["w2"] = inv * rr
        v["sel1"] = sel1
        v["sel2"] = lane_f == v["i2"]

    def ranks():
        oh = jnp.where(v["sel1"] | v["sel2"], 1.0 if valid is None else valid, 0.0).astype(F32)
        ri = lax.broadcasted_iota(jnp.int32, (r, r), 0)
        ci = lax.broadcasted_iota(jnp.int32, (r, r), 1)
        tri = jnp.where(ri > ci, 1.0, 0.0).astype(BF16)
        v["before"] = _dot(tri, oh.astype(BF16)) + cnt
        v["new_cnt"] = cnt + jnp.sum(oh, axis=0, keepdims=True)

    def assemble():
        r1 = jnp.sum(jnp.where(v["sel1"], v["before"], 0.0), axis=1, keepdims=True)
        r2 = jnp.sum(jnp.where(v["sel2"], v["before"], 0.0), axis=1, keepdims=True)
        route = jnp.where(lane == 0, v["i1"], 0.0)
        route = jnp.where(lane == 1, v["i2"], route)
        route = jnp.where(lane == 2, r1, route)
        route = jnp.where(lane == 3, r2, route)
        route = jnp.where(lane == 4, v["w1"], route)
        route = jnp.where(lane == 5, v["w2"], route)
        return route, v["new_cnt"]

    return [logits, group, top1, top2, ranks, assemble]


def _route(h2, whi_ref, br_ref, cnt):
    stages = _route_stages(h2, whi_ref, br_ref, cnt)
    for stage in stages[:-1]:
        stage()
    return stages[-1]()


def _prompt_kernel(x_ref, mod_ref, g1_ref, g2_ref, win_ref, bias_ref, cw_ref, cbias_ref,
                   wpa_ref, wpb_ref, wo_ref, whi_ref, br_ref,
                   x1_ref, h2_ref, ko_ref, vo_ref, uo_ref, route_ref, cnt_ref,
                   kring, vring, att_s, ucarry, cnt_s, h2_prev, *, nt, ntiles):
    g = pl.program_id(0)

    @pl.when(g == 0)
    def _():
        cnt_s[...] = jnp.zeros_like(cnt_s)
        h2_prev[...] = jnp.zeros_like(h2_prev)

    @pl.when(g < ntiles)
    def _():
        _prompt_tile(g, nt, x_ref, mod_ref, g1_ref, g2_ref, win_ref, bias_ref, cw_ref, cbias_ref,
                     wpa_ref, wpb_ref, wo_ref, whi_ref, br_ref,
                     x1_ref, h2_ref, ko_ref, vo_ref, uo_ref, route_ref, cnt_ref,
                     kring, vring, att_s, ucarry, cnt_s, h2_prev)

    @pl.when(g == ntiles)
    def _():
        route, new_cnt = _route(h2_prev[...], whi_ref, br_ref, cnt_s[...])
        route_ref[0] = route
        cnt_ref[...] = new_cnt


def _prompt_tile(g, nt, x_ref, mod_ref, g1_ref, g2_ref, win_ref, bias_ref, cw_ref, cbias_ref,
                 wpa_ref, wpb_ref, wo_ref, whi_ref, br_ref,
                 x1_ref, h2_ref, ko_ref, vo_ref, uo_ref, route_ref, cnt_ref,
                 kring, vring, att_s, ucarry, cnt_s, h2_prev):
    j = g % nt

    @pl.when(j == 0)
    def _():
        kring[0:WINDOW, :] = jnp.zeros((WINDOW, D_ATT), BF16)
        vring[0:WINDOW, :] = jnp.zeros((WINDOW, D_ATT), BF16)
        ucarry[...] = jnp.zeros_like(ucarry)

    route_stages = _route_stages(h2_prev[...], whi_ref, br_ref, cnt_s[...],
                                 valid=jnp.where(g > 0, 1.0, 0.0).astype(F32))
    route_stages[0]()

    sh1 = mod_ref[0, 0:1, :]
    sc1 = mod_ref[0, 1:2, :]
    gt1 = mod_ref[0, 2:3, :]
    sh2 = mod_ref[0, 3:4, :]
    sc2 = mod_ref[0, 4:5, :]

    x = x_ref[0]
    h = _rms(x, g1_ref[...]) * (1.0 + sc1) + sh1
    hb = h.astype(BF16)

    qkv = _dot(hb, win_ref[:, 0:3 * D_ATT])
    q = (qkv[:, 0:D_ATT] * Q_SCALE).astype(BF16)
    k = qkv[:, D_ATT:2 * D_ATT]
    v = qkv[:, 2 * D_ATT:3 * D_ATT]
    ko_ref[0] = k
    vo_ref[0] = v
    kring[WINDOW:RING, :] = k.astype(BF16)
    vring[WINDOW:RING, :] = v.astype(BF16)

    rest0 = 3 * D_ATT
    piece = (3 * D_CONV + 2 * D_MODEL) // (TL // CHUNK - 1)
    assert piece % QUAD == 0 and (3 * D_CONV) % piece == 0
    rest = []

    base = j * TL
    for c in range(TL // CHUNK):
        lim = WINDOW - (base + c * CHUNK)
        for qd in range(2):
            ls = slice(qd * QUAD, (qd + 1) * QUAD)
            o = _attend(q[c * CHUNK:(c + 1) * CHUNK, ls],
                        kring[c * CHUNK:c * CHUNK + BAND, ls],
                        vring[c * CHUNK:c * CHUNK + BAND, ls],
                        bias_ref[qd], lim)
            att_s[c * CHUNK:(c + 1) * CHUNK, ls] = o.astype(BF16)
        if c + 1 < len(route_stages) - 1:
            route_stages[c + 1]()
        if c < TL // CHUNK - 1:
            rest.append(_dot(hb, win_ref[:, rest0 + c * piece:rest0 + (c + 1) * piece]))
    route, new_cnt = route_stages[-1]()
    route_ref[0] = route
    cnt_s[...] = new_cnt
    cnt_ref[...] = new_cnt

    kring[0:WINDOW, :] = kring[TL:RING, :]
    vring[0:WINDOW, :] = vring[TL:RING, :]

    n_conv = 3 * D_CONV // piece
    cbcv = jnp.concatenate(rest[:n_conv], axis=1)
    gates = jnp.concatenate(rest[n_conv:], axis=1)
    cb = cbcv[:, 0:D_CONV]
    u = cbcv[:, D_CONV:2 * D_CONV] * cbcv[:, 2 * D_CONV:3 * D_CONV]
    row = lax.broadcasted_iota(jnp.int32, (8, D_CONV), 0)
    prev = ucarry[...]
    r1 = pltpu.roll(u, 1, axis=0)
    r2 = pltpu.roll(u, 2, axis=0)
    u_m1 = jnp.concatenate(
        [jnp.where(row < 1, pltpu.roll(prev, 1, axis=0), r1[0:8]), r1[8:]], axis=0)
    u_m2 = jnp.concatenate(
        [jnp.where(row < 2, pltpu.roll(prev, 2, axis=0), r2[0:8]), r2[8:]], axis=0)
    yc = cw_ref[0:1, :] * u_m2 + cw_ref[1:2, :] * u_m1 + cw_ref[2:3, :] * u + cbias_ref[...]
    conv_out = (cb * yc).astype(BF16)
    ucarry[...] = u[TL - 8:TL, :]
    uo_ref[0] = u[TL - 8:TL, :]

    pa = _dot(att_s[...], wpa_ref[...])
    pb = _dot(conv_out, wpb_ref[...])
    mixin = _sigmoid(gates[:, 0:D_MODEL]) * pa + _sigmoid(gates[:, D_MODEL:]) * pb
    mix = _dot(mixin.astype(BF16), wo_ref[...])
    x1 = x + gt1 * mix
    x1_ref[0] = x1
    h2 = _rms(x1, g2_ref[...]) * (1.0 + sc2) + sh2
    _store_rows_as_tiles(h2_ref, _pack_rows(h2))

    h2_prev[...] = h2


def _prompt_main(x, mod, g1, g2, win, bias_q, cw, cbias, wpa, wpb, wo, whi, br):
    nb, seq, _ = x.shape
    nt = seq // TL
    ntiles = nb * nt
    keep = WINDOW // TL
    cur = lambda g: jnp.minimum(g, ntiles - 1)
    prv = lambda g: jnp.maximum(g - 1, 0)
    tile = lambda g: (cur(g) // nt, cur(g) % nt, 0)
    last = lambda g: (cur(g) // nt, jnp.maximum(cur(g) % nt - (nt - keep), 0), 0)
    perb = lambda g: (cur(g) // nt, 0, 0)
    in_specs = [
        pl.BlockSpec((1, TL, D_MODEL), tile),
        pl.BlockSpec((1, 6, D_MODEL), perb),
        _const_spec(g1.shape), _const_spec(g2.shape), _const_spec(win.shape),
        _const_spec(bias_q.shape), _const_spec(cw.shape), _const_spec(cbias.shape),
        _const_spec(wpa.shape), _const_spec(wpb.shape), _const_spec(wo.shape),
        _const_spec(whi.shape), _const_spec(br.shape),
    ]
    out_specs = [
        pl.BlockSpec((1, TL, D_MODEL), tile),
        pl.BlockSpec((TL * ROW_TILES, LANES), lambda g: (cur(g), 0)),
        pl.BlockSpec((1, TL, D_ATT), last),
        pl.BlockSpec((1, TL, D_ATT), last),
        pl.BlockSpec((1, 8, D_CONV), perb),
        pl.BlockSpec((1, TL, RLANES), lambda g: (prv(g) // nt, prv(g) % nt, 0)),
        pl.BlockSpec((1, RLANES), lambda g: (0, 0)),
    ]
    out_shape = [
        jax.ShapeDtypeStruct((nb, seq, D_MODEL), F32),
        jax.ShapeDtypeStruct((nb * seq * ROW_TILES, LANES), I32),
        jax.ShapeDtypeStruct((nb, WINDOW, D_ATT), F32),
        jax.ShapeDtypeStruct((nb, WINDOW, D_ATT), F32),
        jax.ShapeDtypeStruct((nb, 8, D_CONV), F32),
        jax.ShapeDtypeStruct((nb, seq, RLANES), F32),
        jax.ShapeDtypeStruct((1, RLANES), F32),
    ]
    scratch = [
        pltpu.VMEM((RING, D_ATT), BF16), pltpu.VMEM((RING, D_ATT), BF16),
        pltpu.VMEM((TL, D_ATT), BF16), pltpu.VMEM((8, D_CONV), F32),
        pltpu.VMEM((1, RLANES), F32), pltpu.VMEM((TL, D_MODEL), F32),
    ]
    return pl.pallas_call(
        functools.partial(_prompt_kernel, nt=nt, ntiles=ntiles),
        grid=(ntiles + 1,),
        in_specs=in_specs, out_specs=out_specs, out_shape=out_shape,
        scratch_shapes=scratch,
        compiler_params=pltpu.CompilerParams(
            dimension_semantics=("arbitrary",), vmem_limit_bytes=VMEM_LIMIT),
        name="prompt_main",
    )(x, mod, g1, g2, win, bias_q, cw, cbias, wpa, wpb, wo, whi, br)


def _sample_kernel(x_ref, mod_ref, ck_ref, cv_ref, up1_ref, up2_ref, cnt_in_ref,
                   g1_ref, g2_ref, win_ref, bias_ref, cw_ref, cbias_ref,
                   wpa_ref, wpb_ref, wo_ref, whi_ref, br_ref,
                   x1_ref, h2_ref, ko_ref, vo_ref, uo_ref, route_ref, cnt_ref,
                   h_s, q_s, kn_s, vn_s, kband, vband, att_s, conv_s, h2_s, *, nseq, slen):
    n = pl.program_id(0)
    ntok = nseq * slen

    @pl.when(n == 0)
    def _():
        def norm_body(i, carry):
            rows = pl.ds(pl.multiple_of(i * slen, slen), slen)
            xi = x_ref[rows, :]
            m = mod_ref[i]
            hi = _rms(xi, g1_ref[...]) * (1.0 + m[1:2, :]) + m[0:1, :]
            h_s[rows, :] = hi.astype(BF16)
            return carry
        lax.fori_loop(0, nseq, norm_body, 0)
        hb = h_s[...]
        qkv = _dot(hb, win_ref[:, 0:3 * D_ATT])
        q_s[...] = (qkv[:, 0:D_ATT] * Q_SCALE).astype(BF16)
        k = qkv[:, D_ATT:2 * D_ATT]
        v = qkv[:, 2 * D_ATT:3 * D_ATT]
        ko_ref[...] = k
        vo_ref[...] = v
        kn_s[...] = k.astype(BF16)
        vn_s[...] = v.astype(BF16)

        cbcv = _dot(hb, win_ref[:, 3 * D_ATT:3 * D_ATT + 3 * D_CONV])
        cb = cbcv[:, 0:D_CONV]
        u = cbcv[:, D_CONV:2 * D_CONV] * cbcv[:, 2 * D_CONV:3 * D_CONV]
        pos = lax.broadcasted_iota(jnp.int32, (ntok, D_CONV), 0) & (slen - 1)
        u_m1 = jnp.where(pos < 1, up1_ref[...], pltpu.roll(u, 1, axis=0))
        u_m2 = jnp.where(pos < 2, up2_ref[...], pltpu.roll(u, 2, axis=0))
        yc = cw_ref[0:1, :] * u_m2 + cw_ref[1:2, :] * u_m1 + cw_ref[2:3, :] * u + cbias_ref[...]
        conv_s[...] = (cb * yc).astype(BF16)
        uo_ref[...] = u

    rows = pl.ds(pl.multiple_of(n * slen, slen), slen)
    kband[0:WINDOW, :] = ck_ref[0].astype(BF16)
    vband[0:WINDOW, :] = cv_ref[0].astype(BF16)
    kband[WINDOW:WINDOW + slen, :] = kn_s[rows, :]
    vband[WINDOW:WINDOW + slen, :] = vn_s[rows, :]
    qn = q_s[rows, :]
    outs = []
    for qd in range(2):
        ls = slice(qd * QUAD, (qd + 1) * QUAD)
        outs.append(_attend(qn[:, ls], kband[:, ls], vband[:, ls], bias_ref[qd], None))
    att_s[rows, :] = jnp.concatenate(outs, axis=1).astype(BF16)

    @pl.when(n == nseq - 1)
    def _():
        gates = _dot(h_s[...], win_ref[:, 3 * D_ATT + 3 * D_CONV:])
        pa = _dot(att_s[...], wpa_ref[...])
        pb = _dot(conv_s[...], wpb_ref[...])
        mixin = _sigmoid(gates[:, 0:D_MODEL]) * pa + _sigmoid(gates[:, D_MODEL:]) * pb
        x1_ref[...] = _dot(mixin.astype(BF16), wo_ref[...])

        def res_body(i, carry):
            r = pl.ds(pl.multiple_of(i * slen, slen), slen)
            m = mod_ref[i]
            x1 = x_ref[r, :] + m[2:3, :] * x1_ref[r, :]
            x1_ref[r, :] = x1
            h2_s[r, :] = _rms(x1, g2_ref[...]) * (1.0 + m[4:5, :]) + m[3:4, :]
            return carry
        lax.fori_loop(0, nseq, res_body, 0)

        h2 = h2_s[...]
        _store_rows_as_tiles(h2_ref, _pack_rows(h2))
        route, new_cnt = _route(h2, whi_ref, br_ref, cnt_in_ref[...])
        route_ref[...] = route
        cnt_ref[...] = new_cnt


def _sample_main(x2d, mod, ck, cv, up1, up2, cnt_in, g1, g2, win, bias_s, cw, cbias,
                 wpa, wpb, wo, whi, br, nseq, slen):
    ntok = nseq * slen
    args = (x2d, mod, ck, cv, up1, up2, cnt_in, g1, g2, win, bias_s, cw, cbias,
            wpa, wpb, wo, whi, br)
    in_specs = []
    for idx, a in enumerate(args):
        if idx in (2, 3):
            in_specs.append(pl.BlockSpec((1, WINDOW, D_ATT), lambda n: (n, 0, 0)))
        else:
            in_specs.append(_const_spec(a.shape))
    whole = lambda shape: pl.BlockSpec(shape, lambda n: (0,) * len(shape))
    outs = [((ntok, D_MODEL), F32), ((ntok * ROW_TILES, LANES), I32), ((ntok, D_ATT), F32),
            ((ntok, D_ATT), F32), ((ntok, D_CONV), F32), ((ntok, RLANES), F32), ((1, RLANES), F32)]
    scratch = [
        pltpu.VMEM((ntok, D_MODEL), BF16), pltpu.VMEM((ntok, D_ATT), BF16),
        pltpu.VMEM((ntok, D_ATT), BF16), pltpu.VMEM((ntok, D_ATT), BF16),
        pltpu.VMEM((WINDOW + slen, D_ATT), BF16), pltpu.VMEM((WINDOW + slen, D_ATT), BF16),
        pltpu.VMEM((ntok, D_ATT), BF16), pltpu.VMEM((ntok, D_CONV), BF16),
        pltpu.VMEM((ntok, D_MODEL), F32),
    ]
    return pl.pallas_call(
        functools.partial(_sample_kernel, nseq=nseq, slen=slen),
        grid=(nseq,),
        in_specs=in_specs,
        out_specs=[whole(s) for s, _ in outs],
        out_shape=[jax.ShapeDtypeStruct(s, d) for s, d in outs],
        scratch_shapes=scratch,
        compiler_params=pltpu.CompilerParams(
            dimension_semantics=("arbitrary",), vmem_limit_bytes=VMEM_LIMIT),
        name="sample_main",
    )(*args)


def _issue_rows(n, body):
    def group(g, carry):
        for u in range(ISSUE_UNROLL):
            body(g * ISSUE_UNROLL + u, u)
        return carry
    lax.fori_loop(0, n // ISSUE_UNROLL, group, 0)


def _dispatch_kernel(pend_ref, d1_ref, d2_ref, hp_ref, hs_ref, xs_out, zbuf, sem, zsem, *,
                     np_tiles, nblocks):
    i = pl.program_id(0)
    blk_rows = MOE_BLK * ROW_TILES

    @pl.when(i == 0)
    def _():
        zbuf[...] = jnp.zeros_like(zbuf)

        def zcopy(e):
            start = pl.multiple_of((pend_ref[e + 1] - MOE_BLK) * ROW_TILES, blk_rows)
            return pltpu.make_async_copy(zbuf, xs_out.at[pl.ds(start, blk_rows)], zsem)

        def zstart(e, carry):
            @pl.when(pend_ref[e + 1] > pend_ref[e])
            def _():
                zcopy(e).start()
            return carry

        def zwait(e, carry):
            @pl.when(pend_ref[e + 1] > pend_ref[e])
            def _():
                zcopy(e).wait()
            return carry
        lax.fori_loop(0, N_EXPERTS, zstart, 0)
        lax.fori_loop(0, N_EXPERTS, zwait, 0)

        def tcopy(bk):
            start = pl.multiple_of(bk * blk_rows, blk_rows)
            return pltpu.make_async_copy(zbuf, xs_out.at[pl.ds(start, blk_rows)], zsem)

        def tstart(bk, carry):
            tcopy(bk).start()
            return carry

        def twait(bk, carry):
            tcopy(bk).wait()
            return carry
        used = pend_ref[N_EXPERTS] // MOE_BLK
        lax.fori_loop(used, nblocks, tstart, 0)
        lax.fori_loop(used, nblocks, twait, 0)

    def scatter_tile(src):
        def row(r, u):
            pltpu.make_async_copy(_tile(src, r), _tile(xs_out, d1_ref[r]), sem).start(priority=u % 2)
            pltpu.make_async_copy(_tile(src, r), _tile(xs_out, d2_ref[r]), sem).start(
                priority=(u + 1) % 2)
        _issue_rows(TM, row)
        for _ in range(2):
            pltpu.make_async_copy(src, xs_out.at[pl.ds(0, TM * ROW_TILES)], sem).wait()

    @pl.when(i < np_tiles)
    def _():
        scatter_tile(hp_ref)

    @pl.when(i >= np_tiles)
    def _():
        scatter_tile(hs_ref)


def _dispatch(pend, d1, d2, h2p, h2s, nslots):
    np_tiles = h2p.shape[0] // (TM * ROW_TILES)
    ns_tiles = h2s.shape[0] // (TM * ROW_TILES)
    smem_tile = pl.BlockSpec((TM,), lambda i, *_: (i,), memory_space=pltpu.SMEM)
    rows = TM * ROW_TILES
    return pl.pallas_call(
        functools.partial(_dispatch_kernel, np_tiles=np_tiles, nblocks=nslots // MOE_BLK),
        grid_spec=pltpu.PrefetchScalarGridSpec(
            num_scalar_prefetch=1,
            grid=(np_tiles + ns_tiles,),
            in_specs=[smem_tile, smem_tile,
                      pl.BlockSpec((rows, LANES), lambda i, *_: (jnp.minimum(i, np_tiles - 1), 0)),
                      pl.BlockSpec((rows, LANES), lambda i, *_: (jnp.maximum(i - np_tiles, 0), 0))],
            out_specs=pl.BlockSpec(memory_space=pl.ANY),
            scratch_shapes=[pltpu.VMEM((MOE_BLK * ROW_TILES, LANES), I32),
                            pltpu.SemaphoreType.DMA(()), pltpu.SemaphoreType.DMA(())],
        ),
        out_shape=jax.ShapeDtypeStruct((nslots * ROW_TILES, LANES), I32),
        compiler_params=pltpu.CompilerParams(dimension_semantics=("arbitrary",)),
        name="dispatch",
    )(pend, d1, d2, h2p, h2s)


def _expert_kernel(blk_e_ref, nblk_ref, first_ref, wslot_ref, next_e_ref,
                   xs_ref, wg_hbm, wu_hbm, wd_hbm, y_ref,
                   wg_f, wu_f, wd_f, wg_b, wu_b, wd_b, wsem):
    i = pl.program_id(0)
    live = i < nblk_ref[0]

    def weight_copies(e, slot):
        return (pltpu.make_async_copy(wg_hbm.at[e], wg_f.at[slot], wsem.at[slot, 0]),
                pltpu.make_async_copy(wu_hbm.at[e], wu_f.at[slot], wsem.at[slot, 1]),
                pltpu.make_async_copy(wd_hbm.at[e], wd_f.at[slot], wsem.at[slot, 2]))

    @pl.when(i == 0)
    def _():
        for cp in weight_copies(blk_e_ref[0], 0):
            cp.start()

    @pl.when(live & (first_ref[i] == 1))
    def _():
        slot = wslot_ref[i]
        for cp in weight_copies(blk_e_ref[i], slot):
            cp.wait()
        wg_b[...] = wg_f[slot].astype(BF16)
        wu_b[...] = wu_f[slot].astype(BF16)
        wd_b[...] = wd_f[slot].astype(BF16)

        @pl.when(next_e_ref[i] >= 0)
        def _():
            for cp in weight_copies(next_e_ref[i], 1 - slot):
                cp.start()

    @pl.when(live)
    def _():
        x_lo, x_hi = _unpack_rows(_load_tiles_as_rows(xs_ref, MOE_BLK))
        x_lo = x_lo.astype(BF16)
        x_hi = x_hi.astype(BF16)
        g = _dot(x_lo, wg_b[0:HALF, :]) + _dot(x_hi, wg_b[HALF:, :])
        u = _dot(x_lo, wu_b[0:HALF, :]) + _dot(x_hi, wu_b[HALF:, :])
        a = (g * _sigmoid(g)) * u
        _store_rows_as_tiles(y_ref, _pack_rows(_dot(a.astype(BF16), wd_b[...])))

    @pl.when(jnp.logical_not(live))
    def _():
        y_ref[...] = jnp.zeros_like(y_ref)


def _experts(blk_e, nblk, first, wslot, next_e, xs, wg, wu, wd):
    blk_rows = MOE_BLK * ROW_TILES
    nblocks = xs.shape[0] // blk_rows
    row_map = lambda i, be, nb, *_: (jnp.minimum(i, nb[0] - 1), 0)
    any_spec = pl.BlockSpec(memory_space=pl.ANY)
    return pl.pallas_call(
        _expert_kernel,
        grid_spec=pltpu.PrefetchScalarGridSpec(
            num_scalar_prefetch=5,
            grid=(nblocks,),
            in_specs=[pl.BlockSpec((blk_rows, LANES), row_map), any_spec, any_spec, any_spec],
            out_specs=pl.BlockSpec((blk_rows, LANES), lambda i, *_: (i, 0)),
            scratch_shapes=[pltpu.VMEM((2, D_MODEL, D_EXPERT), F32),
                            pltpu.VMEM((2, D_MODEL, D_EXPERT), F32),
                            pltpu.VMEM((2, D_EXPERT, D_MODEL), F32),
                            pltpu.VMEM((D_MODEL, D_EXPERT), BF16),
                            pltpu.VMEM((D_MODEL, D_EXPERT), BF16),
                            pltpu.VMEM((D_EXPERT, D_MODEL), BF16),
                            pltpu.SemaphoreType.DMA((2, 3))],
        ),
        out_shape=jax.ShapeDtypeStruct(xs.shape, I32),
        compiler_params=pltpu.CompilerParams(
            dimension_semantics=("arbitrary",), vmem_limit_bytes=VMEM_LIMIT),
        name="experts",
    )(blk_e, nblk, first, wslot, next_e, xs, wg, wu, wd)


def _combine_kernel(d1_ref, d2_ref, d1n_ref, d2n_ref, y_hbm, x1_ref, route_ref, gate_ref, gf_ref,
                    o_ref, a0, b0, a1, b1, sem, *, ntiles):
    t = pl.program_id(0)
    bufs = ((a0, b0), (a1, b1))

    def gather(i1_ref, i2_ref, par):
        buf_a, buf_b = bufs[par]

        def row(r, u):
            pltpu.make_async_copy(_tile(y_hbm, i1_ref[r]), _tile(buf_a, r),
                                  sem.at[par]).start(priority=u % 2)
            pltpu.make_async_copy(_tile(y_hbm, i2_ref[r]), _tile(buf_b, r),
                                  sem.at[par]).start(priority=(u + 1) % 2)
        _issue_rows(TM, row)

    @pl.when(t == 0)
    def _():
        gather(d1_ref, d2_ref, 0)

    def step(par):
        buf_a, buf_b = bufs[par]
        for buf in (buf_a, buf_b):
            pltpu.make_async_copy(y_hbm.at[pl.ds(0, TM * ROW_TILES)], buf, sem.at[par]).wait()

        @pl.when(t + 1 < ntiles)
        def _():
            gather(d1n_ref, d2n_ref, 1 - par)

        route = route_ref[0]
        w1 = route[:, 4:5]
        w2 = route[:, 5:6]
        a_lo, a_hi = _unpack_rows(_load_tiles_as_rows(buf_a, TM))
        b_lo, b_hi = _unpack_rows(_load_tiles_as_rows(buf_b, TM))
        ffn = jnp.concatenate([w1 * a_lo + w2 * b_lo, w1 * a_hi + w2 * b_hi], axis=1)
        x2 = x1_ref[0] + gate_ref[0] * ffn
        o_ref[0] = _rms(x2, gf_ref[...])

    for par in range(2):
        pl.when(t % 2 == par)(functools.partial(step, par))


def _combine(d1, d2, y, x1, route, gate, gf, tok_base):
    nb, seq, _ = x1.shape
    nt = seq // TM
    ntiles = nb * nt
    blk0 = tok_base // TM
    smem = lambda fn: pl.BlockSpec((TM,), fn, memory_space=pltpu.SMEM)
    cur = lambda t: (blk0 + t,)
    nxt = lambda t: (blk0 + jnp.minimum(t + 1, ntiles - 1),)
    tile = lambda t: (t // nt, t % nt, 0)
    grows = gate.shape[1]
    gate_spec = (pl.BlockSpec((1, 1, D_MODEL), lambda t: (t // nt, 0, 0)) if grows == 1
                 else pl.BlockSpec((1, TM, D_MODEL), tile))
    return pl.pallas_call(
        functools.partial(_combine_kernel, ntiles=ntiles),
        grid=(ntiles,),
        in_specs=[smem(cur), smem(cur), smem(nxt), smem(nxt), pl.BlockSpec(memory_space=pl.ANY),
                  pl.BlockSpec((1, TM, D_MODEL), tile),
                  pl.BlockSpec((1, TM, RLANES), tile),
                  gate_spec,
                  pl.BlockSpec((1, D_MODEL), lambda t: (0, 0))],
        out_specs=pl.BlockSpec((1, TM, D_MODEL), tile),
        out_shape=jax.ShapeDtypeStruct(x1.shape, F32),
        scratch_shapes=[pltpu.VMEM((TM * ROW_TILES, LANES), I32)] * 4 + [
            pltpu.SemaphoreType.DMA((2,))],
        compiler_params=pltpu.CompilerParams(
            dimension_semantics=("arbitrary",), vmem_limit_bytes=VMEM_LIMIT),
        name="combine",
    )(d1, d2, d1, d2, y, x1, route, gate, gf)


def _band_bias(rel_bias, rows, keys):
    n = rows - 1 + keys
    dist = WINDOW + rows - 1 - np.arange(n + 1)
    flipped = rel_bias[:, np.clip(dist, -MAX_REL, MAX_REL) + MAX_REL]
    skew = jnp.tile(flipped, (1, rows))[:, :rows * n].reshape(N_HEADS, rows, n)
    b = skew[:, :, rows - 1:rows - 1 + keys]
    return b.reshape(2, 4 * rows, keys)


def kernel(x_prompt, x_sample, cache_attn_k, cache_attn_v, state_conv, c_prompt, c_sample,
           w_ada, b_ada, norm1_g, norm2_g, w_in, rel_bias, conv_w, conv_b, w_pa, w_pb, w_o,
           w_group, b_group, w_expert, b_expert, w_e_gate, w_e_up, w_e_down, final_g):
    assert w_ada.shape[0] == 1, "single trunk layer"
    nb, seq, _ = x_prompt.shape
    nseq, slen, _ = x_sample.shape
    ntok_p = nb * seq
    ntok_s = nseq * slen
    ntok = ntok_p + ntok_s
    assert seq % TL == 0 and WINDOW % TL == 0 and seq % TM == 0 and ntok_s % TM == 0
    assert slen >= 2 and slen & (slen - 1) == 0 and slen % 16 == 0

    n_c = nb + nseq
    n_pad = -(-n_c // 8) * 8
    c_all = jnp.concatenate([c_prompt, c_sample, jnp.zeros((n_pad - n_c, D_MODEL), F32)], axis=0)
    mod = _ada(c_all, w_ada[0], b_ada[0]).reshape(n_pad, 6, D_MODEL)
    mod_p = mod[:nb]
    mod_s = mod[nb:n_c]

    win = w_in[0].astype(BF16)
    wpa = w_pa[0].astype(BF16)
    wpb = w_pb[0].astype(BF16)
    wo = w_o[0].astype(BF16)
    g1 = norm1_g[0].reshape(1, D_MODEL)
    g2 = norm2_g[0].reshape(1, D_MODEL)
    gf = final_g.reshape(1, D_MODEL)
    cw = jnp.concatenate([conv_w[0], jnp.zeros((8 - conv_w.shape[1], D_CONV), F32)], axis=0)
    cbias = conv_b[0].reshape(1, D_CONV)
    wr = jnp.concatenate([w_expert[0], w_group[0],
                          jnp.zeros((D_MODEL, RLANES - N_EXPERTS - N_GROUPS), F32)], axis=1)
    whi = wr.astype(BF16)
    br = jnp.concatenate([b_expert[0], b_group[0],
                          jnp.zeros((RLANES - N_EXPERTS - N_GROUPS,), F32)]).reshape(1, RLANES)
    bias_p = _band_bias(rel_bias[0] * LOG2E, CHUNK, BAND)
    bias_s = _band_bias(rel_bias[0] * LOG2E, slen, WINDOW + slen)

    x1p, h2p, kp, vp, up8, route_p, cnt_p = _prompt_main(
        x_prompt, mod_p, g1, g2, win, bias_p, cw, cbias, wpa, wpb, wo, whi, br)

    st = state_conv[0]
    up1 = jnp.zeros((nseq, slen, D_CONV), F32).at[:, 0].set(st[:, 1]).reshape(ntok_s, D_CONV)
    up2 = (jnp.zeros((nseq, slen, D_CONV), F32).at[:, 0].set(st[:, 0]).at[:, 1].set(st[:, 1])
           .reshape(ntok_s, D_CONV))
    ck = cache_attn_k[0].reshape(nseq, WINDOW, D_ATT)
    cv = cache_attn_v[0].reshape(nseq, WINDOW, D_ATT)
    x1s, h2s, ks, vs, us, route_s, cnt = _sample_main(
        x_sample.reshape(ntok_s, D_MODEL), mod_s, ck, cv, up1, up2, cnt_p,
        g1, g2, win, bias_s, cw, cbias, wpa, wpb, wo, whi, br, nseq, slen)

    route_all = jnp.concatenate([route_p.reshape(ntok_p, RLANES)[:, :4], route_s[:, :4]], axis=0)
    experts = route_all[:, 0:2].astype(jnp.int32)
    ranks = route_all[:, 2:4].astype(jnp.int32)
    counts = cnt[0, :N_EXPERTS].astype(jnp.int32)
    pcounts = (counts + MOE_BLK - 1) // MOE_BLK * MOE_BLK
    pend = jnp.cumsum(pcounts)
    pstart = pend - pcounts
    eids = jnp.arange(N_EXPERTS, dtype=jnp.int32)
    dest = jnp.sum(jnp.where(experts[..., None] == eids, pstart, 0), axis=-1) + ranks
    d1 = dest[:, 0]
    d2 = dest[:, 1]
    nblocks = (2 * ntok) // MOE_BLK + N_EXPERTS
    blk_start = jnp.arange(nblocks, dtype=jnp.int32) * MOE_BLK
    blk_e = jnp.minimum(jnp.sum((pend[None, :] <= blk_start[:, None]).astype(jnp.int32), axis=1),
                        N_EXPERTS - 1)
    nblk = (pend[-1:] // MOE_BLK).astype(jnp.int32)
    pend0 = jnp.concatenate([jnp.zeros((1,), jnp.int32), pend.astype(jnp.int32)])

    xs = _dispatch(pend0, d1, d2, h2p, h2s, nblocks * MOE_BLK)
    blk_id = jnp.arange(nblocks, dtype=jnp.int32)
    first = (blk_id < nblk[0]) & ((blk_id == 0) | (blk_e != jnp.roll(blk_e, 1)))
    wslot = (jnp.cumsum(first.astype(jnp.int32)) - 1) % 2
    later_first = lax.cummin(jnp.where(first, blk_id, nblocks)[::-1])[::-1]
    next_first = jnp.concatenate([later_first[1:], jnp.full((1,), nblocks, jnp.int32)])
    next_e = jnp.where(next_first < nblocks, blk_e[jnp.minimum(next_first, nblocks - 1)], -1)
    y = _experts(blk_e, nblk, first.astype(jnp.int32), wslot.astype(jnp.int32),
                 next_e.astype(jnp.int32), xs, w_e_gate[0], w_e_up[0], w_e_down[0])

    y_prompt = _combine(d1, d2, y, x1p, route_p, mod_p[:, 5:6, :], gf, 0)
    gate_s = jnp.repeat(mod_s[:, 5, :], slen, axis=0).reshape(1, ntok_s, D_MODEL)
    y_sample = _combine(d1, d2, y, x1s.reshape(1, ntok_s, D_MODEL),
                        route_s.reshape(1, ntok_s, RLANES), gate_s, gf, ntok_p)

    new_k_p = kp.reshape(1, nb, WINDOW, N_HEADS, HEAD_DIM)
    new_v_p = vp.reshape(1, nb, WINDOW, N_HEADS, HEAD_DIM)
    new_conv_p = up8[:, 6:8, :].reshape(1, nb, 2, D_CONV)
    new_k_s = ks.reshape(1, nseq, slen, N_HEADS, HEAD_DIM)
    new_v_s = vs.reshape(1, nseq, slen, N_HEADS, HEAD_DIM)
    new_conv_s = us.reshape(nseq, slen, D_CONV)[:, slen - 2:, :].reshape(1, nseq, 2, D_CONV)
    return (y_prompt, y_sample.reshape(nseq, slen, D_MODEL), new_k_p, new_v_p, new_conv_p,
            new_k_s, new_v_s, new_conv_s)
```

```python
import functools

import numpy as np
import jax
import jax.numpy as jnp
from jax import lax
from jax.experimental import pallas as pl
from jax.experimental.pallas import tpu as pltpu

F32 = jnp.float32
BF16 = jnp.bfloat16
I32 = jnp.int32

D_MODEL = 1024
CHUNK = 64
LEFT = 8
WINDOW = LEFT * CHUNK
BAND = WINDOW + CHUNK
N_HEADS = 8
HEAD_DIM = 64
D_ATT = N_HEADS * HEAD_DIM
QUAD = 256
MAX_REL = 128
D_CONV = 512
N_GROUPS = 4
EPG = 8
N_EXPERTS = 32
D_EXPERT = 512
EPS = 1e-6
NEG = -1e30
LOG2E = float(np.log2(np.e))
Q_SCALE = HEAD_DIM ** -0.5 * LOG2E

TL = 512
RING = WINDOW + TL
MOE_BLK = 512
TM = 512
RLANES = 128
LANES = 128
HALF = D_MODEL // 2
ROW_TILES = HALF // LANES
ISSUE_UNROLL = 8
VMEM_LIMIT = 56 * 1024 * 1024


def _const_spec(shape):
    nd = len(shape)
    return pl.BlockSpec(shape, lambda *_: (0,) * nd, pipeline_mode=pl.Buffered(1))


def _dot(a, b):
    return jnp.dot(a, b, preferred_element_type=F32)


def _sigmoid(x):
    return 1.0 / (1.0 + jnp.exp(-x))


def _rms(x, g):
    ms = jnp.mean(x * x, axis=-1, keepdims=True)
    return x * lax.rsqrt(ms + EPS) * g


def _pack_rows(val):
    lo = lax.bitcast_convert_type(val[:, :HALF], I32) + 0x8000
    hi = lax.bitcast_convert_type(val[:, HALF:], I32) + 0x8000
    return (hi & -65536) | lax.shift_right_logical(lo, 16)


def _unpack_rows(packed):
    lo = lax.bitcast_convert_type(lax.shift_left(packed, 16), F32)
    hi = lax.bitcast_convert_type(packed & -65536, F32)
    return lo, hi


def _store_rows_as_tiles(ref, packed):
    r = packed.shape[0]
    for c in range(ROW_TILES):
        ref[pl.ds(c, r, stride=ROW_TILES), :] = packed[:, c * LANES:(c + 1) * LANES]


def _load_tiles_as_rows(ref, r):
    return jnp.concatenate(
        [ref[pl.ds(c, r, stride=ROW_TILES), :] for c in range(ROW_TILES)], axis=1)


def _tile(ref, row):
    return ref.at[pl.ds(pl.multiple_of(row * ROW_TILES, ROW_TILES), ROW_TILES)]


def _ada_kernel(c_ref, w_ref, b_ref, o_ref):
    c = c_ref[...]
    s = c * _sigmoid(c)
    o_ref[...] = _dot(s.astype(BF16), w_ref[...].astype(BF16)) + b_ref[...]


def _ada(c_all, w_ada, b_ada):
    n = c_all.shape[0]
    nb = 1024
    return pl.pallas_call(
        _ada_kernel,
        grid=(6 * D_MODEL // nb,),
        in_specs=[pl.BlockSpec((n, D_MODEL), lambda i: (0, 0)),
                  pl.BlockSpec((D_MODEL, nb), lambda i: (0, i)),
                  pl.BlockSpec((1, nb), lambda i: (0, i))],
        out_specs=pl.BlockSpec((n, nb), lambda i: (0, i)),
        out_shape=jax.ShapeDtypeStruct((n, 6 * D_MODEL), F32),
        name="ada",
    )(c_all, w_ada, b_ada.reshape(1, -1))


def _attend(q, kb, vb, bias, lim):
    r = q.shape[0]
    nk = kb.shape[0]
    assert r & (r - 1) == 0
    qt = jnp.concatenate([q] * 4, axis=0)
    rowh = lax.broadcasted_iota(jnp.int32, (4 * r, QUAD), 0) >> (r.bit_length() - 1)
    laneh = lax.broadcasted_iota(jnp.int32, (4 * r, QUAD), 1) >> 6
    qm = jnp.where(rowh == laneh, qt, jnp.zeros_like(qt))
    s = lax.dot_general(qm, kb, (((1,), (1,)), ((), ())), preferred_element_type=F32)
    valid = None
    if lim is not None:
        valid = lax.broadcasted_iota(jnp.int32, (r, nk), 1) >= lim
    ps, inv_l = [], []
    for h in range(4):
        sh = s[h * r:(h + 1) * r] + bias[h * r:(h + 1) * r]
        if valid is not None:
            sh = jnp.where(valid, sh, NEG)
        m = jnp.max(sh, axis=1, keepdims=True)
        ph = jnp.exp2(sh - m)
        inv_l.append(1.0 / jnp.sum(ph, axis=1, keepdims=True))
        ps.append(ph.astype(BF16))
    o = _dot(jnp.concatenate(ps, axis=0), vb)
    lane_o = lax.broadcasted_iota(jnp.int32, (r, QUAD), 1) >> 6
    out = o[0:r] * inv_l[0]
    for h in range(1, 4):
        out = jnp.where(lane_o == h, o[h * r:(h + 1) * r] * inv_l[h], out)
    return out


def _route_stages(h2, whi_ref, br_ref, cnt, valid=None):
    r = h2.shape[0]
    lane = lax.broadcasted_iota(jnp.int32, (r, RLANES), 1)
    lane_f = lane.astype(F32)
    big = jnp.float32(1000.0)
    v = {}

    def logits():
        v["logits"] = _dot(h2.astype(BF16), whi_ref[...]) + br_ref[...]

    def group():
        lg = jnp.where((lane >= N_EXPERTS) & (lane < N_EXPERTS + N_GROUPS), v["logits"], NEG)
        mg = jnp.max(lg, axis=1, keepdims=True)
        v["gi"] = jnp.min(jnp.where(lg == mg, lane_f, big), axis=1, keepdims=True) - N_EXPERTS
        v["pg"] = 1.0 / jnp.sum(jnp.exp(lg - mg), axis=1, keepdims=True)

    def top1():
        grp_of_lane = (lane >> 3).astype(F32)
        le = jnp.where((lane < N_EXPERTS) & (grp_of_lane == v["gi"]), v["logits"], NEG)
        v["m1"] = jnp.max(le, axis=1, keepdims=True)
        v["i1"] = jnp.min(jnp.where(le == v["m1"], lane_f, big), axis=1, keepdims=True)
        v["le"] = le

    def top2():
        sel1 = lane_f == v["i1"]
        le2 = jnp.where(sel1, NEG, v["le"])
        m2 = jnp.max(le2, axis=1, keepdims=True)
        v["i2"] = jnp.min(jnp.where(le2 == m2, lane_f, big), axis=1, keepdims=True)
        rr = jnp.exp(m2 - v["m1"])
        inv = v["pg"] / (1.0 + rr)
        v["w1"] = inv
        v["w2"] = inv * rr
        v["sel1"] = sel1
        v["sel2"] = lane_f == v["i2"]

    def ranks():
        oh = jnp.where(v["sel1"] | v["sel2"], 1.0 if valid is None else valid, 0.0).astype(F32)
        ri = lax.broadcasted_iota(jnp.int32, (r, r), 0)
        ci = lax.broadcasted_iota(jnp.int32, (r, r), 1)
        tri = jnp.where(ri > ci, 1.0, 0.0).astype(BF16)
        v["before"] = _dot(tri, oh.astype(BF16)) + cnt
        v["new_cnt"] = cnt + jnp.sum(oh, axis=0, keepdims=True)

    def assemble():
        r1 = jnp.sum(jnp.where(v["sel1"], v["before"], 0.0), axis=1, keepdims=True)
        r2 = jnp.sum(jnp.where(v["sel2"], v["before"], 0.0), axis=1, keepdims=True)
        route = jnp.where(lane == 0, v["i1"], 0.0)
        route = jnp.where(lane == 1, v["i2"], route)
        route = jnp.where(lane == 2, r1, route)
        route = jnp.where(lane == 3, r2, route)
        route = jnp.where(lane == 4, v["w1"], route)
        route = jnp.where(lane == 5, v["w2"], route)
        return route, v["new_cnt"]

    return [logits, group, top1, top2, ranks, assemble]


def _route(h2, whi_ref, br_ref, cnt):
    stages = _route_stages(h2, whi_ref, br_ref, cnt)
    for stage in stages[:-1]:
        stage()
    return stages[-1]()


def _prompt_kernel(x_ref, mod_ref, g1_ref, g2_ref, win_ref, bias_ref, cw_ref, cbias_ref,
                   wpa_ref, wpb_ref, wo_ref, whi_ref, br_ref,
                   x1_ref, h2_ref, ko_ref, vo_ref, uo_ref, route_ref, cnt_ref,
                   kring, vring, att_s, ucarry, cnt_s, h2_prev, *, nt, ntiles):
    g = pl.program_id(0)

    @pl.when(g == 0)
    def _():
        cnt_s[...] = jnp.zeros_like(cnt_s)
        h2_prev[...] = jnp.zeros_like(h2_prev)

    @pl.when(g < ntiles)
    def _():
        _prompt_tile(g, nt, x_ref, mod_ref, g1_ref, g2_ref, win_ref, bias_ref, cw_ref, cbias_ref,
                     wpa_ref, wpb_ref, wo_ref, whi_ref, br_ref,
                     x1_ref, h2_ref, ko_ref, vo_ref, uo_ref, route_ref, cnt_ref,
                     kring, vring, att_s, ucarry, cnt_s, h2_prev)

    @pl.when(g == ntiles)
    def _():
        route, new_cnt = _route(h2_prev[...], whi_ref, br_ref, cnt_s[...])
        route_ref[0] = route
        cnt_ref[...] = new_cnt


def _prompt_tile(g, nt, x_ref, mod_ref, g1_ref, g2_ref, win_ref, bias_ref, cw_ref, cbias_ref,
                 wpa_ref, wpb_ref, wo_ref, whi_ref, br_ref,
                 x1_ref, h2_ref, ko_ref, vo_ref, uo_ref, route_ref, cnt_ref,
                 kring, vring, att_s, ucarry, cnt_s, h2_prev):
    j = g % nt

    @pl.when(j == 0)
    def _():
        kring[0:WINDOW, :] = jnp.zeros((WINDOW, D_ATT), BF16)
        vring[0:WINDOW, :] = jnp.zeros((WINDOW, D_ATT), BF16)
        ucarry[...] = jnp.zeros_like(ucarry)

    route_stages = _route_stages(h2_prev[...], whi_ref, br_ref, cnt_s[...],
                                 valid=jnp.where(g > 0, 1.0, 0.0).astype(F32))
    route_stages[0]()

    sh1 = mod_ref[0, 0:1, :]
    sc1 = mod_ref[0, 1:2, :]
    gt1 = mod_ref[0, 2:3, :]
    sh2 = mod_ref[0, 3:4, :]
    sc2 = mod_ref[0, 4:5, :]

    x = x_ref[0]
    h = _rms(x, g1_ref[...]) * (1.0 + sc1) + sh1
    hb = h.astype(BF16)

    qkv = _dot(hb, win_ref[:, 0:3 * D_ATT])
    q = (qkv[:, 0:D_ATT] * Q_SCALE).astype(BF16)
    k = qkv[:, D_ATT:2 * D_ATT]
    v = qkv[:, 2 * D_ATT:3 * D_ATT]
    ko_ref[0] = k
    vo_ref[0] = v
    kring[WINDOW:RING, :] = k.astype(BF16)
    vring[WINDOW:RING, :] = v.astype(BF16)

    base = j * TL
    for c in range(TL // CHUNK):
        lim = WINDOW - (base + c * CHUNK)
        for qd in range(2):
            ls = slice(qd * QUAD, (qd + 1) * QUAD)
            o = _attend(q[c * CHUNK:(c + 1) * CHUNK, ls],
                        kring[c * CHUNK:c * CHUNK + BAND, ls],
                        vring[c * CHUNK:c * CHUNK + BAND, ls],
                        bias_ref[qd], lim)
            att_s[c * CHUNK:(c + 1) * CHUNK, ls] = o.astype(BF16)
        if c + 1 < len(route_stages) - 1:
            route_stages[c + 1]()
    route, new_cnt = route_stages[-1]()
    route_ref[0] = route
    cnt_s[...] = new_cnt
    cnt_ref[...] = new_cnt

    kring[0:WINDOW, :] = kring[TL:RING, :]
    vring[0:WINDOW, :] = vring[TL:RING, :]

    cbcv = _dot(hb, win_ref[:, 3 * D_ATT:3 * D_ATT + 3 * D_CONV])
    gates = _dot(hb, win_ref[:, 3 * D_ATT + 3 * D_CONV:])
    cb = cbcv[:, 0:D_CONV]
    u = cbcv[:, D_CONV:2 * D_CONV] * cbcv[:, 2 * D_CONV:3 * D_CONV]
    row = lax.broadcasted_iota(jnp.int32, (8, D_CONV), 0)
    prev = ucarry[...]
    r1 = pltpu.roll(u, 1, axis=0)
    r2 = pltpu.roll(u, 2, axis=0)
    u_m1 = jnp.concatenate(
        [jnp.where(row < 1, pltpu.roll(prev, 1, axis=0), r1[0:8]), r1[8:]], axis=0)
    u_m2 = jnp.concatenate(
        [jnp.where(row < 2, pltpu.roll(prev, 2, axis=0), r2[0:8]), r2[8:]], axis=0)
    yc = cw_ref[0:1, :] * u_m2 + cw_ref[1:2, :] * u_m1 + cw_ref[2:3, :] * u + cbias_ref[...]
    conv_out = (cb * yc).astype(BF16)
    ucarry[...] = u[TL - 8:TL, :]
    uo_ref[0] = u[TL - 8:TL, :]

    pa = _dot(att_s[...], wpa_ref[...])
    pb = _dot(conv_out, wpb_ref[...])
    mixin = _sigmoid(gates[:, 0:D_MODEL]) * pa + _sigmoid(gates[:, D_MODEL:]) * pb
    mix = _dot(mixin.astype(BF16), wo_ref[...])
    x1 = x + gt1 * mix
    x1_ref[0] = x1
    h2 = _rms(x1, g2_ref[...]) * (1.0 + sc2) + sh2
    _store_rows_as_tiles(h2_ref, _pack_rows(h2))

    h2_prev[...] = h2


def _prompt_main(x, mod, g1, g2, win, bias_q, cw, cbias, wpa, wpb, wo, whi, br):
    nb, seq, _ = x.shape
    nt = seq // TL
    ntiles = nb * nt
    keep = WINDOW // TL
    cur = lambda g: jnp.minimum(g, ntiles - 1)
    prv = lambda g: jnp.maximum(g - 1, 0)
    tile = lambda g: (cur(g) // nt, cur(g) % nt, 0)
    last = lambda g: (cur(g) // nt, jnp.maximum(cur(g) % nt - (nt - keep), 0), 0)
    perb = lambda g: (cur(g) // nt, 0, 0)
    in_specs = [
        pl.BlockSpec((1, TL, D_MODEL), tile),
        pl.BlockSpec((1, 6, D_MODEL), perb),
        _const_spec(g1.shape), _const_spec(g2.shape), _const_spec(win.shape),
        _const_spec(bias_q.shape), _const_spec(cw.shape), _const_spec(cbias.shape),
        _const_spec(wpa.shape), _const_spec(wpb.shape), _const_spec(wo.shape),
        _const_spec(whi.shape), _const_spec(br.shape),
    ]
    out_specs = [
        pl.BlockSpec((1, TL, D_MODEL), tile),
        pl.BlockSpec((TL * ROW_TILES, LANES), lambda g: (cur(g), 0)),
        pl.BlockSpec((1, TL, D_ATT), last),
        pl.BlockSpec((1, TL, D_ATT), last),
        pl.BlockSpec((1, 8, D_CONV), perb),
        pl.BlockSpec((1, TL, RLANES), lambda g: (prv(g) // nt, prv(g) % nt, 0)),
        pl.BlockSpec((1, RLANES), lambda g: (0, 0)),
    ]
    out_shape = [
        jax.ShapeDtypeStruct((nb, seq, D_MODEL), F32),
        jax.ShapeDtypeStruct((nb * seq * ROW_TILES, LANES), I32),
        jax.ShapeDtypeStruct((nb, WINDOW, D_ATT), F32),
        jax.ShapeDtypeStruct((nb, WINDOW, D_ATT), F32),
        jax.ShapeDtypeStruct((nb, 8, D_CONV), F32),
        jax.ShapeDtypeStruct((nb, seq, RLANES), F32),
        jax.ShapeDtypeStruct((1, RLANES), F32),
    ]
    scratch = [
        pltpu.VMEM((RING, D_ATT), BF16), pltpu.VMEM((RING, D_ATT), BF16),
        pltpu.VMEM((TL, D_ATT), BF16), pltpu.VMEM((8, D_CONV), F32),
        pltpu.VMEM((1, RLANES), F32), pltpu.VMEM((TL, D_MODEL), F32),
    ]
    return pl.pallas_call(
        functools.partial(_prompt_kernel, nt=nt, ntiles=ntiles),
        grid=(ntiles + 1,),
        in_specs=in_specs, out_specs=out_specs, out_shape=out_shape,
        scratch_shapes=scratch,
        compiler_params=pltpu.CompilerParams(
            dimension_semantics=("arbitrary",), vmem_limit_bytes=VMEM_LIMIT),
        name="prompt_main",
    )(x, mod, g1, g2, win, bias_q, cw, cbias, wpa, wpb, wo, whi, br)


def _sample_kernel(x_ref, mod_ref, ck_ref, cv_ref, up1_ref, up2_ref, cnt_in_ref,
                   g1_ref, g2_ref, win_ref, bias_old_ref, bias_new_ref, cw_ref, cbias_ref,
                   wpa_ref, wpb_ref, wo_ref, whi_ref, br_ref,
                   x1_ref, h2_ref, ko_ref, vo_ref, uo_ref, route_ref, cnt_ref,
                   h_s, q_s, kn_s, vn_s, att_s, conv_s, h2_s, *, nseq, slen):
    n = pl.program_id(0)
    ntok = nseq * slen

    @pl.when(n == 0)
    def _():
        def norm_body(i, carry):
            rows = pl.ds(pl.multiple_of(i * slen, slen), slen)
            xi = x_ref[rows, :]
            m = mod_ref[i]
            hi = _rms(xi, g1_ref[...]) * (1.0 + m[1:2, :]) + m[0:1, :]
            h_s[rows, :] = hi.astype(BF16)
            return carry
        lax.fori_loop(0, nseq, norm_body, 0)
        hb = h_s[...]
        qkv = _dot(hb, win_ref[:, 0:3 * D_ATT])
        q_s[...] = (qkv[:, 0:D_ATT] * Q_SCALE).astype(BF16)
        k = qkv[:, D_ATT:2 * D_ATT]
        v = qkv[:, 2 * D_ATT:3 * D_ATT]
        ko_ref[...] = k
        vo_ref[...] = v
        kn_s[...] = k.astype(BF16)
        vn_s[...] = v.astype(BF16)

        cbcv = _dot(hb, win_ref[:, 3 * D_ATT:3 * D_ATT + 3 * D_CONV])
        cb = cbcv[:, 0:D_CONV]
        u = cbcv[:, D_CONV:2 * D_CONV] * cbcv[:, 2 * D_CONV:3 * D_CONV]
        pos = lax.broadcasted_iota(jnp.int32, (ntok, D_CONV), 0) & (slen - 1)
        u_m1 = jnp.where(pos < 1, up1_ref[...], pltpu.roll(u, 1, axis=0))
        u_m2 = jnp.where(pos < 2, up2_ref[...], pltpu.roll(u, 2, axis=0))
        yc = cw_ref[0:1, :] * u_m2 + cw_ref[1:2, :] * u_m1 + cw_ref[2:3, :] * u + cbias_ref[...]
        conv_s[...] = (cb * yc).astype(BF16)
        uo_ref[...] = u

    rows = pl.ds(pl.multiple_of(n * slen, slen), slen)

    def heads_to_rows(a):
        return jnp.concatenate(
            [a[:, h * HEAD_DIM:(h + 1) * HEAD_DIM] for h in range(N_HEADS)], axis=0)

    nt_dims = (((1,), (1,)), ((), ()))
    qs = heads_to_rows(q_s[rows, :])
    k_new = heads_to_rows(kn_s[rows, :])
    v_new = heads_to_rows(vn_s[rows, :])
    k_old = ck_ref[0].astype(BF16)
    v_old = cv_ref[0].astype(BF16)
    s_old = lax.dot_general(qs, k_old, nt_dims, preferred_element_type=F32) + bias_old_ref[...]
    s_new = lax.dot_general(qs, k_new, nt_dims, preferred_element_type=F32) + bias_new_ref[...]
    m = jnp.maximum(jnp.max(s_old, axis=1, keepdims=True), jnp.max(s_new, axis=1, keepdims=True))
    p_old = jnp.exp2(s_old - m)
    p_new = jnp.exp2(s_new - m)
    l = jnp.sum(p_old, axis=1, keepdims=True) + jnp.sum(p_new, axis=1, keepdims=True)
    o = (_dot(p_old.astype(BF16), v_old) + _dot(p_new.astype(BF16), v_new)) * (1.0 / l)
    att = jnp.concatenate([o[h * slen:(h + 1) * slen] for h in range(N_HEADS)], axis=1)
    att_s[rows, :] = att.astype(BF16)

    @pl.when(n == nseq - 1)
    def _():
        gates = _dot(h_s[...], win_ref[:, 3 * D_ATT + 3 * D_CONV:])
        pa = _dot(att_s[...], wpa_ref[...])
        pb = _dot(conv_s[...], wpb_ref[...])
        mixin = _sigmoid(gates[:, 0:D_MODEL]) * pa + _sigmoid(gates[:, D_MODEL:]) * pb
        x1_ref[...] = _dot(mixin.astype(BF16), wo_ref[...])

        def res_body(i, carry):
            r = pl.ds(pl.multiple_of(i * slen, slen), slen)
            m = mod_ref[i]
            x1 = x_ref[r, :] + m[2:3, :] * x1_ref[r, :]
            x1_ref[r, :] = x1
            h2_s[r, :] = _rms(x1, g2_ref[...]) * (1.0 + m[4:5, :]) + m[3:4, :]
            return carry
        lax.fori_loop(0, nseq, res_body, 0)

        h2 = h2_s[...]
        _store_rows_as_tiles(h2_ref, _pack_rows(h2))
        route, new_cnt = _route(h2, whi_ref, br_ref, cnt_in_ref[...])
        route_ref[...] = route
        cnt_ref[...] = new_cnt


def _sample_main(x2d, mod, ck, cv, up1, up2, cnt_in, g1, g2, win, bias_old, bias_new, cw, cbias,
                 wpa, wpb, wo, whi, br, nseq, slen):
    ntok = nseq * slen
    args = (x2d, mod, ck, cv, up1, up2, cnt_in, g1, g2, win, bias_old, bias_new, cw, cbias,
            wpa, wpb, wo, whi, br)
    in_specs = []
    for idx, a in enumerate(args):
        if idx in (2, 3):
            in_specs.append(pl.BlockSpec((1,) + a.shape[1:], lambda n: (n, 0, 0)))
        else:
            in_specs.append(_const_spec(a.shape))
    whole = lambda shape: pl.BlockSpec(shape, lambda n: (0,) * len(shape))
    outs = [((ntok, D_MODEL), F32), ((ntok * ROW_TILES, LANES), I32), ((ntok, D_ATT), F32),
            ((ntok, D_ATT), F32), ((ntok, D_CONV), F32), ((ntok, RLANES), F32), ((1, RLANES), F32)]
    scratch = [
        pltpu.VMEM((ntok, D_MODEL), BF16), pltpu.VMEM((ntok, D_ATT), BF16),
        pltpu.VMEM((ntok, D_ATT), BF16), pltpu.VMEM((ntok, D_ATT), BF16),
        pltpu.VMEM((ntok, D_ATT), BF16), pltpu.VMEM((ntok, D_CONV), BF16),
        pltpu.VMEM((ntok, D_MODEL), F32),
    ]
    return pl.pallas_call(
        functools.partial(_sample_kernel, nseq=nseq, slen=slen),
        grid=(nseq,),
        in_specs=in_specs,
        out_specs=[whole(s) for s, _ in outs],
        out_shape=[jax.ShapeDtypeStruct(s, d) for s, d in outs],
        scratch_shapes=scratch,
        compiler_params=pltpu.CompilerParams(
            dimension_semantics=("arbitrary",), vmem_limit_bytes=VMEM_LIMIT),
        name="sample_main",
    )(*args)


def _issue_rows(n, body):
    def group(g, carry):
        for u in range(ISSUE_UNROLL):
            body(g * ISSUE_UNROLL + u, u)
        return carry
    lax.fori_loop(0, n // ISSUE_UNROLL, group, 0)


def _dispatch_kernel(pend_ref, d1_ref, d2_ref, hp_ref, hs_ref, xs_out, zbuf, sem, zsem, *,
                     np_tiles, nblocks):
    i = pl.program_id(0)
    blk_rows = MOE_BLK * ROW_TILES

    @pl.when(i == 0)
    def _():
        zbuf[...] = jnp.zeros_like(zbuf)

        def zcopy(e):
            start = pl.multiple_of((pend_ref[e + 1] - MOE_BLK) * ROW_TILES, blk_rows)
            return pltpu.make_async_copy(zbuf, xs_out.at[pl.ds(start, blk_rows)], zsem)

        def zstart(e, carry):
            @pl.when(pend_ref[e + 1] > pend_ref[e])
            def _():
                zcopy(e).start()
            return carry

        def zwait(e, carry):
            @pl.when(pend_ref[e + 1] > pend_ref[e])
            def _():
                zcopy(e).wait()
            return carry
        lax.fori_loop(0, N_EXPERTS, zstart, 0)
        lax.fori_loop(0, N_EXPERTS, zwait, 0)

        def tcopy(bk):
            start = pl.multiple_of(bk * blk_rows, blk_rows)
            return pltpu.make_async_copy(zbuf, xs_out.at[pl.ds(start, blk_rows)], zsem)

        def tstart(bk, carry):
            tcopy(bk).start()
            return carry

        def twait(bk, carry):
            tcopy(bk).wait()
            return carry
        used = pend_ref[N_EXPERTS] // MOE_BLK
        lax.fori_loop(used, nblocks, tstart, 0)
        lax.fori_loop(used, nblocks, twait, 0)

    def scatter_tile(src):
        def row(r, u):
            pltpu.make_async_copy(_tile(src, r), _tile(xs_out, d1_ref[r]), sem).start(priority=u % 2)
            pltpu.make_async_copy(_tile(src, r), _tile(xs_out, d2_ref[r]), sem).start(
                priority=(u + 1) % 2)
        _issue_rows(TM, row)
        for _ in range(2):
            pltpu.make_async_copy(src, xs_out.at[pl.ds(0, TM * ROW_TILES)], sem).wait()

    @pl.when(i < np_tiles)
    def _():
        scatter_tile(hp_ref)

    @pl.when(i >= np_tiles)
    def _():
        scatter_tile(hs_ref)


def _dispatch(pend, d1, d2, h2p, h2s, nslots):
    np_tiles = h2p.shape[0] // (TM * ROW_TILES)
    ns_tiles = h2s.shape[0] // (TM * ROW_TILES)
    smem_tile = pl.BlockSpec((TM,), lambda i, *_: (i,), memory_space=pltpu.SMEM)
    rows = TM * ROW_TILES
    return pl.pallas_call(
        functools.partial(_dispatch_kernel, np_tiles=np_tiles, nblocks=nslots // MOE_BLK),
        grid_spec=pltpu.PrefetchScalarGridSpec(
            num_scalar_prefetch=1,
            grid=(np_tiles + ns_tiles,),
            in_specs=[smem_tile, smem_tile,
                      pl.BlockSpec((rows, LANES), lambda i, *_: (jnp.minimum(i, np_tiles - 1), 0)),
                      pl.BlockSpec((rows, LANES), lambda i, *_: (jnp.maximum(i - np_tiles, 0), 0))],
            out_specs=pl.BlockSpec(memory_space=pl.ANY),
            scratch_shapes=[pltpu.VMEM((MOE_BLK * ROW_TILES, LANES), I32),
                            pltpu.SemaphoreType.DMA(()), pltpu.SemaphoreType.DMA(())],
        ),
        out_shape=jax.ShapeDtypeStruct((nslots * ROW_TILES, LANES), I32),
        compiler_params=pltpu.CompilerParams(dimension_semantics=("arbitrary",)),
        name="dispatch",
    )(pend, d1, d2, h2p, h2s)


def _expert_kernel(blk_e_ref, nblk_ref, first_ref, wslot_ref, next_e_ref,
                   xs_ref, wg_hbm, wu_hbm, wd_hbm, y_ref,
                   wg_f, wu_f, wd_f, wg_b, wu_b, wd_b, wsem):
    i = pl.program_id(0)
    live = i < nblk_ref[0]

    def weight_copies(e, slot):
        return (pltpu.make_async_copy(wg_hbm.at[e], wg_f.at[slot], wsem.at[slot, 0]),
                pltpu.make_async_copy(wu_hbm.at[e], wu_f.at[slot], wsem.at[slot, 1]),
                pltpu.make_async_copy(wd_hbm.at[e], wd_f.at[slot], wsem.at[slot, 2]))

    @pl.when(i == 0)
    def _():
        for cp in weight_copies(blk_e_ref[0], 0):
            cp.start()

    @pl.when(live & (first_ref[i] == 1))
    def _():
        slot = wslot_ref[i]
        for cp in weight_copies(blk_e_ref[i], slot):
            cp.wait()
        wg_b[...] = wg_f[slot].astype(BF16)
        wu_b[...] = wu_f[slot].astype(BF16)
        wd_b[...] = wd_f[slot].astype(BF16)

        @pl.when(next_e_ref[i] >= 0)
        def _():
            for cp in weight_copies(next_e_ref[i], 1 - slot):
                cp.start()

    @pl.when(live)
    def _():
        x_lo, x_hi = _unpack_rows(_load_tiles_as_rows(xs_ref, MOE_BLK))
        x_lo = x_lo.astype(BF16)
        x_hi = x_hi.astype(BF16)
        g = _dot(x_lo, wg_b[0:HALF, :]) + _dot(x_hi, wg_b[HALF:, :])
        u = _dot(x_lo, wu_b[0:HALF, :]) + _dot(x_hi, wu_b[HALF:, :])
        a = (g * _sigmoid(g)) * u
        _store_rows_as_tiles(y_ref, _pack_rows(_dot(a.astype(BF16), wd_b[...])))

    @pl.when(jnp.logical_not(live))
    def _():
        y_ref[...] = jnp.zeros_like(y_ref)


def _experts(blk_e, nblk, first, wslot, next_e, xs, wg, wu, wd):
    blk_rows = MOE_BLK * ROW_TILES
    nblocks = xs.shape[0] // blk_rows
    row_map = lambda i, be, nb, *_: (jnp.minimum(i, nb[0] - 1), 0)
    any_spec = pl.BlockSpec(memory_space=pl.ANY)
    return pl.pallas_call(
        _expert_kernel,
        grid_spec=pltpu.PrefetchScalarGridSpec(
            num_scalar_prefetch=5,
            grid=(nblocks,),
            in_specs=[pl.BlockSpec((blk_rows, LANES), row_map), any_spec, any_spec, any_spec],
            out_specs=pl.BlockSpec((blk_rows, LANES), lambda i, *_: (i, 0)),
            scratch_shapes=[pltpu.VMEM((2, D_MODEL, D_EXPERT), F32),
                            pltpu.VMEM((2, D_MODEL, D_EXPERT), F32),
                            pltpu.VMEM((2, D_EXPERT, D_MODEL), F32),
                            pltpu.VMEM((D_MODEL, D_EXPERT), BF16),
                            pltpu.VMEM((D_MODEL, D_EXPERT), BF16),
                            pltpu.VMEM((D_EXPERT, D_MODEL), BF16),
                            pltpu.SemaphoreType.DMA((2, 3))],
        ),
        out_shape=jax.ShapeDtypeStruct(xs.shape, I32),
        compiler_params=pltpu.CompilerParams(
            dimension_semantics=("arbitrary",), vmem_limit_bytes=VMEM_LIMIT),
        name="experts",
    )(blk_e, nblk, first, wslot, next_e, xs, wg, wu, wd)


def _combine_kernel(d1_ref, d2_ref, d1n_ref, d2n_ref, y_hbm, x1_ref, route_ref, gate_ref, gf_ref,
                    o_ref, a0, b0, a1, b1, sem, *, ntiles):
    t = pl.program_id(0)
    bufs = ((a0, b0), (a1, b1))

    def gather(i1_ref, i2_ref, par):
        buf_a, buf_b = bufs[par]

        def row(r, u):
            pltpu.make_async_copy(_tile(y_hbm, i1_ref[r]), _tile(buf_a, r),
                                  sem.at[par]).start(priority=u % 2)
            pltpu.make_async_copy(_tile(y_hbm, i2_ref[r]), _tile(buf_b, r),
                                  sem.at[par]).start(priority=(u + 1) % 2)
        _issue_rows(TM, row)

    @pl.when(t == 0)
    def _():
        gather(d1_ref, d2_ref, 0)

    def step(par):
        buf_a, buf_b = bufs[par]
        for buf in (buf_a, buf_b):
            pltpu.make_async_copy(y_hbm.at[pl.ds(0, TM * ROW_TILES)], buf, sem.at[par]).wait()

        @pl.when(t + 1 < ntiles)
        def _():
            gather(d1n_ref, d2n_ref, 1 - par)

        route = route_ref[0]
        w1 = route[:, 4:5]
        w2 = route[:, 5:6]
        a_lo, a_hi = _unpack_rows(_load_tiles_as_rows(buf_a, TM))
        b_lo, b_hi = _unpack_rows(_load_tiles_as_rows(buf_b, TM))
        ffn = jnp.concatenate([w1 * a_lo + w2 * b_lo, w1 * a_hi + w2 * b_hi], axis=1)
        x2 = x1_ref[0] + gate_ref[0] * ffn
        o_ref[0] = _rms(x2, gf_ref[...])

    for par in range(2):
        pl.when(t % 2 == par)(functools.partial(step, par))


def _combine(d1, d2, y, x1, route, gate, gf, tok_base):
    nb, seq, _ = x1.shape
    nt = seq // TM
    ntiles = nb * nt
    blk0 = tok_base // TM
    smem = lambda fn: pl.BlockSpec((TM,), fn, memory_space=pltpu.SMEM)
    cur = lambda t: (blk0 + t,)
    nxt = lambda t: (blk0 + jnp.minimum(t + 1, ntiles - 1),)
    tile = lambda t: (t // nt, t % nt, 0)
    grows = gate.shape[1]
    gate_spec = (pl.BlockSpec((1, 1, D_MODEL), lambda t: (t // nt, 0, 0)) if grows == 1
                 else pl.BlockSpec((1, TM, D_MODEL), tile))
    return pl.pallas_call(
        functools.partial(_combine_kernel, ntiles=ntiles),
        grid=(ntiles,),
        in_specs=[smem(cur), smem(cur), smem(nxt), smem(nxt), pl.BlockSpec(memory_space=pl.ANY),
                  pl.BlockSpec((1, TM, D_MODEL), tile),
                  pl.BlockSpec((1, TM, RLANES), tile),
                  gate_spec,
                  pl.BlockSpec((1, D_MODEL), lambda t: (0, 0))],
        out_specs=pl.BlockSpec((1, TM, D_MODEL), tile),
        out_shape=jax.ShapeDtypeStruct(x1.shape, F32),
        scratch_shapes=[pltpu.VMEM((TM * ROW_TILES, LANES), I32)] * 4 + [
            pltpu.SemaphoreType.DMA((2,))],
        compiler_params=pltpu.CompilerParams(
            dimension_semantics=("arbitrary",), vmem_limit_bytes=VMEM_LIMIT),
        name="combine",
    )(d1, d2, d1, d2, y, x1, route, gate, gf)


def _band_bias(rel_bias, rows, keys):
    n = rows - 1 + keys
    dist = WINDOW + rows - 1 - np.arange(n + 1)
    flipped = rel_bias[:, np.clip(dist, -MAX_REL, MAX_REL) + MAX_REL]
    skew = jnp.tile(flipped, (1, rows))[:, :rows * n].reshape(N_HEADS, rows, n)
    b = skew[:, :, rows - 1:rows - 1 + keys]
    return b.reshape(2, 4 * rows, keys)


def kernel(x_prompt, x_sample, cache_attn_k, cache_attn_v, state_conv, c_prompt, c_sample,
           w_ada, b_ada, norm1_g, norm2_g, w_in, rel_bias, conv_w, conv_b, w_pa, w_pb, w_o,
           w_group, b_group, w_expert, b_expert, w_e_gate, w_e_up, w_e_down, final_g):
    assert w_ada.shape[0] == 1, "single trunk layer"
    nb, seq, _ = x_prompt.shape
    nseq, slen, _ = x_sample.shape
    ntok_p = nb * seq
    ntok_s = nseq * slen
    ntok = ntok_p + ntok_s
    assert seq % TL == 0 and WINDOW % TL == 0 and seq % TM == 0 and ntok_s % TM == 0
    assert slen >= 2 and slen & (slen - 1) == 0 and slen % 16 == 0

    n_c = nb + nseq
    n_pad = -(-n_c // 8) * 8
    c_all = jnp.concatenate([c_prompt, c_sample, jnp.zeros((n_pad - n_c, D_MODEL), F32)], axis=0)
    mod = _ada(c_all, w_ada[0], b_ada[0]).reshape(n_pad, 6, D_MODEL)
    mod_p = mod[:nb]
    mod_s = mod[nb:n_c]

    win = w_in[0].astype(BF16)
    wpa = w_pa[0].astype(BF16)
    wpb = w_pb[0].astype(BF16)
    wo = w_o[0].astype(BF16)
    g1 = norm1_g[0].reshape(1, D_MODEL)
    g2 = norm2_g[0].reshape(1, D_MODEL)
    gf = final_g.reshape(1, D_MODEL)
    cw = jnp.concatenate([conv_w[0], jnp.zeros((8 - conv_w.shape[1], D_CONV), F32)], axis=0)
    cbias = conv_b[0].reshape(1, D_CONV)
    wr = jnp.concatenate([w_expert[0], w_group[0],
                          jnp.zeros((D_MODEL, RLANES - N_EXPERTS - N_GROUPS), F32)], axis=1)
    whi = wr.astype(BF16)
    br = jnp.concatenate([b_expert[0], b_group[0],
                          jnp.zeros((RLANES - N_EXPERTS - N_GROUPS,), F32)]).reshape(1, RLANES)
    bias_p = _band_bias(rel_bias[0] * LOG2E, CHUNK, BAND)
    bias_s = _band_bias(rel_bias[0] * LOG2E, slen, WINDOW + slen).reshape(
        N_HEADS, slen, WINDOW + slen)
    same_head = jnp.eye(N_HEADS, dtype=bool)
    bias_old = jnp.where(same_head[:, None, None, :], bias_s[:, :, :WINDOW, None], NEG).reshape(
        N_HEADS * slen, WINDOW * N_HEADS)
    bias_new = jnp.where(same_head[:, None, :, None], bias_s[:, :, None, WINDOW:], NEG).reshape(
        N_HEADS * slen, N_HEADS * slen)

    x1p, h2p, kp, vp, up8, route_p, cnt_p = _prompt_main(
        x_prompt, mod_p, g1, g2, win, bias_p, cw, cbias, wpa, wpb, wo, whi, br)

    st = state_conv[0]
    up1 = jnp.zeros((nseq, slen, D_CONV), F32).at[:, 0].set(st[:, 1]).reshape(ntok_s, D_CONV)
    up2 = (jnp.zeros((nseq, slen, D_CONV), F32).at[:, 0].set(st[:, 0]).at[:, 1].set(st[:, 1])
           .reshape(ntok_s, D_CONV))
    ck = cache_attn_k[0].reshape(nseq, WINDOW * N_HEADS, HEAD_DIM)
    cv = cache_attn_v[0].reshape(nseq, WINDOW * N_HEADS, HEAD_DIM)
    x1s, h2s, ks, vs, us, route_s, cnt = _sample_main(
        x_sample.reshape(ntok_s, D_MODEL), mod_s, ck, cv, up1, up2, cnt_p,
        g1, g2, win, bias_old, bias_new, cw, cbias, wpa, wpb, wo, whi, br, nseq, slen)

    route_all = jnp.concatenate([route_p.reshape(ntok_p, RLANES)[:, :4], route_s[:, :4]], axis=0)
    experts = route_all[:, 0:2].astype(jnp.int32)
    ranks = route_all[:, 2:4].astype(jnp.int32)
    counts = cnt[0, :N_EXPERTS].astype(jnp.int32)
    pcounts = (counts + MOE_BLK - 1) // MOE_BLK * MOE_BLK
    pend = jnp.cumsum(pcounts)
    pstart = pend - pcounts
    eids = jnp.arange(N_EXPERTS, dtype=jnp.int32)
    dest = jnp.sum(jnp.where(experts[..., None] == eids, pstart, 0), axis=-1) + ranks
    d1 = dest[:, 0]
    d2 = dest[:, 1]
    nblocks = (2 * ntok) // MOE_BLK + N_EXPERTS
    blk_start = jnp.arange(nblocks, dtype=jnp.int32) * MOE_BLK
    blk_e = jnp.minimum(jnp.sum((pend[None, :] <= blk_start[:, None]).astype(jnp.int32), axis=1),
                        N_EXPERTS - 1)
    nblk = (pend[-1:] // MOE_BLK).astype(jnp.int32)
    pend0 = jnp.concatenate([jnp.zeros((1,), jnp.int32), pend.astype(jnp.int32)])

    xs = _dispatch(pend0, d1, d2, h2p, h2s, nblocks * MOE_BLK)
    blk_id = jnp.arange(nblocks, dtype=jnp.int32)
    first = (blk_id < nblk[0]) & ((blk_id == 0) | (blk_e != jnp.roll(blk_e, 1)))
    wslot = (jnp.cumsum(first.astype(jnp.int32)) - 1) % 2
    later_first = lax.cummin(jnp.where(first, blk_id, nblocks)[::-1])[::-1]
    next_first = jnp.concatenate([later_first[1:], jnp.full((1,), nblocks, jnp.int32)])
    next_e = jnp.where(next_first < nblocks, blk_e[jnp.minimum(next_first, nblocks - 1)], -1)
    y = _experts(blk_e, nblk, first.astype(jnp.int32), wslot.astype(jnp.int32),
                 next_e.astype(jnp.int32), xs, w_e_gate[0], w_e_up[0], w_e_down[0])

    y_prompt = _combine(d1, d2, y, x1p, route_p, mod_p[:, 5:6, :], gf, 0)
    gate_s = jnp.repeat(mod_s[:, 5, :], slen, axis=0).reshape(1, ntok_s, D_MODEL)
    y_sample = _combine(d1, d2, y, x1s.reshape(1, ntok_s, D_MODEL),
                        route_s.reshape(1, ntok_s, RLANES), gate_s, gf, ntok_p)

    new_k_p = kp.reshape(1, nb, WINDOW, N_HEADS, HEAD_DIM)
    new_v_p = vp.reshape(1, nb, WINDOW, N_HEADS, HEAD_DIM)
    new_conv_p = up8[:, 6:8, :].reshape(1, nb, 2, D_CONV)
    new_k_s = ks.reshape(1, nseq, slen, N_HEADS, HEAD_DIM)
    new_v_s = vs.reshape(1, nseq, slen, N_HEADS, HEAD_DIM)
    new_conv_s = us.reshape(nseq, slen, D_CONV)[:, slen - 2:, :].reshape(1, nseq, 2, D_CONV)
    return (y_prompt, y_sample.reshape(nseq, slen, D_MODEL), new_k_p, new_v_p, new_conv_p,
            new_k_s, new_v_s, new_conv_s)
```

```python
import functools

import numpy as np
import jax
import jax.numpy as jnp
from jax import lax
from jax.experimental import pallas as pl
from jax.experimental.pallas import tpu as pltpu

F32 = jnp.float32
BF16 = jnp.bfloat16
I32 = jnp.int32

D_MODEL = 1024
CHUNK = 64
LEFT = 8
WINDOW = LEFT * CHUNK
BAND = WINDOW + CHUNK
N_HEADS = 8
HEAD_DIM = 64
D_ATT = N_HEADS * HEAD_DIM
QUAD = 256
MAX_REL = 128
D_CONV = 512
N_GROUPS = 4
EPG = 8
N_EXPERTS = 32
D_EXPERT = 512
EPS = 1e-6
NEG = -1e30
LOG2E = float(np.log2(np.e))
Q_SCALE = HEAD_DIM ** -0.5 * LOG2E

TL = 512
RING = WINDOW + TL
MOE_BLK = 512
TM = 512
RLANES = 128
LANES = 128
HALF = D_MODEL // 2
ROW_TILES = HALF // LANES
ISSUE_UNROLL = 8
VMEM_LIMIT = 56 * 1024 * 1024


def _const_spec(shape):
    nd = len(shape)
    return pl.BlockSpec(shape, lambda *_: (0,) * nd, pipeline_mode=pl.Buffered(1))


def _dot(a, b):
    return jnp.dot(a, b, preferred_element_type=F32)


def _sigmoid(x):
    return 1.0 / (1.0 + jnp.exp(-x))


def _rms(x, g):
    ms = jnp.mean(x * x, axis=-1, keepdims=True)
    return x * lax.rsqrt(ms + EPS) * g


def _pack_rows(val):
    lo = lax.bitcast_convert_type(val[:, :HALF], I32) + 0x8000
    hi = lax.bitcast_convert_type(val[:, HALF:], I32) + 0x8000
    return (hi & -65536) | lax.shift_right_logical(lo, 16)


def _unpack_rows(packed):
    lo = lax.bitcast_convert_type(lax.shift_left(packed, 16), F32)
    hi = lax.bitcast_convert_type(packed & -65536, F32)
    return lo, hi


def _store_rows_as_tiles(ref, packed):
    r = packed.shape[0]
    for c in range(ROW_TILES):
        ref[pl.ds(c, r, stride=ROW_TILES), :] = packed[:, c * LANES:(c + 1) * LANES]


def _load_tiles_as_rows(ref, r):
    return jnp.concatenate(
        [ref[pl.ds(c, r, stride=ROW_TILES), :] for c in range(ROW_TILES)], axis=1)


def _tile(ref, row):
    return ref.at[pl.ds(pl.multiple_of(row * ROW_TILES, ROW_TILES), ROW_TILES)]


def _ada_kernel(c_ref, w_ref, b_ref, o_ref):
    c = c_ref[...]
    s = c * _sigmoid(c)
    o_ref[...] = _dot(s.astype(BF16), w_ref[...].astype(BF16)) + b_ref[...]


def _ada(c_all, w_ada, b_ada):
    n = c_all.shape[0]
    nb = 1024
    return pl.pallas_call(
        _ada_kernel,
        grid=(6 * D_MODEL // nb,),
        in_specs=[pl.BlockSpec((n, D_MODEL), lambda i: (0, 0)),
                  pl.BlockSpec((D_MODEL, nb), lambda i: (0, i)),
                  pl.BlockSpec((1, nb), lambda i: (0, i))],
        out_specs=pl.BlockSpec((n, nb), lambda i: (0, i)),
        out_shape=jax.ShapeDtypeStruct((n, 6 * D_MODEL), F32),
        name="ada",
    )(c_all, w_ada, b_ada.reshape(1, -1))


def _attend(q, kb, vb, bias, lim):
    r = q.shape[0]
    nk = kb.shape[0]
    assert r & (r - 1) == 0
    qt = jnp.concatenate([q] * 4, axis=0)
    rowh = lax.broadcasted_iota(jnp.int32, (4 * r, QUAD), 0) >> (r.bit_length() - 1)
    laneh = lax.broadcasted_iota(jnp.int32, (4 * r, QUAD), 1) >> 6
    qm = jnp.where(rowh == laneh, qt, jnp.zeros_like(qt))
    s = lax.dot_general(qm, kb, (((1,), (1,)), ((), ())), preferred_element_type=F32)
    valid = None
    if lim is not None:
        valid = lax.broadcasted_iota(jnp.int32, (r, nk), 1) >= lim
    ps, inv_l = [], []
    for h in range(4):
        sh = s[h * r:(h + 1) * r] + bias[h * r:(h + 1) * r]
        if valid is not None:
            sh = jnp.where(valid, sh, NEG)
        m = jnp.max(sh, axis=1, keepdims=True)
        ph = jnp.exp2(sh - m)
        inv_l.append(1.0 / jnp.sum(ph, axis=1, keepdims=True))
        ps.append(ph.astype(BF16))
    o = _dot(jnp.concatenate(ps, axis=0), vb)
    lane_o = lax.broadcasted_iota(jnp.int32, (r, QUAD), 1) >> 6
    out = o[0:r] * inv_l[0]
    for h in range(1, 4):
        out = jnp.where(lane_o == h, o[h * r:(h + 1) * r] * inv_l[h], out)
    return out


def _route_stages(h2, whi_ref, br_ref, cnt, valid=None):
    r = h2.shape[0]
    lane = lax.broadcasted_iota(jnp.int32, (r, RLANES), 1)
    lane_f = lane.astype(F32)
    big = jnp.float32(1000.0)
    v = {}

    def logits():
        v["logits"] = _dot(h2.astype(BF16), whi_ref[...]) + br_ref[...]

    def group():
        lg = jnp.where((lane >= N_EXPERTS) & (lane < N_EXPERTS + N_GROUPS), v["logits"], NEG)
        mg = jnp.max(lg, axis=1, keepdims=True)
        v["gi"] = jnp.min(jnp.where(lg == mg, lane_f, big), axis=1, keepdims=True) - N_EXPERTS
        v["pg"] = 1.0 / jnp.sum(jnp.exp(lg - mg), axis=1, keepdims=True)

    def top1():
        grp_of_lane = (lane >> 3).astype(F32)
        le = jnp.where((lane < N_EXPERTS) & (grp_of_lane == v["gi"]), v["logits"], NEG)
        v["m1"] = jnp.max(le, axis=1, keepdims=True)
        v["i1"] = jnp.min(jnp.where(le == v["m1"], lane_f, big), axis=1, keepdims=True)
        v["le"] = le

    def top2():
        sel1 = lane_f == v["i1"]
        le2 = jnp.where(sel1, NEG, v["le"])
        m2 = jnp.max(le2, axis=1, keepdims=True)
        v["i2"] = jnp.min(jnp.where(le2 == m2, lane_f, big), axis=1, keepdims=True)
        rr = jnp.exp(m2 - v["m1"])
        inv = v["pg"] / (1.0 + rr)
        v["w1"] = inv
        v["w2"] = inv * rr
        v["sel1"] = sel1
        v["sel2"] = lane_f == v["i2"]

    def ranks():
        oh = jnp.where(v["sel1"] | v["sel2"], 1.0 if valid is None else valid, 0.0).astype(F32)
        ri = lax.broadcasted_iota(jnp.int32, (r, r), 0)
        ci = lax.broadcasted_iota(jnp.int32, (r, r), 1)
        tri = jnp.where(ri > ci, 1.0, 0.0).astype(BF16)
        v["before"] = _dot(tri, oh.astype(BF16)) + cnt
        v["new_cnt"] = cnt + jnp.sum(oh, axis=0, keepdims=True)

    def assemble():
        r1 = jnp.sum(jnp.where(v["sel1"], v["before"], 0.0), axis=1, keepdims=True)
        r2 = jnp.sum(jnp.where(v["sel2"], v["before"], 0.0), axis=1, keepdims=True)
        route = jnp.where(lane == 0, v["i1"], 0.0)
        route = jnp.where(lane == 1, v["i2"], route)
        route = jnp.where(lane == 2, r1, route)
        route = jnp.where(lane == 3, r2, route)
        route = jnp.where(lane == 4, v["w1"], route)
        route = jnp.where(lane == 5, v["w2"], route)
        return route, v["new_cnt"]

    return [logits, group, top1, top2, ranks, assemble]


def _route(h2, whi_ref, br_ref, cnt):
    stages = _route_stages(h2, whi_ref, br_ref, cnt)
    for stage in stages[:-1]:
        stage()
    return stages[-1]()


def _prompt_kernel(x_ref, mod_ref, g1_ref, g2_ref, win_ref, bias_ref, cw_ref, cbias_ref,
                   wpa_ref, wpb_ref, wo_ref, whi_ref, br_ref,
                   x1_ref, h2_ref, ko_ref, vo_ref, uo_ref, route_ref, cnt_ref,
                   kring, vring, att_s, ucarry, cnt_s, h2_prev, *, nt, ntiles):
    g = pl.program_id(0)

    @pl.when(g == 0)
    def _():
        cnt_s[...] = jnp.zeros_like(cnt_s)
        h2_prev[...] = jnp.zeros_like(h2_prev)

    @pl.when(g < ntiles)
    def _():
        _prompt_tile(g, nt, x_ref, mod_ref, g1_ref, g2_ref, win_ref, bias_ref, cw_ref, cbias_ref,
                     wpa_ref, wpb_ref, wo_ref, whi_ref, br_ref,
                     x1_ref, h2_ref, ko_ref, vo_ref, uo_ref, route_ref, cnt_ref,
                     kring, vring, att_s, ucarry, cnt_s, h2_prev)

    @pl.when(g == ntiles)
    def _():
        route, new_cnt = _route(h2_prev[...], whi_ref, br_ref, cnt_s[...])
        route_ref[0] = route
        cnt_ref[...] = new_cnt


def _prompt_tile(g, nt, x_ref, mod_ref, g1_ref, g2_ref, win_ref, bias_ref, cw_ref, cbias_ref,
                 wpa_ref, wpb_ref, wo_ref, whi_ref, br_ref,
                 x1_ref, h2_ref, ko_ref, vo_ref, uo_ref, route_ref, cnt_ref,
                 kring, vring, att_s, ucarry, cnt_s, h2_prev):
    j = g % nt

    @pl.when(j == 0)
    def _():
        kring[0:WINDOW, :] = jnp.zeros((WINDOW, D_ATT), BF16)
        vring[0:WINDOW, :] = jnp.zeros((WINDOW, D_ATT), BF16)
        ucarry[...] = jnp.zeros_like(ucarry)

    route_stages = _route_stages(h2_prev[...], whi_ref, br_ref, cnt_s[...],
                                 valid=jnp.where(g > 0, 1.0, 0.0).astype(F32))
    route_stages[0]()

    sh1 = mod_ref[0, 0:1, :]
    sc1 = mod_ref[0, 1:2, :]
    gt1 = mod_ref[0, 2:3, :]
    sh2 = mod_ref[0, 3:4, :]
    sc2 = mod_ref[0, 4:5, :]

    x = x_ref[0]
    h = _rms(x, g1_ref[...]) * (1.0 + sc1) + sh1
    hb = h.astype(BF16)

    qkv = _dot(hb, win_ref[:, 0:3 * D_ATT])
    q = (qkv[:, 0:D_ATT] * Q_SCALE).astype(BF16)
    k = qkv[:, D_ATT:2 * D_ATT]
    v = qkv[:, 2 * D_ATT:3 * D_ATT]
    ko_ref[0] = k
    vo_ref[0] = v
    kring[WINDOW:RING, :] = k.astype(BF16)
    vring[WINDOW:RING, :] = v.astype(BF16)

    base = j * TL
    for c in range(TL // CHUNK):
        lim = WINDOW - (base + c * CHUNK)
        for qd in range(2):
            ls = slice(qd * QUAD, (qd + 1) * QUAD)
            o = _attend(q[c * CHUNK:(c + 1) * CHUNK, ls],
                        kring[c * CHUNK:c * CHUNK + BAND, ls],
                        vring[c * CHUNK:c * CHUNK + BAND, ls],
                        bias_ref[qd], lim)
            att_s[c * CHUNK:(c + 1) * CHUNK, ls] = o.astype(BF16)
        if c + 1 < len(route_stages) - 1:
            route_stages[c + 1]()
    route, new_cnt = route_stages[-1]()
    route_ref[0] = route
    cnt_s[...] = new_cnt
    cnt_ref[...] = new_cnt

    kring[0:WINDOW, :] = kring[TL:RING, :]
    vring[0:WINDOW, :] = vring[TL:RING, :]

    cbcv = _dot(hb, win_ref[:, 3 * D_ATT:3 * D_ATT + 3 * D_CONV])
    gates = _dot(hb, win_ref[:, 3 * D_ATT + 3 * D_CONV:])
    cb = cbcv[:, 0:D_CONV]
    u = cbcv[:, D_CONV:2 * D_CONV] * cbcv[:, 2 * D_CONV:3 * D_CONV]
    row = lax.broadcasted_iota(jnp.int32, (8, D_CONV), 0)
    prev = ucarry[...]
    r1 = pltpu.roll(u, 1, axis=0)
    r2 = pltpu.roll(u, 2, axis=0)
    u_m1 = jnp.concatenate(
        [jnp.where(row < 1, pltpu.roll(prev, 1, axis=0), r1[0:8]), r1[8:]], axis=0)
    u_m2 = jnp.concatenate(
        [jnp.where(row < 2, pltpu.roll(prev, 2, axis=0), r2[0:8]), r2[8:]], axis=0)
    yc = cw_ref[0:1, :] * u_m2 + cw_ref[1:2, :] * u_m1 + cw_ref[2:3, :] * u + cbias_ref[...]
    conv_out = (cb * yc).astype(BF16)
    ucarry[...] = u[TL - 8:TL, :]
    uo_ref[0] = u[TL - 8:TL, :]

    pa = _dot(att_s[...], wpa_ref[...])
    pb = _dot(conv_out, wpb_ref[...])
    mixin = _sigmoid(gates[:, 0:D_MODEL]) * pa + _sigmoid(gates[:, D_MODEL:]) * pb
    mix = _dot(mixin.astype(BF16), wo_ref[...])
    x1 = x + gt1 * mix
    x1_ref[0] = x1
    h2 = _rms(x1, g2_ref[...]) * (1.0 + sc2) + sh2
    _store_rows_as_tiles(h2_ref, _pack_rows(h2))

    h2_prev[...] = h2


def _prompt_main(x, mod, g1, g2, win, bias_q, cw, cbias, wpa, wpb, wo, whi, br):
    nb, seq, _ = x.shape
    nt = seq // TL
    ntiles = nb * nt
    keep = WINDOW // TL
    cur = lambda g: jnp.minimum(g, ntiles - 1)
    prv = lambda g: jnp.maximum(g - 1, 0)
    tile = lambda g: (cur(g) // nt, cur(g) % nt, 0)
    last = lambda g: (cur(g) // nt, jnp.maximum(cur(g) % nt - (nt - keep), 0), 0)
    perb = lambda g: (cur(g) // nt, 0, 0)
    in_specs = [
        pl.BlockSpec((1, TL, D_MODEL), tile),
        pl.BlockSpec((1, 6, D_MODEL), perb),
        _const_spec(g1.shape), _const_spec(g2.shape), _const_spec(win.shape),
        _const_spec(bias_q.shape), _const_spec(cw.shape), _const_spec(cbias.shape),
        _const_spec(wpa.shape), _const_spec(wpb.shape), _const_spec(wo.shape),
        _const_spec(whi.shape), _const_spec(br.shape),
    ]
    out_specs = [
        pl.BlockSpec((1, TL, D_MODEL), tile),
        pl.BlockSpec((TL * ROW_TILES, LANES), lambda g: (cur(g), 0)),
        pl.BlockSpec((1, TL, D_ATT), last),
        pl.BlockSpec((1, TL, D_ATT), last),
        pl.BlockSpec((1, 8, D_CONV), perb),
        pl.BlockSpec((1, TL, RLANES), lambda g: (prv(g) // nt, prv(g) % nt, 0)),
        pl.BlockSpec((1, RLANES), lambda g: (0, 0)),
    ]
    out_shape = [
        jax.ShapeDtypeStruct((nb, seq, D_MODEL), F32),
        jax.ShapeDtypeStruct((nb * seq * ROW_TILES, LANES), I32),
        jax.ShapeDtypeStruct((nb, WINDOW, D_ATT), F32),
        jax.ShapeDtypeStruct((nb, WINDOW, D_ATT), F32),
        jax.ShapeDtypeStruct((nb, 8, D_CONV), F32),
        jax.ShapeDtypeStruct((nb, seq, RLANES), F32),
        jax.ShapeDtypeStruct((1, RLANES), F32),
    ]
    scratch = [
        pltpu.VMEM((RING, D_ATT), BF16), pltpu.VMEM((RING, D_ATT), BF16),
        pltpu.VMEM((TL, D_ATT), BF16), pltpu.VMEM((8, D_CONV), F32),
        pltpu.VMEM((1, RLANES), F32), pltpu.VMEM((TL, D_MODEL), F32),
    ]
    return pl.pallas_call(
        functools.partial(_prompt_kernel, nt=nt, ntiles=ntiles),
        grid=(ntiles + 1,),
        in_specs=in_specs, out_specs=out_specs, out_shape=out_shape,
        scratch_shapes=scratch,
        compiler_params=pltpu.CompilerParams(
            dimension_semantics=("arbitrary",), vmem_limit_bytes=VMEM_LIMIT),
        name="prompt_main",
    )(x, mod, g1, g2, win, bias_q, cw, cbias, wpa, wpb, wo, whi, br)


def _sample_kernel(x_ref, mod_ref, ck_ref, cv_ref, up1_ref, up2_ref, cnt_in_ref,
                   g1_ref, g2_ref, win_ref, bias_old_ref, bias_new_ref, cw_ref, cbias_ref,
                   wpa_ref, wpb_ref, wo_ref, whi_ref, br_ref,
                   x1_ref, h2_ref, ko_ref, vo_ref, uo_ref, route_ref, cnt_ref,
                   h_s, q_s, kn_s, vn_s, att_s, conv_s, h2_s, *, nseq, slen):
    n = pl.program_id(0)
    ntok = nseq * slen

    @pl.when(n == 0)
    def _():
        def norm_body(i, carry):
            rows = pl.ds(pl.multiple_of(i * slen, slen), slen)
            xi = x_ref[rows, :]
            m = mod_ref[i]
            hi = _rms(xi, g1_ref[...]) * (1.0 + m[1:2, :]) + m[0:1, :]
            h_s[rows, :] = hi.astype(BF16)
            return carry
        lax.fori_loop(0, nseq, norm_body, 0)
        hb = h_s[...]
        qkv = _dot(hb, win_ref[:, 0:3 * D_ATT])
        q_s[...] = (qkv[:, 0:D_ATT] * Q_SCALE).astype(BF16)
        k = qkv[:, D_ATT:2 * D_ATT]
        v = qkv[:, 2 * D_ATT:3 * D_ATT]
        ko_ref[...] = k
        vo_ref[...] = v
        kn_s[...] = k.astype(BF16)
        vn_s[...] = v.astype(BF16)

        cbcv = _dot(hb, win_ref[:, 3 * D_ATT:3 * D_ATT + 3 * D_CONV])
        cb = cbcv[:, 0:D_CONV]
        u = cbcv[:, D_CONV:2 * D_CONV] * cbcv[:, 2 * D_CONV:3 * D_CONV]
        pos = lax.broadcasted_iota(jnp.int32, (ntok, D_CONV), 0) & (slen - 1)
        u_m1 = jnp.where(pos < 1, up1_ref[...], pltpu.roll(u, 1, axis=0))
        u_m2 = jnp.where(pos < 2, up2_ref[...], pltpu.roll(u, 2, axis=0))
        yc = cw_ref[0:1, :] * u_m2 + cw_ref[1:2, :] * u_m1 + cw_ref[2:3, :] * u + cbias_ref[...]
        conv_s[...] = (cb * yc).astype(BF16)
        uo_ref[...] = u

    rows = pl.ds(pl.multiple_of(n * slen, slen), slen)
    nt_dims = (((1,), (1,)), ((), ()))
    shape = (N_HEADS * slen, D_ATT)
    rowh = lax.broadcasted_iota(jnp.int32, shape, 0) >> (slen.bit_length() - 1)
    laneh = lax.broadcasted_iota(jnp.int32, shape, 1) >> 6
    qt = jnp.concatenate([q_s[rows, :]] * N_HEADS, axis=0)
    qm = jnp.where(rowh == laneh, qt, jnp.zeros_like(qt))
    s_old = _dot(qm, ck_ref[0].astype(BF16)) + bias_old_ref[...]
    s_new = lax.dot_general(qm, kn_s[rows, :], nt_dims,
                            preferred_element_type=F32) + bias_new_ref[...]
    m = jnp.maximum(jnp.max(s_old, axis=1, keepdims=True), jnp.max(s_new, axis=1, keepdims=True))
    p_old = jnp.exp2(s_old - m)
    p_new = jnp.exp2(s_new - m)
    l = jnp.sum(p_old, axis=1, keepdims=True) + jnp.sum(p_new, axis=1, keepdims=True)
    o = lax.dot_general(p_old.astype(BF16), cv_ref[0].astype(BF16), nt_dims,
                        preferred_element_type=F32)
    o = (o + _dot(p_new.astype(BF16), vn_s[rows, :])) * (1.0 / l)
    lane_o = lax.broadcasted_iota(jnp.int32, (slen, D_ATT), 1) >> 6
    att = o[0:slen]
    for h in range(1, N_HEADS):
        att = jnp.where(lane_o == h, o[h * slen:(h + 1) * slen], att)
    att_s[rows, :] = att.astype(BF16)

    @pl.when(n == nseq - 1)
    def _():
        gates = _dot(h_s[...], win_ref[:, 3 * D_ATT + 3 * D_CONV:])
        pa = _dot(att_s[...], wpa_ref[...])
        pb = _dot(conv_s[...], wpb_ref[...])
        mixin = _sigmoid(gates[:, 0:D_MODEL]) * pa + _sigmoid(gates[:, D_MODEL:]) * pb
        x1_ref[...] = _dot(mixin.astype(BF16), wo_ref[...])

        def res_body(i, carry):
            r = pl.ds(pl.multiple_of(i * slen, slen), slen)
            m = mod_ref[i]
            x1 = x_ref[r, :] + m[2:3, :] * x1_ref[r, :]
            x1_ref[r, :] = x1
            h2_s[r, :] = _rms(x1, g2_ref[...]) * (1.0 + m[4:5, :]) + m[3:4, :]
            return carry
        lax.fori_loop(0, nseq, res_body, 0)

        h2 = h2_s[...]
        _store_rows_as_tiles(h2_ref, _pack_rows(h2))
        route, new_cnt = _route(h2, whi_ref, br_ref, cnt_in_ref[...])
        route_ref[...] = route
        cnt_ref[...] = new_cnt


def _sample_main(x2d, mod, ck, cv, up1, up2, cnt_in, g1, g2, win, bias_old, bias_new, cw, cbias,
                 wpa, wpb, wo, whi, br, nseq, slen):
    ntok = nseq * slen
    args = (x2d, mod, ck, cv, up1, up2, cnt_in, g1, g2, win, bias_old, bias_new, cw, cbias,
            wpa, wpb, wo, whi, br)
    in_specs = []
    for idx, a in enumerate(args):
        if idx in (2, 3):
            in_specs.append(pl.BlockSpec((1,) + a.shape[1:], lambda n: (n, 0, 0)))
        else:
            in_specs.append(_const_spec(a.shape))
    whole = lambda shape: pl.BlockSpec(shape, lambda n: (0,) * len(shape))
    outs = [((ntok, D_MODEL), F32), ((ntok * ROW_TILES, LANES), I32), ((ntok, D_ATT), F32),
            ((ntok, D_ATT), F32), ((ntok, D_CONV), F32), ((ntok, RLANES), F32), ((1, RLANES), F32)]
    scratch = [
        pltpu.VMEM((ntok, D_MODEL), BF16), pltpu.VMEM((ntok, D_ATT), BF16),
        pltpu.VMEM((ntok, D_ATT), BF16), pltpu.VMEM((ntok, D_ATT), BF16),
        pltpu.VMEM((ntok, D_ATT), BF16), pltpu.VMEM((ntok, D_CONV), BF16),
        pltpu.VMEM((ntok, D_MODEL), F32),
    ]
    return pl.pallas_call(
        functools.partial(_sample_kernel, nseq=nseq, slen=slen),
        grid=(nseq,),
        in_specs=in_specs,
        out_specs=[whole(s) for s, _ in outs],
        out_shape=[jax.ShapeDtypeStruct(s, d) for s, d in outs],
        scratch_shapes=scratch,
        compiler_params=pltpu.CompilerParams(
            dimension_semantics=("arbitrary",), vmem_limit_bytes=VMEM_LIMIT),
        name="sample_main",
    )(*args)


def _issue_rows(n, body):
    def group(g, carry):
        for u in range(ISSUE_UNROLL):
            body(g * ISSUE_UNROLL + u, u)
        return carry
    lax.fori_loop(0, n // ISSUE_UNROLL, group, 0)


def _dispatch_kernel(pend_ref, d1_ref, d2_ref, hp_ref, hs_ref, xs_out, zbuf, sem, zsem, *,
                     np_tiles, nblocks):
    i = pl.program_id(0)
    blk_rows = MOE_BLK * ROW_TILES

    @pl.when(i == 0)
    def _():
        zbuf[...] = jnp.zeros_like(zbuf)

        def zcopy(e):
            start = pl.multiple_of((pend_ref[e + 1] - MOE_BLK) * ROW_TILES, blk_rows)
            return pltpu.make_async_copy(zbuf, xs_out.at[pl.ds(start, blk_rows)], zsem)

        def zstart(e, carry):
            @pl.when(pend_ref[e + 1] > pend_ref[e])
            def _():
                zcopy(e).start()
            return carry

        def zwait(e, carry):
            @pl.when(pend_ref[e + 1] > pend_ref[e])
            def _():
                zcopy(e).wait()
            return carry
        lax.fori_loop(0, N_EXPERTS, zstart, 0)
        lax.fori_loop(0, N_EXPERTS, zwait, 0)

        def tcopy(bk):
            start = pl.multiple_of(bk * blk_rows, blk_rows)
            return pltpu.make_async_copy(zbuf, xs_out.at[pl.ds(start, blk_rows)], zsem)

        def tstart(bk, carry):
            tcopy(bk).start()
            return carry

        def twait(bk, carry):
            tcopy(bk).wait()
            return carry
        used = pend_ref[N_EXPERTS] // MOE_BLK
        lax.fori_loop(used, nblocks, tstart, 0)
        lax.fori_loop(used, nblocks, twait, 0)

    def scatter_tile(src):
        def row(r, u):
            pltpu.make_async_copy(_tile(src, r), _tile(xs_out, d1_ref[r]), sem).start(priority=u % 2)
            pltpu.make_async_copy(_tile(src, r), _tile(xs_out, d2_ref[r]), sem).start(
                priority=(u + 1) % 2)
        _issue_rows(TM, row)
        for _ in range(2):
            pltpu.make_async_copy(src, xs_out.at[pl.ds(0, TM * ROW_TILES)], sem).wait()

    @pl.when(i < np_tiles)
    def _():
        scatter_tile(hp_ref)

    @pl.when(i >= np_tiles)
    def _():
        scatter_tile(hs_ref)


def _dispatch(pend, d1, d2, h2p, h2s, nslots):
    np_tiles = h2p.shape[0] // (TM * ROW_TILES)
    ns_tiles = h2s.shape[0] // (TM * ROW_TILES)
    smem_tile = pl.BlockSpec((TM,), lambda i, *_: (i,), memory_space=pltpu.SMEM)
    rows = TM * ROW_TILES
    return pl.pallas_call(
        functools.partial(_dispatch_kernel, np_tiles=np_tiles, nblocks=nslots // MOE_BLK),
        grid_spec=pltpu.PrefetchScalarGridSpec(
            num_scalar_prefetch=1,
            grid=(np_tiles + ns_tiles,),
            in_specs=[smem_tile, smem_tile,
                      pl.BlockSpec((rows, LANES), lambda i, *_: (jnp.minimum(i, np_tiles - 1), 0)),
                      pl.BlockSpec((rows, LANES), lambda i, *_: (jnp.maximum(i - np_tiles, 0), 0))],
            out_specs=pl.BlockSpec(memory_space=pl.ANY),
            scratch_shapes=[pltpu.VMEM((MOE_BLK * ROW_TILES, LANES), I32),
                            pltpu.SemaphoreType.DMA(()), pltpu.SemaphoreType.DMA(())],
        ),
        out_shape=jax.ShapeDtypeStruct((nslots * ROW_TILES, LANES), I32),
        compiler_params=pltpu.CompilerParams(dimension_semantics=("arbitrary",)),
        name="dispatch",
    )(pend, d1, d2, h2p, h2s)


def _expert_kernel(blk_e_ref, nblk_ref, first_ref, wslot_ref, next_e_ref,
                   xs_ref, wg_hbm, wu_hbm, wd_hbm, y_ref,
                   wg_f, wu_f, wd_f, wg_b, wu_b, wd_b, wsem):
    i = pl.program_id(0)
    live = i < nblk_ref[0]

    def weight_copies(e, slot):
        return (pltpu.make_async_copy(wg_hbm.at[e], wg_f.at[slot], wsem.at[slot, 0]),
                pltpu.make_async_copy(wu_hbm.at[e], wu_f.at[slot], wsem.at[slot, 1]),
                pltpu.make_async_copy(wd_hbm.at[e], wd_f.at[slot], wsem.at[slot, 2]))

    @pl.when(i == 0)
    def _():
        for cp in weight_copies(blk_e_ref[0], 0):
            cp.start()

    @pl.when(live & (first_ref[i] == 1))
    def _():
        slot = wslot_ref[i]
        for cp in weight_copies(blk_e_ref[i], slot):
            cp.wait()
        wg_b[...] = wg_f[slot].astype(BF16)
        wu_b[...] = wu_f[slot].astype(BF16)
        wd_b[...] = wd_f[slot].astype(BF16)

        @pl.when(next_e_ref[i] >= 0)
        def _():
            for cp in weight_copies(next_e_ref[i], 1 - slot):
                cp.start()

    @pl.when(live)
    def _():
        x_lo, x_hi = _unpack_rows(_load_tiles_as_rows(xs_ref, MOE_BLK))
        x_lo = x_lo.astype(BF16)
        x_hi = x_hi.astype(BF16)
        g = _dot(x_lo, wg_b[0:HALF, :]) + _dot(x_hi, wg_b[HALF:, :])
        u = _dot(x_lo, wu_b[0:HALF, :]) + _dot(x_hi, wu_b[HALF:, :])
        a = (g * _sigmoid(g)) * u
        _store_rows_as_tiles(y_ref, _pack_rows(_dot(a.astype(BF16), wd_b[...])))

    @pl.when(jnp.logical_not(live))
    def _():
        y_ref[...] = jnp.zeros_like(y_ref)


def _experts(blk_e, nblk, first, wslot, next_e, xs, wg, wu, wd):
    blk_rows = MOE_BLK * ROW_TILES
    nblocks = xs.shape[0] // blk_rows
    row_map = lambda i, be, nb, *_: (jnp.minimum(i, nb[0] - 1), 0)
    any_spec = pl.BlockSpec(memory_space=pl.ANY)
    return pl.pallas_call(
        _expert_kernel,
        grid_spec=pltpu.PrefetchScalarGridSpec(
            num_scalar_prefetch=5,
            grid=(nblocks,),
            in_specs=[pl.BlockSpec((blk_rows, LANES), row_map), any_spec, any_spec, any_spec],
            out_specs=pl.BlockSpec((blk_rows, LANES), lambda i, *_: (i, 0)),
            scratch_shapes=[pltpu.VMEM((2, D_MODEL, D_EXPERT), F32),
                            pltpu.VMEM((2, D_MODEL, D_EXPERT), F32),
                            pltpu.VMEM((2, D_EXPERT, D_MODEL), F32),
                            pltpu.VMEM((D_MODEL, D_EXPERT), BF16),
                            pltpu.VMEM((D_MODEL, D_EXPERT), BF16),
                            pltpu.VMEM((D_EXPERT, D_MODEL), BF16),
                            pltpu.SemaphoreType.DMA((2, 3))],
        ),
        out_shape=jax.ShapeDtypeStruct(xs.shape, I32),
        compiler_params=pltpu.CompilerParams(
            dimension_semantics=("arbitrary",), vmem_limit_bytes=VMEM_LIMIT),
        name="experts",
    )(blk_e, nblk, first, wslot, next_e, xs, wg, wu, wd)


def _combine_kernel(d1_ref, d2_ref, d1n_ref, d2n_ref, y_hbm, x1_ref, route_ref, gate_ref, gf_ref,
                    o_ref, a0, b0, a1, b1, sem, *, ntiles):
    t = pl.program_id(0)
    bufs = ((a0, b0), (a1, b1))

    def gather(i1_ref, i2_ref, par):
        buf_a, buf_b = bufs[par]

        def row(r, u):
            pltpu.make_async_copy(_tile(y_hbm, i1_ref[r]), _tile(buf_a, r),
                                  sem.at[par]).start(priority=u % 2)
            pltpu.make_async_copy(_tile(y_hbm, i2_ref[r]), _tile(buf_b, r),
                                  sem.at[par]).start(priority=(u + 1) % 2)
        _issue_rows(TM, row)

    @pl.when(t == 0)
    def _():
        gather(d1_ref, d2_ref, 0)

    def step(par):
        buf_a, buf_b = bufs[par]
        for buf in (buf_a, buf_b):
            pltpu.make_async_copy(y_hbm.at[pl.ds(0, TM * ROW_TILES)], buf, sem.at[par]).wait()

        @pl.when(t + 1 < ntiles)
        def _():
            gather(d1n_ref, d2n_ref, 1 - par)

        route = route_ref[0]
        w1 = route[:, 4:5]
        w2 = route[:, 5:6]
        a_lo, a_hi = _unpack_rows(_load_tiles_as_rows(buf_a, TM))
        b_lo, b_hi = _unpack_rows(_load_tiles_as_rows(buf_b, TM))
        ffn = jnp.concatenate([w1 * a_lo + w2 * b_lo, w1 * a_hi + w2 * b_hi], axis=1)
        x2 = x1_ref[0] + gate_ref[0] * ffn
        o_ref[0] = _rms(x2, gf_ref[...])

    for par in range(2):
        pl.when(t % 2 == par)(functools.partial(step, par))


def _combine(d1, d2, y, x1, route, gate, gf, tok_base):
    nb, seq, _ = x1.shape
    nt = seq // TM
    ntiles = nb * nt
    blk0 = tok_base // TM
    smem = lambda fn: pl.BlockSpec((TM,), fn, memory_space=pltpu.SMEM)
    cur = lambda t: (blk0 + t,)
    nxt = lambda t: (blk0 + jnp.minimum(t + 1, ntiles - 1),)
    tile = lambda t: (t // nt, t % nt, 0)
    grows = gate.shape[1]
    gate_spec = (pl.BlockSpec((1, 1, D_MODEL), lambda t: (t // nt, 0, 0)) if grows == 1
                 else pl.BlockSpec((1, TM, D_MODEL), tile))
    return pl.pallas_call(
        functools.partial(_combine_kernel, ntiles=ntiles),
        grid=(ntiles,),
        in_specs=[smem(cur), smem(cur), smem(nxt), smem(nxt), pl.BlockSpec(memory_space=pl.ANY),
                  pl.BlockSpec((1, TM, D_MODEL), tile),
                  pl.BlockSpec((1, TM, RLANES), tile),
                  gate_spec,
                  pl.BlockSpec((1, D_MODEL), lambda t: (0, 0))],
        out_specs=pl.BlockSpec((1, TM, D_MODEL), tile),
        out_shape=jax.ShapeDtypeStruct(x1.shape, F32),
        scratch_shapes=[pltpu.VMEM((TM * ROW_TILES, LANES), I32)] * 4 + [
            pltpu.SemaphoreType.DMA((2,))],
        compiler_params=pltpu.CompilerParams(
            dimension_semantics=("arbitrary",), vmem_limit_bytes=VMEM_LIMIT),
        name="combine",
    )(d1, d2, d1, d2, y, x1, route, gate, gf)


def _band_bias(rel_bias, rows, keys):
    n = rows - 1 + keys
    dist = WINDOW + rows - 1 - np.arange(n + 1)
    flipped = rel_bias[:, np.clip(dist, -MAX_REL, MAX_REL) + MAX_REL]
    skew = jnp.tile(flipped, (1, rows))[:, :rows * n].reshape(N_HEADS, rows, n)
    b = skew[:, :, rows - 1:rows - 1 + keys]
    return b.reshape(2, 4 * rows, keys)


def kernel(x_prompt, x_sample, cache_attn_k, cache_attn_v, state_conv, c_prompt, c_sample,
           w_ada, b_ada, norm1_g, norm2_g, w_in, rel_bias, conv_w, conv_b, w_pa, w_pb, w_o,
           w_group, b_group, w_expert, b_expert, w_e_gate, w_e_up, w_e_down, final_g):
    assert w_ada.shape[0] == 1, "single trunk layer"
    nb, seq, _ = x_prompt.shape
    nseq, slen, _ = x_sample.shape
    ntok_p = nb * seq
    ntok_s = nseq * slen
    ntok = ntok_p + ntok_s
    assert seq % TL == 0 and WINDOW % TL == 0 and seq % TM == 0 and ntok_s % TM == 0
    assert slen >= 2 and slen & (slen - 1) == 0 and slen % 16 == 0

    n_c = nb + nseq
    n_pad = -(-n_c // 8) * 8
    c_all = jnp.concatenate([c_prompt, c_sample, jnp.zeros((n_pad - n_c, D_MODEL), F32)], axis=0)
    mod = _ada(c_all, w_ada[0], b_ada[0]).reshape(n_pad, 6, D_MODEL)
    mod_p = mod[:nb]
    mod_s = mod[nb:n_c]

    win = w_in[0].astype(BF16)
    wpa = w_pa[0].astype(BF16)
    wpb = w_pb[0].astype(BF16)
    wo = w_o[0].astype(BF16)
    g1 = norm1_g[0].reshape(1, D_MODEL)
    g2 = norm2_g[0].reshape(1, D_MODEL)
    gf = final_g.reshape(1, D_MODEL)
    cw = jnp.concatenate([conv_w[0], jnp.zeros((8 - conv_w.shape[1], D_CONV), F32)], axis=0)
    cbias = conv_b[0].reshape(1, D_CONV)
    wr = jnp.concatenate([w_expert[0], w_group[0],
                          jnp.zeros((D_MODEL, RLANES - N_EXPERTS - N_GROUPS), F32)], axis=1)
    whi = wr.astype(BF16)
    br = jnp.concatenate([b_expert[0], b_group[0],
                          jnp.zeros((RLANES - N_EXPERTS - N_GROUPS,), F32)]).reshape(1, RLANES)
    bias_p = _band_bias(rel_bias[0] * LOG2E, CHUNK, BAND)
    bias_s = _band_bias(rel_bias[0] * LOG2E, slen, WINDOW + slen).reshape(
        N_HEADS * slen, WINDOW + slen)
    bias_old = bias_s[:, :WINDOW]
    bias_new = bias_s[:, WINDOW:]

    x1p, h2p, kp, vp, up8, route_p, cnt_p = _prompt_main(
        x_prompt, mod_p, g1, g2, win, bias_p, cw, cbias, wpa, wpb, wo, whi, br)

    st = state_conv[0]
    up1 = jnp.zeros((nseq, slen, D_CONV), F32).at[:, 0].set(st[:, 1]).reshape(ntok_s, D_CONV)
    up2 = (jnp.zeros((nseq, slen, D_CONV), F32).at[:, 0].set(st[:, 0]).at[:, 1].set(st[:, 1])
           .reshape(ntok_s, D_CONV))
    ck = jnp.transpose(cache_attn_k[0], (0, 2, 3, 1)).reshape(nseq, D_ATT, WINDOW)
    cv = jnp.transpose(cache_attn_v[0], (0, 2, 3, 1)).reshape(nseq, D_ATT, WINDOW)
    x1s, h2s, ks, vs, us, route_s, cnt = _sample_main(
        x_sample.reshape(ntok_s, D_MODEL), mod_s, ck, cv, up1, up2, cnt_p,
        g1, g2, win, bias_old, bias_new, cw, cbias, wpa, wpb, wo, whi, br, nseq, slen)

    route_all = jnp.concatenate([route_p.reshape(ntok_p, RLANES)[:, :4], route_s[:, :4]], axis=0)
    experts = route_all[:, 0:2].astype(jnp.int32)
    ranks = route_all[:, 2:4].astype(jnp.int32)
    counts = cnt[0, :N_EXPERTS].astype(jnp.int32)
    pcounts = (counts + MOE_BLK - 1) // MOE_BLK * MOE_BLK
    pend = jnp.cumsum(pcounts)
    pstart = pend - pcounts
    eids = jnp.arange(N_EXPERTS, dtype=jnp.int32)
    dest = jnp.sum(jnp.where(experts[..., None] == eids, pstart, 0), axis=-1) + ranks
    d1 = dest[:, 0]
    d2 = dest[:, 1]
    nblocks = (2 * ntok) // MOE_BLK + N_EXPERTS
    blk_start = jnp.arange(nblocks, dtype=jnp.int32) * MOE_BLK
    blk_e = jnp.minimum(jnp.sum((pend[None, :] <= blk_start[:, None]).astype(jnp.int32), axis=1),
                        N_EXPERTS - 1)
    nblk = (pend[-1:] // MOE_BLK).astype(jnp.int32)
    pend0 = jnp.concatenate([jnp.zeros((1,), jnp.int32), pend.astype(jnp.int32)])

    xs = _dispatch(pend0, d1, d2, h2p, h2s, nblocks * MOE_BLK)
    blk_id = jnp.arange(nblocks, dtype=jnp.int32)
    first = (blk_id < nblk[0]) & ((blk_id == 0) | (blk_e != jnp.roll(blk_e, 1)))
    wslot = (jnp.cumsum(first.astype(jnp.int32)) - 1) % 2
    later_first = lax.cummin(jnp.where(first, blk_id, nblocks)[::-1])[::-1]
    next_first = jnp.concatenate([later_first[1:], jnp.full((1,), nblocks, jnp.int32)])
    next_e = jnp.where(next_first < nblocks, blk_e[jnp.minimum(next_first, nblocks - 1)], -1)
    y = _experts(blk_e, nblk, first.astype(jnp.int32), wslot.astype(jnp.int32),
                 next_e.astype(jnp.int32), xs, w_e_gate[0], w_e_up[0], w_e_down[0])

    y_prompt = _combine(d1, d2, y, x1p, route_p, mod_p[:, 5:6, :], gf, 0)
    gate_s = jnp.repeat(mod_s[:, 5, :], slen, axis=0).reshape(1, ntok_s, D_MODEL)
    y_sample = _combine(d1, d2, y, x1s.reshape(1, ntok_s, D_MODEL),
                        route_s.reshape(1, ntok_s, RLANES), gate_s, gf, ntok_p)

    new_k_p = kp.reshape(1, nb, WINDOW, N_HEADS, HEAD_DIM)
    new_v_p = vp.reshape(1, nb, WINDOW, N_HEADS, HEAD_DIM)
    new_conv_p = up8[:, 6:8, :].reshape(1, nb, 2, D_CONV)
    new_k_s = ks.reshape(1, nseq, slen, N_HEADS, HEAD_DIM)
    new_v_s = vs.reshape(1, nseq, slen, N_HEADS, HEAD_DIM)
    new_conv_s = us.reshape(nseq, slen, D_CONV)[:, slen - 2:, :].reshape(1, nseq, 2, D_CONV)
    return (y_prompt, y_sample.reshape(nseq, slen, D_MODEL), new_k_p, new_v_p, new_conv_p,
            new_k_s, new_v_s, new_conv_s)
```

```python
import functools

import numpy as np
import jax
import jax.numpy as jnp
from jax import lax
from jax.experimental import pallas as pl
from jax.experimental.pallas import tpu as pltpu

F32 = jnp.float32
BF16 = jnp.bfloat16
I32 = jnp.int32

D_MODEL = 1024
CHUNK = 64
LEFT = 8
WINDOW = LEFT * CHUNK
BAND = WINDOW + CHUNK
N_HEADS = 8
HEAD_DIM = 64
D_ATT = N_HEADS * HEAD_DIM
QUAD = 256
MAX_REL = 128
D_CONV = 512
N_GROUPS = 4
EPG = 8
N_EXPERTS = 32
D_EXPERT = 512
EPS = 1e-6
NEG = -1e30
LOG2E = float(np.log2(np.e))
Q_SCALE = HEAD_DIM ** -0.5 * LOG2E

TL = 512
RING = WINDOW + TL
MOE_BLK = 512
TM = 512
SAMPLE_SEQS = 2
ZERO_ROWS = 64
RLANES = 128
LANES = 128
HALF = D_MODEL // 2
ROW_TILES = HALF // LANES
ISSUE_UNROLL = 8
VMEM_LIMIT = 56 * 1024 * 1024


def _const_spec(shape):
    nd = len(shape)
    return pl.BlockSpec(shape, lambda *_: (0,) * nd, pipeline_mode=pl.Buffered(1))


def _dot(a, b):
    return jnp.dot(a, b, preferred_element_type=F32)


def _sigmoid(x):
    return 1.0 / (1.0 + jnp.exp(-x))


def _rms(x, g):
    ms = jnp.mean(x * x, axis=-1, keepdims=True)
    return x * lax.rsqrt(ms + EPS) * g


def _pack_rows(val):
    lo = lax.bitcast_convert_type(val[:, :HALF], I32) + 0x8000
    hi = lax.bitcast_convert_type(val[:, HALF:], I32) + 0x8000
    return (hi & -65536) | lax.shift_right_logical(lo, 16)


def _unpack_rows(packed):
    lo = lax.bitcast_convert_type(lax.shift_left(packed, 16), F32)
    hi = lax.bitcast_convert_type(packed & -65536, F32)
    return lo, hi


def _store_rows_as_tiles(ref, packed):
    r = packed.shape[0]
    for c in range(ROW_TILES):
        ref[pl.ds(c, r, stride=ROW_TILES), :] = packed[:, c * LANES:(c + 1) * LANES]


def _load_tiles_as_rows(ref, r):
    return jnp.concatenate(
        [ref[pl.ds(c, r, stride=ROW_TILES), :] for c in range(ROW_TILES)], axis=1)


def _tile(ref, row):
    return ref.at[pl.ds(pl.multiple_of(row * ROW_TILES, ROW_TILES), ROW_TILES)]


def _ada_kernel(c_ref, w_ref, b_ref, o_ref):
    c = c_ref[...]
    s = c * _sigmoid(c)
    o_ref[...] = _dot(s.astype(BF16), w_ref[...].astype(BF16)) + b_ref[...]


def _ada(c_all, w_ada, b_ada):
    n = c_all.shape[0]
    nb = 1024
    return pl.pallas_call(
        _ada_kernel,
        grid=(6 * D_MODEL // nb,),
        in_specs=[pl.BlockSpec((n, D_MODEL), lambda i: (0, 0)),
                  pl.BlockSpec((D_MODEL, nb), lambda i: (0, i)),
                  pl.BlockSpec((1, nb), lambda i: (0, i))],
        out_specs=pl.BlockSpec((n, nb), lambda i: (0, i)),
        out_shape=jax.ShapeDtypeStruct((n, 6 * D_MODEL), F32),
        name="ada",
    )(c_all, w_ada, b_ada.reshape(1, -1))


def _attend(q, kb, vb, bias, lim):
    r = q.shape[0]
    nk = kb.shape[0]
    assert r & (r - 1) == 0
    qt = jnp.concatenate([q] * 4, axis=0)
    rowh = lax.broadcasted_iota(jnp.int32, (4 * r, QUAD), 0) >> (r.bit_length() - 1)
    laneh = lax.broadcasted_iota(jnp.int32, (4 * r, QUAD), 1) >> 6
    qm = jnp.where(rowh == laneh, qt, jnp.zeros_like(qt))
    s = lax.dot_general(qm, kb, (((1,), (1,)), ((), ())), preferred_element_type=F32)
    valid = None
    if lim is not None:
        valid = lax.broadcasted_iota(jnp.int32, (r, nk), 1) >= lim
    ps, inv_l = [], []
    for h in range(4):
        sh = s[h * r:(h + 1) * r] + bias[h * r:(h + 1) * r]
        if valid is not None:
            sh = jnp.where(valid, sh, NEG)
        m = jnp.max(sh, axis=1, keepdims=True)
        ph = jnp.exp2(sh - m)
        inv_l.append(1.0 / jnp.sum(ph, axis=1, keepdims=True))
        ps.append(ph.astype(BF16))
    o = _dot(jnp.concatenate(ps, axis=0), vb)
    lane_o = lax.broadcasted_iota(jnp.int32, (r, QUAD), 1) >> 6
    out = o[0:r] * inv_l[0]
    for h in range(1, 4):
        out = jnp.where(lane_o == h, o[h * r:(h + 1) * r] * inv_l[h], out)
    return out


def _route_stages(h2, whi_ref, br_ref, cnt, valid=None):
    r = h2.shape[0]
    lane = lax.broadcasted_iota(jnp.int32, (r, RLANES), 1)
    lane_f = lane.astype(F32)
    big = jnp.float32(1000.0)
    v = {}

    def logits():
        v["logits"] = _dot(h2.astype(BF16), whi_ref[...]) + br_ref[...]

    def group():
        lg = jnp.where((lane >= N_EXPERTS) & (lane < N_EXPERTS + N_GROUPS), v["logits"], NEG)
        mg = jnp.max(lg, axis=1, keepdims=True)
        v["gi"] = jnp.min(jnp.where(lg == mg, lane_f, big), axis=1, keepdims=True) - N_EXPERTS
        v["pg"] = 1.0 / jnp.sum(jnp.exp(lg - mg), axis=1, keepdims=True)

    def top1():
        grp_of_lane = (lane >> 3).astype(F32)
        le = jnp.where((lane < N_EXPERTS) & (grp_of_lane == v["gi"]), v["logits"], NEG)
        v["m1"] = jnp.max(le, axis=1, keepdims=True)
        v["i1"] = jnp.min(jnp.where(le == v["m1"], lane_f, big), axis=1, keepdims=True)
        v["le"] = le

    def top2():
        sel1 = lane_f == v["i1"]
        le2 = jnp.where(sel1, NEG, v["le"])
        m2 = jnp.max(le2, axis=1, keepdims=True)
        v["i2"] = jnp.min(jnp.where(le2 == m2, lane_f, big), axis=1, keepdims=True)
        rr = jnp.exp(m2 - v["m1"])
        inv = v["pg"] / (1.0 + rr)
        v["w1"] = inv
        v["w2"] = inv * rr
        v["sel1"] = sel1
        v["sel2"] = lane_f == v["i2"]

    def ranks():
        oh = jnp.where(v["sel1"] | v["sel2"], 1.0 if valid is None else valid, 0.0).astype(F32)
        ri = lax.broadcasted_iota(jnp.int32, (r, r), 0)
        ci = lax.broadcasted_iota(jnp.int32, (r, r), 1)
        tri = jnp.where(ri > ci, 1.0, 0.0).astype(BF16)
        v["before"] = _dot(tri, oh.astype(BF16)) + cnt
        v["new_cnt"] = cnt + jnp.sum(oh, axis=0, keepdims=True)

    def assemble():
        r1 = jnp.sum(jnp.where(v["sel1"], v["before"], 0.0), axis=1, keepdims=True)
        r2 = jnp.sum(jnp.where(v["sel2"], v["before"], 0.0), axis=1, keepdims=True)
        route = jnp.where(lane == 0, v["i1"], 0.0)
        route = jnp.where(lane == 1, v["i2"], route)
        route = jnp.where(lane == 2, r1, route)
        route = jnp.where(lane == 3, r2, route)
        route = jnp.where(lane == 4, v["w1"], route)
        route = jnp.where(lane == 5, v["w2"], route)
        return route, v["new_cnt"]

    return [logits, group, top1, top2, ranks, assemble]


def _route(h2, whi_ref, br_ref, cnt):
    stages = _route_stages(h2, whi_ref, br_ref, cnt)
    for stage in stages[:-1]:
        stage()
    return stages[-1]()


def _prompt_kernel(x_ref, mod_ref, g1_ref, g2_ref, win_ref, bias_ref, cw_ref, cbias_ref,
                   wpa_ref, wpb_ref, wo_ref, whi_ref, br_ref,
                   x1_ref, h2_ref, ko_ref, vo_ref, uo_ref, route_ref, cnt_ref,
                   kring, vring, att_s, ucarry, cnt_s, h2_prev, *, nt, ntiles):
    g = pl.program_id(0)

    @pl.when(g == 0)
    def _():
        cnt_s[...] = jnp.zeros_like(cnt_s)
        h2_prev[...] = jnp.zeros_like(h2_prev)

    @pl.when(g < ntiles)
    def _():
        _prompt_tile(g, nt, x_ref, mod_ref, g1_ref, g2_ref, win_ref, bias_ref, cw_ref, cbias_ref,
                     wpa_ref, wpb_ref, wo_ref, whi_ref, br_ref,
                     x1_ref, h2_ref, ko_ref, vo_ref, uo_ref, route_ref, cnt_ref,
                     kring, vring, att_s, ucarry, cnt_s, h2_prev)

    @pl.when(g == ntiles)
    def _():
        route, new_cnt = _route(h2_prev[...], whi_ref, br_ref, cnt_s[...])
        route_ref[0] = route
        cnt_ref[...] = new_cnt


def _prompt_tile(g, nt, x_ref, mod_ref, g1_ref, g2_ref, win_ref, bias_ref, cw_ref, cbias_ref,
                 wpa_ref, wpb_ref, wo_ref, whi_ref, br_ref,
                 x1_ref, h2_ref, ko_ref, vo_ref, uo_ref, route_ref, cnt_ref,
                 kring, vring, att_s, ucarry, cnt_s, h2_prev):
    j = g % nt

    @pl.when(j == 0)
    def _():
        kring[0:WINDOW, :] = jnp.zeros((WINDOW, D_ATT), BF16)
        vring[0:WINDOW, :] = jnp.zeros((WINDOW, D_ATT), BF16)
        ucarry[...] = jnp.zeros_like(ucarry)

    route_stages = _route_stages(h2_prev[...], whi_ref, br_ref, cnt_s[...],
                                 valid=jnp.where(g > 0, 1.0, 0.0).astype(F32))
    route_stages[0]()

    sh1 = mod_ref[0, 0:1, :]
    sc1 = mod_ref[0, 1:2, :]
    gt1 = mod_ref[0, 2:3, :]
    sh2 = mod_ref[0, 3:4, :]
    sc2 = mod_ref[0, 4:5, :]

    x = x_ref[0]
    h = _rms(x, g1_ref[...]) * (1.0 + sc1) + sh1
    hb = h.astype(BF16)

    qkv = _dot(hb, win_ref[:, 0:3 * D_ATT])
    q = (qkv[:, 0:D_ATT] * Q_SCALE).astype(BF16)
    k = qkv[:, D_ATT:2 * D_ATT]
    v = qkv[:, 2 * D_ATT:3 * D_ATT]
    ko_ref[0] = k
    vo_ref[0] = v
    kring[WINDOW:RING, :] = k.astype(BF16)
    vring[WINDOW:RING, :] = v.astype(BF16)

    base = j * TL
    for c in range(TL // CHUNK):
        lim = WINDOW - (base + c * CHUNK)
        for qd in range(2):
            ls = slice(qd * QUAD, (qd + 1) * QUAD)
            o = _attend(q[c * CHUNK:(c + 1) * CHUNK, ls],
                        kring[c * CHUNK:c * CHUNK + BAND, ls],
                        vring[c * CHUNK:c * CHUNK + BAND, ls],
                        bias_ref[qd], lim)
            att_s[c * CHUNK:(c + 1) * CHUNK, ls] = o.astype(BF16)
        if c + 1 < len(route_stages) - 1:
            route_stages[c + 1]()
    route, new_cnt = route_stages[-1]()
    route_ref[0] = route
    cnt_s[...] = new_cnt
    cnt_ref[...] = new_cnt

    kring[0:WINDOW, :] = kring[TL:RING, :]
    vring[0:WINDOW, :] = vring[TL:RING, :]

    cbcv = _dot(hb, win_ref[:, 3 * D_ATT:3 * D_ATT + 3 * D_CONV])
    gates = _dot(hb, win_ref[:, 3 * D_ATT + 3 * D_CONV:])
    cb = cbcv[:, 0:D_CONV]
    u = cbcv[:, D_CONV:2 * D_CONV] * cbcv[:, 2 * D_CONV:3 * D_CONV]
    row = lax.broadcasted_iota(jnp.int32, (8, D_CONV), 0)
    prev = ucarry[...]
    r1 = pltpu.roll(u, 1, axis=0)
    r2 = pltpu.roll(u, 2, axis=0)
    u_m1 = jnp.concatenate(
        [jnp.where(row < 1, pltpu.roll(prev, 1, axis=0), r1[0:8]), r1[8:]], axis=0)
    u_m2 = jnp.concatenate(
        [jnp.where(row < 2, pltpu.roll(prev, 2, axis=0), r2[0:8]), r2[8:]], axis=0)
    yc = cw_ref[0:1, :] * u_m2 + cw_ref[1:2, :] * u_m1 + cw_ref[2:3, :] * u + cbias_ref[...]
    conv_out = (cb * yc).astype(BF16)
    ucarry[...] = u[TL - 8:TL, :]
    uo_ref[0] = u[TL - 8:TL, :]

    pa = _dot(att_s[...], wpa_ref[...])
    pb = _dot(conv_out, wpb_ref[...])
    mixin = _sigmoid(gates[:, 0:D_MODEL]) * pa + _sigmoid(gates[:, D_MODEL:]) * pb
    mix = _dot(mixin.astype(BF16), wo_ref[...])
    x1 = x + gt1 * mix
    x1_ref[0] = x1
    h2 = _rms(x1, g2_ref[...]) * (1.0 + sc2) + sh2
    _store_rows_as_tiles(h2_ref, _pack_rows(h2))

    h2_prev[...] = h2


def _prompt_main(x, mod, g1, g2, win, bias_q, cw, cbias, wpa, wpb, wo, whi, br):
    nb, seq, _ = x.shape
    nt = seq // TL
    ntiles = nb * nt
    keep = WINDOW // TL
    cur = lambda g: jnp.minimum(g, ntiles - 1)
    prv = lambda g: jnp.maximum(g - 1, 0)
    tile = lambda g: (cur(g) // nt, cur(g) % nt, 0)
    last = lambda g: (cur(g) // nt, jnp.maximum(cur(g) % nt - (nt - keep), 0), 0)
    perb = lambda g: (cur(g) // nt, 0, 0)
    in_specs = [
        pl.BlockSpec((1, TL, D_MODEL), tile),
        pl.BlockSpec((1, 6, D_MODEL), perb),
        _const_spec(g1.shape), _const_spec(g2.shape), _const_spec(win.shape),
        _const_spec(bias_q.shape), _const_spec(cw.shape), _const_spec(cbias.shape),
        _const_spec(wpa.shape), _const_spec(wpb.shape), _const_spec(wo.shape),
        _const_spec(whi.shape), _const_spec(br.shape),
    ]
    out_specs = [
        pl.BlockSpec((1, TL, D_MODEL), tile),
        pl.BlockSpec((TL * ROW_TILES, LANES), lambda g: (cur(g), 0)),
        pl.BlockSpec((1, TL, D_ATT), last),
        pl.BlockSpec((1, TL, D_ATT), last),
        pl.BlockSpec((1, 8, D_CONV), perb),
        pl.BlockSpec((1, TL, RLANES), lambda g: (prv(g) // nt, prv(g) % nt, 0)),
        pl.BlockSpec((1, RLANES), lambda g: (0, 0)),
    ]
    out_shape = [
        jax.ShapeDtypeStruct((nb, seq, D_MODEL), F32),
        jax.ShapeDtypeStruct((nb * seq * ROW_TILES, LANES), I32),
        jax.ShapeDtypeStruct((nb, WINDOW, D_ATT), F32),
        jax.ShapeDtypeStruct((nb, WINDOW, D_ATT), F32),
        jax.ShapeDtypeStruct((nb, 8, D_CONV), F32),
        jax.ShapeDtypeStruct((nb, seq, RLANES), F32),
        jax.ShapeDtypeStruct((1, RLANES), F32),
    ]
    scratch = [
        pltpu.VMEM((RING, D_ATT), BF16), pltpu.VMEM((RING, D_ATT), BF16),
        pltpu.VMEM((TL, D_ATT), BF16), pltpu.VMEM((8, D_CONV), F32),
        pltpu.VMEM((1, RLANES), F32), pltpu.VMEM((TL, D_MODEL), F32),
    ]
    return pl.pallas_call(
        functools.partial(_prompt_kernel, nt=nt, ntiles=ntiles),
        grid=(ntiles + 1,),
        in_specs=in_specs, out_specs=out_specs, out_shape=out_shape,
        scratch_shapes=scratch,
        compiler_params=pltpu.CompilerParams(
            dimension_semantics=("arbitrary",), vmem_limit_bytes=VMEM_LIMIT),
        name="prompt_main",
    )(x, mod, g1, g2, win, bias_q, cw, cbias, wpa, wpb, wo, whi, br)


def _sample_kernel(x_ref, mod_ref, ck_ref, cv_ref, up1_ref, up2_ref, cnt_in_ref,
                   g1_ref, g2_ref, win_ref, bias_old_ref, bias_new_ref, cw_ref, cbias_ref,
                   wpa_ref, wpb_ref, wo_ref, whi_ref, br_ref,
                   x1_ref, h2_ref, ko_ref, vo_ref, uo_ref, route_ref, cnt_ref,
                   h_s, q_s, kn_s, vn_s, att_s, conv_s, h2_s, *, nseq, slen):
    n = pl.program_id(0)
    ntok = nseq * slen

    @pl.when(n == 0)
    def _():
        def norm_body(i, carry):
            rows = pl.ds(pl.multiple_of(i * slen, slen), slen)
            xi = x_ref[rows, :]
            m = mod_ref[i]
            hi = _rms(xi, g1_ref[...]) * (1.0 + m[1:2, :]) + m[0:1, :]
            h_s[rows, :] = hi.astype(BF16)
            return carry
        lax.fori_loop(0, nseq, norm_body, 0)
        hb = h_s[...]
        qkv = _dot(hb, win_ref[:, 0:3 * D_ATT])
        q_s[...] = (qkv[:, 0:D_ATT] * Q_SCALE).astype(BF16)
        k = qkv[:, D_ATT:2 * D_ATT]
        v = qkv[:, 2 * D_ATT:3 * D_ATT]
        ko_ref[...] = k
        vo_ref[...] = v
        kn_s[...] = k.astype(BF16)
        vn_s[...] = v.astype(BF16)

        cbcv = _dot(hb, win_ref[:, 3 * D_ATT:3 * D_ATT + 3 * D_CONV])
        cb = cbcv[:, 0:D_CONV]
        u = cbcv[:, D_CONV:2 * D_CONV] * cbcv[:, 2 * D_CONV:3 * D_CONV]
        pos = lax.broadcasted_iota(jnp.int32, (ntok, D_CONV), 0) & (slen - 1)
        u_m1 = jnp.where(pos < 1, up1_ref[...], pltpu.roll(u, 1, axis=0))
        u_m2 = jnp.where(pos < 2, up2_ref[...], pltpu.roll(u, 2, axis=0))
        yc = cw_ref[0:1, :] * u_m2 + cw_ref[1:2, :] * u_m1 + cw_ref[2:3, :] * u + cbias_ref[...]
        conv_s[...] = (cb * yc).astype(BF16)
        uo_ref[...] = u

    nt_dims = (((1,), (1,)), ((), ()))
    shape = (N_HEADS * slen, D_ATT)
    rowh = lax.broadcasted_iota(jnp.int32, shape, 0) >> (slen.bit_length() - 1)
    laneh = lax.broadcasted_iota(jnp.int32, shape, 1) >> 6
    lane_o = lax.broadcasted_iota(jnp.int32, (slen, D_ATT), 1) >> 6
    for sq in range(SAMPLE_SEQS):
        rows = pl.ds(pl.multiple_of((n * SAMPLE_SEQS + sq) * slen, slen), slen)
        qt = jnp.concatenate([q_s[rows, :]] * N_HEADS, axis=0)
        qm = jnp.where(rowh == laneh, qt, jnp.zeros_like(qt))
        s_old = _dot(qm, ck_ref[sq].astype(BF16)) + bias_old_ref[...]
        s_new = lax.dot_general(qm, kn_s[rows, :], nt_dims,
                                preferred_element_type=F32) + bias_new_ref[...]
        m = jnp.maximum(jnp.max(s_old, axis=1, keepdims=True),
                        jnp.max(s_new, axis=1, keepdims=True))
        p_old = jnp.exp2(s_old - m)
        p_new = jnp.exp2(s_new - m)
        l = jnp.sum(p_old, axis=1, keepdims=True) + jnp.sum(p_new, axis=1, keepdims=True)
        o = lax.dot_general(p_old.astype(BF16), cv_ref[sq].astype(BF16), nt_dims,
                            preferred_element_type=F32)
        o = (o + _dot(p_new.astype(BF16), vn_s[rows, :])) * (1.0 / l)
        att = o[0:slen]
        for h in range(1, N_HEADS):
            att = jnp.where(lane_o == h, o[h * slen:(h + 1) * slen], att)
        att_s[rows, :] = att.astype(BF16)

    @pl.when(n == nseq // SAMPLE_SEQS - 1)
    def _():
        gates = _dot(h_s[...], win_ref[:, 3 * D_ATT + 3 * D_CONV:])
        pa = _dot(att_s[...], wpa_ref[...])
        pb = _dot(conv_s[...], wpb_ref[...])
        mixin = _sigmoid(gates[:, 0:D_MODEL]) * pa + _sigmoid(gates[:, D_MODEL:]) * pb
        x1_ref[...] = _dot(mixin.astype(BF16), wo_ref[...])

        def res_body(i, carry):
            r = pl.ds(pl.multiple_of(i * slen, slen), slen)
            m = mod_ref[i]
            x1 = x_ref[r, :] + m[2:3, :] * x1_ref[r, :]
            x1_ref[r, :] = x1
            h2_s[r, :] = _rms(x1, g2_ref[...]) * (1.0 + m[4:5, :]) + m[3:4, :]
            return carry
        lax.fori_loop(0, nseq, res_body, 0)

        h2 = h2_s[...]
        _store_rows_as_tiles(h2_ref, _pack_rows(h2))
        route, new_cnt = _route(h2, whi_ref, br_ref, cnt_in_ref[...])
        route_ref[...] = route
        cnt_ref[...] = new_cnt


def _sample_main(x2d, mod, ck, cv, up1, up2, cnt_in, g1, g2, win, bias_old, bias_new, cw, cbias,
                 wpa, wpb, wo, whi, br, nseq, slen):
    ntok = nseq * slen
    args = (x2d, mod, ck, cv, up1, up2, cnt_in, g1, g2, win, bias_old, bias_new, cw, cbias,
            wpa, wpb, wo, whi, br)
    in_specs = []
    for idx, a in enumerate(args):
        if idx in (2, 3):
            in_specs.append(pl.BlockSpec((SAMPLE_SEQS,) + a.shape[1:], lambda n: (n, 0, 0)))
        else:
            in_specs.append(_const_spec(a.shape))
    whole = lambda shape: pl.BlockSpec(shape, lambda n: (0,) * len(shape))
    outs = [((ntok, D_MODEL), F32), ((ntok * ROW_TILES, LANES), I32), ((ntok, D_ATT), F32),
            ((ntok, D_ATT), F32), ((ntok, D_CONV), F32), ((ntok, RLANES), F32), ((1, RLANES), F32)]
    scratch = [
        pltpu.VMEM((ntok, D_MODEL), BF16), pltpu.VMEM((ntok, D_ATT), BF16),
        pltpu.VMEM((ntok, D_ATT), BF16), pltpu.VMEM((ntok, D_ATT), BF16),
        pltpu.VMEM((ntok, D_ATT), BF16), pltpu.VMEM((ntok, D_CONV), BF16),
        pltpu.VMEM((ntok, D_MODEL), F32),
    ]
    return pl.pallas_call(
        functools.partial(_sample_kernel, nseq=nseq, slen=slen),
        grid=(nseq // SAMPLE_SEQS,),
        in_specs=in_specs,
        out_specs=[whole(s) for s, _ in outs],
        out_shape=[jax.ShapeDtypeStruct(s, d) for s, d in outs],
        scratch_shapes=scratch,
        compiler_params=pltpu.CompilerParams(
            dimension_semantics=("arbitrary",), vmem_limit_bytes=VMEM_LIMIT),
        name="sample_main",
    )(*args)


def _issue_rows(n, body):
    def group(g, carry):
        for u in range(ISSUE_UNROLL):
            body(g * ISSUE_UNROLL + u, u)
        return carry
    lax.fori_loop(0, n // ISSUE_UNROLL, group, 0)


def _dispatch_kernel(pend_ref, cnt_ref, d1_ref, d2_ref, hp_ref, hs_ref, xs_out, zbuf, sem, zsem, *,
                     np_tiles, nslots):
    i = pl.program_id(0)
    zrows = ZERO_ROWS * ROW_TILES

    @pl.when(i == 0)
    def _():
        zbuf[...] = jnp.zeros_like(zbuf)

        def zcopy(piece):
            start = pl.multiple_of(piece * zrows, zrows)
            return pltpu.make_async_copy(zbuf, xs_out.at[pl.ds(start, zrows)], zsem)

        def pieces(e):
            lo = (pend_ref[e] + cnt_ref[jnp.minimum(e, N_EXPERTS - 1)]) // ZERO_ROWS
            hi = pend_ref[jnp.minimum(e + 1, N_EXPERTS)] // ZERO_ROWS
            lo = jnp.where(e == N_EXPERTS, pend_ref[N_EXPERTS] // ZERO_ROWS, lo)
            hi = jnp.where(e == N_EXPERTS, nslots // ZERO_ROWS, hi)
            return lo, hi

        def start_all(e, carry):
            lo, hi = pieces(e)
            return lax.fori_loop(lo, hi, lambda p, c: (zcopy(p).start(), c)[1], carry)

        def wait_all(e, carry):
            lo, hi = pieces(e)
            return lax.fori_loop(lo, hi, lambda p, c: (zcopy(p).wait(), c)[1], carry)
        lax.fori_loop(0, N_EXPERTS + 1, start_all, 0)
        lax.fori_loop(0, N_EXPERTS + 1, wait_all, 0)

    def scatter_tile(src):
        def row(r, u):
            pltpu.make_async_copy(_tile(src, r), _tile(xs_out, d1_ref[r]), sem).start(priority=u % 2)
            pltpu.make_async_copy(_tile(src, r), _tile(xs_out, d2_ref[r]), sem).start(
                priority=(u + 1) % 2)
        _issue_rows(TM, row)
        for _ in range(2):
            pltpu.make_async_copy(src, xs_out.at[pl.ds(0, TM * ROW_TILES)], sem).wait()

    @pl.when(i < np_tiles)
    def _():
        scatter_tile(hp_ref)

    @pl.when(i >= np_tiles)
    def _():
        scatter_tile(hs_ref)


def _dispatch(pend, counts, d1, d2, h2p, h2s, nslots):
    np_tiles = h2p.shape[0] // (TM * ROW_TILES)
    ns_tiles = h2s.shape[0] // (TM * ROW_TILES)
    smem_tile = pl.BlockSpec((TM,), lambda i, *_: (i,), memory_space=pltpu.SMEM)
    rows = TM * ROW_TILES
    return pl.pallas_call(
        functools.partial(_dispatch_kernel, np_tiles=np_tiles, nslots=nslots),
        grid_spec=pltpu.PrefetchScalarGridSpec(
            num_scalar_prefetch=2,
            grid=(np_tiles + ns_tiles,),
            in_specs=[smem_tile, smem_tile,
                      pl.BlockSpec((rows, LANES), lambda i, *_: (jnp.minimum(i, np_tiles - 1), 0)),
                      pl.BlockSpec((rows, LANES), lambda i, *_: (jnp.maximum(i - np_tiles, 0), 0))],
            out_specs=pl.BlockSpec(memory_space=pl.ANY),
            scratch_shapes=[pltpu.VMEM((ZERO_ROWS * ROW_TILES, LANES), I32),
                            pltpu.SemaphoreType.DMA(()), pltpu.SemaphoreType.DMA(())],
        ),
        out_shape=jax.ShapeDtypeStruct((nslots * ROW_TILES, LANES), I32),
        compiler_params=pltpu.CompilerParams(dimension_semantics=("arbitrary",)),
        name="dispatch",
    )(pend, counts, d1, d2, h2p, h2s)


def _expert_kernel(blk_e_ref, nblk_ref, first_ref, wslot_ref, next_e_ref,
                   xs_ref, wg_hbm, wu_hbm, wd_hbm, y_ref,
                   wg_f, wu_f, wd_f, wg_b, wu_b, wd_b, wsem):
    i = pl.program_id(0)
    live = i < nblk_ref[0]

    def weight_copies(e, slot):
        return (pltpu.make_async_copy(wg_hbm.at[e], wg_f.at[slot], wsem.at[slot, 0]),
                pltpu.make_async_copy(wu_hbm.at[e], wu_f.at[slot], wsem.at[slot, 1]),
                pltpu.make_async_copy(wd_hbm.at[e], wd_f.at[slot], wsem.at[slot, 2]))

    @pl.when(i == 0)
    def _():
        for cp in weight_copies(blk_e_ref[0], 0):
            cp.start()

    @pl.when(live & (first_ref[i] == 1))
    def _():
        slot = wslot_ref[i]
        for cp in weight_copies(blk_e_ref[i], slot):
            cp.wait()
        wg_b[...] = wg_f[slot].astype(BF16)
        wu_b[...] = wu_f[slot].astype(BF16)
        wd_b[...] = wd_f[slot].astype(BF16)

        @pl.when(next_e_ref[i] >= 0)
        def _():
            for cp in weight_copies(next_e_ref[i], 1 - slot):
                cp.start()

    @pl.when(live)
    def _():
        x_lo, x_hi = _unpack_rows(_load_tiles_as_rows(xs_ref, MOE_BLK))
        x_lo = x_lo.astype(BF16)
        x_hi = x_hi.astype(BF16)
        g = _dot(x_lo, wg_b[0:HALF, :]) + _dot(x_hi, wg_b[HALF:, :])
        u = _dot(x_lo, wu_b[0:HALF, :]) + _dot(x_hi, wu_b[HALF:, :])
        a = (g * _sigmoid(g)) * u
        _store_rows_as_tiles(y_ref, _pack_rows(_dot(a.astype(BF16), wd_b[...])))

    @pl.when(jnp.logical_not(live))
    def _():
        y_ref[...] = jnp.zeros_like(y_ref)


def _experts(blk_e, nblk, first, wslot, next_e, xs, wg, wu, wd):
    blk_rows = MOE_BLK * ROW_TILES
    nblocks = xs.shape[0] // blk_rows
    row_map = lambda i, be, nb, *_: (jnp.minimum(i, nb[0] - 1), 0)
    any_spec = pl.BlockSpec(memory_space=pl.ANY)
    return pl.pallas_call(
        _expert_kernel,
        grid_spec=pltpu.PrefetchScalarGridSpec(
            num_scalar_prefetch=5,
            grid=(nblocks,),
            in_specs=[pl.BlockSpec((blk_rows, LANES), row_map), any_spec, any_spec, any_spec],
            out_specs=pl.BlockSpec((blk_rows, LANES), lambda i, *_: (i, 0)),
            scratch_shapes=[pltpu.VMEM((2, D_MODEL, D_EXPERT), F32),
                            pltpu.VMEM((2, D_MODEL, D_EXPERT), F32),
                            pltpu.VMEM((2, D_EXPERT, D_MODEL), F32),
                            pltpu.VMEM((D_MODEL, D_EXPERT), BF16),
                            pltpu.VMEM((D_MODEL, D_EXPERT), BF16),
                            pltpu.VMEM((D_EXPERT, D_MODEL), BF16),
                            pltpu.SemaphoreType.DMA((2, 3))],
        ),
        out_shape=jax.ShapeDtypeStruct(xs.shape, I32),
        compiler_params=pltpu.CompilerParams(
            dimension_semantics=("arbitrary",), vmem_limit_bytes=VMEM_LIMIT),
        name="experts",
    )(blk_e, nblk, first, wslot, next_e, xs, wg, wu, wd)


def _combine_kernel(d1_ref, d2_ref, d1n_ref, d2n_ref, y_hbm, x1_ref, route_ref, gate_ref, gf_ref,
                    o_ref, a0, b0, a1, b1, sem, *, ntiles):
    t = pl.program_id(0)
    bufs = ((a0, b0), (a1, b1))

    def gather(i1_ref, i2_ref, par):
        buf_a, buf_b = bufs[par]

        def row(r, u):
            pltpu.make_async_copy(_tile(y_hbm, i1_ref[r]), _tile(buf_a, r),
                                  sem.at[par]).start(priority=u % 2)
            pltpu.make_async_copy(_tile(y_hbm, i2_ref[r]), _tile(buf_b, r),
                                  sem.at[par]).start(priority=(u + 1) % 2)
        _issue_rows(TM, row)

    @pl.when(t == 0)
    def _():
        gather(d1_ref, d2_ref, 0)

    def step(par):
        buf_a, buf_b = bufs[par]
        for buf in (buf_a, buf_b):
            pltpu.make_async_copy(y_hbm.at[pl.ds(0, TM * ROW_TILES)], buf, sem.at[par]).wait()

        @pl.when(t + 1 < ntiles)
        def _():
            gather(d1n_ref, d2n_ref, 1 - par)

        route = route_ref[0]
        w1 = route[:, 4:5]
        w2 = route[:, 5:6]
        a_lo, a_hi = _unpack_rows(_load_tiles_as_rows(buf_a, TM))
        b_lo, b_hi = _unpack_rows(_load_tiles_as_rows(buf_b, TM))
        ffn = jnp.concatenate([w1 * a_lo + w2 * b_lo, w1 * a_hi + w2 * b_hi], axis=1)
        x2 = x1_ref[0] + gate_ref[0] * ffn
        o_ref[0] = _rms(x2, gf_ref[...])

    for par in range(2):
        pl.when(t % 2 == par)(functools.partial(step, par))


def _combine(d1, d2, y, x1, route, gate, gf, tok_base):
    nb, seq, _ = x1.shape
    nt = seq // TM
    ntiles = nb * nt
    blk0 = tok_base // TM
    smem = lambda fn: pl.BlockSpec((TM,), fn, memory_space=pltpu.SMEM)
    cur = lambda t: (blk0 + t,)
    nxt = lambda t: (blk0 + jnp.minimum(t + 1, ntiles - 1),)
    tile = lambda t: (t // nt, t % nt, 0)
    grows = gate.shape[1]
    gate_spec = (pl.BlockSpec((1, 1, D_MODEL), lambda t: (t // nt, 0, 0)) if grows == 1
                 else pl.BlockSpec((1, TM, D_MODEL), tile))
    return pl.pallas_call(
        functools.partial(_combine_kernel, ntiles=ntiles),
        grid=(ntiles,),
        in_specs=[smem(cur), smem(cur), smem(nxt), smem(nxt), pl.BlockSpec(memory_space=pl.ANY),
                  pl.BlockSpec((1, TM, D_MODEL), tile),
                  pl.BlockSpec((1, TM, RLANES), tile),
                  gate_spec,
                  pl.BlockSpec((1, D_MODEL), lambda t: (0, 0))],
        out_specs=pl.BlockSpec((1, TM, D_MODEL), tile),
        out_shape=jax.ShapeDtypeStruct(x1.shape, F32),
        scratch_shapes=[pltpu.VMEM((TM * ROW_TILES, LANES), I32)] * 4 + [
            pltpu.SemaphoreType.DMA((2,))],
        compiler_params=pltpu.CompilerParams(
            dimension_semantics=("arbitrary",), vmem_limit_bytes=VMEM_LIMIT),
        name="combine",
    )(d1, d2, d1, d2, y, x1, route, gate, gf)


def _band_bias(rel_bias, rows, keys):
    n = rows - 1 + keys
    dist = WINDOW + rows - 1 - np.arange(n + 1)
    flipped = rel_bias[:, np.clip(dist, -MAX_REL, MAX_REL) + MAX_REL]
    skew = jnp.tile(flipped, (1, rows))[:, :rows * n].reshape(N_HEADS, rows, n)
    b = skew[:, :, rows - 1:rows - 1 + keys]
    return b.reshape(2, 4 * rows, keys)


def kernel(x_prompt, x_sample, cache_attn_k, cache_attn_v, state_conv, c_prompt, c_sample,
           w_ada, b_ada, norm1_g, norm2_g, w_in, rel_bias, conv_w, conv_b, w_pa, w_pb, w_o,
           w_group, b_group, w_expert, b_expert, w_e_gate, w_e_up, w_e_down, final_g):
    assert w_ada.shape[0] == 1, "single trunk layer"
    nb, seq, _ = x_prompt.shape
    nseq, slen, _ = x_sample.shape
    ntok_p = nb * seq
    ntok_s = nseq * slen
    ntok = ntok_p + ntok_s
    assert seq % TL == 0 and WINDOW % TL == 0 and seq % TM == 0 and ntok_s % TM == 0
    assert slen >= 2 and slen & (slen - 1) == 0 and slen % 16 == 0 and nseq % SAMPLE_SEQS == 0
    assert MOE_BLK % ZERO_ROWS == 0

    n_c = nb + nseq
    n_pad = -(-n_c // 8) * 8
    c_all = jnp.concatenate([c_prompt, c_sample, jnp.zeros((n_pad - n_c, D_MODEL), F32)], axis=0)
    mod = _ada(c_all, w_ada[0], b_ada[0]).reshape(n_pad, 6, D_MODEL)
    mod_p = mod[:nb]
    mod_s = mod[nb:n_c]

    win = w_in[0].astype(BF16)
    wpa = w_pa[0].astype(BF16)
    wpb = w_pb[0].astype(BF16)
    wo = w_o[0].astype(BF16)
    g1 = norm1_g[0].reshape(1, D_MODEL)
    g2 = norm2_g[0].reshape(1, D_MODEL)
    gf = final_g.reshape(1, D_MODEL)
    cw = jnp.concatenate([conv_w[0], jnp.zeros((8 - conv_w.shape[1], D_CONV), F32)], axis=0)
    cbias = conv_b[0].reshape(1, D_CONV)
    wr = jnp.concatenate([w_expert[0], w_group[0],
                          jnp.zeros((D_MODEL, RLANES - N_EXPERTS - N_GROUPS), F32)], axis=1)
    whi = wr.astype(BF16)
    br = jnp.concatenate([b_expert[0], b_group[0],
                          jnp.zeros((RLANES - N_EXPERTS - N_GROUPS,), F32)]).reshape(1, RLANES)
    bias_p = _band_bias(rel_bias[0] * LOG2E, CHUNK, BAND)
    bias_s = _band_bias(rel_bias[0] * LOG2E, slen, WINDOW + slen).reshape(
        N_HEADS * slen, WINDOW + slen)
    bias_old = bias_s[:, :WINDOW]
    bias_new = bias_s[:, WINDOW:]

    x1p, h2p, kp, vp, up8, route_p, cnt_p = _prompt_main(
        x_prompt, mod_p, g1, g2, win, bias_p, cw, cbias, wpa, wpb, wo, whi, br)

    st = state_conv[0]
    up1 = jnp.zeros((nseq, slen, D_CONV), F32).at[:, 0].set(st[:, 1]).reshape(ntok_s, D_CONV)
    up2 = (jnp.zeros((nseq, slen, D_CONV), F32).at[:, 0].set(st[:, 0]).at[:, 1].set(st[:, 1])
           .reshape(ntok_s, D_CONV))
    ck = jnp.transpose(cache_attn_k[0], (0, 2, 3, 1)).reshape(nseq, D_ATT, WINDOW)
    cv = jnp.transpose(cache_attn_v[0], (0, 2, 3, 1)).reshape(nseq, D_ATT, WINDOW)
    x1s, h2s, ks, vs, us, route_s, cnt = _sample_main(
        x_sample.reshape(ntok_s, D_MODEL), mod_s, ck, cv, up1, up2, cnt_p,
        g1, g2, win, bias_old, bias_new, cw, cbias, wpa, wpb, wo, whi, br, nseq, slen)

    route_all = jnp.concatenate([route_p.reshape(ntok_p, RLANES)[:, :4], route_s[:, :4]], axis=0)
    experts = route_all[:, 0:2].astype(jnp.int32)
    ranks = route_all[:, 2:4].astype(jnp.int32)
    counts = cnt[0, :N_EXPERTS].astype(jnp.int32)
    pcounts = (counts + MOE_BLK - 1) // MOE_BLK * MOE_BLK
    pend = jnp.cumsum(pcounts)
    pstart = pend - pcounts
    eids = jnp.arange(N_EXPERTS, dtype=jnp.int32)
    dest = jnp.sum(jnp.where(experts[..., None] == eids, pstart, 0), axis=-1) + ranks
    d1 = dest[:, 0]
    d2 = dest[:, 1]
    nblocks = (2 * ntok) // MOE_BLK + N_EXPERTS
    blk_start = jnp.arange(nblocks, dtype=jnp.int32) * MOE_BLK
    blk_e = jnp.minimum(jnp.sum((pend[None, :] <= blk_start[:, None]).astype(jnp.int32), axis=1),
                        N_EXPERTS - 1)
    nblk = (pend[-1:] // MOE_BLK).astype(jnp.int32)
    pend0 = jnp.concatenate([jnp.zeros((1,), jnp.int32), pend.astype(jnp.int32)])

    xs = _dispatch(pend0, counts, d1, d2, h2p, h2s, nblocks * MOE_BLK)
    blk_id = jnp.arange(nblocks, dtype=jnp.int32)
    first = (blk_id < nblk[0]) & ((blk_id == 0) | (blk_e != jnp.roll(blk_e, 1)))
    wslot = (jnp.cumsum(first.astype(jnp.int32)) - 1) % 2
    later_first = lax.cummin(jnp.where(first, blk_id, nblocks)[::-1])[::-1]
    next_first = jnp.concatenate([later_first[1:], jnp.full((1,), nblocks, jnp.int32)])
    next_e = jnp.where(next_first < nblocks, blk_e[jnp.minimum(next_first, nblocks - 1)], -1)
    y = _experts(blk_e, nblk, first.astype(jnp.int32), wslot.astype(jnp.int32),
                 next_e.astype(jnp.int32), xs, w_e_gate[0], w_e_up[0], w_e_down[0])

    y_prompt = _combine(d1, d2, y, x1p, route_p, mod_p[:, 5:6, :], gf, 0)
    gate_s = jnp.repeat(mod_s[:, 5, :], slen, axis=0).reshape(1, ntok_s, D_MODEL)
    y_sample = _combine(d1, d2, y, x1s.reshape(1, ntok_s, D_MODEL),
                        route_s.reshape(1, ntok_s, RLANES), gate_s, gf, ntok_p)

    new_k_p = kp.reshape(1, nb, WINDOW, N_HEADS, HEAD_DIM)
    new_v_p = vp.reshape(1, nb, WINDOW, N_HEADS, HEAD_DIM)
    new_conv_p = up8[:, 6:8, :].reshape(1, nb, 2, D_CONV)
    new_k_s = ks.reshape(1, nseq, slen, N_HEADS, HEAD_DIM)
    new_v_s = vs.reshape(1, nseq, slen, N_HEADS, HEAD_DIM)
    new_conv_s = us.reshape(nseq, slen, D_CONV)[:, slen - 2:, :].reshape(1, nseq, 2, D_CONV)
    return (y_prompt, y_sample.reshape(nseq, slen, D_MODEL), new_k_p, new_v_p, new_conv_p,
            new_k_s, new_v_s, new_conv_s)
```

```python
import functools

import numpy as np
import jax
import jax.numpy as jnp
from jax import lax
from jax.experimental import pallas as pl
from jax.experimental.pallas import tpu as pltpu

F32 = jnp.float32
BF16 = jnp.bfloat16
I32 = jnp.int32

D_MODEL = 1024
CHUNK = 64
LEFT = 8
WINDOW = LEFT * CHUNK
BAND = WINDOW + CHUNK
N_HEADS = 8
HEAD_DIM = 64
D_ATT = N_HEADS * HEAD_DIM
QUAD = 256
MAX_REL = 128
D_CONV = 512
N_GROUPS = 4
EPG = 8
N_EXPERTS = 32
D_EXPERT = 512
EPS = 1e-6
NEG = -1e30
LOG2E = float(np.log2(np.e))
Q_SCALE = HEAD_DIM ** -0.5 * LOG2E

TL = 512
RING = WINDOW + TL
MOE_BLK = 512
TM = 512
SAMPLE_SEQS = 2
ZERO_ROWS = 64
RLANES = 128
LANES = 128
HALF = D_MODEL // 2
ROW_TILES = HALF // LANES
ISSUE_UNROLL = 8
VMEM_LIMIT = 56 * 1024 * 1024


def _const_spec(shape):
    nd = len(shape)
    return pl.BlockSpec(shape, lambda *_: (0,) * nd, pipeline_mode=pl.Buffered(1))


def _dot(a, b):
    return jnp.dot(a, b, preferred_element_type=F32)


def _sigmoid(x):
    return 1.0 / (1.0 + jnp.exp(-x))


def _rms(x, g):
    ms = jnp.mean(x * x, axis=-1, keepdims=True)
    return x * lax.rsqrt(ms + EPS) * g


def _pack_rows(val):
    lo = lax.bitcast_convert_type(val[:, :HALF], I32) + 0x8000
    hi = lax.bitcast_convert_type(val[:, HALF:], I32) + 0x8000
    return (hi & -65536) | lax.shift_right_logical(lo, 16)


def _unpack_rows(packed):
    lo = lax.bitcast_convert_type(lax.shift_left(packed, 16), F32)
    hi = lax.bitcast_convert_type(packed & -65536, F32)
    return lo, hi


def _store_rows_as_tiles(ref, packed):
    r = packed.shape[0]
    for c in range(ROW_TILES):
        ref[pl.ds(c, r, stride=ROW_TILES), :] = packed[:, c * LANES:(c + 1) * LANES]


def _load_tiles_as_rows(ref, r):
    return jnp.concatenate(
        [ref[pl.ds(c, r, stride=ROW_TILES), :] for c in range(ROW_TILES)], axis=1)


def _tile(ref, row):
    return ref.at[pl.ds(pl.multiple_of(row * ROW_TILES, ROW_TILES), ROW_TILES)]


def _ada_kernel(c_ref, w_ref, b_ref, o_ref):
    c = c_ref[...]
    s = c * _sigmoid(c)
    o_ref[...] = _dot(s.astype(BF16), w_ref[...].astype(BF16)) + b_ref[...]


def _ada(c_all, w_ada, b_ada):
    n = c_all.shape[0]
    nb = 1024
    return pl.pallas_call(
        _ada_kernel,
        grid=(6 * D_MODEL // nb,),
        in_specs=[pl.BlockSpec((n, D_MODEL), lambda i: (0, 0)),
                  pl.BlockSpec((D_MODEL, nb), lambda i: (0, i)),
                  pl.BlockSpec((1, nb), lambda i: (0, i))],
        out_specs=pl.BlockSpec((n, nb), lambda i: (0, i)),
        out_shape=jax.ShapeDtypeStruct((n, 6 * D_MODEL), F32),
        name="ada",
    )(c_all, w_ada, b_ada.reshape(1, -1))


def _attend(q, kb, vb, bias, lim):
    r = q.shape[0]
    nk = kb.shape[0]
    assert r & (r - 1) == 0
    qt = jnp.concatenate([q] * 4, axis=0)
    rowh = lax.broadcasted_iota(jnp.int32, (4 * r, QUAD), 0) >> (r.bit_length() - 1)
    laneh = lax.broadcasted_iota(jnp.int32, (4 * r, QUAD), 1) >> 6
    qm = jnp.where(rowh == laneh, qt, jnp.zeros_like(qt))
    s = lax.dot_general(qm, kb, (((1,), (1,)), ((), ())), preferred_element_type=F32)
    valid = None
    if lim is not None:
        valid = lax.broadcasted_iota(jnp.int32, (r, nk), 1) >= lim
    ps, inv_l = [], []
    for h in range(4):
        sh = s[h * r:(h + 1) * r] + bias[h * r:(h + 1) * r]
        if valid is not None:
            sh = jnp.where(valid, sh, NEG)
        m = jnp.max(sh, axis=1, keepdims=True)
        ph = jnp.exp2(sh - m)
        inv_l.append(1.0 / jnp.sum(ph, axis=1, keepdims=True))
        ps.append(ph.astype(BF16))
    o = _dot(jnp.concatenate(ps, axis=0), vb)
    lane_o = lax.broadcasted_iota(jnp.int32, (r, QUAD), 1) >> 6
    out = o[0:r] * inv_l[0]
    for h in range(1, 4):
        out = jnp.where(lane_o == h, o[h * r:(h + 1) * r] * inv_l[h], out)
    return out


def _route_stages(h2, whi_ref, br_ref, cnt, valid=None):
    r = h2.shape[0]
    lane = lax.broadcasted_iota(jnp.int32, (r, RLANES), 1)
    lane_f = lane.astype(F32)
    big = jnp.float32(1000.0)
    v = {}

    def logits():
        v["logits"] = _dot(h2.astype(BF16), whi_ref[...]) + br_ref[...]

    def group():
        lg = jnp.where((lane >= N_EXPERTS) & (lane < N_EXPERTS + N_GROUPS), v["logits"], NEG)
        mg = jnp.max(lg, axis=1, keepdims=True)
        v["gi"] = jnp.min(jnp.where(lg == mg, lane_f, big), axis=1, keepdims=True) - N_EXPERTS
        v["pg"] = 1.0 / jnp.sum(jnp.exp(lg - mg), axis=1, keepdims=True)

    def top1():
        grp_of_lane = (lane >> 3).astype(F32)
        le = jnp.where((lane < N_EXPERTS) & (grp_of_lane == v["gi"]), v["logits"], NEG)
        v["m1"] = jnp.max(le, axis=1, keepdims=True)
        v["i1"] = jnp.min(jnp.where(le == v["m1"], lane_f, big), axis=1, keepdims=True)
        v["le"] = le

    def top2():
        sel1 = lane_f == v["i1"]
        le2 = jnp.where(sel1, NEG, v["le"])
        m2 = jnp.max(le2, axis=1, keepdims=True)
        v["i2"] = jnp.min(jnp.where(le2 == m2, lane_f, big), axis=1, keepdims=True)
        rr = jnp.exp(m2 - v["m1"])
        inv = v["pg"] / (1.0 + rr)
        v["w1"] = inv
        v["w2"] = inv * rr
        v["sel1"] = sel1
        v["sel2"] = lane_f == v["i2"]

    def ranks():
        oh = jnp.where(v["sel1"] | v["sel2"], 1.0 if valid is None else valid, 0.0).astype(F32)
        ri = lax.broadcasted_iota(jnp.int32, (r, r), 0)
        ci = lax.broadcasted_iota(jnp.int32, (r, r), 1)
        tri = jnp.where(ri > ci, 1.0, 0.0).astype(BF16)
        v["before"] = _dot(tri, oh.astype(BF16)) + cnt
        v["new_cnt"] = cnt + jnp.sum(oh, axis=0, keepdims=True)

    def assemble():
        r1 = jnp.sum(jnp.where(v["sel1"], v["before"], 0.0), axis=1, keepdims=True)
        r2 = jnp.sum(jnp.where(v["sel2"], v["before"], 0.0), axis=1, keepdims=True)
        route = jnp.where(lane == 0, v["i1"], 0.0)
        route = jnp.where(lane == 1, v["i2"], route)
        route = jnp.where(lane == 2, r1, route)
        route = jnp.where(lane == 3, r2, route)
        route = jnp.where(lane == 4, v["w1"], route)
        route = jnp.where(lane == 5, v["w2"], route)
        return route, v["new_cnt"]

    return [logits, group, top1, top2, ranks, assemble]


def _route(h2, whi_ref, br_ref, cnt):
    stages = _route_stages(h2, whi_ref, br_ref, cnt)
    for stage in stages[:-1]:
        stage()
    return stages[-1]()


def _prompt_kernel(x_ref, mod_ref, g1_ref, g2_ref, win_ref, bias_ref, cw_ref, cbias_ref,
                   wpa_ref, wpb_ref, wo_ref, whi_ref, br_ref,
                   x1_ref, h2_ref, ko_ref, vo_ref, uo_ref, route_ref, cnt_ref,
                   kring, vring, att_s, ucarry, cnt_s, h2_prev, *, nt, ntiles):
    g = pl.program_id(0)

    @pl.when(g == 0)
    def _():
        cnt_s[...] = jnp.zeros_like(cnt_s)
        h2_prev[...] = jnp.zeros_like(h2_prev)

    @pl.when(g < ntiles)
    def _():
        _prompt_tile(g, nt, x_ref, mod_ref, g1_ref, g2_ref, win_ref, bias_ref, cw_ref, cbias_ref,
                     wpa_ref, wpb_ref, wo_ref, whi_ref, br_ref,
                     x1_ref, h2_ref, ko_ref, vo_ref, uo_ref, route_ref, cnt_ref,
                     kring, vring, att_s, ucarry, cnt_s, h2_prev)

    @pl.when(g == ntiles)
    def _():
        route, new_cnt = _route(h2_prev[...], whi_ref, br_ref, cnt_s[...])
        route_ref[0] = route
        cnt_ref[...] = new_cnt


def _prompt_tile(g, nt, x_ref, mod_ref, g1_ref, g2_ref, win_ref, bias_ref, cw_ref, cbias_ref,
                 wpa_ref, wpb_ref, wo_ref, whi_ref, br_ref,
                 x1_ref, h2_ref, ko_ref, vo_ref, uo_ref, route_ref, cnt_ref,
                 kring, vring, att_s, ucarry, cnt_s, h2_prev):
    j = g % nt

    @pl.when(j == 0)
    def _():
        kring[0:WINDOW, :] = jnp.zeros((WINDOW, D_ATT), BF16)
        vring[0:WINDOW, :] = jnp.zeros((WINDOW, D_ATT), BF16)
        ucarry[...] = jnp.zeros_like(ucarry)

    route_stages = _route_stages(h2_prev[...], whi_ref, br_ref, cnt_s[...],
                                 valid=jnp.where(g > 0, 1.0, 0.0).astype(F32))
    route_stages[0]()

    sh1 = mod_ref[0, 0:1, :]
    sc1 = mod_ref[0, 1:2, :]
    gt1 = mod_ref[0, 2:3, :]
    sh2 = mod_ref[0, 3:4, :]
    sc2 = mod_ref[0, 4:5, :]

    x = x_ref[0]
    h = _rms(x, g1_ref[...]) * (1.0 + sc1) + sh1
    hb = h.astype(BF16)

    qkv = _dot(hb, win_ref[:, 0:3 * D_ATT])
    q = (qkv[:, 0:D_ATT] * Q_SCALE).astype(BF16)
    k = qkv[:, D_ATT:2 * D_ATT]
    v = qkv[:, 2 * D_ATT:3 * D_ATT]
    ko_ref[0] = k
    vo_ref[0] = v
    kring[WINDOW:RING, :] = k.astype(BF16)
    vring[WINDOW:RING, :] = v.astype(BF16)

    base = j * TL
    for c in range(TL // CHUNK):
        lim = WINDOW - (base + c * CHUNK)
        for qd in range(2):
            ls = slice(qd * QUAD, (qd + 1) * QUAD)
            o = _attend(q[c * CHUNK:(c + 1) * CHUNK, ls],
                        kring[c * CHUNK:c * CHUNK + BAND, ls],
                        vring[c * CHUNK:c * CHUNK + BAND, ls],
                        bias_ref[qd], lim)
            att_s[c * CHUNK:(c + 1) * CHUNK, ls] = o.astype(BF16)
        if c + 1 < len(route_stages) - 1:
            route_stages[c + 1]()
    route, new_cnt = route_stages[-1]()
    route_ref[0] = route
    cnt_s[...] = new_cnt
    cnt_ref[...] = new_cnt

    kring[0:WINDOW, :] = kring[TL:RING, :]
    vring[0:WINDOW, :] = vring[TL:RING, :]

    cbcv = _dot(hb, win_ref[:, 3 * D_ATT:3 * D_ATT + 3 * D_CONV])
    gates = _dot(hb, win_ref[:, 3 * D_ATT + 3 * D_CONV:])
    cb = cbcv[:, 0:D_CONV]
    u = cbcv[:, D_CONV:2 * D_CONV] * cbcv[:, 2 * D_CONV:3 * D_CONV]
    row = lax.broadcasted_iota(jnp.int32, (8, D_CONV), 0)
    prev = ucarry[...]
    r1 = pltpu.roll(u, 1, axis=0)
    r2 = pltpu.roll(u, 2, axis=0)
    u_m1 = jnp.concatenate(
        [jnp.where(row < 1, pltpu.roll(prev, 1, axis=0), r1[0:8]), r1[8:]], axis=0)
    u_m2 = jnp.concatenate(
        [jnp.where(row < 2, pltpu.roll(prev, 2, axis=0), r2[0:8]), r2[8:]], axis=0)
    yc = cw_ref[0:1, :] * u_m2 + cw_ref[1:2, :] * u_m1 + cw_ref[2:3, :] * u + cbias_ref[...]
    conv_out = (cb * yc).astype(BF16)
    ucarry[...] = u[TL - 8:TL, :]
    uo_ref[0] = u[TL - 8:TL, :]

    pa = _dot(att_s[...], wpa_ref[...])
    pb = _dot(conv_out, wpb_ref[...])
    mixin = _sigmoid(gates[:, 0:D_MODEL]) * pa + _sigmoid(gates[:, D_MODEL:]) * pb
    mix = _dot(mixin.astype(BF16), wo_ref[...])
    x1 = x + gt1 * mix
    x1_ref[0] = x1
    h2 = _rms(x1, g2_ref[...]) * (1.0 + sc2) + sh2
    _store_rows_as_tiles(h2_ref, _pack_rows(h2))

    h2_prev[...] = h2


def _prompt_main(x, mod, g1, g2, win, bias_q, cw, cbias, wpa, wpb, wo, whi, br):
    nb, seq, _ = x.shape
    nt = seq // TL
    ntiles = nb * nt
    keep = WINDOW // TL
    cur = lambda g: jnp.minimum(g, ntiles - 1)
    prv = lambda g: jnp.maximum(g - 1, 0)
    tile = lambda g: (cur(g) // nt, cur(g) % nt, 0)
    last = lambda g: (cur(g) // nt, jnp.maximum(cur(g) % nt - (nt - keep), 0), 0)
    perb = lambda g: (cur(g) // nt, 0, 0)
    in_specs = [
        pl.BlockSpec((1, TL, D_MODEL), tile),
        pl.BlockSpec((1, 6, D_MODEL), perb),
        _const_spec(g1.shape), _const_spec(g2.shape), _const_spec(win.shape),
        _const_spec(bias_q.shape), _const_spec(cw.shape), _const_spec(cbias.shape),
        _const_spec(wpa.shape), _const_spec(wpb.shape), _const_spec(wo.shape),
        _const_spec(whi.shape), _const_spec(br.shape),
    ]
    out_specs = [
        pl.BlockSpec((1, TL, D_MODEL), tile),
        pl.BlockSpec((TL * ROW_TILES, LANES), lambda g: (cur(g), 0)),
        pl.BlockSpec((1, TL, D_ATT), last),
        pl.BlockSpec((1, TL, D_ATT), last),
        pl.BlockSpec((1, 8, D_CONV), perb),
        pl.BlockSpec((1, TL, RLANES), lambda g: (prv(g) // nt, prv(g) % nt, 0)),
        pl.BlockSpec((1, RLANES), lambda g: (0, 0)),
    ]
    out_shape = [
        jax.ShapeDtypeStruct((nb, seq, D_MODEL), F32),
        jax.ShapeDtypeStruct((nb * seq * ROW_TILES, LANES), I32),
        jax.ShapeDtypeStruct((nb, WINDOW, D_ATT), F32),
        jax.ShapeDtypeStruct((nb, WINDOW, D_ATT), F32),
        jax.ShapeDtypeStruct((nb, 8, D_CONV), F32),
        jax.ShapeDtypeStruct((nb, seq, RLANES), F32),
        jax.ShapeDtypeStruct((1, RLANES), F32),
    ]
    scratch = [
        pltpu.VMEM((RING, D_ATT), BF16), pltpu.VMEM((RING, D_ATT), BF16),
        pltpu.VMEM((TL, D_ATT), BF16), pltpu.VMEM((8, D_CONV), F32),
        pltpu.VMEM((1, RLANES), F32), pltpu.VMEM((TL, D_MODEL), F32),
    ]
    return pl.pallas_call(
        functools.partial(_prompt_kernel, nt=nt, ntiles=ntiles),
        grid=(ntiles + 1,),
        in_specs=in_specs, out_specs=out_specs, out_shape=out_shape,
        scratch_shapes=scratch,
        compiler_params=pltpu.CompilerParams(
            dimension_semantics=("arbitrary",), vmem_limit_bytes=VMEM_LIMIT),
        name="prompt_main",
    )(x, mod, g1, g2, win, bias_q, cw, cbias, wpa, wpb, wo, whi, br)


def _sample_kernel(x_ref, mod_ref, ck_ref, cv_ref, up1_ref, up2_ref, cnt_in_ref,
                   g1_ref, g2_ref, win_ref, bias_old_ref, bias_new_ref, cw_ref, cbias_ref,
                   wpa_ref, wpb_ref, wo_ref, whi_ref, br_ref,
                   x1_ref, h2_ref, ko_ref, vo_ref, uo_ref, route_ref, cnt_ref,
                   h_s, q_s, kn_s, vn_s, att_s, conv_s, h2_s, *, nseq, slen):
    n = pl.program_id(0)
    ntok = nseq * slen

    @pl.when(n == 0)
    def _():
        def norm_body(i, carry):
            rows = pl.ds(pl.multiple_of(i * slen, slen), slen)
            xi = x_ref[rows, :]
            m = mod_ref[i]
            hi = _rms(xi, g1_ref[...]) * (1.0 + m[1:2, :]) + m[0:1, :]
            h_s[rows, :] = hi.astype(BF16)
            return carry
        lax.fori_loop(0, nseq, norm_body, 0)
        hb = h_s[...]
        qkv = _dot(hb, win_ref[:, 0:3 * D_ATT])
        q_s[...] = (qkv[:, 0:D_ATT] * Q_SCALE).astype(BF16)
        k = qkv[:, D_ATT:2 * D_ATT]
        v = qkv[:, 2 * D_ATT:3 * D_ATT]
        ko_ref[...] = k
        vo_ref[...] = v
        kn_s[...] = k.astype(BF16)
        vn_s[...] = v.astype(BF16)

        cbcv = _dot(hb, win_ref[:, 3 * D_ATT:3 * D_ATT + 3 * D_CONV])
        cb = cbcv[:, 0:D_CONV]
        u = cbcv[:, D_CONV:2 * D_CONV] * cbcv[:, 2 * D_CONV:3 * D_CONV]
        pos = lax.broadcasted_iota(jnp.int32, (ntok, D_CONV), 0) & (slen - 1)
        u_m1 = jnp.where(pos < 1, up1_ref[...], pltpu.roll(u, 1, axis=0))
        u_m2 = jnp.where(pos < 2, up2_ref[...], pltpu.roll(u, 2, axis=0))
        yc = cw_ref[0:1, :] * u_m2 + cw_ref[1:2, :] * u_m1 + cw_ref[2:3, :] * u + cbias_ref[...]
        conv_s[...] = (cb * yc).astype(BF16)
        uo_ref[...] = u

    nt_dims = (((1,), (1,)), ((), ()))
    shape = (N_HEADS * slen, D_ATT)
    rowh = lax.broadcasted_iota(jnp.int32, shape, 0) >> (slen.bit_length() - 1)
    laneh = lax.broadcasted_iota(jnp.int32, shape, 1) >> 6
    lane_o = lax.broadcasted_iota(jnp.int32, (slen, D_ATT), 1) >> 6
    for sq in range(SAMPLE_SEQS):
        rows = pl.ds(pl.multiple_of((n * SAMPLE_SEQS + sq) * slen, slen), slen)
        qt = jnp.concatenate([q_s[rows, :]] * N_HEADS, axis=0)
        qm = jnp.where(rowh == laneh, qt, jnp.zeros_like(qt))
        s_old = _dot(qm, ck_ref[sq].astype(BF16)) + bias_old_ref[...]
        s_new = lax.dot_general(qm, kn_s[rows, :], nt_dims,
                                preferred_element_type=F32) + bias_new_ref[...]
        m = jnp.maximum(jnp.max(s_old, axis=1, keepdims=True),
                        jnp.max(s_new, axis=1, keepdims=True))
        p_old = jnp.exp2(s_old - m)
        p_new = jnp.exp2(s_new - m)
        l = jnp.sum(p_old, axis=1, keepdims=True) + jnp.sum(p_new, axis=1, keepdims=True)
        o = lax.dot_general(p_old.astype(BF16), cv_ref[sq].astype(BF16), nt_dims,
                            preferred_element_type=F32)
        o = (o + _dot(p_new.astype(BF16), vn_s[rows, :])) * (1.0 / l)
        att = o[0:slen]
        for h in range(1, N_HEADS):
            att = jnp.where(lane_o == h, o[h * slen:(h + 1) * slen], att)
        att_s[rows, :] = att.astype(BF16)

    @pl.when(n == nseq // SAMPLE_SEQS - 1)
    def _():
        gates = _dot(h_s[...], win_ref[:, 3 * D_ATT + 3 * D_CONV:])
        pa = _dot(att_s[...], wpa_ref[...])
        pb = _dot(conv_s[...], wpb_ref[...])
        mixin = _sigmoid(gates[:, 0:D_MODEL]) * pa + _sigmoid(gates[:, D_MODEL:]) * pb
        x1_ref[...] = _dot(mixin.astype(BF16), wo_ref[...])

        def res_body(i, carry):
            r = pl.ds(pl.multiple_of(i * slen, slen), slen)
            m = mod_ref[i]
            x1 = x_ref[r, :] + m[2:3, :] * x1_ref[r, :]
            x1_ref[r, :] = x1
            h2_s[r, :] = _rms(x1, g2_ref[...]) * (1.0 + m[4:5, :]) + m[3:4, :]
            return carry
        lax.fori_loop(0, nseq, res_body, 0)

        h2 = h2_s[...]
        _store_rows_as_tiles(h2_ref, _pack_rows(h2))
        route, new_cnt = _route(h2, whi_ref, br_ref, cnt_in_ref[...])
        route_ref[...] = route
        cnt_ref[...] = new_cnt


def _sample_main(x2d, mod, ck, cv, up1, up2, cnt_in, g1, g2, win, bias_old, bias_new, cw, cbias,
                 wpa, wpb, wo, whi, br, nseq, slen):
    ntok = nseq * slen
    args = (x2d, mod, ck, cv, up1, up2, cnt_in, g1, g2, win, bias_old, bias_new, cw, cbias,
            wpa, wpb, wo, whi, br)
    in_specs = []
    for idx, a in enumerate(args):
        if idx in (2, 3):
            in_specs.append(pl.BlockSpec((SAMPLE_SEQS,) + a.shape[1:], lambda n: (n, 0, 0)))
        else:
            in_specs.append(_const_spec(a.shape))
    whole = lambda shape: pl.BlockSpec(shape, lambda n: (0,) * len(shape))
    outs = [((ntok, D_MODEL), F32), ((ntok * ROW_TILES, LANES), I32), ((ntok, D_ATT), F32),
            ((ntok, D_ATT), F32), ((ntok, D_CONV), F32), ((ntok, RLANES), F32), ((1, RLANES), F32)]
    scratch = [
        pltpu.VMEM((ntok, D_MODEL), BF16), pltpu.VMEM((ntok, D_ATT), BF16),
        pltpu.VMEM((ntok, D_ATT), BF16), pltpu.VMEM((ntok, D_ATT), BF16),
        pltpu.VMEM((ntok, D_ATT), BF16), pltpu.VMEM((ntok, D_CONV), BF16),
        pltpu.VMEM((ntok, D_MODEL), F32),
    ]
    return pl.pallas_call(
        functools.partial(_sample_kernel, nseq=nseq, slen=slen),
        grid=(nseq // SAMPLE_SEQS,),
        in_specs=in_specs,
        out_specs=[whole(s) for s, _ in outs],
        out_shape=[jax.ShapeDtypeStruct(s, d) for s, d in outs],
        scratch_shapes=scratch,
        compiler_params=pltpu.CompilerParams(
            dimension_semantics=("arbitrary",), vmem_limit_bytes=VMEM_LIMIT),
        name="sample_main",
    )(*args)


def _issue_rows(n, body):
    def group(g, carry):
        for u in range(ISSUE_UNROLL):
            body(g * ISSUE_UNROLL + u, u)
        return carry
    lax.fori_loop(0, n // ISSUE_UNROLL, group, 0)


def _dispatch_kernel(pend_ref, cnt_ref, d1_ref, d2_ref, hp_ref, hs_ref, xs_out, zbuf, sem, zsem, *,
                     np_tiles, nslots):
    i = pl.program_id(0)
    zrows = ZERO_ROWS * ROW_TILES

    @pl.when(i == 0)
    def _():
        zbuf[...] = jnp.zeros_like(zbuf)

        def zcopy(piece):
            start = pl.multiple_of(piece * zrows, zrows)
            return pltpu.make_async_copy(zbuf, xs_out.at[pl.ds(start, zrows)], zsem)

        def pieces(e):
            lo = (pend_ref[e] + cnt_ref[jnp.minimum(e, N_EXPERTS - 1)]) // ZERO_ROWS
            hi = pend_ref[jnp.minimum(e + 1, N_EXPERTS)] // ZERO_ROWS
            lo = jnp.where(e == N_EXPERTS, pend_ref[N_EXPERTS] // ZERO_ROWS, lo)
            hi = jnp.where(e == N_EXPERTS, nslots // ZERO_ROWS, hi)
            return lo, hi

        def start_all(e, carry):
            lo, hi = pieces(e)
            return lax.fori_loop(lo, hi, lambda p, c: (zcopy(p).start(), c)[1], carry)

        def wait_all(e, carry):
            lo, hi = pieces(e)
            return lax.fori_loop(lo, hi, lambda p, c: (zcopy(p).wait(), c)[1], carry)
        lax.fori_loop(0, N_EXPERTS + 1, start_all, 0)
        lax.fori_loop(0, N_EXPERTS + 1, wait_all, 0)

    def scatter_tile(src):
        def row(r, u):
            pltpu.make_async_copy(_tile(src, r), _tile(xs_out, d1_ref[r]), sem).start(priority=u % 2)
            pltpu.make_async_copy(_tile(src, r), _tile(xs_out, d2_ref[r]), sem).start(
                priority=(u + 1) % 2)
        _issue_rows(TM, row)
        for _ in range(2):
            pltpu.make_async_copy(src, xs_out.at[pl.ds(0, TM * ROW_TILES)], sem).wait()

    @pl.when(i < np_tiles)
    def _():
        scatter_tile(hp_ref)

    @pl.when(i >= np_tiles)
    def _():
        scatter_tile(hs_ref)


def _dispatch(pend, counts, d1, d2, h2p, h2s, nslots):
    np_tiles = h2p.shape[0] // (TM * ROW_TILES)
    ns_tiles = h2s.shape[0] // (TM * ROW_TILES)
    smem_tile = pl.BlockSpec((TM,), lambda i, *_: (i,), memory_space=pltpu.SMEM)
    rows = TM * ROW_TILES
    return pl.pallas_call(
        functools.partial(_dispatch_kernel, np_tiles=np_tiles, nslots=nslots),
        grid_spec=pltpu.PrefetchScalarGridSpec(
            num_scalar_prefetch=2,
            grid=(np_tiles + ns_tiles,),
            in_specs=[smem_tile, smem_tile,
                      pl.BlockSpec((rows, LANES), lambda i, *_: (jnp.minimum(i, np_tiles - 1), 0)),
                      pl.BlockSpec((rows, LANES), lambda i, *_: (jnp.maximum(i - np_tiles, 0), 0))],
            out_specs=pl.BlockSpec(memory_space=pl.ANY),
            scratch_shapes=[pltpu.VMEM((ZERO_ROWS * ROW_TILES, LANES), I32),
                            pltpu.SemaphoreType.DMA(()), pltpu.SemaphoreType.DMA(())],
        ),
        out_shape=jax.ShapeDtypeStruct((nslots * ROW_TILES, LANES), I32),
        compiler_params=pltpu.CompilerParams(dimension_semantics=("arbitrary",)),
        name="dispatch",
    )(pend, counts, d1, d2, h2p, h2s)


def _expert_kernel(blk_e_ref, nblk_ref, first_ref, wslot_ref, next_e_ref,
                   xs_ref, wg_hbm, wu_hbm, wd_hbm, y_ref,
                   wg_f, wu_f, wd_f, wg_b, wu_b, wd_b, wsem):
    i = pl.program_id(0)
    live = i < nblk_ref[0]

    def weight_copies(e, slot):
        return (pltpu.make_async_copy(wg_hbm.at[e], wg_f.at[slot], wsem.at[slot, 0]),
                pltpu.make_async_copy(wu_hbm.at[e], wu_f.at[slot], wsem.at[slot, 1]),
                pltpu.make_async_copy(wd_hbm.at[e], wd_f.at[slot], wsem.at[slot, 2]))

    @pl.when(i == 0)
    def _():
        for cp in weight_copies(blk_e_ref[0], 0):
            cp.start()

    @pl.when(live & (first_ref[i] == 1))
    def _():
        slot = wslot_ref[i]
        for cp in weight_copies(blk_e_ref[i], slot):
            cp.wait()
        wg_b[...] = wg_f[slot].astype(BF16)
        wu_b[...] = wu_f[slot].astype(BF16)
        wd_b[...] = wd_f[slot].astype(BF16)

        @pl.when(next_e_ref[i] >= 0)
        def _():
            for cp in weight_copies(next_e_ref[i], 1 - slot):
                cp.start(priority=1)

    @pl.when(live)
    def _():
        x_lo, x_hi = _unpack_rows(_load_tiles_as_rows(xs_ref, MOE_BLK))
        x_lo = x_lo.astype(BF16)
        x_hi = x_hi.astype(BF16)
        g = _dot(x_lo, wg_b[0:HALF, :]) + _dot(x_hi, wg_b[HALF:, :])
        u = _dot(x_lo, wu_b[0:HALF, :]) + _dot(x_hi, wu_b[HALF:, :])
        a = (g * _sigmoid(g)) * u
        _store_rows_as_tiles(y_ref, _pack_rows(_dot(a.astype(BF16), wd_b[...])))

    @pl.when(jnp.logical_not(live))
    def _():
        y_ref[...] = jnp.zeros_like(y_ref)


def _experts(blk_e, nblk, first, wslot, next_e, xs, wg, wu, wd):
    blk_rows = MOE_BLK * ROW_TILES
    nblocks = xs.shape[0] // blk_rows
    row_map = lambda i, be, nb, *_: (jnp.minimum(i, nb[0] - 1), 0)
    any_spec = pl.BlockSpec(memory_space=pl.ANY)
    return pl.pallas_call(
        _expert_kernel,
        grid_spec=pltpu.PrefetchScalarGridSpec(
            num_scalar_prefetch=5,
            grid=(nblocks,),
            in_specs=[pl.BlockSpec((blk_rows, LANES), row_map), any_spec, any_spec, any_spec],
            out_specs=pl.BlockSpec((blk_rows, LANES), lambda i, *_: (i, 0)),
            scratch_shapes=[pltpu.VMEM((2, D_MODEL, D_EXPERT), F32),
                            pltpu.VMEM((2, D_MODEL, D_EXPERT), F32),
                            pltpu.VMEM((2, D_EXPERT, D_MODEL), F32),
                            pltpu.VMEM((D_MODEL, D_EXPERT), BF16),
                            pltpu.VMEM((D_MODEL, D_EXPERT), BF16),
                            pltpu.VMEM((D_EXPERT, D_MODEL), BF16),
                            pltpu.SemaphoreType.DMA((2, 3))],
        ),
        out_shape=jax.ShapeDtypeStruct(xs.shape, I32),
        compiler_params=pltpu.CompilerParams(
            dimension_semantics=("arbitrary",), vmem_limit_bytes=VMEM_LIMIT),
        name="experts",
    )(blk_e, nblk, first, wslot, next_e, xs, wg, wu, wd)


def _combine_kernel(d1_ref, d2_ref, d1n_ref, d2n_ref, y_hbm, x1_ref, route_ref, gate_ref, gf_ref,
                    o_ref, a0, b0, a1, b1, sem, *, ntiles):
    t = pl.program_id(0)
    bufs = ((a0, b0), (a1, b1))

    def gather(i1_ref, i2_ref, par):
        buf_a, buf_b = bufs[par]

        def row(r, u):
            pltpu.make_async_copy(_tile(y_hbm, i1_ref[r]), _tile(buf_a, r),
                                  sem.at[par]).start(priority=u % 2)
            pltpu.make_async_copy(_tile(y_hbm, i2_ref[r]), _tile(buf_b, r),
                                  sem.at[par]).start(priority=(u + 1) % 2)
        _issue_rows(TM, row)

    @pl.when(t == 0)
    def _():
        gather(d1_ref, d2_ref, 0)

    def step(par):
        buf_a, buf_b = bufs[par]
        for buf in (buf_a, buf_b):
            pltpu.make_async_copy(y_hbm.at[pl.ds(0, TM * ROW_TILES)], buf, sem.at[par]).wait()

        @pl.when(t + 1 < ntiles)
        def _():
            gather(d1n_ref, d2n_ref, 1 - par)

        route = route_ref[0]
        w1 = route[:, 4:5]
        w2 = route[:, 5:6]
        a_lo, a_hi = _unpack_rows(_load_tiles_as_rows(buf_a, TM))
        b_lo, b_hi = _unpack_rows(_load_tiles_as_rows(buf_b, TM))
        ffn = jnp.concatenate([w1 * a_lo + w2 * b_lo, w1 * a_hi + w2 * b_hi], axis=1)
        x2 = x1_ref[0] + gate_ref[0] * ffn
        o_ref[0] = _rms(x2, gf_ref[...])

    for par in range(2):
        pl.when(t % 2 == par)(functools.partial(step, par))


def _combine(d1, d2, y, x1, route, gate, gf, tok_base):
    nb, seq, _ = x1.shape
    nt = seq // TM
    ntiles = nb * nt
    blk0 = tok_base // TM
    smem = lambda fn: pl.BlockSpec((TM,), fn, memory_space=pltpu.SMEM)
    cur = lambda t: (blk0 + t,)
    nxt = lambda t: (blk0 + jnp.minimum(t + 1, ntiles - 1),)
    tile = lambda t: (t // nt, t % nt, 0)
    grows = gate.shape[1]
    gate_spec = (pl.BlockSpec((1, 1, D_MODEL), lambda t: (t // nt, 0, 0)) if grows == 1
                 else pl.BlockSpec((1, TM, D_MODEL), tile))
    return pl.pallas_call(
        functools.partial(_combine_kernel, ntiles=ntiles),
        grid=(ntiles,),
        in_specs=[smem(cur), smem(cur), smem(nxt), smem(nxt), pl.BlockSpec(memory_space=pl.ANY),
                  pl.BlockSpec((1, TM, D_MODEL), tile),
                  pl.BlockSpec((1, TM, RLANES), tile),
                  gate_spec,
                  pl.BlockSpec((1, D_MODEL), lambda t: (0, 0))],
        out_specs=pl.BlockSpec((1, TM, D_MODEL), tile),
        out_shape=jax.ShapeDtypeStruct(x1.shape, F32),
        scratch_shapes=[pltpu.VMEM((TM * ROW_TILES, LANES), I32)] * 4 + [
            pltpu.SemaphoreType.DMA((2,))],
        compiler_params=pltpu.CompilerParams(
            dimension_semantics=("arbitrary",), vmem_limit_bytes=VMEM_LIMIT),
        name="combine",
    )(d1, d2, d1, d2, y, x1, route, gate, gf)


def _band_bias(rel_bias, rows, keys):
    n = rows - 1 + keys
    dist = WINDOW + rows - 1 - np.arange(n + 1)
    flipped = rel_bias[:, np.clip(dist, -MAX_REL, MAX_REL) + MAX_REL]
    skew = jnp.tile(flipped, (1, rows))[:, :rows * n].reshape(N_HEADS, rows, n)
    b = skew[:, :, rows - 1:rows - 1 + keys]
    return b.reshape(2, 4 * rows, keys)


def kernel(x_prompt, x_sample, cache_attn_k, cache_attn_v, state_conv, c_prompt, c_sample,
           w_ada, b_ada, norm1_g, norm2_g, w_in, rel_bias, conv_w, conv_b, w_pa, w_pb, w_o,
           w_group, b_group, w_expert, b_expert, w_e_gate, w_e_up, w_e_down, final_g):
    assert w_ada.shape[0] == 1, "single trunk layer"
    nb, seq, _ = x_prompt.shape
    nseq, slen, _ = x_sample.shape
    ntok_p = nb * seq
    ntok_s = nseq * slen
    ntok = ntok_p + ntok_s
    assert seq % TL == 0 and WINDOW % TL == 0 and seq % TM == 0 and ntok_s % TM == 0
    assert slen >= 2 and slen & (slen - 1) == 0 and slen % 16 == 0 and nseq % SAMPLE_SEQS == 0
    assert MOE_BLK % ZERO_ROWS == 0

    n_c = nb + nseq
    n_pad = -(-n_c // 8) * 8
    c_all = jnp.concatenate([c_prompt, c_sample, jnp.zeros((n_pad - n_c, D_MODEL), F32)], axis=0)
    mod = _ada(c_all, w_ada[0], b_ada[0]).reshape(n_pad, 6, D_MODEL)
    mod_p = mod[:nb]
    mod_s = mod[nb:n_c]

    win = w_in[0].astype(BF16)
    wpa = w_pa[0].astype(BF16)
    wpb = w_pb[0].astype(BF16)
    wo = w_o[0].astype(BF16)
    g1 = norm1_g[0].reshape(1, D_MODEL)
    g2 = norm2_g[0].reshape(1, D_MODEL)
    gf = final_g.reshape(1, D_MODEL)
    cw = jnp.concatenate([conv_w[0], jnp.zeros((8 - conv_w.shape[1], D_CONV), F32)], axis=0)
    cbias = conv_b[0].reshape(1, D_CONV)
    wr = jnp.concatenate([w_expert[0], w_group[0],
                          jnp.zeros((D_MODEL, RLANES - N_EXPERTS - N_GROUPS), F32)], axis=1)
    whi = wr.astype(BF16)
    br = jnp.concatenate([b_expert[0], b_group[0],
                          jnp.zeros((RLANES - N_EXPERTS - N_GROUPS,), F32)]).reshape(1, RLANES)
    bias_p = _band_bias(rel_bias[0] * LOG2E, CHUNK, BAND)
    bias_s = _band_bias(rel_bias[0] * LOG2E, slen, WINDOW + slen).reshape(
        N_HEADS * slen, WINDOW + slen)
    bias_old = bias_s[:, :WINDOW]
    bias_new = bias_s[:, WINDOW:]

    x1p, h2p, kp, vp, up8, route_p, cnt_p = _prompt_main(
        x_prompt, mod_p, g1, g2, win, bias_p, cw, cbias, wpa, wpb, wo, whi, br)

    st = state_conv[0]
    up1 = jnp.zeros((nseq, slen, D_CONV), F32).at[:, 0].set(st[:, 1]).reshape(ntok_s, D_CONV)
    up2 = (jnp.zeros((nseq, slen, D_CONV), F32).at[:, 0].set(st[:, 0]).at[:, 1].set(st[:, 1])
           .reshape(ntok_s, D_CONV))
    ck = jnp.transpose(cache_attn_k[0], (0, 2, 3, 1)).reshape(nseq, D_ATT, WINDOW)
    cv = jnp.transpose(cache_attn_v[0], (0, 2, 3, 1)).reshape(nseq, D_ATT, WINDOW)
    x1s, h2s, ks, vs, us, route_s, cnt = _sample_main(
        x_sample.reshape(ntok_s, D_MODEL), mod_s, ck, cv, up1, up2, cnt_p,
        g1, g2, win, bias_old, bias_new, cw, cbias, wpa, wpb, wo, whi, br, nseq, slen)

    route_all = jnp.concatenate([route_p.reshape(ntok_p, RLANES)[:, :4], route_s[:, :4]], axis=0)
    experts = route_all[:, 0:2].astype(jnp.int32)
    ranks = route_all[:, 2:4].astype(jnp.int32)
    counts = cnt[0, :N_EXPERTS].astype(jnp.int32)
    pcounts = (counts + MOE_BLK - 1) // MOE_BLK * MOE_BLK
    pend = jnp.cumsum(pcounts)
    pstart = pend - pcounts
    eids = jnp.arange(N_EXPERTS, dtype=jnp.int32)
    dest = jnp.sum(jnp.where(experts[..., None] == eids, pstart, 0), axis=-1) + ranks
    d1 = dest[:, 0]
    d2 = dest[:, 1]
    nblocks = (2 * ntok) // MOE_BLK + N_EXPERTS
    blk_start = jnp.arange(nblocks, dtype=jnp.int32) * MOE_BLK
    blk_e = jnp.minimum(jnp.sum((pend[None, :] <= blk_start[:, None]).astype(jnp.int32), axis=1),
                        N_EXPERTS - 1)
    nblk = (pend[-1:] // MOE_BLK).astype(jnp.int32)
    pend0 = jnp.concatenate([jnp.zeros((1,), jnp.int32), pend.astype(jnp.int32)])

    xs = _dispatch(pend0, counts, d1, d2, h2p, h2s, nblocks * MOE_BLK)
    blk_id = jnp.arange(nblocks, dtype=jnp.int32)
    first = (blk_id < nblk[0]) & ((blk_id == 0) | (blk_e != jnp.roll(blk_e, 1)))
    wslot = (jnp.cumsum(first.astype(jnp.int32)) - 1) % 2
    later_first = lax.cummin(jnp.where(first, blk_id, nblocks)[::-1])[::-1]
    next_first = jnp.concatenate([later_first[1:], jnp.full((1,), nblocks, jnp.int32)])
    next_e = jnp.where(next_first < nblocks, blk_e[jnp.minimum(next_first, nblocks - 1)], -1)
    y = _experts(blk_e, nblk, first.astype(jnp.int32), wslot.astype(jnp.int32),
                 next_e.astype(jnp.int32), xs, w_e_gate[0], w_e_up[0], w_e_down[0])

    y_prompt = _combine(d1, d2, y, x1p, route_p, mod_p[:, 5:6, :], gf, 0)
    gate_s = jnp.repeat(mod_s[:, 5, :], slen, axis=0).reshape(1, ntok_s, D_MODEL)
    y_sample = _combine(d1, d2, y, x1s.reshape(1, ntok_s, D_MODEL),
                        route_s.reshape(1, ntok_s, RLANES), gate_s, gf, ntok_p)

    new_k_p = kp.reshape(1, nb, WINDOW, N_HEADS, HEAD_DIM)
    new_v_p = vp.reshape(1, nb, WINDOW, N_HEADS, HEAD_DIM)
    new_conv_p = up8[:, 6:8, :].reshape(1, nb, 2, D_CONV)
    new_k_s = ks.reshape(1, nseq, slen, N_HEADS, HEAD_DIM)
    new_v_s = vs.reshape(1, nseq, slen, N_HEADS, HEAD_DIM)
    new_conv_s = us.reshape(nseq, slen, D_CONV)[:, slen - 2:, :].reshape(1, nseq, 2, D_CONV)
    return (y_prompt, y_sample.reshape(nseq, slen, D_MODEL), new_k_p, new_v_p, new_conv_p,
            new_k_s, new_v_s, new_conv_s)
```

```python
import functools

import numpy as np
import jax
import jax.numpy as jnp
from jax import lax
from jax.experimental import pallas as pl
from jax.experimental.pallas import tpu as pltpu

F32 = jnp.float32
BF16 = jnp.bfloat16
I32 = jnp.int32

D_MODEL = 1024
CHUNK = 64
LEFT = 8
WINDOW = LEFT * CHUNK
BAND = WINDOW + CHUNK
N_HEADS = 8
HEAD_DIM = 64
D_ATT = N_HEADS * HEAD_DIM
QUAD = 256
MAX_REL = 128
D_CONV = 512
N_GROUPS = 4
EPG = 8
N_EXPERTS = 32
D_EXPERT = 512
EPS = 1e-6
NEG = -1e30
LOG2E = float(np.log2(np.e))
Q_SCALE = HEAD_DIM ** -0.5 * LOG2E

TL = 512
RING = WINDOW + TL
MOE_BLK = 512
TM = 512
SAMPLE_SEQS = 2
ZERO_ROWS = 64
RLANES = 128
LANES = 128
HALF = D_MODEL // 2
ROW_TILES = HALF // LANES
ISSUE_UNROLL = 8
VMEM_LIMIT = 56 * 1024 * 1024


def _const_spec(shape):
    nd = len(shape)
    return pl.BlockSpec(shape, lambda *_: (0,) * nd, pipeline_mode=pl.Buffered(1))


def _dot(a, b):
    return jnp.dot(a, b, preferred_element_type=F32)


def _sigmoid(x):
    return 1.0 / (1.0 + jnp.exp(-x))


def _rms(x, g):
    ms = jnp.mean(x * x, axis=-1, keepdims=True)
    return x * lax.rsqrt(ms + EPS) * g


def _pack_rows(val):
    lo = lax.bitcast_convert_type(val[:, :HALF], I32) + 0x8000
    hi = lax.bitcast_convert_type(val[:, HALF:], I32) + 0x8000
    return (hi & -65536) | lax.shift_right_logical(lo, 16)


def _unpack_rows(packed):
    lo = lax.bitcast_convert_type(lax.shift_left(packed, 16), F32)
    hi = lax.bitcast_convert_type(packed & -65536, F32)
    return lo, hi


def _store_rows_as_tiles(ref, packed):
    r = packed.shape[0]
    for c in range(ROW_TILES):
        ref[pl.ds(c, r, stride=ROW_TILES), :] = packed[:, c * LANES:(c + 1) * LANES]


def _load_tiles_as_rows(ref, r):
    return jnp.concatenate(
        [ref[pl.ds(c, r, stride=ROW_TILES), :] for c in range(ROW_TILES)], axis=1)


def _tile(ref, row):
    return ref.at[pl.ds(pl.multiple_of(row * ROW_TILES, ROW_TILES), ROW_TILES)]


def _ada_kernel(c_ref, w_ref, b_ref, o_ref):
    c = c_ref[...]
    s = c * _sigmoid(c)
    o_ref[...] = _dot(s.astype(BF16), w_ref[...].astype(BF16)) + b_ref[...]


def _ada(c_all, w_ada, b_ada):
    n = c_all.shape[0]
    nb = 1024
    return pl.pallas_call(
        _ada_kernel,
        grid=(6 * D_MODEL // nb,),
        in_specs=[pl.BlockSpec((n, D_MODEL), lambda i: (0, 0)),
                  pl.BlockSpec((D_MODEL, nb), lambda i: (0, i)),
                  pl.BlockSpec((1, nb), lambda i: (0, i))],
        out_specs=pl.BlockSpec((n, nb), lambda i: (0, i)),
        out_shape=jax.ShapeDtypeStruct((n, 6 * D_MODEL), F32),
        name="ada",
    )(c_all, w_ada, b_ada.reshape(1, -1))


def _attend(q, kb, vb, bias, lim):
    r = q.shape[0]
    nk = kb.shape[0]
    assert r & (r - 1) == 0
    qt = jnp.concatenate([q] * 4, axis=0)
    rowh = lax.broadcasted_iota(jnp.int32, (4 * r, QUAD), 0) >> (r.bit_length() - 1)
    laneh = lax.broadcasted_iota(jnp.int32, (4 * r, QUAD), 1) >> 6
    qm = jnp.where(rowh == laneh, qt, jnp.zeros_like(qt))
    s = lax.dot_general(qm, kb, (((1,), (1,)), ((), ())), preferred_element_type=F32)
    valid = None
    if lim is not None:
        valid = lax.broadcasted_iota(jnp.int32, (r, nk), 1) >= lim
    ps, inv_l = [], []
    for h in range(4):
        sh = s[h * r:(h + 1) * r] + bias[h * r:(h + 1) * r]
        if valid is not None:
            sh = jnp.where(valid, sh, NEG)
        m = jnp.max(sh, axis=1, keepdims=True)
        ph = jnp.exp2(sh - m)
        inv_l.append(1.0 / jnp.sum(ph, axis=1, keepdims=True))
        ps.append(ph.astype(BF16))
    o = _dot(jnp.concatenate(ps, axis=0), vb)
    lane_o = lax.broadcasted_iota(jnp.int32, (r, QUAD), 1) >> 6
    out = o[0:r] * inv_l[0]
    for h in range(1, 4):
        out = jnp.where(lane_o == h, o[h * r:(h + 1) * r] * inv_l[h], out)
    return out


def _route_stages(h2, whi_ref, br_ref, cnt, valid=None):
    r = h2.shape[0]
    lane = lax.broadcasted_iota(jnp.int32, (r, RLANES), 1)
    lane_f = lane.astype(F32)
    big = jnp.float32(1000.0)
    v = {}

    def logits():
        v["logits"] = _dot(h2.astype(BF16), whi_ref[...]) + br_ref[...]

    def group():
        lg = jnp.where((lane >= N_EXPERTS) & (lane < N_EXPERTS + N_GROUPS), v["logits"], NEG)
        mg = jnp.max(lg, axis=1, keepdims=True)
        v["gi"] = jnp.min(jnp.where(lg == mg, lane_f, big), axis=1, keepdims=True) - N_EXPERTS
        v["pg"] = 1.0 / jnp.sum(jnp.exp(lg - mg), axis=1, keepdims=True)

    def top1():
        grp_of_lane = (lane >> 3).astype(F32)
        le = jnp.where((lane < N_EXPERTS) & (grp_of_lane == v["gi"]), v["logits"], NEG)
        v["m1"] = jnp.max(le, axis=1, keepdims=True)
        v["i1"] = jnp.min(jnp.where(le == v["m1"], lane_f, big), axis=1, keepdims=True)
        v["le"] = le

    def top2():
        sel1 = lane_f == v["i1"]
        le2 = jnp.where(sel1, NEG, v["le"])
        m2 = jnp.max(le2, axis=1, keepdims=True)
        v["i2"] = jnp.min(jnp.where(le2 == m2, lane_f, big), axis=1, keepdims=True)
        rr = jnp.exp(m2 - v["m1"])
        inv = v["pg"] / (1.0 + rr)
        v["w1"] = inv
        v["w2"] = inv * rr
        v["sel1"] = sel1
        v["sel2"] = lane_f == v["i2"]

    def ranks():
        oh = jnp.where(v["sel1"] | v["sel2"], 1.0 if valid is None else valid, 0.0).astype(F32)
        ri = lax.broadcasted_iota(jnp.int32, (r, r), 0)
        ci = lax.broadcasted_iota(jnp.int32, (r, r), 1)
        tri = jnp.where(ri > ci, 1.0, 0.0).astype(BF16)
        v["before"] = _dot(tri, oh.astype(BF16)) + cnt
        v["new_cnt"] = cnt + jnp.sum(oh, axis=0, keepdims=True)

    def assemble():
        r1 = jnp.sum(jnp.where(v["sel1"], v["before"], 0.0), axis=1, keepdims=True)
        r2 = jnp.sum(jnp.where(v["sel2"], v["before"], 0.0), axis=1, keepdims=True)
        route = jnp.where(lane == 0, v["i1"], 0.0)
        route = jnp.where(lane == 1, v["i2"], route)
        route = jnp.where(lane == 2, r1, route)
        route = jnp.where(lane == 3, r2, route)
        route = jnp.where(lane == 4, v["w1"], route)
        route = jnp.where(lane == 5, v["w2"], route)
        return route, v["new_cnt"]

    return [logits, group, top1, top2, ranks, assemble]


def _slot_info(route):
    return jnp.transpose(route)[0:8, :]


def _route(h2, whi_ref, br_ref, cnt):
    stages = _route_stages(h2, whi_ref, br_ref, cnt)
    for stage in stages[:-1]:
        stage()
    return stages[-1]()


def _prompt_kernel(x_ref, mod_ref, g1_ref, g2_ref, win_ref, bias_ref, cw_ref, cbias_ref,
                   wpa_ref, wpb_ref, wo_ref, whi_ref, br_ref,
                   x1_ref, h2_ref, ko_ref, vo_ref, uo_ref, route_ref, info_ref, cnt_ref,
                   kring, vring, att_s, ucarry, cnt_s, h2_prev, *, nt, ntiles):
    g = pl.program_id(0)

    @pl.when(g == 0)
    def _():
        cnt_s[...] = jnp.zeros_like(cnt_s)
        h2_prev[...] = jnp.zeros_like(h2_prev)

    @pl.when(g < ntiles)
    def _():
        _prompt_tile(g, nt, x_ref, mod_ref, g1_ref, g2_ref, win_ref, bias_ref, cw_ref, cbias_ref,
                     wpa_ref, wpb_ref, wo_ref, whi_ref, br_ref,
                     x1_ref, h2_ref, ko_ref, vo_ref, uo_ref, route_ref, info_ref, cnt_ref,
                     kring, vring, att_s, ucarry, cnt_s, h2_prev)

    @pl.when(g == ntiles)
    def _():
        route, new_cnt = _route(h2_prev[...], whi_ref, br_ref, cnt_s[...])
        route_ref[0] = route
        info_ref[...] = _slot_info(route)
        cnt_ref[...] = new_cnt


def _prompt_tile(g, nt, x_ref, mod_ref, g1_ref, g2_ref, win_ref, bias_ref, cw_ref, cbias_ref,
                 wpa_ref, wpb_ref, wo_ref, whi_ref, br_ref,
                 x1_ref, h2_ref, ko_ref, vo_ref, uo_ref, route_ref, info_ref, cnt_ref,
                 kring, vring, att_s, ucarry, cnt_s, h2_prev):
    j = g % nt

    @pl.when(j == 0)
    def _():
        kring[0:WINDOW, :] = jnp.zeros((WINDOW, D_ATT), BF16)
        vring[0:WINDOW, :] = jnp.zeros((WINDOW, D_ATT), BF16)
        ucarry[...] = jnp.zeros_like(ucarry)

    route_stages = _route_stages(h2_prev[...], whi_ref, br_ref, cnt_s[...],
                                 valid=jnp.where(g > 0, 1.0, 0.0).astype(F32))
    route_stages[0]()

    sh1 = mod_ref[0, 0:1, :]
    sc1 = mod_ref[0, 1:2, :]
    gt1 = mod_ref[0, 2:3, :]
    sh2 = mod_ref[0, 3:4, :]
    sc2 = mod_ref[0, 4:5, :]

    x = x_ref[0]
    h = _rms(x, g1_ref[...]) * (1.0 + sc1) + sh1
    hb = h.astype(BF16)

    qkv = _dot(hb, win_ref[:, 0:3 * D_ATT])
    q = (qkv[:, 0:D_ATT] * Q_SCALE).astype(BF16)
    k = qkv[:, D_ATT:2 * D_ATT]
    v = qkv[:, 2 * D_ATT:3 * D_ATT]
    ko_ref[0] = k
    vo_ref[0] = v
    kring[WINDOW:RING, :] = k.astype(BF16)
    vring[WINDOW:RING, :] = v.astype(BF16)

    base = j * TL
    for c in range(TL // CHUNK):
        lim = WINDOW - (base + c * CHUNK)
        for qd in range(2):
            ls = slice(qd * QUAD, (qd + 1) * QUAD)
            o = _attend(q[c * CHUNK:(c + 1) * CHUNK, ls],
                        kring[c * CHUNK:c * CHUNK + BAND, ls],
                        vring[c * CHUNK:c * CHUNK + BAND, ls],
                        bias_ref[qd], lim)
            att_s[c * CHUNK:(c + 1) * CHUNK, ls] = o.astype(BF16)
        if c + 1 < len(route_stages) - 1:
            route_stages[c + 1]()
    route, new_cnt = route_stages[-1]()
    route_ref[0] = route
    info_ref[...] = _slot_info(route)
    cnt_s[...] = new_cnt
    cnt_ref[...] = new_cnt

    kring[0:WINDOW, :] = kring[TL:RING, :]
    vring[0:WINDOW, :] = vring[TL:RING, :]

    cbcv = _dot(hb, win_ref[:, 3 * D_ATT:3 * D_ATT + 3 * D_CONV])
    gates = _dot(hb, win_ref[:, 3 * D_ATT + 3 * D_CONV:])
    cb = cbcv[:, 0:D_CONV]
    u = cbcv[:, D_CONV:2 * D_CONV] * cbcv[:, 2 * D_CONV:3 * D_CONV]
    row = lax.broadcasted_iota(jnp.int32, (8, D_CONV), 0)
    prev = ucarry[...]
    r1 = pltpu.roll(u, 1, axis=0)
    r2 = pltpu.roll(u, 2, axis=0)
    u_m1 = jnp.concatenate(
        [jnp.where(row < 1, pltpu.roll(prev, 1, axis=0), r1[0:8]), r1[8:]], axis=0)
    u_m2 = jnp.concatenate(
        [jnp.where(row < 2, pltpu.roll(prev, 2, axis=0), r2[0:8]), r2[8:]], axis=0)
    yc = cw_ref[0:1, :] * u_m2 + cw_ref[1:2, :] * u_m1 + cw_ref[2:3, :] * u + cbias_ref[...]
    conv_out = (cb * yc).astype(BF16)
    ucarry[...] = u[TL - 8:TL, :]
    uo_ref[0] = u[TL - 8:TL, :]

    pa = _dot(att_s[...], wpa_ref[...])
    pb = _dot(conv_out, wpb_ref[...])
    mixin = _sigmoid(gates[:, 0:D_MODEL]) * pa + _sigmoid(gates[:, D_MODEL:]) * pb
    mix = _dot(mixin.astype(BF16), wo_ref[...])
    x1 = x + gt1 * mix
    x1_ref[0] = x1
    h2 = _rms(x1, g2_ref[...]) * (1.0 + sc2) + sh2
    _store_rows_as_tiles(h2_ref, _pack_rows(h2))

    h2_prev[...] = h2


def _prompt_main(x, mod, g1, g2, win, bias_q, cw, cbias, wpa, wpb, wo, whi, br):
    nb, seq, _ = x.shape
    nt = seq // TL
    ntiles = nb * nt
    keep = WINDOW // TL
    cur = lambda g: jnp.minimum(g, ntiles - 1)
    prv = lambda g: jnp.maximum(g - 1, 0)
    tile = lambda g: (cur(g) // nt, cur(g) % nt, 0)
    last = lambda g: (cur(g) // nt, jnp.maximum(cur(g) % nt - (nt - keep), 0), 0)
    perb = lambda g: (cur(g) // nt, 0, 0)
    in_specs = [
        pl.BlockSpec((1, TL, D_MODEL), tile),
        pl.BlockSpec((1, 6, D_MODEL), perb),
        _const_spec(g1.shape), _const_spec(g2.shape), _const_spec(win.shape),
        _const_spec(bias_q.shape), _const_spec(cw.shape), _const_spec(cbias.shape),
        _const_spec(wpa.shape), _const_spec(wpb.shape), _const_spec(wo.shape),
        _const_spec(whi.shape), _const_spec(br.shape),
    ]
    out_specs = [
        pl.BlockSpec((1, TL, D_MODEL), tile),
        pl.BlockSpec((TL * ROW_TILES, LANES), lambda g: (cur(g), 0)),
        pl.BlockSpec((1, TL, D_ATT), last),
        pl.BlockSpec((1, TL, D_ATT), last),
        pl.BlockSpec((1, 8, D_CONV), perb),
        pl.BlockSpec((1, TL, RLANES), lambda g: (prv(g) // nt, prv(g) % nt, 0)),
        pl.BlockSpec((8, TL), lambda g: (0, prv(g))),
        pl.BlockSpec((1, RLANES), lambda g: (0, 0)),
    ]
    out_shape = [
        jax.ShapeDtypeStruct((nb, seq, D_MODEL), F32),
        jax.ShapeDtypeStruct((nb * seq * ROW_TILES, LANES), I32),
        jax.ShapeDtypeStruct((nb, WINDOW, D_ATT), F32),
        jax.ShapeDtypeStruct((nb, WINDOW, D_ATT), F32),
        jax.ShapeDtypeStruct((nb, 8, D_CONV), F32),
        jax.ShapeDtypeStruct((nb, seq, RLANES), F32),
        jax.ShapeDtypeStruct((8, nb * seq), F32),
        jax.ShapeDtypeStruct((1, RLANES), F32),
    ]
    scratch = [
        pltpu.VMEM((RING, D_ATT), BF16), pltpu.VMEM((RING, D_ATT), BF16),
        pltpu.VMEM((TL, D_ATT), BF16), pltpu.VMEM((8, D_CONV), F32),
        pltpu.VMEM((1, RLANES), F32), pltpu.VMEM((TL, D_MODEL), F32),
    ]
    return pl.pallas_call(
        functools.partial(_prompt_kernel, nt=nt, ntiles=ntiles),
        grid=(ntiles + 1,),
        in_specs=in_specs, out_specs=out_specs, out_shape=out_shape,
        scratch_shapes=scratch,
        compiler_params=pltpu.CompilerParams(
            dimension_semantics=("arbitrary",), vmem_limit_bytes=VMEM_LIMIT),
        name="prompt_main",
    )(x, mod, g1, g2, win, bias_q, cw, cbias, wpa, wpb, wo, whi, br)


def _sample_kernel(x_ref, mod_ref, ck_ref, cv_ref, up1_ref, up2_ref, cnt_in_ref,
                   g1_ref, g2_ref, win_ref, bias_old_ref, bias_new_ref, cw_ref, cbias_ref,
                   wpa_ref, wpb_ref, wo_ref, whi_ref, br_ref,
                   x1_ref, h2_ref, ko_ref, vo_ref, uo_ref, route_ref, info_ref, cnt_ref,
                   h_s, q_s, kn_s, vn_s, att_s, conv_s, h2_s, *, nseq, slen):
    n = pl.program_id(0)
    ntok = nseq * slen

    @pl.when(n == 0)
    def _():
        def norm_body(i, carry):
            rows = pl.ds(pl.multiple_of(i * slen, slen), slen)
            xi = x_ref[rows, :]
            m = mod_ref[i]
            hi = _rms(xi, g1_ref[...]) * (1.0 + m[1:2, :]) + m[0:1, :]
            h_s[rows, :] = hi.astype(BF16)
            return carry
        lax.fori_loop(0, nseq, norm_body, 0)
        hb = h_s[...]
        qkv = _dot(hb, win_ref[:, 0:3 * D_ATT])
        q_s[...] = (qkv[:, 0:D_ATT] * Q_SCALE).astype(BF16)
        k = qkv[:, D_ATT:2 * D_ATT]
        v = qkv[:, 2 * D_ATT:3 * D_ATT]
        ko_ref[...] = k
        vo_ref[...] = v
        kn_s[...] = k.astype(BF16)
        vn_s[...] = v.astype(BF16)

        cbcv = _dot(hb, win_ref[:, 3 * D_ATT:3 * D_ATT + 3 * D_CONV])
        cb = cbcv[:, 0:D_CONV]
        u = cbcv[:, D_CONV:2 * D_CONV] * cbcv[:, 2 * D_CONV:3 * D_CONV]
        pos = lax.broadcasted_iota(jnp.int32, (ntok, D_CONV), 0) & (slen - 1)
        u_m1 = jnp.where(pos < 1, up1_ref[...], pltpu.roll(u, 1, axis=0))
        u_m2 = jnp.where(pos < 2, up2_ref[...], pltpu.roll(u, 2, axis=0))
        yc = cw_ref[0:1, :] * u_m2 + cw_ref[1:2, :] * u_m1 + cw_ref[2:3, :] * u + cbias_ref[...]
        conv_s[...] = (cb * yc).astype(BF16)
        uo_ref[...] = u

    nt_dims = (((1,), (1,)), ((), ()))
    shape = (N_HEADS * slen, D_ATT)
    rowh = lax.broadcasted_iota(jnp.int32, shape, 0) >> (slen.bit_length() - 1)
    laneh = lax.broadcasted_iota(jnp.int32, shape, 1) >> 6
    lane_o = lax.broadcasted_iota(jnp.int32, (slen, D_ATT), 1) >> 6
    for sq in range(SAMPLE_SEQS):
        rows = pl.ds(pl.multiple_of((n * SAMPLE_SEQS + sq) * slen, slen), slen)
        qt = jnp.concatenate([q_s[rows, :]] * N_HEADS, axis=0)
        qm = jnp.where(rowh == laneh, qt, jnp.zeros_like(qt))
        s_old = _dot(qm, ck_ref[sq].astype(BF16)) + bias_old_ref[...]
        s_new = lax.dot_general(qm, kn_s[rows, :], nt_dims,
                                preferred_element_type=F32) + bias_new_ref[...]
        m = jnp.maximum(jnp.max(s_old, axis=1, keepdims=True),
                        jnp.max(s_new, axis=1, keepdims=True))
        p_old = jnp.exp2(s_old - m)
        p_new = jnp.exp2(s_new - m)
        l = jnp.sum(p_old, axis=1, keepdims=True) + jnp.sum(p_new, axis=1, keepdims=True)
        o = lax.dot_general(p_old.astype(BF16), cv_ref[sq].astype(BF16), nt_dims,
                            preferred_element_type=F32)
        o = (o + _dot(p_new.astype(BF16), vn_s[rows, :])) * (1.0 / l)
        att = o[0:slen]
        for h in range(1, N_HEADS):
            att = jnp.where(lane_o == h, o[h * slen:(h + 1) * slen], att)
        att_s[rows, :] = att.astype(BF16)

    @pl.when(n == nseq // SAMPLE_SEQS - 1)
    def _():
        gates = _dot(h_s[...], win_ref[:, 3 * D_ATT + 3 * D_CONV:])
        pa = _dot(att_s[...], wpa_ref[...])
        pb = _dot(conv_s[...], wpb_ref[...])
        mixin = _sigmoid(gates[:, 0:D_MODEL]) * pa + _sigmoid(gates[:, D_MODEL:]) * pb
        x1_ref[...] = _dot(mixin.astype(BF16), wo_ref[...])

        def res_body(i, carry):
            r = pl.ds(pl.multiple_of(i * slen, slen), slen)
            m = mod_ref[i]
            x1 = x_ref[r, :] + m[2:3, :] * x1_ref[r, :]
            x1_ref[r, :] = x1
            h2_s[r, :] = _rms(x1, g2_ref[...]) * (1.0 + m[4:5, :]) + m[3:4, :]
            return carry
        lax.fori_loop(0, nseq, res_body, 0)

        h2 = h2_s[...]
        _store_rows_as_tiles(h2_ref, _pack_rows(h2))
        route, new_cnt = _route(h2, whi_ref, br_ref, cnt_in_ref[...])
        route_ref[...] = route
        info_ref[...] = _slot_info(route)
        cnt_ref[...] = new_cnt


def _sample_main(x2d, mod, ck, cv, up1, up2, cnt_in, g1, g2, win, bias_old, bias_new, cw, cbias,
                 wpa, wpb, wo, whi, br, nseq, slen):
    ntok = nseq * slen
    args = (x2d, mod, ck, cv, up1, up2, cnt_in, g1, g2, win, bias_old, bias_new, cw, cbias,
            wpa, wpb, wo, whi, br)
    in_specs = []
    for idx, a in enumerate(args):
        if idx in (2, 3):
            in_specs.append(pl.BlockSpec((SAMPLE_SEQS,) + a.shape[1:], lambda n: (n, 0, 0)))
        else:
            in_specs.append(_const_spec(a.shape))
    whole = lambda shape: pl.BlockSpec(shape, lambda n: (0,) * len(shape))
    outs = [((ntok, D_MODEL), F32), ((ntok * ROW_TILES, LANES), I32), ((ntok, D_ATT), F32),
            ((ntok, D_ATT), F32), ((ntok, D_CONV), F32), ((ntok, RLANES), F32), ((8, ntok), F32),
            ((1, RLANES), F32)]
    scratch = [
        pltpu.VMEM((ntok, D_MODEL), BF16), pltpu.VMEM((ntok, D_ATT), BF16),
        pltpu.VMEM((ntok, D_ATT), BF16), pltpu.VMEM((ntok, D_ATT), BF16),
        pltpu.VMEM((ntok, D_ATT), BF16), pltpu.VMEM((ntok, D_CONV), BF16),
        pltpu.VMEM((ntok, D_MODEL), F32),
    ]
    return pl.pallas_call(
        functools.partial(_sample_kernel, nseq=nseq, slen=slen),
        grid=(nseq // SAMPLE_SEQS,),
        in_specs=in_specs,
        out_specs=[whole(s) for s, _ in outs],
        out_shape=[jax.ShapeDtypeStruct(s, d) for s, d in outs],
        scratch_shapes=scratch,
        compiler_params=pltpu.CompilerParams(
            dimension_semantics=("arbitrary",), vmem_limit_bytes=VMEM_LIMIT),
        name="sample_main",
    )(*args)


def _issue_rows(n, body):
    def group(g, carry):
        for u in range(ISSUE_UNROLL):
            body(g * ISSUE_UNROLL + u, u)
        return carry
    lax.fori_loop(0, n // ISSUE_UNROLL, group, 0)


def _dispatch_kernel(pend_ref, cnt_ref, d1_ref, d2_ref, hp_ref, hs_ref, xs_out, zbuf, sem, zsem, *,
                     np_tiles, nslots):
    i = pl.program_id(0)
    zrows = ZERO_ROWS * ROW_TILES

    @pl.when(i == 0)
    def _():
        zbuf[...] = jnp.zeros_like(zbuf)

        def zcopy(piece):
            start = pl.multiple_of(piece * zrows, zrows)
            return pltpu.make_async_copy(zbuf, xs_out.at[pl.ds(start, zrows)], zsem)

        def pieces(e):
            lo = (pend_ref[e] + cnt_ref[jnp.minimum(e, N_EXPERTS - 1)]) // ZERO_ROWS
            hi = pend_ref[jnp.minimum(e + 1, N_EXPERTS)] // ZERO_ROWS
            lo = jnp.where(e == N_EXPERTS, pend_ref[N_EXPERTS] // ZERO_ROWS, lo)
            hi = jnp.where(e == N_EXPERTS, nslots // ZERO_ROWS, hi)
            return lo, hi

        def start_all(e, carry):
            lo, hi = pieces(e)
            return lax.fori_loop(lo, hi, lambda p, c: (zcopy(p).start(), c)[1], carry)

        def wait_all(e, carry):
            lo, hi = pieces(e)
            return lax.fori_loop(lo, hi, lambda p, c: (zcopy(p).wait(), c)[1], carry)
        lax.fori_loop(0, N_EXPERTS + 1, start_all, 0)
        lax.fori_loop(0, N_EXPERTS + 1, wait_all, 0)

    def scatter_tile(src):
        def row(r, u):
            pltpu.make_async_copy(_tile(src, r), _tile(xs_out, d1_ref[r]), sem).start(priority=u % 2)
            pltpu.make_async_copy(_tile(src, r), _tile(xs_out, d2_ref[r]), sem).start(
                priority=(u + 1) % 2)
        _issue_rows(TM, row)
        for _ in range(2):
            pltpu.make_async_copy(src, xs_out.at[pl.ds(0, TM * ROW_TILES)], sem).wait()

    @pl.when(i < np_tiles)
    def _():
        scatter_tile(hp_ref)

    @pl.when(i >= np_tiles)
    def _():
        scatter_tile(hs_ref)


def _dispatch(pend, counts, d1, d2, h2p, h2s, nslots):
    np_tiles = h2p.shape[0] // (TM * ROW_TILES)
    ns_tiles = h2s.shape[0] // (TM * ROW_TILES)
    smem_tile = pl.BlockSpec((TM,), lambda i, *_: (i,), memory_space=pltpu.SMEM)
    rows = TM * ROW_TILES
    return pl.pallas_call(
        functools.partial(_dispatch_kernel, np_tiles=np_tiles, nslots=nslots),
        grid_spec=pltpu.PrefetchScalarGridSpec(
            num_scalar_prefetch=2,
            grid=(np_tiles + ns_tiles,),
            in_specs=[smem_tile, smem_tile,
                      pl.BlockSpec((rows, LANES), lambda i, *_: (jnp.minimum(i, np_tiles - 1), 0)),
                      pl.BlockSpec((rows, LANES), lambda i, *_: (jnp.maximum(i - np_tiles, 0), 0))],
            out_specs=pl.BlockSpec(memory_space=pl.ANY),
            scratch_shapes=[pltpu.VMEM((ZERO_ROWS * ROW_TILES, LANES), I32),
                            pltpu.SemaphoreType.DMA(()), pltpu.SemaphoreType.DMA(())],
        ),
        out_shape=jax.ShapeDtypeStruct((nslots * ROW_TILES, LANES), I32),
        compiler_params=pltpu.CompilerParams(dimension_semantics=("arbitrary",)),
        name="dispatch",
    )(pend, counts, d1, d2, h2p, h2s)


def _expert_kernel(blk_e_ref, nblk_ref, first_ref, wslot_ref, next_e_ref,
                   xs_ref, wg_hbm, wu_hbm, wd_hbm, y_ref,
                   wg_f, wu_f, wd_f, wg_b, wu_b, wd_b, wsem):
    i = pl.program_id(0)
    live = i < nblk_ref[0]

    def weight_copies(e, slot):
        return (pltpu.make_async_copy(wg_hbm.at[e], wg_f.at[slot], wsem.at[slot, 0]),
                pltpu.make_async_copy(wu_hbm.at[e], wu_f.at[slot], wsem.at[slot, 1]),
                pltpu.make_async_copy(wd_hbm.at[e], wd_f.at[slot], wsem.at[slot, 2]))

    @pl.when(i == 0)
    def _():
        for cp in weight_copies(blk_e_ref[0], 0):
            cp.start()

    @pl.when(live & (first_ref[i] == 1))
    def _():
        slot = wslot_ref[i]
        for cp in weight_copies(blk_e_ref[i], slot):
            cp.wait()
        wg_b[...] = wg_f[slot].astype(BF16)
        wu_b[...] = wu_f[slot].astype(BF16)
        wd_b[...] = wd_f[slot].astype(BF16)

        @pl.when(next_e_ref[i] >= 0)
        def _():
            for cp in weight_copies(next_e_ref[i], 1 - slot):
                cp.start()

    @pl.when(live)
    def _():
        x_lo, x_hi = _unpack_rows(_load_tiles_as_rows(xs_ref, MOE_BLK))
        x_lo = x_lo.astype(BF16)
        x_hi = x_hi.astype(BF16)
        g = _dot(x_lo, wg_b[0:HALF, :]) + _dot(x_hi, wg_b[HALF:, :])
        u = _dot(x_lo, wu_b[0:HALF, :]) + _dot(x_hi, wu_b[HALF:, :])
        a = (g * _sigmoid(g)) * u
        _store_rows_as_tiles(y_ref, _pack_rows(_dot(a.astype(BF16), wd_b[...])))

    @pl.when(jnp.logical_not(live))
    def _():
        y_ref[...] = jnp.zeros_like(y_ref)


def _experts(blk_e, nblk, first, wslot, next_e, xs, wg, wu, wd):
    blk_rows = MOE_BLK * ROW_TILES
    nblocks = xs.shape[0] // blk_rows
    row_map = lambda i, be, nb, *_: (jnp.minimum(i, nb[0] - 1), 0)
    any_spec = pl.BlockSpec(memory_space=pl.ANY)
    return pl.pallas_call(
        _expert_kernel,
        grid_spec=pltpu.PrefetchScalarGridSpec(
            num_scalar_prefetch=5,
            grid=(nblocks,),
            in_specs=[pl.BlockSpec((blk_rows, LANES), row_map), any_spec, any_spec, any_spec],
            out_specs=pl.BlockSpec((blk_rows, LANES), lambda i, *_: (i, 0)),
            scratch_shapes=[pltpu.VMEM((2, D_MODEL, D_EXPERT), F32),
                            pltpu.VMEM((2, D_MODEL, D_EXPERT), F32),
                            pltpu.VMEM((2, D_EXPERT, D_MODEL), F32),
                            pltpu.VMEM((D_MODEL, D_EXPERT), BF16),
                            pltpu.VMEM((D_MODEL, D_EXPERT), BF16),
                            pltpu.VMEM((D_EXPERT, D_MODEL), BF16),
                            pltpu.SemaphoreType.DMA((2, 3))],
        ),
        out_shape=jax.ShapeDtypeStruct(xs.shape, I32),
        compiler_params=pltpu.CompilerParams(
            dimension_semantics=("arbitrary",), vmem_limit_bytes=VMEM_LIMIT),
        name="experts",
    )(blk_e, nblk, first, wslot, next_e, xs, wg, wu, wd)


def _combine_kernel(d1_ref, d2_ref, d1n_ref, d2n_ref, y_hbm, x1_ref, route_ref, gate_ref, gf_ref,
                    o_ref, a0, b0, a1, b1, sem, *, ntiles):
    t = pl.program_id(0)
    bufs = ((a0, b0), (a1, b1))

    def gather(i1_ref, i2_ref, par):
        buf_a, buf_b = bufs[par]

        def row(r, u):
            pltpu.make_async_copy(_tile(y_hbm, i1_ref[r]), _tile(buf_a, r),
                                  sem.at[par]).start(priority=u % 2)
            pltpu.make_async_copy(_tile(y_hbm, i2_ref[r]), _tile(buf_b, r),
                                  sem.at[par]).start(priority=(u + 1) % 2)
        _issue_rows(TM, row)

    @pl.when(t == 0)
    def _():
        gather(d1_ref, d2_ref, 0)

    def step(par):
        buf_a, buf_b = bufs[par]
        for buf in (buf_a, buf_b):
            pltpu.make_async_copy(y_hbm.at[pl.ds(0, TM * ROW_TILES)], buf, sem.at[par]).wait()

        @pl.when(t + 1 < ntiles)
        def _():
            gather(d1n_ref, d2n_ref, 1 - par)

        route = route_ref[0]
        w1 = route[:, 4:5]
        w2 = route[:, 5:6]
        a_lo, a_hi = _unpack_rows(_load_tiles_as_rows(buf_a, TM))
        b_lo, b_hi = _unpack_rows(_load_tiles_as_rows(buf_b, TM))
        ffn = jnp.concatenate([w1 * a_lo + w2 * b_lo, w1 * a_hi + w2 * b_hi], axis=1)
        x2 = x1_ref[0] + gate_ref[0] * ffn
        o_ref[0] = _rms(x2, gf_ref[...])

    for par in range(2):
        pl.when(t % 2 == par)(functools.partial(step, par))


def _combine(d1, d2, y, x1, route, gate, gf, tok_base):
    nb, seq, _ = x1.shape
    nt = seq // TM
    ntiles = nb * nt
    blk0 = tok_base // TM
    smem = lambda fn: pl.BlockSpec((TM,), fn, memory_space=pltpu.SMEM)
    cur = lambda t: (blk0 + t,)
    nxt = lambda t: (blk0 + jnp.minimum(t + 1, ntiles - 1),)
    tile = lambda t: (t // nt, t % nt, 0)
    grows = gate.shape[1]
    gate_spec = (pl.BlockSpec((1, 1, D_MODEL), lambda t: (t // nt, 0, 0)) if grows == 1
                 else pl.BlockSpec((1, TM, D_MODEL), tile))
    return pl.pallas_call(
        functools.partial(_combine_kernel, ntiles=ntiles),
        grid=(ntiles,),
        in_specs=[smem(cur), smem(cur), smem(nxt), smem(nxt), pl.BlockSpec(memory_space=pl.ANY),
                  pl.BlockSpec((1, TM, D_MODEL), tile),
                  pl.BlockSpec((1, TM, RLANES), tile),
                  gate_spec,
                  pl.BlockSpec((1, D_MODEL), lambda t: (0, 0))],
        out_specs=pl.BlockSpec((1, TM, D_MODEL), tile),
        out_shape=jax.ShapeDtypeStruct(x1.shape, F32),
        scratch_shapes=[pltpu.VMEM((TM * ROW_TILES, LANES), I32)] * 4 + [
            pltpu.SemaphoreType.DMA((2,))],
        compiler_params=pltpu.CompilerParams(
            dimension_semantics=("arbitrary",), vmem_limit_bytes=VMEM_LIMIT),
        name="combine",
    )(d1, d2, d1, d2, y, x1, route, gate, gf)


def _band_bias(rel_bias, rows, keys):
    n = rows - 1 + keys
    dist = WINDOW + rows - 1 - np.arange(n + 1)
    flipped = rel_bias[:, np.clip(dist, -MAX_REL, MAX_REL) + MAX_REL]
    skew = jnp.tile(flipped, (1, rows))[:, :rows * n].reshape(N_HEADS, rows, n)
    b = skew[:, :, rows - 1:rows - 1 + keys]
    return b.reshape(2, 4 * rows, keys)


def kernel(x_prompt, x_sample, cache_attn_k, cache_attn_v, state_conv, c_prompt, c_sample,
           w_ada, b_ada, norm1_g, norm2_g, w_in, rel_bias, conv_w, conv_b, w_pa, w_pb, w_o,
           w_group, b_group, w_expert, b_expert, w_e_gate, w_e_up, w_e_down, final_g):
    assert w_ada.shape[0] == 1, "single trunk layer"
    nb, seq, _ = x_prompt.shape
    nseq, slen, _ = x_sample.shape
    ntok_p = nb * seq
    ntok_s = nseq * slen
    ntok = ntok_p + ntok_s
    assert seq % TL == 0 and WINDOW % TL == 0 and seq % TM == 0 and ntok_s % TM == 0
    assert slen >= 2 and slen & (slen - 1) == 0 and slen % 16 == 0 and nseq % SAMPLE_SEQS == 0
    assert MOE_BLK % ZERO_ROWS == 0

    n_c = nb + nseq
    n_pad = -(-n_c // 8) * 8
    c_all = jnp.concatenate([c_prompt, c_sample, jnp.zeros((n_pad - n_c, D_MODEL), F32)], axis=0)
    mod = _ada(c_all, w_ada[0], b_ada[0]).reshape(n_pad, 6, D_MODEL)
    mod_p = mod[:nb]
    mod_s = mod[nb:n_c]

    win = w_in[0].astype(BF16)
    wpa = w_pa[0].astype(BF16)
    wpb = w_pb[0].astype(BF16)
    wo = w_o[0].astype(BF16)
    g1 = norm1_g[0].reshape(1, D_MODEL)
    g2 = norm2_g[0].reshape(1, D_MODEL)
    gf = final_g.reshape(1, D_MODEL)
    cw = jnp.concatenate([conv_w[0], jnp.zeros((8 - conv_w.shape[1], D_CONV), F32)], axis=0)
    cbias = conv_b[0].reshape(1, D_CONV)
    wr = jnp.concatenate([w_expert[0], w_group[0],
                          jnp.zeros((D_MODEL, RLANES - N_EXPERTS - N_GROUPS), F32)], axis=1)
    whi = wr.astype(BF16)
    br = jnp.concatenate([b_expert[0], b_group[0],
                          jnp.zeros((RLANES - N_EXPERTS - N_GROUPS,), F32)]).reshape(1, RLANES)
    bias_p = _band_bias(rel_bias[0] * LOG2E, CHUNK, BAND)
    bias_s = _band_bias(rel_bias[0] * LOG2E, slen, WINDOW + slen).reshape(
        N_HEADS * slen, WINDOW + slen)
    bias_old = bias_s[:, :WINDOW]
    bias_new = bias_s[:, WINDOW:]

    x1p, h2p, kp, vp, up8, route_p, info_p, cnt_p = _prompt_main(
        x_prompt, mod_p, g1, g2, win, bias_p, cw, cbias, wpa, wpb, wo, whi, br)

    st = state_conv[0]
    up1 = jnp.zeros((nseq, slen, D_CONV), F32).at[:, 0].set(st[:, 1]).reshape(ntok_s, D_CONV)
    up2 = (jnp.zeros((nseq, slen, D_CONV), F32).at[:, 0].set(st[:, 0]).at[:, 1].set(st[:, 1])
           .reshape(ntok_s, D_CONV))
    ck = jnp.transpose(cache_attn_k[0], (0, 2, 3, 1)).reshape(nseq, D_ATT, WINDOW)
    cv = jnp.transpose(cache_attn_v[0], (0, 2, 3, 1)).reshape(nseq, D_ATT, WINDOW)
    x1s, h2s, ks, vs, us, route_s, info_s, cnt = _sample_main(
        x_sample.reshape(ntok_s, D_MODEL), mod_s, ck, cv, up1, up2, cnt_p,
        g1, g2, win, bias_old, bias_new, cw, cbias, wpa, wpb, wo, whi, br, nseq, slen)

    info = jnp.concatenate([info_p[0:4], info_s[0:4]], axis=1).astype(jnp.int32)
    experts = info[0:2]
    ranks = info[2:4]
    counts = cnt[0, :N_EXPERTS].astype(jnp.int32)
    pcounts = (counts + MOE_BLK - 1) // MOE_BLK * MOE_BLK
    pend = jnp.cumsum(pcounts)
    pstart = pend - pcounts
    eids = jnp.arange(N_EXPERTS, dtype=jnp.int32)
    dest = jnp.sum(jnp.where(experts[..., None] == eids, pstart, 0), axis=-1) + ranks
    d1 = dest[0]
    d2 = dest[1]
    nblocks = (2 * ntok) // MOE_BLK + N_EXPERTS
    blk_start = jnp.arange(nblocks, dtype=jnp.int32) * MOE_BLK
    blk_e = jnp.minimum(jnp.sum((pend[None, :] <= blk_start[:, None]).astype(jnp.int32), axis=1),
                        N_EXPERTS - 1)
    nblk = (pend[-1:] // MOE_BLK).astype(jnp.int32)
    pend0 = jnp.concatenate([jnp.zeros((1,), jnp.int32), pend.astype(jnp.int32)])

    xs = _dispatch(pend0, counts, d1, d2, h2p, h2s, nblocks * MOE_BLK)
    blk_id = jnp.arange(nblocks, dtype=jnp.int32)
    first = (blk_id < nblk[0]) & ((blk_id == 0) | (blk_e != jnp.roll(blk_e, 1)))
    wslot = (jnp.cumsum(first.astype(jnp.int32)) - 1) % 2
    later_first = lax.cummin(jnp.where(first, blk_id, nblocks)[::-1])[::-1]
    next_first = jnp.concatenate([later_first[1:], jnp.full((1,), nblocks, jnp.int32)])
    next_e = jnp.where(next_first < nblocks, blk_e[jnp.minimum(next_first, nblocks - 1)], -1)
    y = _experts(blk_e, nblk, first.astype(jnp.int32), wslot.astype(jnp.int32),
                 next_e.astype(jnp.int32), xs, w_e_gate[0], w_e_up[0], w_e_down[0])

    y_prompt = _combine(d1, d2, y, x1p, route_p, mod_p[:, 5:6, :], gf, 0)
    gate_s = jnp.repeat(mod_s[:, 5, :], slen, axis=0).reshape(1, ntok_s, D_MODEL)
    y_sample = _combine(d1, d2, y, x1s.reshape(1, ntok_s, D_MODEL),
                        route_s.reshape(1, ntok_s, RLANES), gate_s, gf, ntok_p)

    new_k_p = kp.reshape(1, nb, WINDOW, N_HEADS, HEAD_DIM)
    new_v_p = vp.reshape(1, nb, WINDOW, N_HEADS, HEAD_DIM)
    new_conv_p = up8[:, 6:8, :].reshape(1, nb, 2, D_CONV)
    new_k_s = ks.reshape(1, nseq, slen, N_HEADS, HEAD_DIM)
    new_v_s = vs.reshape(1, nseq, slen, N_HEADS, HEAD_DIM)
    new_conv_s = us.reshape(nseq, slen, D_CONV)[:, slen - 2:, :].reshape(1, nseq, 2, D_CONV)
    return (y_prompt, y_sample.reshape(nseq, slen, D_MODEL), new_k_p, new_v_p, new_conv_p,
            new_k_s, new_v_s, new_conv_s)
```

```python
import functools

import numpy as np
import jax
import jax.numpy as jnp
from jax import lax
from jax.experimental import pallas as pl
from jax.experimental.pallas import tpu as pltpu

F32 = jnp.float32
BF16 = jnp.bfloat16
I32 = jnp.int32

D_MODEL = 1024
CHUNK = 64
LEFT = 8
WINDOW = LEFT * CHUNK
BAND = WINDOW + CHUNK
PAIR = 2 * CHUNK
PAIR_BAND = WINDOW + PAIR
N_HEADS = 8
HEAD_DIM = 64
D_ATT = N_HEADS * HEAD_DIM
QUAD = 256
MAX_REL = 128
D_CONV = 512
N_GROUPS = 4
EPG = 8
N_EXPERTS = 32
D_EXPERT = 512
EPS = 1e-6
NEG = -1e30
LOG2E = float(np.log2(np.e))
Q_SCALE = HEAD_DIM ** -0.5 * LOG2E

TL = 512
RING = WINDOW + TL
MOE_BLK = 512
TM = 512
SAMPLE_SEQS = 2
ZERO_ROWS = 64
RLANES = 128
LANES = 128
HALF = D_MODEL // 2
ROW_TILES = HALF // LANES
ISSUE_UNROLL = 8
VMEM_LIMIT = 56 * 1024 * 1024


def _const_spec(shape):
    nd = len(shape)
    return pl.BlockSpec(shape, lambda *_: (0,) * nd, pipeline_mode=pl.Buffered(1))


def _dot(a, b):
    return jnp.dot(a, b, preferred_element_type=F32)


def _sigmoid(x):
    return 1.0 / (1.0 + jnp.exp(-x))


def _rms(x, g):
    ms = jnp.mean(x * x, axis=-1, keepdims=True)
    return x * lax.rsqrt(ms + EPS) * g


def _pack_rows(val):
    lo = lax.bitcast_convert_type(val[:, :HALF], I32) + 0x8000
    hi = lax.bitcast_convert_type(val[:, HALF:], I32) + 0x8000
    return (hi & -65536) | lax.shift_right_logical(lo, 16)


def _unpack_rows(packed):
    lo = lax.bitcast_convert_type(lax.shift_left(packed, 16), F32)
    hi = lax.bitcast_convert_type(packed & -65536, F32)
    return lo, hi


def _store_rows_as_tiles(ref, packed):
    r = packed.shape[0]
    for c in range(ROW_TILES):
        ref[pl.ds(c, r, stride=ROW_TILES), :] = packed[:, c * LANES:(c + 1) * LANES]


def _load_tiles_as_rows(ref, r):
    return jnp.concatenate(
        [ref[pl.ds(c, r, stride=ROW_TILES), :] for c in range(ROW_TILES)], axis=1)


def _tile(ref, row):
    return ref.at[pl.ds(pl.multiple_of(row * ROW_TILES, ROW_TILES), ROW_TILES)]


def _ada_kernel(c_ref, w_ref, b_ref, o_ref):
    c = c_ref[...]
    s = c * _sigmoid(c)
    o_ref[...] = _dot(s.astype(BF16), w_ref[...].astype(BF16)) + b_ref[...]


def _ada(c_all, w_ada, b_ada):
    n = c_all.shape[0]
    nb = 1024
    return pl.pallas_call(
        _ada_kernel,
        grid=(6 * D_MODEL // nb,),
        in_specs=[pl.BlockSpec((n, D_MODEL), lambda i: (0, 0)),
                  pl.BlockSpec((D_MODEL, nb), lambda i: (0, i)),
                  pl.BlockSpec((1, nb), lambda i: (0, i))],
        out_specs=pl.BlockSpec((n, nb), lambda i: (0, i)),
        out_shape=jax.ShapeDtypeStruct((n, 6 * D_MODEL), F32),
        name="ada",
    )(c_all, w_ada, b_ada.reshape(1, -1))


def _attend(q, kb, vb, bias, lim):
    r = q.shape[0]
    nk = kb.shape[0]
    assert r & (r - 1) == 0
    qt = jnp.concatenate([q] * 4, axis=0)
    rowh = lax.broadcasted_iota(jnp.int32, (4 * r, QUAD), 0) >> (r.bit_length() - 1)
    laneh = lax.broadcasted_iota(jnp.int32, (4 * r, QUAD), 1) >> 6
    qm = jnp.where(rowh == laneh, qt, jnp.zeros_like(qt))
    s = lax.dot_general(qm, kb, (((1,), (1,)), ((), ())), preferred_element_type=F32)
    valid = None
    if lim is not None:
        valid = lax.broadcasted_iota(jnp.int32, (r, nk), 1) >= lim
    ps, inv_l = [], []
    for h in range(4):
        sh = s[h * r:(h + 1) * r] + bias[h * r:(h + 1) * r]
        if valid is not None:
            sh = jnp.where(valid, sh, NEG)
        m = jnp.max(sh, axis=1, keepdims=True)
        ph = jnp.exp2(sh - m)
        inv_l.append(1.0 / jnp.sum(ph, axis=1, keepdims=True))
        ps.append(ph.astype(BF16))
    o = _dot(jnp.concatenate(ps, axis=0), vb)
    lane_o = lax.broadcasted_iota(jnp.int32, (r, QUAD), 1) >> 6
    out = o[0:r] * inv_l[0]
    for h in range(1, 4):
        out = jnp.where(lane_o == h, o[h * r:(h + 1) * r] * inv_l[h], out)
    return out


def _route_stages(h2, whi_ref, br_ref, cnt, valid=None):
    r = h2.shape[0]
    lane = lax.broadcasted_iota(jnp.int32, (r, RLANES), 1)
    lane_f = lane.astype(F32)
    big = jnp.float32(1000.0)
    v = {}

    def logits():
        v["logits"] = _dot(h2.astype(BF16), whi_ref[...]) + br_ref[...]

    def group():
        lg = jnp.where((lane >= N_EXPERTS) & (lane < N_EXPERTS + N_GROUPS), v["logits"], NEG)
        mg = jnp.max(lg, axis=1, keepdims=True)
        v["gi"] = jnp.min(jnp.where(lg == mg, lane_f, big), axis=1, keepdims=True) - N_EXPERTS
        v["pg"] = 1.0 / jnp.sum(jnp.exp(lg - mg), axis=1, keepdims=True)

    def top1():
        grp_of_lane = (lane >> 3).astype(F32)
        le = jnp.where((lane < N_EXPERTS) & (grp_of_lane == v["gi"]), v["logits"], NEG)
        v["m1"] = jnp.max(le, axis=1, keepdims=True)
        v["i1"] = jnp.min(jnp.where(le == v["m1"], lane_f, big), axis=1, keepdims=True)
        v["le"] = le

    def top2():
        sel1 = lane_f == v["i1"]
        le2 = jnp.where(sel1, NEG, v["le"])
        m2 = jnp.max(le2, axis=1, keepdims=True)
        v["i2"] = jnp.min(jnp.where(le2 == m2, lane_f, big), axis=1, keepdims=True)
        rr = jnp.exp(m2 - v["m1"])
        inv = v["pg"] / (1.0 + rr)
        v["w1"] = inv
        v["w2"] = inv * rr
        v["sel1"] = sel1
        v["sel2"] = lane_f == v["i2"]

    def ranks():
        oh = jnp.where(v["sel1"] | v["sel2"], 1.0 if valid is None else valid, 0.0).astype(F32)
        ri = lax.broadcasted_iota(jnp.int32, (r, r), 0)
        ci = lax.broadcasted_iota(jnp.int32, (r, r), 1)
        tri = jnp.where(ri > ci, 1.0, 0.0).astype(BF16)
        v["before"] = _dot(tri, oh.astype(BF16)) + cnt
        v["new_cnt"] = cnt + jnp.sum(oh, axis=0, keepdims=True)

    def assemble():
        r1 = jnp.sum(jnp.where(v["sel1"], v["before"], 0.0), axis=1, keepdims=True)
        r2 = jnp.sum(jnp.where(v["sel2"], v["before"], 0.0), axis=1, keepdims=True)
        route = jnp.where(lane == 0, v["i1"], 0.0)
        route = jnp.where(lane == 1, v["i2"], route)
        route = jnp.where(lane == 2, r1, route)
        route = jnp.where(lane == 3, r2, route)
        route = jnp.where(lane == 4, v["w1"], route)
        route = jnp.where(lane == 5, v["w2"], route)
        return route, v["new_cnt"]

    return [logits, group, top1, top2, ranks, assemble]


def _route(h2, whi_ref, br_ref, cnt):
    stages = _route_stages(h2, whi_ref, br_ref, cnt)
    for stage in stages[:-1]:
        stage()
    return stages[-1]()


def _prompt_kernel(x_ref, mod_ref, g1_ref, g2_ref, win_ref, bias_ref, cw_ref, cbias_ref,
                   wpa_ref, wpb_ref, wo_ref, whi_ref, br_ref,
                   x1_ref, h2_ref, ko_ref, vo_ref, uo_ref, route_ref, cnt_ref,
                   kring, vring, att_s, ucarry, cnt_s, h2_prev, *, nt, ntiles):
    g = pl.program_id(0)

    @pl.when(g == 0)
    def _():
        cnt_s[...] = jnp.zeros_like(cnt_s)
        h2_prev[...] = jnp.zeros_like(h2_prev)

    @pl.when(g < ntiles)
    def _():
        _prompt_tile(g, nt, x_ref, mod_ref, g1_ref, g2_ref, win_ref, bias_ref, cw_ref, cbias_ref,
                     wpa_ref, wpb_ref, wo_ref, whi_ref, br_ref,
                     x1_ref, h2_ref, ko_ref, vo_ref, uo_ref, route_ref, cnt_ref,
                     kring, vring, att_s, ucarry, cnt_s, h2_prev)

    @pl.when(g == ntiles)
    def _():
        route, new_cnt = _route(h2_prev[...], whi_ref, br_ref, cnt_s[...])
        route_ref[0] = route
        cnt_ref[...] = new_cnt


def _prompt_tile(g, nt, x_ref, mod_ref, g1_ref, g2_ref, win_ref, bias_ref, cw_ref, cbias_ref,
                 wpa_ref, wpb_ref, wo_ref, whi_ref, br_ref,
                 x1_ref, h2_ref, ko_ref, vo_ref, uo_ref, route_ref, cnt_ref,
                 kring, vring, att_s, ucarry, cnt_s, h2_prev):
    j = g % nt

    @pl.when(j == 0)
    def _():
        kring[0:WINDOW, :] = jnp.zeros((WINDOW, D_ATT), BF16)
        vring[0:WINDOW, :] = jnp.zeros((WINDOW, D_ATT), BF16)
        ucarry[...] = jnp.zeros_like(ucarry)

    route_stages = _route_stages(h2_prev[...], whi_ref, br_ref, cnt_s[...],
                                 valid=jnp.where(g > 0, 1.0, 0.0).astype(F32))
    route_stages[0]()

    sh1 = mod_ref[0, 0:1, :]
    sc1 = mod_ref[0, 1:2, :]
    gt1 = mod_ref[0, 2:3, :]
    sh2 = mod_ref[0, 3:4, :]
    sc2 = mod_ref[0, 4:5, :]

    x = x_ref[0]
    h = _rms(x, g1_ref[...]) * (1.0 + sc1) + sh1
    hb = h.astype(BF16)

    qkv = _dot(hb, win_ref[:, 0:3 * D_ATT])
    q = (qkv[:, 0:D_ATT] * Q_SCALE).astype(BF16)
    k = qkv[:, D_ATT:2 * D_ATT]
    v = qkv[:, 2 * D_ATT:3 * D_ATT]
    ko_ref[0] = k
    vo_ref[0] = v
    kring[WINDOW:RING, :] = k.astype(BF16)
    vring[WINDOW:RING, :] = v.astype(BF16)

    base = j * TL
    for c in range(TL // PAIR):
        lim = WINDOW - (base + c * PAIR)
        for qd in range(2):
            ls = slice(qd * QUAD, (qd + 1) * QUAD)
            o = _attend(q[c * PAIR:(c + 1) * PAIR, ls],
                        kring[c * PAIR:c * PAIR + PAIR_BAND, ls],
                        vring[c * PAIR:c * PAIR + PAIR_BAND, ls],
                        bias_ref[qd], lim)
            att_s[c * PAIR:(c + 1) * PAIR, ls] = o.astype(BF16)
        if c + 1 < len(route_stages) - 1:
            route_stages[c + 1]()
    route, new_cnt = route_stages[-1]()
    route_ref[0] = route
    cnt_s[...] = new_cnt
    cnt_ref[...] = new_cnt

    kring[0:WINDOW, :] = kring[TL:RING, :]
    vring[0:WINDOW, :] = vring[TL:RING, :]

    cbcv = _dot(hb, win_ref[:, 3 * D_ATT:3 * D_ATT + 3 * D_CONV])
    gates = _dot(hb, win_ref[:, 3 * D_ATT + 3 * D_CONV:])
    cb = cbcv[:, 0:D_CONV]
    u = cbcv[:, D_CONV:2 * D_CONV] * cbcv[:, 2 * D_CONV:3 * D_CONV]
    row = lax.broadcasted_iota(jnp.int32, (8, D_CONV), 0)
    prev = ucarry[...]
    r1 = pltpu.roll(u, 1, axis=0)
    r2 = pltpu.roll(u, 2, axis=0)
    u_m1 = jnp.concatenate(
        [jnp.where(row < 1, pltpu.roll(prev, 1, axis=0), r1[0:8]), r1[8:]], axis=0)
    u_m2 = jnp.concatenate(
        [jnp.where(row < 2, pltpu.roll(prev, 2, axis=0), r2[0:8]), r2[8:]], axis=0)
    yc = cw_ref[0:1, :] * u_m2 + cw_ref[1:2, :] * u_m1 + cw_ref[2:3, :] * u + cbias_ref[...]
    conv_out = (cb * yc).astype(BF16)
    ucarry[...] = u[TL - 8:TL, :]
    uo_ref[0] = u[TL - 8:TL, :]

    pa = _dot(att_s[...], wpa_ref[...])
    pb = _dot(conv_out, wpb_ref[...])
    mixin = _sigmoid(gates[:, 0:D_MODEL]) * pa + _sigmoid(gates[:, D_MODEL:]) * pb
    mix = _dot(mixin.astype(BF16), wo_ref[...])
    x1 = x + gt1 * mix
    x1_ref[0] = x1
    h2 = _rms(x1, g2_ref[...]) * (1.0 + sc2) + sh2
    _store_rows_as_tiles(h2_ref, _pack_rows(h2))

    h2_prev[...] = h2


def _prompt_main(x, mod, g1, g2, win, bias_q, cw, cbias, wpa, wpb, wo, whi, br):
    nb, seq, _ = x.shape
    nt = seq // TL
    ntiles = nb * nt
    keep = WINDOW // TL
    cur = lambda g: jnp.minimum(g, ntiles - 1)
    prv = lambda g: jnp.maximum(g - 1, 0)
    tile = lambda g: (cur(g) // nt, cur(g) % nt, 0)
    last = lambda g: (cur(g) // nt, jnp.maximum(cur(g) % nt - (nt - keep), 0), 0)
    perb = lambda g: (cur(g) // nt, 0, 0)
    in_specs = [
        pl.BlockSpec((1, TL, D_MODEL), tile),
        pl.BlockSpec((1, 6, D_MODEL), perb),
        _const_spec(g1.shape), _const_spec(g2.shape), _const_spec(win.shape),
        _const_spec(bias_q.shape), _const_spec(cw.shape), _const_spec(cbias.shape),
        _const_spec(wpa.shape), _const_spec(wpb.shape), _const_spec(wo.shape),
        _const_spec(whi.shape), _const_spec(br.shape),
    ]
    out_specs = [
        pl.BlockSpec((1, TL, D_MODEL), tile),
        pl.BlockSpec((TL * ROW_TILES, LANES), lambda g: (cur(g), 0)),
        pl.BlockSpec((1, TL, D_ATT), last),
        pl.BlockSpec((1, TL, D_ATT), last),
        pl.BlockSpec((1, 8, D_CONV), perb),
        pl.BlockSpec((1, TL, RLANES), lambda g: (prv(g) // nt, prv(g) % nt, 0)),
        pl.BlockSpec((1, RLANES), lambda g: (0, 0)),
    ]
    out_shape = [
        jax.ShapeDtypeStruct((nb, seq, D_MODEL), F32),
        jax.ShapeDtypeStruct((nb * seq * ROW_TILES, LANES), I32),
        jax.ShapeDtypeStruct((nb, WINDOW, D_ATT), F32),
        jax.ShapeDtypeStruct((nb, WINDOW, D_ATT), F32),
        jax.ShapeDtypeStruct((nb, 8, D_CONV), F32),
        jax.ShapeDtypeStruct((nb, seq, RLANES), F32),
        jax.ShapeDtypeStruct((1, RLANES), F32),
    ]
    scratch = [
        pltpu.VMEM((RING, D_ATT), BF16), pltpu.VMEM((RING, D_ATT), BF16),
        pltpu.VMEM((TL, D_ATT), BF16), pltpu.VMEM((8, D_CONV), F32),
        pltpu.VMEM((1, RLANES), F32), pltpu.VMEM((TL, D_MODEL), F32),
    ]
    return pl.pallas_call(
        functools.partial(_prompt_kernel, nt=nt, ntiles=ntiles),
        grid=(ntiles + 1,),
        in_specs=in_specs, out_specs=out_specs, out_shape=out_shape,
        scratch_shapes=scratch,
        compiler_params=pltpu.CompilerParams(
            dimension_semantics=("arbitrary",), vmem_limit_bytes=VMEM_LIMIT),
        name="prompt_main",
    )(x, mod, g1, g2, win, bias_q, cw, cbias, wpa, wpb, wo, whi, br)


def _sample_kernel(x_ref, mod_ref, ck_ref, cv_ref, up1_ref, up2_ref, cnt_in_ref,
                   g1_ref, g2_ref, win_ref, bias_old_ref, bias_new_ref, cw_ref, cbias_ref,
                   wpa_ref, wpb_ref, wo_ref, whi_ref, br_ref,
                   x1_ref, h2_ref, ko_ref, vo_ref, uo_ref, route_ref, cnt_ref,
                   h_s, q_s, kn_s, vn_s, att_s, conv_s, h2_s, *, nseq, slen):
    n = pl.program_id(0)
    ntok = nseq * slen

    @pl.when(n == 0)
    def _():
        def norm_body(i, carry):
            rows = pl.ds(pl.multiple_of(i * slen, slen), slen)
            xi = x_ref[rows, :]
            m = mod_ref[i]
            hi = _rms(xi, g1_ref[...]) * (1.0 + m[1:2, :]) + m[0:1, :]
            h_s[rows, :] = hi.astype(BF16)
            return carry
        lax.fori_loop(0, nseq, norm_body, 0)
        hb = h_s[...]
        qkv = _dot(hb, win_ref[:, 0:3 * D_ATT])
        q_s[...] = (qkv[:, 0:D_ATT] * Q_SCALE).astype(BF16)
        k = qkv[:, D_ATT:2 * D_ATT]
        v = qkv[:, 2 * D_ATT:3 * D_ATT]
        ko_ref[...] = k
        vo_ref[...] = v
        kn_s[...] = k.astype(BF16)
        vn_s[...] = v.astype(BF16)

        cbcv = _dot(hb, win_ref[:, 3 * D_ATT:3 * D_ATT + 3 * D_CONV])
        cb = cbcv[:, 0:D_CONV]
        u = cbcv[:, D_CONV:2 * D_CONV] * cbcv[:, 2 * D_CONV:3 * D_CONV]
        pos = lax.broadcasted_iota(jnp.int32, (ntok, D_CONV), 0) & (slen - 1)
        u_m1 = jnp.where(pos < 1, up1_ref[...], pltpu.roll(u, 1, axis=0))
        u_m2 = jnp.where(pos < 2, up2_ref[...], pltpu.roll(u, 2, axis=0))
        yc = cw_ref[0:1, :] * u_m2 + cw_ref[1:2, :] * u_m1 + cw_ref[2:3, :] * u + cbias_ref[...]
        conv_s[...] = (cb * yc).astype(BF16)
        uo_ref[...] = u

    nt_dims = (((1,), (1,)), ((), ()))
    shape = (N_HEADS * slen, D_ATT)
    rowh = lax.broadcasted_iota(jnp.int32, shape, 0) >> (slen.bit_length() - 1)
    laneh = lax.broadcasted_iota(jnp.int32, shape, 1) >> 6
    lane_o = lax.broadcasted_iota(jnp.int32, (slen, D_ATT), 1) >> 6
    for sq in range(SAMPLE_SEQS):
        rows = pl.ds(pl.multiple_of((n * SAMPLE_SEQS + sq) * slen, slen), slen)
        qt = jnp.concatenate([q_s[rows, :]] * N_HEADS, axis=0)
        qm = jnp.where(rowh == laneh, qt, jnp.zeros_like(qt))
        s_old = _dot(qm, ck_ref[sq].astype(BF16)) + bias_old_ref[...]
        s_new = lax.dot_general(qm, kn_s[rows, :], nt_dims,
                                preferred_element_type=F32) + bias_new_ref[...]
        m = jnp.maximum(jnp.max(s_old, axis=1, keepdims=True),
                        jnp.max(s_new, axis=1, keepdims=True))
        p_old = jnp.exp2(s_old - m)
        p_new = jnp.exp2(s_new - m)
        l = jnp.sum(p_old, axis=1, keepdims=True) + jnp.sum(p_new, axis=1, keepdims=True)
        o = lax.dot_general(p_old.astype(BF16), cv_ref[sq].astype(BF16), nt_dims,
                            preferred_element_type=F32)
        o = (o + _dot(p_new.astype(BF16), vn_s[rows, :])) * (1.0 / l)
        att = o[0:slen]
        for h in range(1, N_HEADS):
            att = jnp.where(lane_o == h, o[h * slen:(h + 1) * slen], att)
        att_s[rows, :] = att.astype(BF16)

    @pl.when(n == nseq // SAMPLE_SEQS - 1)
    def _():
        gates = _dot(h_s[...], win_ref[:, 3 * D_ATT + 3 * D_CONV:])
        pa = _dot(att_s[...], wpa_ref[...])
        pb = _dot(conv_s[...], wpb_ref[...])
        mixin = _sigmoid(gates[:, 0:D_MODEL]) * pa + _sigmoid(gates[:, D_MODEL:]) * pb
        x1_ref[...] = _dot(mixin.astype(BF16), wo_ref[...])

        def res_body(i, carry):
            r = pl.ds(pl.multiple_of(i * slen, slen), slen)
            m = mod_ref[i]
            x1 = x_ref[r, :] + m[2:3, :] * x1_ref[r, :]
            x1_ref[r, :] = x1
            h2_s[r, :] = _rms(x1, g2_ref[...]) * (1.0 + m[4:5, :]) + m[3:4, :]
            return carry
        lax.fori_loop(0, nseq, res_body, 0)

        h2 = h2_s[...]
        _store_rows_as_tiles(h2_ref, _pack_rows(h2))
        route, new_cnt = _route(h2, whi_ref, br_ref, cnt_in_ref[...])
        route_ref[...] = route
        cnt_ref[...] = new_cnt


def _sample_main(x2d, mod, ck, cv, up1, up2, cnt_in, g1, g2, win, bias_old, bias_new, cw, cbias,
                 wpa, wpb, wo, whi, br, nseq, slen):
    ntok = nseq * slen
    args = (x2d, mod, ck, cv, up1, up2, cnt_in, g1, g2, win, bias_old, bias_new, cw, cbias,
            wpa, wpb, wo, whi, br)
    in_specs = []
    for idx, a in enumerate(args):
        if idx in (2, 3):
            in_specs.append(pl.BlockSpec((SAMPLE_SEQS,) + a.shape[1:], lambda n: (n, 0, 0)))
        else:
            in_specs.append(_const_spec(a.shape))
    whole = lambda shape: pl.BlockSpec(shape, lambda n: (0,) * len(shape))
    outs = [((ntok, D_MODEL), F32), ((ntok * ROW_TILES, LANES), I32), ((ntok, D_ATT), F32),
            ((ntok, D_ATT), F32), ((ntok, D_CONV), F32), ((ntok, RLANES), F32), ((1, RLANES), F32)]
    scratch = [
        pltpu.VMEM((ntok, D_MODEL), BF16), pltpu.VMEM((ntok, D_ATT), BF16),
        pltpu.VMEM((ntok, D_ATT), BF16), pltpu.VMEM((ntok, D_ATT), BF16),
        pltpu.VMEM((ntok, D_ATT), BF16), pltpu.VMEM((ntok, D_CONV), BF16),
        pltpu.VMEM((ntok, D_MODEL), F32),
    ]
    return pl.pallas_call(
        functools.partial(_sample_kernel, nseq=nseq, slen=slen),
        grid=(nseq // SAMPLE_SEQS,),
        in_specs=in_specs,
        out_specs=[whole(s) for s, _ in outs],
        out_shape=[jax.ShapeDtypeStruct(s, d) for s, d in outs],
        scratch_shapes=scratch,
        compiler_params=pltpu.CompilerParams(
            dimension_semantics=("arbitrary",), vmem_limit_bytes=VMEM_LIMIT),
        name="sample_main",
    )(*args)


def _issue_rows(n, body):
    def group(g, carry):
        for u in range(ISSUE_UNROLL):
            body(g * ISSUE_UNROLL + u, u)
        return carry
    lax.fori_loop(0, n // ISSUE_UNROLL, group, 0)


def _dispatch_kernel(pend_ref, cnt_ref, d1_ref, d2_ref, hp_ref, hs_ref, xs_out, zbuf, sem, zsem, *,
                     np_tiles, nslots):
    i = pl.program_id(0)
    zrows = ZERO_ROWS * ROW_TILES

    @pl.when(i == 0)
    def _():
        zbuf[...] = jnp.zeros_like(zbuf)

        def zcopy(piece):
            start = pl.multiple_of(piece * zrows, zrows)
            return pltpu.make_async_copy(zbuf, xs_out.at[pl.ds(start, zrows)], zsem)

        def pieces(e):
            lo = (pend_ref[e] + cnt_ref[jnp.minimum(e, N_EXPERTS - 1)]) // ZERO_ROWS
            hi = pend_ref[jnp.minimum(e + 1, N_EXPERTS)] // ZERO_ROWS
            lo = jnp.where(e == N_EXPERTS, pend_ref[N_EXPERTS] // ZERO_ROWS, lo)
            hi = jnp.where(e == N_EXPERTS, nslots // ZERO_ROWS, hi)
            return lo, hi

        def start_all(e, carry):
            lo, hi = pieces(e)
            return lax.fori_loop(lo, hi, lambda p, c: (zcopy(p).start(), c)[1], carry)

        def wait_all(e, carry):
            lo, hi = pieces(e)
            return lax.fori_loop(lo, hi, lambda p, c: (zcopy(p).wait(), c)[1], carry)
        lax.fori_loop(0, N_EXPERTS + 1, start_all, 0)
        lax.fori_loop(0, N_EXPERTS + 1, wait_all, 0)

    def scatter_tile(src):
        def row(r, u):
            pltpu.make_async_copy(_tile(src, r), _tile(xs_out, d1_ref[r]), sem).start(priority=u % 2)
            pltpu.make_async_copy(_tile(src, r), _tile(xs_out, d2_ref[r]), sem).start(
                priority=(u + 1) % 2)
        _issue_rows(TM, row)
        for _ in range(2):
            pltpu.make_async_copy(src, xs_out.at[pl.ds(0, TM * ROW_TILES)], sem).wait()

    @pl.when(i < np_tiles)
    def _():
        scatter_tile(hp_ref)

    @pl.when(i >= np_tiles)
    def _():
        scatter_tile(hs_ref)


def _dispatch(pend, counts, d1, d2, h2p, h2s, nslots):
    np_tiles = h2p.shape[0] // (TM * ROW_TILES)
    ns_tiles = h2s.shape[0] // (TM * ROW_TILES)
    smem_tile = pl.BlockSpec((TM,), lambda i, *_: (i,), memory_space=pltpu.SMEM)
    rows = TM * ROW_TILES
    return pl.pallas_call(
        functools.partial(_dispatch_kernel, np_tiles=np_tiles, nslots=nslots),
        grid_spec=pltpu.PrefetchScalarGridSpec(
            num_scalar_prefetch=2,
            grid=(np_tiles + ns_tiles,),
            in_specs=[smem_tile, smem_tile,
                      pl.BlockSpec((rows, LANES), lambda i, *_: (jnp.minimum(i, np_tiles - 1), 0)),
                      pl.BlockSpec((rows, LANES), lambda i, *_: (jnp.maximum(i - np_tiles, 0), 0))],
            out_specs=pl.BlockSpec(memory_space=pl.ANY),
            scratch_shapes=[pltpu.VMEM((ZERO_ROWS * ROW_TILES, LANES), I32),
                            pltpu.SemaphoreType.DMA(()), pltpu.SemaphoreType.DMA(())],
        ),
        out_shape=jax.ShapeDtypeStruct((nslots * ROW_TILES, LANES), I32),
        compiler_params=pltpu.CompilerParams(dimension_semantics=("arbitrary",)),
        name="dispatch",
    )(pend, counts, d1, d2, h2p, h2s)


def _expert_kernel(blk_e_ref, nblk_ref, first_ref, wslot_ref, next_e_ref,
                   xs_ref, wg_hbm, wu_hbm, wd_hbm, y_ref,
                   wg_f, wu_f, wd_f, wg_b, wu_b, wd_b, wsem):
    i = pl.program_id(0)
    live = i < nblk_ref[0]

    def weight_copies(e, slot):
        return (pltpu.make_async_copy(wg_hbm.at[e], wg_f.at[slot], wsem.at[slot, 0]),
                pltpu.make_async_copy(wu_hbm.at[e], wu_f.at[slot], wsem.at[slot, 1]),
                pltpu.make_async_copy(wd_hbm.at[e], wd_f.at[slot], wsem.at[slot, 2]))

    @pl.when(i == 0)
    def _():
        for cp in weight_copies(blk_e_ref[0], 0):
            cp.start()

    @pl.when(live & (first_ref[i] == 1))
    def _():
        slot = wslot_ref[i]
        for cp in weight_copies(blk_e_ref[i], slot):
            cp.wait()
        wg_b[...] = wg_f[slot].astype(BF16)
        wu_b[...] = wu_f[slot].astype(BF16)
        wd_b[...] = wd_f[slot].astype(BF16)

        @pl.when(next_e_ref[i] >= 0)
        def _():
            for cp in weight_copies(next_e_ref[i], 1 - slot):
                cp.start()

    @pl.when(live)
    def _():
        x_lo, x_hi = _unpack_rows(_load_tiles_as_rows(xs_ref, MOE_BLK))
        x_lo = x_lo.astype(BF16)
        x_hi = x_hi.astype(BF16)
        g = _dot(x_lo, wg_b[0:HALF, :]) + _dot(x_hi, wg_b[HALF:, :])
        u = _dot(x_lo, wu_b[0:HALF, :]) + _dot(x_hi, wu_b[HALF:, :])
        a = (g * _sigmoid(g)) * u
        _store_rows_as_tiles(y_ref, _pack_rows(_dot(a.astype(BF16), wd_b[...])))

    @pl.when(jnp.logical_not(live))
    def _():
        y_ref[...] = jnp.zeros_like(y_ref)


def _experts(blk_e, nblk, first, wslot, next_e, xs, wg, wu, wd):
    blk_rows = MOE_BLK * ROW_TILES
    nblocks = xs.shape[0] // blk_rows
    row_map = lambda i, be, nb, *_: (jnp.minimum(i, nb[0] - 1), 0)
    any_spec = pl.BlockSpec(memory_space=pl.ANY)
    return pl.pallas_call(
        _expert_kernel,
        grid_spec=pltpu.PrefetchScalarGridSpec(
            num_scalar_prefetch=5,
            grid=(nblocks,),
            in_specs=[pl.BlockSpec((blk_rows, LANES), row_map), any_spec, any_spec, any_spec],
            out_specs=pl.BlockSpec((blk_rows, LANES), lambda i, *_: (i, 0)),
            scratch_shapes=[pltpu.VMEM((2, D_MODEL, D_EXPERT), F32),
                            pltpu.VMEM((2, D_MODEL, D_EXPERT), F32),
                            pltpu.VMEM((2, D_EXPERT, D_MODEL), F32),
                            pltpu.VMEM((D_MODEL, D_EXPERT), BF16),
                            pltpu.VMEM((D_MODEL, D_EXPERT), BF16),
                            pltpu.VMEM((D_EXPERT, D_MODEL), BF16),
                            pltpu.SemaphoreType.DMA((2, 3))],
        ),
        out_shape=jax.ShapeDtypeStruct(xs.shape, I32),
        compiler_params=pltpu.CompilerParams(
            dimension_semantics=("arbitrary",), vmem_limit_bytes=VMEM_LIMIT),
        name="experts",
    )(blk_e, nblk, first, wslot, next_e, xs, wg, wu, wd)


def _combine_kernel(d1_ref, d2_ref, d1n_ref, d2n_ref, y_hbm, x1_ref, route_ref, gate_ref, gf_ref,
                    o_ref, a0, b0, a1, b1, sem, *, ntiles):
    t = pl.program_id(0)
    bufs = ((a0, b0), (a1, b1))

    def gather(i1_ref, i2_ref, par):
        buf_a, buf_b = bufs[par]

        def row(r, u):
            pltpu.make_async_copy(_tile(y_hbm, i1_ref[r]), _tile(buf_a, r),
                                  sem.at[par]).start(priority=u % 2)
            pltpu.make_async_copy(_tile(y_hbm, i2_ref[r]), _tile(buf_b, r),
                                  sem.at[par]).start(priority=(u + 1) % 2)
        _issue_rows(TM, row)

    @pl.when(t == 0)
    def _():
        gather(d1_ref, d2_ref, 0)

    def step(par):
        buf_a, buf_b = bufs[par]
        for buf in (buf_a, buf_b):
            pltpu.make_async_copy(y_hbm.at[pl.ds(0, TM * ROW_TILES)], buf, sem.at[par]).wait()

        @pl.when(t + 1 < ntiles)
        def _():
            gather(d1n_ref, d2n_ref, 1 - par)

        route = route_ref[0]
        w1 = route[:, 4:5]
        w2 = route[:, 5:6]
        a_lo, a_hi = _unpack_rows(_load_tiles_as_rows(buf_a, TM))
        b_lo, b_hi = _unpack_rows(_load_tiles_as_rows(buf_b, TM))
        ffn = jnp.concatenate([w1 * a_lo + w2 * b_lo, w1 * a_hi + w2 * b_hi], axis=1)
        x2 = x1_ref[0] + gate_ref[0] * ffn
        o_ref[0] = _rms(x2, gf_ref[...])

    for par in range(2):
        pl.when(t % 2 == par)(functools.partial(step, par))


def _combine(d1, d2, y, x1, route, gate, gf, tok_base):
    nb, seq, _ = x1.shape
    nt = seq // TM
    ntiles = nb * nt
    blk0 = tok_base // TM
    smem = lambda fn: pl.BlockSpec((TM,), fn, memory_space=pltpu.SMEM)
    cur = lambda t: (blk0 + t,)
    nxt = lambda t: (blk0 + jnp.minimum(t + 1, ntiles - 1),)
    tile = lambda t: (t // nt, t % nt, 0)
    grows = gate.shape[1]
    gate_spec = (pl.BlockSpec((1, 1, D_MODEL), lambda t: (t // nt, 0, 0)) if grows == 1
                 else pl.BlockSpec((1, TM, D_MODEL), tile))
    return pl.pallas_call(
        functools.partial(_combine_kernel, ntiles=ntiles),
        grid=(ntiles,),
        in_specs=[smem(cur), smem(cur), smem(nxt), smem(nxt), pl.BlockSpec(memory_space=pl.ANY),
                  pl.BlockSpec((1, TM, D_MODEL), tile),
                  pl.BlockSpec((1, TM, RLANES), tile),
                  gate_spec,
                  pl.BlockSpec((1, D_MODEL), lambda t: (0, 0))],
        out_specs=pl.BlockSpec((1, TM, D_MODEL), tile),
        out_shape=jax.ShapeDtypeStruct(x1.shape, F32),
        scratch_shapes=[pltpu.VMEM((TM * ROW_TILES, LANES), I32)] * 4 + [
            pltpu.SemaphoreType.DMA((2,))],
        compiler_params=pltpu.CompilerParams(
            dimension_semantics=("arbitrary",), vmem_limit_bytes=VMEM_LIMIT),
        name="combine",
    )(d1, d2, d1, d2, y, x1, route, gate, gf)


def _band_bias(rel_bias, rows, keys):
    n = rows - 1 + keys
    dist = WINDOW + rows - 1 - np.arange(n + 1)
    flipped = rel_bias[:, np.clip(dist, -MAX_REL, MAX_REL) + MAX_REL]
    skew = jnp.tile(flipped, (1, rows))[:, :rows * n].reshape(N_HEADS, rows, n)
    b = skew[:, :, rows - 1:rows - 1 + keys]
    return b.reshape(2, 4 * rows, keys)


def kernel(x_prompt, x_sample, cache_attn_k, cache_attn_v, state_conv, c_prompt, c_sample,
           w_ada, b_ada, norm1_g, norm2_g, w_in, rel_bias, conv_w, conv_b, w_pa, w_pb, w_o,
           w_group, b_group, w_expert, b_expert, w_e_gate, w_e_up, w_e_down, final_g):
    assert w_ada.shape[0] == 1, "single trunk layer"
    nb, seq, _ = x_prompt.shape
    nseq, slen, _ = x_sample.shape
    ntok_p = nb * seq
    ntok_s = nseq * slen
    ntok = ntok_p + ntok_s
    assert seq % TL == 0 and WINDOW % TL == 0 and seq % TM == 0 and ntok_s % TM == 0
    assert slen >= 2 and slen & (slen - 1) == 0 and slen % 16 == 0 and nseq % SAMPLE_SEQS == 0
    assert MOE_BLK % ZERO_ROWS == 0

    n_c = nb + nseq
    n_pad = -(-n_c // 8) * 8
    c_all = jnp.concatenate([c_prompt, c_sample, jnp.zeros((n_pad - n_c, D_MODEL), F32)], axis=0)
    mod = _ada(c_all, w_ada[0], b_ada[0]).reshape(n_pad, 6, D_MODEL)
    mod_p = mod[:nb]
    mod_s = mod[nb:n_c]

    win = w_in[0].astype(BF16)
    wpa = w_pa[0].astype(BF16)
    wpb = w_pb[0].astype(BF16)
    wo = w_o[0].astype(BF16)
    g1 = norm1_g[0].reshape(1, D_MODEL)
    g2 = norm2_g[0].reshape(1, D_MODEL)
    gf = final_g.reshape(1, D_MODEL)
    cw = jnp.concatenate([conv_w[0], jnp.zeros((8 - conv_w.shape[1], D_CONV), F32)], axis=0)
    cbias = conv_b[0].reshape(1, D_CONV)
    wr = jnp.concatenate([w_expert[0], w_group[0],
                          jnp.zeros((D_MODEL, RLANES - N_EXPERTS - N_GROUPS), F32)], axis=1)
    whi = wr.astype(BF16)
    br = jnp.concatenate([b_expert[0], b_group[0],
                          jnp.zeros((RLANES - N_EXPERTS - N_GROUPS,), F32)]).reshape(1, RLANES)
    in_band = np.arange(PAIR_BAND)[None, :] - (np.arange(PAIR)[:, None] // CHUNK) * CHUNK
    in_band = np.tile((in_band >= 0) & (in_band < BAND), (4, 1))
    bias_p = jnp.where(in_band, _band_bias(rel_bias[0] * LOG2E, PAIR, PAIR_BAND), NEG)
    bias_s = _band_bias(rel_bias[0] * LOG2E, slen, WINDOW + slen).reshape(
        N_HEADS * slen, WINDOW + slen)
    bias_old = bias_s[:, :WINDOW]
    bias_new = bias_s[:, WINDOW:]

    x1p, h2p, kp, vp, up8, route_p, cnt_p = _prompt_main(
        x_prompt, mod_p, g1, g2, win, bias_p, cw, cbias, wpa, wpb, wo, whi, br)

    st = state_conv[0]
    up1 = jnp.zeros((nseq, slen, D_CONV), F32).at[:, 0].set(st[:, 1]).reshape(ntok_s, D_CONV)
    up2 = (jnp.zeros((nseq, slen, D_CONV), F32).at[:, 0].set(st[:, 0]).at[:, 1].set(st[:, 1])
           .reshape(ntok_s, D_CONV))
    ck = jnp.transpose(cache_attn_k[0], (0, 2, 3, 1)).reshape(nseq, D_ATT, WINDOW)
    cv = jnp.transpose(cache_attn_v[0], (0, 2, 3, 1)).reshape(nseq, D_ATT, WINDOW)
    x1s, h2s, ks, vs, us, route_s, cnt = _sample_main(
        x_sample.reshape(ntok_s, D_MODEL), mod_s, ck, cv, up1, up2, cnt_p,
        g1, g2, win, bias_old, bias_new, cw, cbias, wpa, wpb, wo, whi, br, nseq, slen)

    route_all = jnp.concatenate([route_p.reshape(ntok_p, RLANES)[:, :4], route_s[:, :4]], axis=0)
    experts = route_all[:, 0:2].astype(jnp.int32)
    ranks = route_all[:, 2:4].astype(jnp.int32)
    counts = cnt[0, :N_EXPERTS].astype(jnp.int32)
    pcounts = (counts + MOE_BLK - 1) // MOE_BLK * MOE_BLK
    pend = jnp.cumsum(pcounts)
    pstart = pend - pcounts
    eids = jnp.arange(N_EXPERTS, dtype=jnp.int32)
    dest = jnp.sum(jnp.where(experts[..., None] == eids, pstart, 0), axis=-1) + ranks
    d1 = dest[:, 0]
    d2 = dest[:, 1]
    nblocks = (2 * ntok) // MOE_BLK + N_EXPERTS
    blk_start = jnp.arange(nblocks, dtype=jnp.int32) * MOE_BLK
    blk_e = jnp.minimum(jnp.sum((pend[None, :] <= blk_start[:, None]).astype(jnp.int32), axis=1),
                        N_EXPERTS - 1)
    nblk = (pend[-1:] // MOE_BLK).astype(jnp.int32)
    pend0 = jnp.concatenate([jnp.zeros((1,), jnp.int32), pend.astype(jnp.int32)])

    xs = _dispatch(pend0, counts, d1, d2, h2p, h2s, nblocks * MOE_BLK)
    blk_id = jnp.arange(nblocks, dtype=jnp.int32)
    first = (blk_id < nblk[0]) & ((blk_id == 0) | (blk_e != jnp.roll(blk_e, 1)))
    wslot = (jnp.cumsum(first.astype(jnp.int32)) - 1) % 2
    later_first = lax.cummin(jnp.where(first, blk_id, nblocks)[::-1])[::-1]
    next_first = jnp.concatenate([later_first[1:], jnp.full((1,), nblocks, jnp.int32)])
    next_e = jnp.where(next_first < nblocks, blk_e[jnp.minimum(next_first, nblocks - 1)], -1)
    y = _experts(blk_e, nblk, first.astype(jnp.int32), wslot.astype(jnp.int32),
                 next_e.astype(jnp.int32), xs, w_e_gate[0], w_e_up[0], w_e_down[0])

    y_prompt = _combine(d1, d2, y, x1p, route_p, mod_p[:, 5:6, :], gf, 0)
    gate_s = jnp.repeat(mod_s[:, 5, :], slen, axis=0).reshape(1, ntok_s, D_MODEL)
    y_sample = _combine(d1, d2, y, x1s.reshape(1, ntok_s, D_MODEL),
                        route_s.reshape(1, ntok_s, RLANES), gate_s, gf, ntok_p)

    new_k_p = kp.reshape(1, nb, WINDOW, N_HEADS, HEAD_DIM)
    new_v_p = vp.reshape(1, nb, WINDOW, N_HEADS, HEAD_DIM)
    new_conv_p = up8[:, 6:8, :].reshape(1, nb, 2, D_CONV)
    new_k_s = ks.reshape(1, nseq, slen, N_HEADS, HEAD_DIM)
    new_v_s = vs.reshape(1, nseq, slen, N_HEADS, HEAD_DIM)
    new_conv_s = us.reshape(nseq, slen, D_CONV)[:, slen - 2:, :].reshape(1, nseq, 2, D_CONV)
    return (y_prompt, y_sample.reshape(nseq, slen, D_MODEL), new_k_p, new_v_p, new_conv_p,
            new_k_s, new_v_s, new_conv_s)
```

```python
import functools

import numpy as np
import jax
import jax.numpy as jnp
from jax import lax
from jax.experimental import pallas as pl
from jax.experimental.pallas import tpu as pltpu

F32 = jnp.float32
BF16 = jnp.bfloat16
I32 = jnp.int32

D_MODEL = 1024
CHUNK = 64
LEFT = 8
WINDOW = LEFT * CHUNK
BAND = WINDOW + CHUNK
PAIR = 2 * CHUNK
PAIR_BAND = WINDOW + PAIR
N_HEADS = 8
HEAD_DIM = 64
HEAD_SHIFT = HEAD_DIM.bit_length() - 1
D_ATT = N_HEADS * HEAD_DIM
QUAD = 256
MAX_REL = 128
D_CONV = 512
N_GROUPS = 4
EPG = 8
N_EXPERTS = 32
D_EXPERT = 512
EPS = 1e-6
NEG = -1e30
LOG2E = float(np.log2(np.e))
Q_SCALE = HEAD_DIM ** -0.5 * LOG2E

TL = 512
RING = WINDOW + TL
MOE_BLK = 512
TM = 512
SAMPLE_SEQS = 2
ZERO_ROWS = 64
RLANES = 128
LANES = 128
HALF = D_MODEL // 2
ROW_TILES = HALF // LANES
ISSUE_UNROLL = 8
VMEM_LIMIT = 56 * 1024 * 1024


def _const_spec(shape):
    nd = len(shape)
    return pl.BlockSpec(shape, lambda *_: (0,) * nd, pipeline_mode=pl.Buffered(1))


def _dot(a, b):
    return jnp.dot(a, b, preferred_element_type=F32)


def _sigmoid(x):
    return 1.0 / (1.0 + jnp.exp(-x))


def _rms(x, g):
    ms = jnp.mean(x * x, axis=-1, keepdims=True)
    return x * lax.rsqrt(ms + EPS) * g


def _pack_rows(val):
    lo = lax.bitcast_convert_type(val[:, :HALF], I32) + 0x8000
    hi = lax.bitcast_convert_type(val[:, HALF:], I32) + 0x8000
    return (hi & -65536) | lax.shift_right_logical(lo, 16)


def _unpack_rows(packed):
    lo = lax.bitcast_convert_type(lax.shift_left(packed, 16), F32)
    hi = lax.bitcast_convert_type(packed & -65536, F32)
    return lo, hi


def _store_rows_as_tiles(ref, packed):
    r = packed.shape[0]
    for c in range(ROW_TILES):
        ref[pl.ds(c, r, stride=ROW_TILES), :] = packed[:, c * LANES:(c + 1) * LANES]


def _load_tiles_as_rows(ref, r):
    return jnp.concatenate(
        [ref[pl.ds(c, r, stride=ROW_TILES), :] for c in range(ROW_TILES)], axis=1)


def _tile(ref, row):
    return ref.at[pl.ds(pl.multiple_of(row * ROW_TILES, ROW_TILES), ROW_TILES)]


def _ada_kernel(c_ref, w_ref, b_ref, o_ref):
    c = c_ref[...]
    s = c * _sigmoid(c)
    o_ref[...] = _dot(s.astype(BF16), w_ref[...].astype(BF16)) + b_ref[...]


def _ada(c_all, w_ada, b_ada):
    n = c_all.shape[0]
    nb = 1024
    return pl.pallas_call(
        _ada_kernel,
        grid=(6 * D_MODEL // nb,),
        in_specs=[pl.BlockSpec((n, D_MODEL), lambda i: (0, 0)),
                  pl.BlockSpec((D_MODEL, nb), lambda i: (0, i)),
                  pl.BlockSpec((1, nb), lambda i: (0, i))],
        out_specs=pl.BlockSpec((n, nb), lambda i: (0, i)),
        out_shape=jax.ShapeDtypeStruct((n, 6 * D_MODEL), F32),
        name="ada",
    )(c_all, w_ada, b_ada.reshape(1, -1))


def _attend(q, kb, vb, bias, lim):
    r = q.shape[0]
    nk = kb.shape[0]
    assert r & (r - 1) == 0
    qt = jnp.concatenate([q] * 4, axis=0)
    rowh = lax.broadcasted_iota(jnp.int32, (4 * r, QUAD), 0) >> (r.bit_length() - 1)
    laneh = lax.broadcasted_iota(jnp.int32, (4 * r, QUAD), 1) >> HEAD_SHIFT
    qm = jnp.where(rowh == laneh, qt, jnp.zeros_like(qt))
    s = lax.dot_general(qm, kb, (((1,), (1,)), ((), ())), preferred_element_type=F32)
    valid = None
    if lim is not None:
        valid = lax.broadcasted_iota(jnp.int32, (r, nk), 1) >= lim
    ps, inv_l = [], []
    for h in range(4):
        sh = s[h * r:(h + 1) * r] + bias[h * r:(h + 1) * r]
        if valid is not None:
            sh = jnp.where(valid, sh, NEG)
        m = jnp.max(sh, axis=1, keepdims=True)
        ph = jnp.exp2(sh - m)
        inv_l.append(1.0 / jnp.sum(ph, axis=1, keepdims=True))
        ps.append(ph.astype(BF16))
    o = _dot(jnp.concatenate(ps, axis=0), vb)
    lane_o = lax.broadcasted_iota(jnp.int32, (r, QUAD), 1) >> HEAD_SHIFT
    out = o[0:r] * inv_l[0]
    for h in range(1, 4):
        out = jnp.where(lane_o == h, o[h * r:(h + 1) * r] * inv_l[h], out)
    return out


def _route_stages(h2, whi_ref, br_ref, cnt, valid=None):
    r = h2.shape[0]
    lane = lax.broadcasted_iota(jnp.int32, (r, RLANES), 1)
    lane_f = lane.astype(F32)
    big = jnp.float32(1000.0)
    v = {}

    def logits():
        v["logits"] = _dot(h2.astype(BF16), whi_ref[...]) + br_ref[...]

    def group():
        lg = jnp.where((lane >= N_EXPERTS) & (lane < N_EXPERTS + N_GROUPS), v["logits"], NEG)
        mg = jnp.max(lg, axis=1, keepdims=True)
        v["gi"] = jnp.min(jnp.where(lg == mg, lane_f, big), axis=1, keepdims=True) - N_EXPERTS
        v["pg"] = 1.0 / jnp.sum(jnp.exp(lg - mg), axis=1, keepdims=True)

    def top1():
        grp_of_lane = (lane >> (EPG.bit_length() - 1)).astype(F32)
        le = jnp.where((lane < N_EXPERTS) & (grp_of_lane == v["gi"]), v["logits"], NEG)
        v["m1"] = jnp.max(le, axis=1, keepdims=True)
        v["i1"] = jnp.min(jnp.where(le == v["m1"], lane_f, big), axis=1, keepdims=True)
        v["le"] = le

    def top2():
        sel1 = lane_f == v["i1"]
        le2 = jnp.where(sel1, NEG, v["le"])
        m2 = jnp.max(le2, axis=1, keepdims=True)
        v["i2"] = jnp.min(jnp.where(le2 == m2, lane_f, big), axis=1, keepdims=True)
        rr = jnp.exp(m2 - v["m1"])
        inv = v["pg"] / (1.0 + rr)
        v["w1"] = inv
        v["w2"] = inv * rr
        v["sel1"] = sel1
        v["sel2"] = lane_f == v["i2"]

    def ranks():
        oh = jnp.where(v["sel1"] | v["sel2"], 1.0 if valid is None else valid, 0.0).astype(F32)
        ri = lax.broadcasted_iota(jnp.int32, (r, r), 0)
        ci = lax.broadcasted_iota(jnp.int32, (r, r), 1)
        tri = jnp.where(ri > ci, 1.0, 0.0).astype(BF16)
        v["before"] = _dot(tri, oh.astype(BF16)) + cnt
        v["new_cnt"] = cnt + jnp.sum(oh, axis=0, keepdims=True)

    def assemble():
        r1 = jnp.sum(jnp.where(v["sel1"], v["before"], 0.0), axis=1, keepdims=True)
        r2 = jnp.sum(jnp.where(v["sel2"], v["before"], 0.0), axis=1, keepdims=True)
        route = jnp.where(lane == 0, v["i1"], 0.0)
        route = jnp.where(lane == 1, v["i2"], route)
        route = jnp.where(lane == 2, r1, route)
        route = jnp.where(lane == 3, r2, route)
        route = jnp.where(lane == 4, v["w1"], route)
        route = jnp.where(lane == 5, v["w2"], route)
        return route, v["new_cnt"]

    return [logits, group, top1, top2, ranks, assemble]


def _route(h2, whi_ref, br_ref, cnt):
    stages = _route_stages(h2, whi_ref, br_ref, cnt)
    for stage in stages[:-1]:
        stage()
    return stages[-1]()


def _prompt_kernel(x_ref, mod_ref, g1_ref, g2_ref, win_ref, bias_ref, cw_ref, cbias_ref,
                   wpa_ref, wpb_ref, wo_ref, whi_ref, br_ref,
                   x1_ref, h2_ref, ko_ref, vo_ref, uo_ref, route_ref, cnt_ref,
                   kring, vring, att_s, ucarry, cnt_s, h2_prev, *, nt, ntiles):
    g = pl.program_id(0)

    @pl.when(g == 0)
    def _():
        cnt_s[...] = jnp.zeros_like(cnt_s)
        h2_prev[...] = jnp.zeros_like(h2_prev)

    @pl.when(g < ntiles)
    def _():
        _prompt_tile(g, nt, x_ref, mod_ref, g1_ref, g2_ref, win_ref, bias_ref, cw_ref, cbias_ref,
                     wpa_ref, wpb_ref, wo_ref, whi_ref, br_ref,
                     x1_ref, h2_ref, ko_ref, vo_ref, uo_ref, route_ref, cnt_ref,
                     kring, vring, att_s, ucarry, cnt_s, h2_prev)

    @pl.when(g == ntiles)
    def _():
        route, new_cnt = _route(h2_prev[...], whi_ref, br_ref, cnt_s[...])
        route_ref[0] = route
        cnt_ref[...] = new_cnt


def _prompt_tile(g, nt, x_ref, mod_ref, g1_ref, g2_ref, win_ref, bias_ref, cw_ref, cbias_ref,
                 wpa_ref, wpb_ref, wo_ref, whi_ref, br_ref,
                 x1_ref, h2_ref, ko_ref, vo_ref, uo_ref, route_ref, cnt_ref,
                 kring, vring, att_s, ucarry, cnt_s, h2_prev):
    j = g % nt

    @pl.when(j == 0)
    def _():
        kring[0:WINDOW, :] = jnp.zeros((WINDOW, D_ATT), BF16)
        vring[0:WINDOW, :] = jnp.zeros((WINDOW, D_ATT), BF16)
        ucarry[...] = jnp.zeros_like(ucarry)

    route_stages = _route_stages(h2_prev[...], whi_ref, br_ref, cnt_s[...],
                                 valid=jnp.where(g > 0, 1.0, 0.0).astype(F32))
    route_stages[0]()

    sh1 = mod_ref[0, 0:1, :]
    sc1 = mod_ref[0, 1:2, :]
    gt1 = mod_ref[0, 2:3, :]
    sh2 = mod_ref[0, 3:4, :]
    sc2 = mod_ref[0, 4:5, :]

    x = x_ref[0]
    h = _rms(x, g1_ref[...]) * (1.0 + sc1) + sh1
    hb = h.astype(BF16)

    qkv = _dot(hb, win_ref[:, 0:3 * D_ATT])
    q = (qkv[:, 0:D_ATT] * Q_SCALE).astype(BF16)
    k = qkv[:, D_ATT:2 * D_ATT]
    v = qkv[:, 2 * D_ATT:3 * D_ATT]
    ko_ref[0] = k
    vo_ref[0] = v
    kring[WINDOW:RING, :] = k.astype(BF16)
    vring[WINDOW:RING, :] = v.astype(BF16)

    base = j * TL
    for c in range(TL // PAIR):
        lim = WINDOW - (base + c * PAIR)
        for qd in range(2):
            ls = slice(qd * QUAD, (qd + 1) * QUAD)
            o = _attend(q[c * PAIR:(c + 1) * PAIR, ls],
                        kring[c * PAIR:c * PAIR + PAIR_BAND, ls],
                        vring[c * PAIR:c * PAIR + PAIR_BAND, ls],
                        bias_ref[qd], lim)
            att_s[c * PAIR:(c + 1) * PAIR, ls] = o.astype(BF16)
        if c + 1 < len(route_stages) - 1:
            route_stages[c + 1]()
    route, new_cnt = route_stages[-1]()
    route_ref[0] = route
    cnt_s[...] = new_cnt
    cnt_ref[...] = new_cnt

    kring[0:WINDOW, :] = kring[TL:RING, :]
    vring[0:WINDOW, :] = vring[TL:RING, :]

    cbcv = _dot(hb, win_ref[:, 3 * D_ATT:3 * D_ATT + 3 * D_CONV])
    gates = _dot(hb, win_ref[:, 3 * D_ATT + 3 * D_CONV:])
    cb = cbcv[:, 0:D_CONV]
    u = cbcv[:, D_CONV:2 * D_CONV] * cbcv[:, 2 * D_CONV:3 * D_CONV]
    row = lax.broadcasted_iota(jnp.int32, (8, D_CONV), 0)
    prev = ucarry[...]
    r1 = pltpu.roll(u, 1, axis=0)
    r2 = pltpu.roll(u, 2, axis=0)
    u_m1 = jnp.concatenate(
        [jnp.where(row < 1, pltpu.roll(prev, 1, axis=0), r1[0:8]), r1[8:]], axis=0)
    u_m2 = jnp.concatenate(
        [jnp.where(row < 2, pltpu.roll(prev, 2, axis=0), r2[0:8]), r2[8:]], axis=0)
    yc = cw_ref[0:1, :] * u_m2 + cw_ref[1:2, :] * u_m1 + cw_ref[2:3, :] * u + cbias_ref[...]
    conv_out = (cb * yc).astype(BF16)
    ucarry[...] = u[TL - 8:TL, :]
    uo_ref[0] = u[TL - 8:TL, :]

    pa = _dot(att_s[...], wpa_ref[...])
    pb = _dot(conv_out, wpb_ref[...])
    mixin = _sigmoid(gates[:, 0:D_MODEL]) * pa + _sigmoid(gates[:, D_MODEL:]) * pb
    mix = _dot(mixin.astype(BF16), wo_ref[...])
    x1 = x + gt1 * mix
    x1_ref[0] = x1
    h2 = _rms(x1, g2_ref[...]) * (1.0 + sc2) + sh2
    _store_rows_as_tiles(h2_ref, _pack_rows(h2))

    h2_prev[...] = h2


def _prompt_main(x, mod, g1, g2, win, bias_q, cw, cbias, wpa, wpb, wo, whi, br):
    nb, seq, _ = x.shape
    nt = seq // TL
    ntiles = nb * nt
    keep = WINDOW // TL
    cur = lambda g: jnp.minimum(g, ntiles - 1)
    prv = lambda g: jnp.maximum(g - 1, 0)
    tile = lambda g: (cur(g) // nt, cur(g) % nt, 0)
    last = lambda g: (cur(g) // nt, jnp.maximum(cur(g) % nt - (nt - keep), 0), 0)
    perb = lambda g: (cur(g) // nt, 0, 0)
    in_specs = [
        pl.BlockSpec((1, TL, D_MODEL), tile),
        pl.BlockSpec((1, 6, D_MODEL), perb),
        _const_spec(g1.shape), _const_spec(g2.shape), _const_spec(win.shape),
        _const_spec(bias_q.shape), _const_spec(cw.shape), _const_spec(cbias.shape),
        _const_spec(wpa.shape), _const_spec(wpb.shape), _const_spec(wo.shape),
        _const_spec(whi.shape), _const_spec(br.shape),
    ]
    out_specs = [
        pl.BlockSpec((1, TL, D_MODEL), tile),
        pl.BlockSpec((TL * ROW_TILES, LANES), lambda g: (cur(g), 0)),
        pl.BlockSpec((1, TL, D_ATT), last),
        pl.BlockSpec((1, TL, D_ATT), last),
        pl.BlockSpec((1, 8, D_CONV), perb),
        pl.BlockSpec((1, TL, RLANES), lambda g: (prv(g) // nt, prv(g) % nt, 0)),
        pl.BlockSpec((1, RLANES), lambda g: (0, 0)),
    ]
    out_shape = [
        jax.ShapeDtypeStruct((nb, seq, D_MODEL), F32),
        jax.ShapeDtypeStruct((nb * seq * ROW_TILES, LANES), I32),
        jax.ShapeDtypeStruct((nb, WINDOW, D_ATT), F32),
        jax.ShapeDtypeStruct((nb, WINDOW, D_ATT), F32),
        jax.ShapeDtypeStruct((nb, 8, D_CONV), F32),
        jax.ShapeDtypeStruct((nb, seq, RLANES), F32),
        jax.ShapeDtypeStruct((1, RLANES), F32),
    ]
    scratch = [
        pltpu.VMEM((RING, D_ATT), BF16), pltpu.VMEM((RING, D_ATT), BF16),
        pltpu.VMEM((TL, D_ATT), BF16), pltpu.VMEM((8, D_CONV), F32),
        pltpu.VMEM((1, RLANES), F32), pltpu.VMEM((TL, D_MODEL), F32),
    ]
    return pl.pallas_call(
        functools.partial(_prompt_kernel, nt=nt, ntiles=ntiles),
        grid=(ntiles + 1,),
        in_specs=in_specs, out_specs=out_specs, out_shape=out_shape,
        scratch_shapes=scratch,
        compiler_params=pltpu.CompilerParams(
            dimension_semantics=("arbitrary",), vmem_limit_bytes=VMEM_LIMIT),
        name="prompt_main",
    )(x, mod, g1, g2, win, bias_q, cw, cbias, wpa, wpb, wo, whi, br)


def _sample_kernel(x_ref, mod_ref, ck_ref, cv_ref, up1_ref, up2_ref, cnt_in_ref,
                   g1_ref, g2_ref, win_ref, bias_old_ref, bias_new_ref, cw_ref, cbias_ref,
                   wpa_ref, wpb_ref, wo_ref, whi_ref, br_ref,
                   x1_ref, h2_ref, ko_ref, vo_ref, uo_ref, route_ref, cnt_ref,
                   h_s, q_s, kn_s, vn_s, att_s, conv_s, h2_s, *, nseq, slen):
    n = pl.program_id(0)
    ntok = nseq * slen

    @pl.when(n == 0)
    def _():
        def norm_body(i, carry):
            rows = pl.ds(pl.multiple_of(i * slen, slen), slen)
            xi = x_ref[rows, :]
            m = mod_ref[i]
            hi = _rms(xi, g1_ref[...]) * (1.0 + m[1:2, :]) + m[0:1, :]
            h_s[rows, :] = hi.astype(BF16)
            return carry
        lax.fori_loop(0, nseq, norm_body, 0)
        hb = h_s[...]
        qkv = _dot(hb, win_ref[:, 0:3 * D_ATT])
        q_s[...] = (qkv[:, 0:D_ATT] * Q_SCALE).astype(BF16)
        k = qkv[:, D_ATT:2 * D_ATT]
        v = qkv[:, 2 * D_ATT:3 * D_ATT]
        ko_ref[...] = k
        vo_ref[...] = v
        kn_s[...] = k.astype(BF16)
        vn_s[...] = v.astype(BF16)

        cbcv = _dot(hb, win_ref[:, 3 * D_ATT:3 * D_ATT + 3 * D_CONV])
        cb = cbcv[:, 0:D_CONV]
        u = cbcv[:, D_CONV:2 * D_CONV] * cbcv[:, 2 * D_CONV:3 * D_CONV]
        pos = lax.broadcasted_iota(jnp.int32, (ntok, D_CONV), 0) & (slen - 1)
        u_m1 = jnp.where(pos < 1, up1_ref[...], pltpu.roll(u, 1, axis=0))
        u_m2 = jnp.where(pos < 2, up2_ref[...], pltpu.roll(u, 2, axis=0))
        yc = cw_ref[0:1, :] * u_m2 + cw_ref[1:2, :] * u_m1 + cw_ref[2:3, :] * u + cbias_ref[...]
        conv_s[...] = (cb * yc).astype(BF16)
        uo_ref[...] = u

    nt_dims = (((1,), (1,)), ((), ()))
    shape = (N_HEADS * slen, D_ATT)
    rowh = lax.broadcasted_iota(jnp.int32, shape, 0) >> (slen.bit_length() - 1)
    laneh = lax.broadcasted_iota(jnp.int32, shape, 1) >> HEAD_SHIFT
    lane_o = lax.broadcasted_iota(jnp.int32, (slen, D_ATT), 1) >> HEAD_SHIFT
    for sq in range(SAMPLE_SEQS):
        rows = pl.ds(pl.multiple_of((n * SAMPLE_SEQS + sq) * slen, slen), slen)
        qt = jnp.concatenate([q_s[rows, :]] * N_HEADS, axis=0)
        qm = jnp.where(rowh == laneh, qt, jnp.zeros_like(qt))
        s_old = _dot(qm, ck_ref[sq].astype(BF16)) + bias_old_ref[...]
        s_new = lax.dot_general(qm, kn_s[rows, :], nt_dims,
                                preferred_element_type=F32) + bias_new_ref[...]
        m = jnp.maximum(jnp.max(s_old, axis=1, keepdims=True),
                        jnp.max(s_new, axis=1, keepdims=True))
        p_old = jnp.exp2(s_old - m)
        p_new = jnp.exp2(s_new - m)
        l = jnp.sum(p_old, axis=1, keepdims=True) + jnp.sum(p_new, axis=1, keepdims=True)
        o = lax.dot_general(p_old.astype(BF16), cv_ref[sq].astype(BF16), nt_dims,
                            preferred_element_type=F32)
        o = (o + _dot(p_new.astype(BF16), vn_s[rows, :])) * (1.0 / l)
        att = o[0:slen]
        for h in range(1, N_HEADS):
            att = jnp.where(lane_o == h, o[h * slen:(h + 1) * slen], att)
        att_s[rows, :] = att.astype(BF16)

    @pl.when(n == nseq // SAMPLE_SEQS - 1)
    def _():
        gates = _dot(h_s[...], win_ref[:, 3 * D_ATT + 3 * D_CONV:])
        pa = _dot(att_s[...], wpa_ref[...])
        pb = _dot(conv_s[...], wpb_ref[...])
        mixin = _sigmoid(gates[:, 0:D_MODEL]) * pa + _sigmoid(gates[:, D_MODEL:]) * pb
        x1_ref[...] = _dot(mixin.astype(BF16), wo_ref[...])

        def res_body(i, carry):
            r = pl.ds(pl.multiple_of(i * slen, slen), slen)
            m = mod_ref[i]
            x1 = x_ref[r, :] + m[2:3, :] * x1_ref[r, :]
            x1_ref[r, :] = x1
            h2_s[r, :] = _rms(x1, g2_ref[...]) * (1.0 + m[4:5, :]) + m[3:4, :]
            return carry
        lax.fori_loop(0, nseq, res_body, 0)

        h2 = h2_s[...]
        _store_rows_as_tiles(h2_ref, _pack_rows(h2))
        route, new_cnt = _route(h2, whi_ref, br_ref, cnt_in_ref[...])
        route_ref[...] = route
        cnt_ref[...] = new_cnt


def _sample_main(x2d, mod, ck, cv, up1, up2, cnt_in, g1, g2, win, bias_old, bias_new, cw, cbias,
                 wpa, wpb, wo, whi, br, nseq, slen):
    ntok = nseq * slen
    args = (x2d, mod, ck, cv, up1, up2, cnt_in, g1, g2, win, bias_old, bias_new, cw, cbias,
            wpa, wpb, wo, whi, br)
    in_specs = []
    for idx, a in enumerate(args):
        if idx in (2, 3):
            in_specs.append(pl.BlockSpec((SAMPLE_SEQS,) + a.shape[1:], lambda n: (n, 0, 0)))
        else:
            in_specs.append(_const_spec(a.shape))
    whole = lambda shape: pl.BlockSpec(shape, lambda n: (0,) * len(shape))
    outs = [((ntok, D_MODEL), F32), ((ntok * ROW_TILES, LANES), I32), ((ntok, D_ATT), F32),
            ((ntok, D_ATT), F32), ((ntok, D_CONV), F32), ((ntok, RLANES), F32), ((1, RLANES), F32)]
    scratch = [
        pltpu.VMEM((ntok, D_MODEL), BF16), pltpu.VMEM((ntok, D_ATT), BF16),
        pltpu.VMEM((ntok, D_ATT), BF16), pltpu.VMEM((ntok, D_ATT), BF16),
        pltpu.VMEM((ntok, D_ATT), BF16), pltpu.VMEM((ntok, D_CONV), BF16),
        pltpu.VMEM((ntok, D_MODEL), F32),
    ]
    return pl.pallas_call(
        functools.partial(_sample_kernel, nseq=nseq, slen=slen),
        grid=(nseq // SAMPLE_SEQS,),
        in_specs=in_specs,
        out_specs=[whole(s) for s, _ in outs],
        out_shape=[jax.ShapeDtypeStruct(s, d) for s, d in outs],
        scratch_shapes=scratch,
        compiler_params=pltpu.CompilerParams(
            dimension_semantics=("arbitrary",), vmem_limit_bytes=VMEM_LIMIT),
        name="sample_main",
    )(*args)


def _issue_rows(n, body):
    def group(g, carry):
        for u in range(ISSUE_UNROLL):
            body(g * ISSUE_UNROLL + u, u)
        return carry
    lax.fori_loop(0, n // ISSUE_UNROLL, group, 0)


def _dispatch_kernel(pend_ref, cnt_ref, d1_ref, d2_ref, hp_ref, hs_ref, xs_out, zbuf, sem, zsem, *,
                     np_tiles, nslots):
    i = pl.program_id(0)
    zrows = ZERO_ROWS * ROW_TILES

    @pl.when(i == 0)
    def _():
        zbuf[...] = jnp.zeros_like(zbuf)

        def zcopy(piece):
            start = pl.multiple_of(piece * zrows, zrows)
            return pltpu.make_async_copy(zbuf, xs_out.at[pl.ds(start, zrows)], zsem)

        def pieces(e):
            lo = (pend_ref[e] + cnt_ref[jnp.minimum(e, N_EXPERTS - 1)]) // ZERO_ROWS
            hi = pend_ref[jnp.minimum(e + 1, N_EXPERTS)] // ZERO_ROWS
            lo = jnp.where(e == N_EXPERTS, pend_ref[N_EXPERTS] // ZERO_ROWS, lo)
            hi = jnp.where(e == N_EXPERTS, nslots // ZERO_ROWS, hi)
            return lo, hi

        def start_all(e, carry):
            lo, hi = pieces(e)
            return lax.fori_loop(lo, hi, lambda p, c: (zcopy(p).start(), c)[1], carry)

        def wait_all(e, carry):
            lo, hi = pieces(e)
            return lax.fori_loop(lo, hi, lambda p, c: (zcopy(p).wait(), c)[1], carry)
        lax.fori_loop(0, N_EXPERTS + 1, start_all, 0)
        lax.fori_loop(0, N_EXPERTS + 1, wait_all, 0)

    def scatter_tile(src):
        def row(r, u):
            pltpu.make_async_copy(_tile(src, r), _tile(xs_out, d1_ref[r]), sem).start(priority=u % 2)
            pltpu.make_async_copy(_tile(src, r), _tile(xs_out, d2_ref[r]), sem).start(
                priority=(u + 1) % 2)
        _issue_rows(TM, row)
        for _ in range(2):
            pltpu.make_async_copy(src, xs_out.at[pl.ds(0, TM * ROW_TILES)], sem).wait()

    @pl.when(i < np_tiles)
    def _():
        scatter_tile(hp_ref)

    @pl.when(i >= np_tiles)
    def _():
        scatter_tile(hs_ref)


def _dispatch(pend, counts, d1, d2, h2p, h2s, nslots):
    np_tiles = h2p.shape[0] // (TM * ROW_TILES)
    ns_tiles = h2s.shape[0] // (TM * ROW_TILES)
    smem_tile = pl.BlockSpec((TM,), lambda i, *_: (i,), memory_space=pltpu.SMEM)
    rows = TM * ROW_TILES
    return pl.pallas_call(
        functools.partial(_dispatch_kernel, np_tiles=np_tiles, nslots=nslots),
        grid_spec=pltpu.PrefetchScalarGridSpec(
            num_scalar_prefetch=2,
            grid=(np_tiles + ns_tiles,),
            in_specs=[smem_tile, smem_tile,
                      pl.BlockSpec((rows, LANES), lambda i, *_: (jnp.minimum(i, np_tiles - 1), 0)),
                      pl.BlockSpec((rows, LANES), lambda i, *_: (jnp.maximum(i - np_tiles, 0), 0))],
            out_specs=pl.BlockSpec(memory_space=pl.ANY),
            scratch_shapes=[pltpu.VMEM((ZERO_ROWS * ROW_TILES, LANES), I32),
                            pltpu.SemaphoreType.DMA(()), pltpu.SemaphoreType.DMA(())],
        ),
        out_shape=jax.ShapeDtypeStruct((nslots * ROW_TILES, LANES), I32),
        compiler_params=pltpu.CompilerParams(dimension_semantics=("arbitrary",)),
        name="dispatch",
    )(pend, counts, d1, d2, h2p, h2s)


def _expert_kernel(blk_e_ref, nblk_ref, first_ref, wslot_ref, next_e_ref,
                   xs_ref, wg_hbm, wu_hbm, wd_hbm, y_ref,
                   wg_f, wu_f, wd_f, wg_b, wu_b, wd_b, wsem):
    i = pl.program_id(0)
    live = i < nblk_ref[0]

    def weight_copies(e, slot):
        return (pltpu.make_async_copy(wg_hbm.at[e], wg_f.at[slot], wsem.at[slot, 0]),
                pltpu.make_async_copy(wu_hbm.at[e], wu_f.at[slot], wsem.at[slot, 1]),
                pltpu.make_async_copy(wd_hbm.at[e], wd_f.at[slot], wsem.at[slot, 2]))

    @pl.when(i == 0)
    def _():
        for cp in weight_copies(blk_e_ref[0], 0):
            cp.start()

    @pl.when(live & (first_ref[i] == 1))
    def _():
        slot = wslot_ref[i]
        for cp in weight_copies(blk_e_ref[i], slot):
            cp.wait()
        wg_b[...] = wg_f[slot].astype(BF16)
        wu_b[...] = wu_f[slot].astype(BF16)
        wd_b[...] = wd_f[slot].astype(BF16)

        @pl.when(next_e_ref[i] >= 0)
        def _():
            for cp in weight_copies(next_e_ref[i], 1 - slot):
                cp.start()

    @pl.when(live)
    def _():
        x_lo, x_hi = _unpack_rows(_load_tiles_as_rows(xs_ref, MOE_BLK))
        x_lo = x_lo.astype(BF16)
        x_hi = x_hi.astype(BF16)
        g = _dot(x_lo, wg_b[0:HALF, :]) + _dot(x_hi, wg_b[HALF:, :])
        u = _dot(x_lo, wu_b[0:HALF, :]) + _dot(x_hi, wu_b[HALF:, :])
        a = (g * _sigmoid(g)) * u
        _store_rows_as_tiles(y_ref, _pack_rows(_dot(a.astype(BF16), wd_b[...])))

    @pl.when(jnp.logical_not(live))
    def _():
        y_ref[...] = jnp.zeros_like(y_ref)


def _experts(blk_e, nblk, first, wslot, next_e, xs, wg, wu, wd):
    blk_rows = MOE_BLK * ROW_TILES
    nblocks = xs.shape[0] // blk_rows
    row_map = lambda i, be, nb, *_: (jnp.minimum(i, nb[0] - 1), 0)
    any_spec = pl.BlockSpec(memory_space=pl.ANY)
    return pl.pallas_call(
        _expert_kernel,
        grid_spec=pltpu.PrefetchScalarGridSpec(
            num_scalar_prefetch=5,
            grid=(nblocks,),
            in_specs=[pl.BlockSpec((blk_rows, LANES), row_map), any_spec, any_spec, any_spec],
            out_specs=pl.BlockSpec((blk_rows, LANES), lambda i, *_: (i, 0)),
            scratch_shapes=[pltpu.VMEM((2, D_MODEL, D_EXPERT), F32),
                            pltpu.VMEM((2, D_MODEL, D_EXPERT), F32),
                            pltpu.VMEM((2, D_EXPERT, D_MODEL), F32),
                            pltpu.VMEM((D_MODEL, D_EXPERT), BF16),
                            pltpu.VMEM((D_MODEL, D_EXPERT), BF16),
                            pltpu.VMEM((D_EXPERT, D_MODEL), BF16),
                            pltpu.SemaphoreType.DMA((2, 3))],
        ),
        out_shape=jax.ShapeDtypeStruct(xs.shape, I32),
        compiler_params=pltpu.CompilerParams(
            dimension_semantics=("arbitrary",), vmem_limit_bytes=VMEM_LIMIT),
        name="experts",
    )(blk_e, nblk, first, wslot, next_e, xs, wg, wu, wd)


def _combine_kernel(d1_ref, d2_ref, d1n_ref, d2n_ref, y_hbm, x1_ref, route_ref, gate_ref, gf_ref,
                    o_ref, a0, b0, a1, b1, sem, *, ntiles):
    t = pl.program_id(0)
    bufs = ((a0, b0), (a1, b1))

    def gather(i1_ref, i2_ref, par):
        buf_a, buf_b = bufs[par]

        def row(r, u):
            pltpu.make_async_copy(_tile(y_hbm, i1_ref[r]), _tile(buf_a, r),
                                  sem.at[par]).start(priority=u % 2)
            pltpu.make_async_copy(_tile(y_hbm, i2_ref[r]), _tile(buf_b, r),
                                  sem.at[par]).start(priority=(u + 1) % 2)
        _issue_rows(TM, row)

    @pl.when(t == 0)
    def _():
        gather(d1_ref, d2_ref, 0)

    def step(par):
        buf_a, buf_b = bufs[par]
        for buf in (buf_a, buf_b):
            pltpu.make_async_copy(y_hbm.at[pl.ds(0, TM * ROW_TILES)], buf, sem.at[par]).wait()

        @pl.when(t + 1 < ntiles)
        def _():
            gather(d1n_ref, d2n_ref, 1 - par)

        route = route_ref[0]
        w1 = route[:, 4:5]
        w2 = route[:, 5:6]
        a_lo, a_hi = _unpack_rows(_load_tiles_as_rows(buf_a, TM))
        b_lo, b_hi = _unpack_rows(_load_tiles_as_rows(buf_b, TM))
        ffn = jnp.concatenate([w1 * a_lo + w2 * b_lo, w1 * a_hi + w2 * b_hi], axis=1)
        x2 = x1_ref[0] + gate_ref[0] * ffn
        o_ref[0] = _rms(x2, gf_ref[...])

    for par in range(2):
        pl.when(t % 2 == par)(functools.partial(step, par))


def _combine(d1, d2, y, x1, route, gate, gf, tok_base):
    nb, seq, _ = x1.shape
    nt = seq // TM
    ntiles = nb * nt
    blk0 = tok_base // TM
    smem = lambda fn: pl.BlockSpec((TM,), fn, memory_space=pltpu.SMEM)
    cur = lambda t: (blk0 + t,)
    nxt = lambda t: (blk0 + jnp.minimum(t + 1, ntiles - 1),)
    tile = lambda t: (t // nt, t % nt, 0)
    grows = gate.shape[1]
    gate_spec = (pl.BlockSpec((1, 1, D_MODEL), lambda t: (t // nt, 0, 0)) if grows == 1
                 else pl.BlockSpec((1, TM, D_MODEL), tile))
    return pl.pallas_call(
        functools.partial(_combine_kernel, ntiles=ntiles),
        grid=(ntiles,),
        in_specs=[smem(cur), smem(cur), smem(nxt), smem(nxt), pl.BlockSpec(memory_space=pl.ANY),
                  pl.BlockSpec((1, TM, D_MODEL), tile),
                  pl.BlockSpec((1, TM, RLANES), tile),
                  gate_spec,
                  pl.BlockSpec((1, D_MODEL), lambda t: (0, 0))],
        out_specs=pl.BlockSpec((1, TM, D_MODEL), tile),
        out_shape=jax.ShapeDtypeStruct(x1.shape, F32),
        scratch_shapes=[pltpu.VMEM((TM * ROW_TILES, LANES), I32)] * 4 + [
            pltpu.SemaphoreType.DMA((2,))],
        compiler_params=pltpu.CompilerParams(
            dimension_semantics=("arbitrary",), vmem_limit_bytes=VMEM_LIMIT),
        name="combine",
    )(d1, d2, d1, d2, y, x1, route, gate, gf)


def _band_bias(rel_bias, rows, keys):
    n = rows - 1 + keys
    dist = WINDOW + rows - 1 - np.arange(n + 1)
    flipped = rel_bias[:, np.clip(dist, -MAX_REL, MAX_REL) + MAX_REL]
    skew = jnp.tile(flipped, (1, rows))[:, :rows * n].reshape(N_HEADS, rows, n)
    b = skew[:, :, rows - 1:rows - 1 + keys]
    return b.reshape(2, 4 * rows, keys)


def kernel(x_prompt, x_sample, cache_attn_k, cache_attn_v, state_conv, c_prompt, c_sample,
           w_ada, b_ada, norm1_g, norm2_g, w_in, rel_bias, conv_w, conv_b, w_pa, w_pb, w_o,
           w_group, b_group, w_expert, b_expert, w_e_gate, w_e_up, w_e_down, final_g):
    assert w_ada.shape[0] == 1, "single trunk layer"
    nb, seq, _ = x_prompt.shape
    nseq, slen, _ = x_sample.shape
    ntok_p = nb * seq
    ntok_s = nseq * slen
    ntok = ntok_p + ntok_s
    assert seq % TL == 0 and WINDOW % TL == 0 and seq % TM == 0 and ntok_s % TM == 0
    assert slen >= 2 and slen & (slen - 1) == 0 and slen % 16 == 0 and nseq % SAMPLE_SEQS == 0
    assert MOE_BLK % ZERO_ROWS == 0

    n_c = nb + nseq
    n_pad = -(-n_c // 8) * 8
    c_all = jnp.concatenate([c_prompt, c_sample, jnp.zeros((n_pad - n_c, D_MODEL), F32)], axis=0)
    mod = _ada(c_all, w_ada[0], b_ada[0]).reshape(n_pad, 6, D_MODEL)
    mod_p = mod[:nb]
    mod_s = mod[nb:n_c]

    win = w_in[0].astype(BF16)
    wpa = w_pa[0].astype(BF16)
    wpb = w_pb[0].astype(BF16)
    wo = w_o[0].astype(BF16)
    g1 = norm1_g[0].reshape(1, D_MODEL)
    g2 = norm2_g[0].reshape(1, D_MODEL)
    gf = final_g.reshape(1, D_MODEL)
    cw = jnp.concatenate([conv_w[0], jnp.zeros((8 - conv_w.shape[1], D_CONV), F32)], axis=0)
    cbias = conv_b[0].reshape(1, D_CONV)
    wr = jnp.concatenate([w_expert[0], w_group[0],
                          jnp.zeros((D_MODEL, RLANES - N_EXPERTS - N_GROUPS), F32)], axis=1)
    whi = wr.astype(BF16)
    br = jnp.concatenate([b_expert[0], b_group[0],
                          jnp.zeros((RLANES - N_EXPERTS - N_GROUPS,), F32)]).reshape(1, RLANES)
    in_band = np.arange(PAIR_BAND)[None, :] - (np.arange(PAIR)[:, None] // CHUNK) * CHUNK
    in_band = np.tile((in_band >= 0) & (in_band < BAND), (4, 1))
    bias_p = jnp.where(in_band, _band_bias(rel_bias[0] * LOG2E, PAIR, PAIR_BAND), NEG)
    bias_s = _band_bias(rel_bias[0] * LOG2E, slen, WINDOW + slen).reshape(
        N_HEADS * slen, WINDOW + slen)
    bias_old = bias_s[:, :WINDOW]
    bias_new = bias_s[:, WINDOW:]

    x1p, h2p, kp, vp, up8, route_p, cnt_p = _prompt_main(
        x_prompt, mod_p, g1, g2, win, bias_p, cw, cbias, wpa, wpb, wo, whi, br)

    st = state_conv[0]
    up1 = jnp.zeros((nseq, slen, D_CONV), F32).at[:, 0].set(st[:, 1]).reshape(ntok_s, D_CONV)
    up2 = (jnp.zeros((nseq, slen, D_CONV), F32).at[:, 0].set(st[:, 0]).at[:, 1].set(st[:, 1])
           .reshape(ntok_s, D_CONV))
    ck = jnp.transpose(cache_attn_k[0], (0, 2, 3, 1)).reshape(nseq, D_ATT, WINDOW)
    cv = jnp.transpose(cache_attn_v[0], (0, 2, 3, 1)).reshape(nseq, D_ATT, WINDOW)
    x1s, h2s, ks, vs, us, route_s, cnt = _sample_main(
        x_sample.reshape(ntok_s, D_MODEL), mod_s, ck, cv, up1, up2, cnt_p,
        g1, g2, win, bias_old, bias_new, cw, cbias, wpa, wpb, wo, whi, br, nseq, slen)

    route_all = jnp.concatenate([route_p.reshape(ntok_p, RLANES)[:, :4], route_s[:, :4]], axis=0)
    experts = route_all[:, 0:2].astype(jnp.int32)
    ranks = route_all[:, 2:4].astype(jnp.int32)
    counts = cnt[0, :N_EXPERTS].astype(jnp.int32)
    pcounts = (counts + MOE_BLK - 1) // MOE_BLK * MOE_BLK
    pend = jnp.cumsum(pcounts)
    pstart = pend - pcounts
    eids = jnp.arange(N_EXPERTS, dtype=jnp.int32)
    dest = jnp.sum(jnp.where(experts[..., None] == eids, pstart, 0), axis=-1) + ranks
    d1 = dest[:, 0]
    d2 = dest[:, 1]
    nblocks = (2 * ntok) // MOE_BLK + N_EXPERTS
    blk_start = jnp.arange(nblocks, dtype=jnp.int32) * MOE_BLK
    blk_e = jnp.minimum(jnp.sum((pend[None, :] <= blk_start[:, None]).astype(jnp.int32), axis=1),
                        N_EXPERTS - 1)
    nblk = (pend[-1:] // MOE_BLK).astype(jnp.int32)
    pend0 = jnp.concatenate([jnp.zeros((1,), jnp.int32), pend.astype(jnp.int32)])

    xs = _dispatch(pend0, counts, d1, d2, h2p, h2s, nblocks * MOE_BLK)
    blk_id = jnp.arange(nblocks, dtype=jnp.int32)
    first = (blk_id < nblk[0]) & ((blk_id == 0) | (blk_e != jnp.roll(blk_e, 1)))
    wslot = (jnp.cumsum(first.astype(jnp.int32)) - 1) % 2
    later_first = lax.cummin(jnp.where(first, blk_id, nblocks)[::-1])[::-1]
    next_first = jnp.concatenate([later_first[1:], jnp.full((1,), nblocks, jnp.int32)])
    next_e = jnp.where(next_first < nblocks, blk_e[jnp.minimum(next_first, nblocks - 1)], -1)
    y = _experts(blk_e, nblk, first.astype(jnp.int32), wslot.astype(jnp.int32),
                 next_e.astype(jnp.int32), xs, w_e_gate[0], w_e_up[0], w_e_down[0])

    y_prompt = _combine(d1, d2, y, x1p, route_p, mod_p[:, 5:6, :], gf, 0)
    gate_s = jnp.repeat(mod_s[:, 5, :], slen, axis=0).reshape(1, ntok_s, D_MODEL)
    y_sample = _combine(d1, d2, y, x1s.reshape(1, ntok_s, D_MODEL),
                        route_s.reshape(1, ntok_s, RLANES), gate_s, gf, ntok_p)

    new_k_p = kp.reshape(1, nb, WINDOW, N_HEADS, HEAD_DIM)
    new_v_p = vp.reshape(1, nb, WINDOW, N_HEADS, HEAD_DIM)
    new_conv_p = up8[:, 6:8, :].reshape(1, nb, 2, D_CONV)
    new_k_s = ks.reshape(1, nseq, slen, N_HEADS, HEAD_DIM)
    new_v_s = vs.reshape(1, nseq, slen, N_HEADS, HEAD_DIM)
    new_conv_s = us.reshape(nseq, slen, D_CONV)[:, slen - 2:, :].reshape(1, nseq, 2, D_CONV)
    return (y_prompt, y_sample.reshape(nseq, slen, D_MODEL), new_k_p, new_v_p, new_conv_p,
            new_k_s, new_v_s, new_conv_s)
```

```python
import functools

import numpy as np
import jax
import jax.numpy as jnp
from jax import lax
from jax.experimental import pallas as pl
from jax.experimental.pallas import tpu as pltpu

F32 = jnp.float32
BF16 = jnp.bfloat16
I32 = jnp.int32

D_MODEL = 1024
CHUNK = 64
LEFT = 8
WINDOW = LEFT * CHUNK
BAND = WINDOW + CHUNK
PAIR = 2 * CHUNK
PAIR_BAND = WINDOW + PAIR
N_HEADS = 8
HEAD_DIM = 64
HEAD_SHIFT = HEAD_DIM.bit_length() - 1
D_ATT = N_HEADS * HEAD_DIM
QUAD = 256
MAX_REL = 128
D_CONV = 512
N_GROUPS = 4
EPG = 8
N_EXPERTS = 32
D_EXPERT = 512
EPS = 1e-6
NEG = -1e30
LOG2E = float(np.log2(np.e))
Q_SCALE = HEAD_DIM ** -0.5 * LOG2E

TL = 512
RING = WINDOW + TL
MOE_BLK = 512
TM = 512
SAMPLE_SEQS = 2
ZERO_ROWS = 64
RLANES = 128
LANES = 128
HALF = D_MODEL // 2
ROW_TILES = HALF // LANES
ISSUE_UNROLL = 16
VMEM_LIMIT = 56 * 1024 * 1024


def _const_spec(shape):
    nd = len(shape)
    return pl.BlockSpec(shape, lambda *_: (0,) * nd, pipeline_mode=pl.Buffered(1))


def _dot(a, b):
    return jnp.dot(a, b, preferred_element_type=F32)


def _sigmoid(x):
    return 1.0 / (1.0 + jnp.exp(-x))


def _rms(x, g):
    ms = jnp.mean(x * x, axis=-1, keepdims=True)
    return x * lax.rsqrt(ms + EPS) * g


def _pack_rows(val):
    lo = lax.bitcast_convert_type(val[:, :HALF], I32) + 0x8000
    hi = lax.bitcast_convert_type(val[:, HALF:], I32) + 0x8000
    return (hi & -65536) | lax.shift_right_logical(lo, 16)


def _unpack_rows(packed):
    lo = lax.bitcast_convert_type(lax.shift_left(packed, 16), F32)
    hi = lax.bitcast_convert_type(packed & -65536, F32)
    return lo, hi


def _store_rows_as_tiles(ref, packed):
    r = packed.shape[0]
    for c in range(ROW_TILES):
        ref[pl.ds(c, r, stride=ROW_TILES), :] = packed[:, c * LANES:(c + 1) * LANES]


def _load_tiles_as_rows(ref, r):
    return jnp.concatenate(
        [ref[pl.ds(c, r, stride=ROW_TILES), :] for c in range(ROW_TILES)], axis=1)


def _tile(ref, row):
    return ref.at[pl.ds(pl.multiple_of(row * ROW_TILES, ROW_TILES), ROW_TILES)]


def _ada_kernel(c_ref, w_ref, b_ref, o_ref):
    c = c_ref[...]
    s = c * _sigmoid(c)
    o_ref[...] = _dot(s.astype(BF16), w_ref[...].astype(BF16)) + b_ref[...]


def _ada(c_all, w_ada, b_ada):
    n = c_all.shape[0]
    nb = 1024
    return pl.pallas_call(
        _ada_kernel,
        grid=(6 * D_MODEL // nb,),
        in_specs=[pl.BlockSpec((n, D_MODEL), lambda i: (0, 0)),
                  pl.BlockSpec((D_MODEL, nb), lambda i: (0, i)),
                  pl.BlockSpec((1, nb), lambda i: (0, i))],
        out_specs=pl.BlockSpec((n, nb), lambda i: (0, i)),
        out_shape=jax.ShapeDtypeStruct((n, 6 * D_MODEL), F32),
        name="ada",
    )(c_all, w_ada, b_ada.reshape(1, -1))


def _attend(q, kb, vb, bias, lim):
    r = q.shape[0]
    nk = kb.shape[0]
    assert r & (r - 1) == 0
    qt = jnp.concatenate([q] * 4, axis=0)
    rowh = lax.broadcasted_iota(jnp.int32, (4 * r, QUAD), 0) >> (r.bit_length() - 1)
    laneh = lax.broadcasted_iota(jnp.int32, (4 * r, QUAD), 1) >> HEAD_SHIFT
    qm = jnp.where(rowh == laneh, qt, jnp.zeros_like(qt))
    s = lax.dot_general(qm, kb, (((1,), (1,)), ((), ())), preferred_element_type=F32)
    valid = None
    if lim is not None:
        valid = lax.broadcasted_iota(jnp.int32, (r, nk), 1) >= lim
    ps, inv_l = [], []
    for h in range(4):
        sh = s[h * r:(h + 1) * r] + bias[h * r:(h + 1) * r]
        if valid is not None:
            sh = jnp.where(valid, sh, NEG)
        m = jnp.max(sh, axis=1, keepdims=True)
        ph = jnp.exp2(sh - m)
        inv_l.append(1.0 / jnp.sum(ph, axis=1, keepdims=True))
        ps.append(ph.astype(BF16))
    o = _dot(jnp.concatenate(ps, axis=0), vb)
    lane_o = lax.broadcasted_iota(jnp.int32, (r, QUAD), 1) >> HEAD_SHIFT
    out = o[0:r] * inv_l[0]
    for h in range(1, 4):
        out = jnp.where(lane_o == h, o[h * r:(h + 1) * r] * inv_l[h], out)
    return out


def _route_stages(h2, whi_ref, br_ref, cnt, valid=None):
    r = h2.shape[0]
    lane = lax.broadcasted_iota(jnp.int32, (r, RLANES), 1)
    lane_f = lane.astype(F32)
    big = jnp.float32(1000.0)
    v = {}

    def logits():
        v["logits"] = _dot(h2.astype(BF16), whi_ref[...]) + br_ref[...]

    def group():
        lg = jnp.where((lane >= N_EXPERTS) & (lane < N_EXPERTS + N_GROUPS), v["logits"], NEG)
        mg = jnp.max(lg, axis=1, keepdims=True)
        v["gi"] = jnp.min(jnp.where(lg == mg, lane_f, big), axis=1, keepdims=True) - N_EXPERTS
        v["pg"] = 1.0 / jnp.sum(jnp.exp(lg - mg), axis=1, keepdims=True)

    def top1():
        grp_of_lane = (lane >> (EPG.bit_length() - 1)).astype(F32)
        le = jnp.where((lane < N_EXPERTS) & (grp_of_lane == v["gi"]), v["logits"], NEG)
        v["m1"] = jnp.max(le, axis=1, keepdims=True)
        v["i1"] = jnp.min(jnp.where(le == v["m1"], lane_f, big), axis=1, keepdims=True)
        v["le"] = le

    def top2():
        sel1 = lane_f == v["i1"]
        le2 = jnp.where(sel1, NEG, v["le"])
        m2 = jnp.max(le2, axis=1, keepdims=True)
        v["i2"] = jnp.min(jnp.where(le2 == m2, lane_f, big), axis=1, keepdims=True)
        rr = jnp.exp(m2 - v["m1"])
        inv = v["pg"] / (1.0 + rr)
        v["w1"] = inv
        v["w2"] = inv * rr
        v["sel1"] = sel1
        v["sel2"] = lane_f == v["i2"]

    def ranks():
        oh = jnp.where(v["sel1"] | v["sel2"], 1.0 if valid is None else valid, 0.0).astype(F32)
        ri = lax.broadcasted_iota(jnp.int32, (r, r), 0)
        ci = lax.broadcasted_iota(jnp.int32, (r, r), 1)
        tri = jnp.where(ri > ci, 1.0, 0.0).astype(BF16)
        v["before"] = _dot(tri, oh.astype(BF16)) + cnt
        v["new_cnt"] = cnt + jnp.sum(oh, axis=0, keepdims=True)

    def assemble():
        r1 = jnp.sum(jnp.where(v["sel1"], v["before"], 0.0), axis=1, keepdims=True)
        r2 = jnp.sum(jnp.where(v["sel2"], v["before"], 0.0), axis=1, keepdims=True)
        route = jnp.where(lane == 0, v["i1"], 0.0)
        route = jnp.where(lane == 1, v["i2"], route)
        route = jnp.where(lane == 2, r1, route)
        route = jnp.where(lane == 3, r2, route)
        route = jnp.where(lane == 4, v["w1"], route)
        route = jnp.where(lane == 5, v["w2"], route)
        return route, v["new_cnt"]

    return [logits, group, top1, top2, ranks, assemble]


def _route(h2, whi_ref, br_ref, cnt):
    stages = _route_stages(h2, whi_ref, br_ref, cnt)
    for stage in stages[:-1]:
        stage()
    return stages[-1]()


def _prompt_kernel(x_ref, mod_ref, g1_ref, g2_ref, win_ref, bias_ref, cw_ref, cbias_ref,
                   wpa_ref, wpb_ref, wo_ref, whi_ref, br_ref,
                   x1_ref, h2_ref, ko_ref, vo_ref, uo_ref, route_ref, cnt_ref,
                   kring, vring, att_s, ucarry, cnt_s, h2_prev, *, nt, ntiles):
    g = pl.program_id(0)

    @pl.when(g == 0)
    def _():
        cnt_s[...] = jnp.zeros_like(cnt_s)
        h2_prev[...] = jnp.zeros_like(h2_prev)

    @pl.when(g < ntiles)
    def _():
        _prompt_tile(g, nt, x_ref, mod_ref, g1_ref, g2_ref, win_ref, bias_ref, cw_ref, cbias_ref,
                     wpa_ref, wpb_ref, wo_ref, whi_ref, br_ref,
                     x1_ref, h2_ref, ko_ref, vo_ref, uo_ref, route_ref, cnt_ref,
                     kring, vring, att_s, ucarry, cnt_s, h2_prev)

    @pl.when(g == ntiles)
    def _():
        route, new_cnt = _route(h2_prev[...], whi_ref, br_ref, cnt_s[...])
        route_ref[0] = route
        cnt_ref[...] = new_cnt


def _prompt_tile(g, nt, x_ref, mod_ref, g1_ref, g2_ref, win_ref, bias_ref, cw_ref, cbias_ref,
                 wpa_ref, wpb_ref, wo_ref, whi_ref, br_ref,
                 x1_ref, h2_ref, ko_ref, vo_ref, uo_ref, route_ref, cnt_ref,
                 kring, vring, att_s, ucarry, cnt_s, h2_prev):
    j = g % nt

    @pl.when(j == 0)
    def _():
        kring[0:WINDOW, :] = jnp.zeros((WINDOW, D_ATT), BF16)
        vring[0:WINDOW, :] = jnp.zeros((WINDOW, D_ATT), BF16)
        ucarry[...] = jnp.zeros_like(ucarry)

    route_stages = _route_stages(h2_prev[...], whi_ref, br_ref, cnt_s[...],
                                 valid=jnp.where(g > 0, 1.0, 0.0).astype(F32))
    route_stages[0]()

    sh1 = mod_ref[0, 0:1, :]
    sc1 = mod_ref[0, 1:2, :]
    gt1 = mod_ref[0, 2:3, :]
    sh2 = mod_ref[0, 3:4, :]
    sc2 = mod_ref[0, 4:5, :]

    x = x_ref[0]
    h = _rms(x, g1_ref[...]) * (1.0 + sc1) + sh1
    hb = h.astype(BF16)

    qkv = _dot(hb, win_ref[:, 0:3 * D_ATT])
    q = (qkv[:, 0:D_ATT] * Q_SCALE).astype(BF16)
    k = qkv[:, D_ATT:2 * D_ATT]
    v = qkv[:, 2 * D_ATT:3 * D_ATT]
    ko_ref[0] = k
    vo_ref[0] = v
    kring[WINDOW:RING, :] = k.astype(BF16)
    vring[WINDOW:RING, :] = v.astype(BF16)

    base = j * TL
    for c in range(TL // PAIR):
        lim = WINDOW - (base + c * PAIR)
        for qd in range(2):
            ls = slice(qd * QUAD, (qd + 1) * QUAD)
            o = _attend(q[c * PAIR:(c + 1) * PAIR, ls],
                        kring[c * PAIR:c * PAIR + PAIR_BAND, ls],
                        vring[c * PAIR:c * PAIR + PAIR_BAND, ls],
                        bias_ref[qd], lim)
            att_s[c * PAIR:(c + 1) * PAIR, ls] = o.astype(BF16)
        if c + 1 < len(route_stages) - 1:
            route_stages[c + 1]()
    route, new_cnt = route_stages[-1]()
    route_ref[0] = route
    cnt_s[...] = new_cnt
    cnt_ref[...] = new_cnt

    kring[0:WINDOW, :] = kring[TL:RING, :]
    vring[0:WINDOW, :] = vring[TL:RING, :]

    cbcv = _dot(hb, win_ref[:, 3 * D_ATT:3 * D_ATT + 3 * D_CONV])
    gates = _dot(hb, win_ref[:, 3 * D_ATT + 3 * D_CONV:])
    cb = cbcv[:, 0:D_CONV]
    u = cbcv[:, D_CONV:2 * D_CONV] * cbcv[:, 2 * D_CONV:3 * D_CONV]
    row = lax.broadcasted_iota(jnp.int32, (8, D_CONV), 0)
    prev = ucarry[...]
    r1 = pltpu.roll(u, 1, axis=0)
    r2 = pltpu.roll(u, 2, axis=0)
    u_m1 = jnp.concatenate(
        [jnp.where(row < 1, pltpu.roll(prev, 1, axis=0), r1[0:8]), r1[8:]], axis=0)
    u_m2 = jnp.concatenate(
        [jnp.where(row < 2, pltpu.roll(prev, 2, axis=0), r2[0:8]), r2[8:]], axis=0)
    yc = cw_ref[0:1, :] * u_m2 + cw_ref[1:2, :] * u_m1 + cw_ref[2:3, :] * u + cbias_ref[...]
    conv_out = (cb * yc).astype(BF16)
    ucarry[...] = u[TL - 8:TL, :]
    uo_ref[0] = u[TL - 8:TL, :]

    pa = _dot(att_s[...], wpa_ref[...])
    pb = _dot(conv_out, wpb_ref[...])
    mixin = _sigmoid(gates[:, 0:D_MODEL]) * pa + _sigmoid(gates[:, D_MODEL:]) * pb
    mix = _dot(mixin.astype(BF16), wo_ref[...])
    x1 = x + gt1 * mix
    x1_ref[0] = x1
    h2 = _rms(x1, g2_ref[...]) * (1.0 + sc2) + sh2
    _store_rows_as_tiles(h2_ref, _pack_rows(h2))

    h2_prev[...] = h2


def _prompt_main(x, mod, g1, g2, win, bias_q, cw, cbias, wpa, wpb, wo, whi, br):
    nb, seq, _ = x.shape
    nt = seq // TL
    ntiles = nb * nt
    keep = WINDOW // TL
    cur = lambda g: jnp.minimum(g, ntiles - 1)
    prv = lambda g: jnp.maximum(g - 1, 0)
    tile = lambda g: (cur(g) // nt, cur(g) % nt, 0)
    last = lambda g: (cur(g) // nt, jnp.maximum(cur(g) % nt - (nt - keep), 0), 0)
    perb = lambda g: (cur(g) // nt, 0, 0)
    in_specs = [
        pl.BlockSpec((1, TL, D_MODEL), tile),
        pl.BlockSpec((1, 6, D_MODEL), perb),
        _const_spec(g1.shape), _const_spec(g2.shape), _const_spec(win.shape),
        _const_spec(bias_q.shape), _const_spec(cw.shape), _const_spec(cbias.shape),
        _const_spec(wpa.shape), _const_spec(wpb.shape), _const_spec(wo.shape),
        _const_spec(whi.shape), _const_spec(br.shape),
    ]
    out_specs = [
        pl.BlockSpec((1, TL, D_MODEL), tile),
        pl.BlockSpec((TL * ROW_TILES, LANES), lambda g: (cur(g), 0)),
        pl.BlockSpec((1, TL, D_ATT), last),
        pl.BlockSpec((1, TL, D_ATT), last),
        pl.BlockSpec((1, 8, D_CONV), perb),
        pl.BlockSpec((1, TL, RLANES), lambda g: (prv(g) // nt, prv(g) % nt, 0)),
        pl.BlockSpec((1, RLANES), lambda g: (0, 0)),
    ]
    out_shape = [
        jax.ShapeDtypeStruct((nb, seq, D_MODEL), F32),
        jax.ShapeDtypeStruct((nb * seq * ROW_TILES, LANES), I32),
        jax.ShapeDtypeStruct((nb, WINDOW, D_ATT), F32),
        jax.ShapeDtypeStruct((nb, WINDOW, D_ATT), F32),
        jax.ShapeDtypeStruct((nb, 8, D_CONV), F32),
        jax.ShapeDtypeStruct((nb, seq, RLANES), F32),
        jax.ShapeDtypeStruct((1, RLANES), F32),
    ]
    scratch = [
        pltpu.VMEM((RING, D_ATT), BF16), pltpu.VMEM((RING, D_ATT), BF16),
        pltpu.VMEM((TL, D_ATT), BF16), pltpu.VMEM((8, D_CONV), F32),
        pltpu.VMEM((1, RLANES), F32), pltpu.VMEM((TL, D_MODEL), F32),
    ]
    return pl.pallas_call(
        functools.partial(_prompt_kernel, nt=nt, ntiles=ntiles),
        grid=(ntiles + 1,),
        in_specs=in_specs, out_specs=out_specs, out_shape=out_shape,
        scratch_shapes=scratch,
        compiler_params=pltpu.CompilerParams(
            dimension_semantics=("arbitrary",), vmem_limit_bytes=VMEM_LIMIT),
        name="prompt_main",
    )(x, mod, g1, g2, win, bias_q, cw, cbias, wpa, wpb, wo, whi, br)


def _sample_kernel(x_ref, mod_ref, ck_ref, cv_ref, up1_ref, up2_ref, cnt_in_ref,
                   g1_ref, g2_ref, win_ref, bias_old_ref, bias_new_ref, cw_ref, cbias_ref,
                   wpa_ref, wpb_ref, wo_ref, whi_ref, br_ref,
                   x1_ref, h2_ref, ko_ref, vo_ref, uo_ref, route_ref, cnt_ref,
                   h_s, q_s, kn_s, vn_s, att_s, conv_s, h2_s, *, nseq, slen):
    n = pl.program_id(0)
    ntok = nseq * slen

    @pl.when(n == 0)
    def _():
        def norm_body(i, carry):
            rows = pl.ds(pl.multiple_of(i * slen, slen), slen)
            xi = x_ref[rows, :]
            m = mod_ref[i]
            hi = _rms(xi, g1_ref[...]) * (1.0 + m[1:2, :]) + m[0:1, :]
            h_s[rows, :] = hi.astype(BF16)
            return carry
        lax.fori_loop(0, nseq, norm_body, 0)
        hb = h_s[...]
        qkv = _dot(hb, win_ref[:, 0:3 * D_ATT])
        q_s[...] = (qkv[:, 0:D_ATT] * Q_SCALE).astype(BF16)
        k = qkv[:, D_ATT:2 * D_ATT]
        v = qkv[:, 2 * D_ATT:3 * D_ATT]
        ko_ref[...] = k
        vo_ref[...] = v
        kn_s[...] = k.astype(BF16)
        vn_s[...] = v.astype(BF16)

        cbcv = _dot(hb, win_ref[:, 3 * D_ATT:3 * D_ATT + 3 * D_CONV])
        cb = cbcv[:, 0:D_CONV]
        u = cbcv[:, D_CONV:2 * D_CONV] * cbcv[:, 2 * D_CONV:3 * D_CONV]
        pos = lax.broadcasted_iota(jnp.int32, (ntok, D_CONV), 0) & (slen - 1)
        u_m1 = jnp.where(pos < 1, up1_ref[...], pltpu.roll(u, 1, axis=0))
        u_m2 = jnp.where(pos < 2, up2_ref[...], pltpu.roll(u, 2, axis=0))
        yc = cw_ref[0:1, :] * u_m2 + cw_ref[1:2, :] * u_m1 + cw_ref[2:3, :] * u + cbias_ref[...]
        conv_s[...] = (cb * yc).astype(BF16)
        uo_ref[...] = u

    nt_dims = (((1,), (1,)), ((), ()))
    shape = (N_HEADS * slen, D_ATT)
    rowh = lax.broadcasted_iota(jnp.int32, shape, 0) >> (slen.bit_length() - 1)
    laneh = lax.broadcasted_iota(jnp.int32, shape, 1) >> HEAD_SHIFT
    lane_o = lax.broadcasted_iota(jnp.int32, (slen, D_ATT), 1) >> HEAD_SHIFT
    for sq in range(SAMPLE_SEQS):
        rows = pl.ds(pl.multiple_of((n * SAMPLE_SEQS + sq) * slen, slen), slen)
        qt = jnp.concatenate([q_s[rows, :]] * N_HEADS, axis=0)
        qm = jnp.where(rowh == laneh, qt, jnp.zeros_like(qt))
        s_old = _dot(qm, ck_ref[sq].astype(BF16)) + bias_old_ref[...]
        s_new = lax.dot_general(qm, kn_s[rows, :], nt_dims,
                                preferred_element_type=F32) + bias_new_ref[...]
        m = jnp.maximum(jnp.max(s_old, axis=1, keepdims=True),
                        jnp.max(s_new, axis=1, keepdims=True))
        p_old = jnp.exp2(s_old - m)
        p_new = jnp.exp2(s_new - m)
        l = jnp.sum(p_old, axis=1, keepdims=True) + jnp.sum(p_new, axis=1, keepdims=True)
        o = lax.dot_general(p_old.astype(BF16), cv_ref[sq].astype(BF16), nt_dims,
                            preferred_element_type=F32)
        o = (o + _dot(p_new.astype(BF16), vn_s[rows, :])) * (1.0 / l)
        att = o[0:slen]
        for h in range(1, N_HEADS):
            att = jnp.where(lane_o == h, o[h * slen:(h + 1) * slen], att)
        att_s[rows, :] = att.astype(BF16)

    @pl.when(n == nseq // SAMPLE_SEQS - 1)
    def _():
        gates = _dot(h_s[...], win_ref[:, 3 * D_ATT + 3 * D_CONV:])
        pa = _dot(att_s[...], wpa_ref[...])
        pb = _dot(conv_s[...], wpb_ref[...])
        mixin = _sigmoid(gates[:, 0:D_MODEL]) * pa + _sigmoid(gates[:, D_MODEL:]) * pb
        x1_ref[...] = _dot(mixin.astype(BF16), wo_ref[...])

        def res_body(i, carry):
            r = pl.ds(pl.multiple_of(i * slen, slen), slen)
            m = mod_ref[i]
            x1 = x_ref[r, :] + m[2:3, :] * x1_ref[r, :]
            x1_ref[r, :] = x1
            h2_s[r, :] = _rms(x1, g2_ref[...]) * (1.0 + m[4:5, :]) + m[3:4, :]
            return carry
        lax.fori_loop(0, nseq, res_body, 0)

        h2 = h2_s[...]
        _store_rows_as_tiles(h2_ref, _pack_rows(h2))
        route, new_cnt = _route(h2, whi_ref, br_ref, cnt_in_ref[...])
        route_ref[...] = route
        cnt_ref[...] = new_cnt


def _sample_main(x2d, mod, ck, cv, up1, up2, cnt_in, g1, g2, win, bias_old, bias_new, cw, cbias,
                 wpa, wpb, wo, whi, br, nseq, slen):
    ntok = nseq * slen
    args = (x2d, mod, ck, cv, up1, up2, cnt_in, g1, g2, win, bias_old, bias_new, cw, cbias,
            wpa, wpb, wo, whi, br)
    in_specs = []
    for idx, a in enumerate(args):
        if idx in (2, 3):
            in_specs.append(pl.BlockSpec((SAMPLE_SEQS,) + a.shape[1:], lambda n: (n, 0, 0)))
        else:
            in_specs.append(_const_spec(a.shape))
    whole = lambda shape: pl.BlockSpec(shape, lambda n: (0,) * len(shape))
    outs = [((ntok, D_MODEL), F32), ((ntok * ROW_TILES, LANES), I32), ((ntok, D_ATT), F32),
            ((ntok, D_ATT), F32), ((ntok, D_CONV), F32), ((ntok, RLANES), F32), ((1, RLANES), F32)]
    scratch = [
        pltpu.VMEM((ntok, D_MODEL), BF16), pltpu.VMEM((ntok, D_ATT), BF16),
        pltpu.VMEM((ntok, D_ATT), BF16), pltpu.VMEM((ntok, D_ATT), BF16),
        pltpu.VMEM((ntok, D_ATT), BF16), pltpu.VMEM((ntok, D_CONV), BF16),
        pltpu.VMEM((ntok, D_MODEL), F32),
    ]
    return pl.pallas_call(
        functools.partial(_sample_kernel, nseq=nseq, slen=slen),
        grid=(nseq // SAMPLE_SEQS,),
        in_specs=in_specs,
        out_specs=[whole(s) for s, _ in outs],
        out_shape=[jax.ShapeDtypeStruct(s, d) for s, d in outs],
        scratch_shapes=scratch,
        compiler_params=pltpu.CompilerParams(
            dimension_semantics=("arbitrary",), vmem_limit_bytes=VMEM_LIMIT),
        name="sample_main",
    )(*args)


def _issue_rows(n, body):
    def group(g, carry):
        for u in range(ISSUE_UNROLL):
            body(g * ISSUE_UNROLL + u, u)
        return carry
    lax.fori_loop(0, n // ISSUE_UNROLL, group, 0)


def _dispatch_kernel(pend_ref, cnt_ref, d1_ref, d2_ref, hp_ref, hs_ref, xs_out, zbuf, sem, zsem, *,
                     np_tiles, nslots):
    i = pl.program_id(0)
    zrows = ZERO_ROWS * ROW_TILES

    @pl.when(i == 0)
    def _():
        zbuf[...] = jnp.zeros_like(zbuf)

        def zcopy(piece):
            start = pl.multiple_of(piece * zrows, zrows)
            return pltpu.make_async_copy(zbuf, xs_out.at[pl.ds(start, zrows)], zsem)

        def pieces(e):
            lo = (pend_ref[e] + cnt_ref[jnp.minimum(e, N_EXPERTS - 1)]) // ZERO_ROWS
            hi = pend_ref[jnp.minimum(e + 1, N_EXPERTS)] // ZERO_ROWS
            lo = jnp.where(e == N_EXPERTS, pend_ref[N_EXPERTS] // ZERO_ROWS, lo)
            hi = jnp.where(e == N_EXPERTS, nslots // ZERO_ROWS, hi)
            return lo, hi

        def start_all(e, carry):
            lo, hi = pieces(e)
            return lax.fori_loop(lo, hi, lambda p, c: (zcopy(p).start(), c)[1], carry)

        def wait_all(e, carry):
            lo, hi = pieces(e)
            return lax.fori_loop(lo, hi, lambda p, c: (zcopy(p).wait(), c)[1], carry)
        lax.fori_loop(0, N_EXPERTS + 1, start_all, 0)
        lax.fori_loop(0, N_EXPERTS + 1, wait_all, 0)

    def scatter_tile(src):
        def row(r, u):
            pltpu.make_async_copy(_tile(src, r), _tile(xs_out, d1_ref[r]), sem).start(priority=u % 2)
            pltpu.make_async_copy(_tile(src, r), _tile(xs_out, d2_ref[r]), sem).start(
                priority=(u + 1) % 2)
        _issue_rows(TM, row)
        for _ in range(2):
            pltpu.make_async_copy(src, xs_out.at[pl.ds(0, TM * ROW_TILES)], sem).wait()

    @pl.when(i < np_tiles)
    def _():
        scatter_tile(hp_ref)

    @pl.when(i >= np_tiles)
    def _():
        scatter_tile(hs_ref)


def _dispatch(pend, counts, d1, d2, h2p, h2s, nslots):
    np_tiles = h2p.shape[0] // (TM * ROW_TILES)
    ns_tiles = h2s.shape[0] // (TM * ROW_TILES)
    smem_tile = pl.BlockSpec((TM,), lambda i, *_: (i,), memory_space=pltpu.SMEM)
    rows = TM * ROW_TILES
    return pl.pallas_call(
        functools.partial(_dispatch_kernel, np_tiles=np_tiles, nslots=nslots),
        grid_spec=pltpu.PrefetchScalarGridSpec(
            num_scalar_prefetch=2,
            grid=(np_tiles + ns_tiles,),
            in_specs=[smem_tile, smem_tile,
                      pl.BlockSpec((rows, LANES), lambda i, *_: (jnp.minimum(i, np_tiles - 1), 0)),
                      pl.BlockSpec((rows, LANES), lambda i, *_: (jnp.maximum(i - np_tiles, 0), 0))],
            out_specs=pl.BlockSpec(memory_space=pl.ANY),
            scratch_shapes=[pltpu.VMEM((ZERO_ROWS * ROW_TILES, LANES), I32),
                            pltpu.SemaphoreType.DMA(()), pltpu.SemaphoreType.DMA(())],
        ),
        out_shape=jax.ShapeDtypeStruct((nslots * ROW_TILES, LANES), I32),
        compiler_params=pltpu.CompilerParams(dimension_semantics=("arbitrary",)),
        name="dispatch",
    )(pend, counts, d1, d2, h2p, h2s)


def _expert_kernel(blk_e_ref, nblk_ref, first_ref, wslot_ref, next_e_ref,
                   xs_ref, wg_hbm, wu_hbm, wd_hbm, y_ref,
                   wg_f, wu_f, wd_f, wg_b, wu_b, wd_b, wsem):
    i = pl.program_id(0)
    live = i < nblk_ref[0]

    def weight_copies(e, slot):
        return (pltpu.make_async_copy(wg_hbm.at[e], wg_f.at[slot], wsem.at[slot, 0]),
                pltpu.make_async_copy(wu_hbm.at[e], wu_f.at[slot], wsem.at[slot, 1]),
                pltpu.make_async_copy(wd_hbm.at[e], wd_f.at[slot], wsem.at[slot, 2]))

    @pl.when(i == 0)
    def _():
        for cp in weight_copies(blk_e_ref[0], 0):
            cp.start()

    @pl.when(live & (first_ref[i] == 1))
    def _():
        slot = wslot_ref[i]
        for cp in weight_copies(blk_e_ref[i], slot):
            cp.wait()
        wg_b[...] = wg_f[slot].astype(BF16)
        wu_b[...] = wu_f[slot].astype(BF16)
        wd_b[...] = wd_f[slot].astype(BF16)

        @pl.when(next_e_ref[i] >= 0)
        def _():
            for cp in weight_copies(next_e_ref[i], 1 - slot):
                cp.start()

    @pl.when(live)
    def _():
        x_lo, x_hi = _unpack_rows(_load_tiles_as_rows(xs_ref, MOE_BLK))
        x_lo = x_lo.astype(BF16)
        x_hi = x_hi.astype(BF16)
        g = _dot(x_lo, wg_b[0:HALF, :]) + _dot(x_hi, wg_b[HALF:, :])
        u = _dot(x_lo, wu_b[0:HALF, :]) + _dot(x_hi, wu_b[HALF:, :])
        a = (g * _sigmoid(g)) * u
        _store_rows_as_tiles(y_ref, _pack_rows(_dot(a.astype(BF16), wd_b[...])))

    @pl.when(jnp.logical_not(live))
    def _():
        y_ref[...] = jnp.zeros_like(y_ref)


def _experts(blk_e, nblk, first, wslot, next_e, xs, wg, wu, wd):
    blk_rows = MOE_BLK * ROW_TILES
    nblocks = xs.shape[0] // blk_rows
    row_map = lambda i, be, nb, *_: (jnp.minimum(i, nb[0] - 1), 0)
    any_spec = pl.BlockSpec(memory_space=pl.ANY)
    return pl.pallas_call(
        _expert_kernel,
        grid_spec=pltpu.PrefetchScalarGridSpec(
            num_scalar_prefetch=5,
            grid=(nblocks,),
            in_specs=[pl.BlockSpec((blk_rows, LANES), row_map), any_spec, any_spec, any_spec],
            out_specs=pl.BlockSpec((blk_rows, LANES), lambda i, *_: (i, 0)),
            scratch_shapes=[pltpu.VMEM((2, D_MODEL, D_EXPERT), F32),
                            pltpu.VMEM((2, D_MODEL, D_EXPERT), F32),
                            pltpu.VMEM((2, D_EXPERT, D_MODEL), F32),
                            pltpu.VMEM((D_MODEL, D_EXPERT), BF16),
                            pltpu.VMEM((D_MODEL, D_EXPERT), BF16),
                            pltpu.VMEM((D_EXPERT, D_MODEL), BF16),
                            pltpu.SemaphoreType.DMA((2, 3))],
        ),
        out_shape=jax.ShapeDtypeStruct(xs.shape, I32),
        compiler_params=pltpu.CompilerParams(
            dimension_semantics=("arbitrary",), vmem_limit_bytes=VMEM_LIMIT),
        name="experts",
    )(blk_e, nblk, first, wslot, next_e, xs, wg, wu, wd)


def _combine_kernel(d1_ref, d2_ref, d1n_ref, d2n_ref, y_hbm, x1_ref, route_ref, gate_ref, gf_ref,
                    o_ref, a0, b0, a1, b1, sem, *, ntiles):
    t = pl.program_id(0)
    bufs = ((a0, b0), (a1, b1))

    def gather(i1_ref, i2_ref, par):
        buf_a, buf_b = bufs[par]

        def row(r, u):
            pltpu.make_async_copy(_tile(y_hbm, i1_ref[r]), _tile(buf_a, r),
                                  sem.at[par]).start(priority=u % 2)
            pltpu.make_async_copy(_tile(y_hbm, i2_ref[r]), _tile(buf_b, r),
                                  sem.at[par]).start(priority=(u + 1) % 2)
        _issue_rows(TM, row)

    @pl.when(t == 0)
    def _():
        gather(d1_ref, d2_ref, 0)

    def step(par):
        buf_a, buf_b = bufs[par]
        for buf in (buf_a, buf_b):
            pltpu.make_async_copy(y_hbm.at[pl.ds(0, TM * ROW_TILES)], buf, sem.at[par]).wait()

        @pl.when(t + 1 < ntiles)
        def _():
            gather(d1n_ref, d2n_ref, 1 - par)

        route = route_ref[0]
        w1 = route[:, 4:5]
        w2 = route[:, 5:6]
        a_lo, a_hi = _unpack_rows(_load_tiles_as_rows(buf_a, TM))
        b_lo, b_hi = _unpack_rows(_load_tiles_as_rows(buf_b, TM))
        ffn = jnp.concatenate([w1 * a_lo + w2 * b_lo, w1 * a_hi + w2 * b_hi], axis=1)
        x2 = x1_ref[0] + gate_ref[0] * ffn
        o_ref[0] = _rms(x2, gf_ref[...])

    for par in range(2):
        pl.when(t % 2 == par)(functools.partial(step, par))


def _combine(d1, d2, y, x1, route, gate, gf, tok_base):
    nb, seq, _ = x1.shape
    nt = seq // TM
    ntiles = nb * nt
    blk0 = tok_base // TM
    smem = lambda fn: pl.BlockSpec((TM,), fn, memory_space=pltpu.SMEM)
    cur = lambda t: (blk0 + t,)
    nxt = lambda t: (blk0 + jnp.minimum(t + 1, ntiles - 1),)
    tile = lambda t: (t // nt, t % nt, 0)
    grows = gate.shape[1]
    gate_spec = (pl.BlockSpec((1, 1, D_MODEL), lambda t: (t // nt, 0, 0)) if grows == 1
                 else pl.BlockSpec((1, TM, D_MODEL), tile))
    return pl.pallas_call(
        functools.partial(_combine_kernel, ntiles=ntiles),
        grid=(ntiles,),
        in_specs=[smem(cur), smem(cur), smem(nxt), smem(nxt), pl.BlockSpec(memory_space=pl.ANY),
                  pl.BlockSpec((1, TM, D_MODEL), tile),
                  pl.BlockSpec((1, TM, RLANES), tile),
                  gate_spec,
                  pl.BlockSpec((1, D_MODEL), lambda t: (0, 0))],
        out_specs=pl.BlockSpec((1, TM, D_MODEL), tile),
        out_shape=jax.ShapeDtypeStruct(x1.shape, F32),
        scratch_shapes=[pltpu.VMEM((TM * ROW_TILES, LANES), I32)] * 4 + [
            pltpu.SemaphoreType.DMA((2,))],
        compiler_params=pltpu.CompilerParams(
            dimension_semantics=("arbitrary",), vmem_limit_bytes=VMEM_LIMIT),
        name="combine",
    )(d1, d2, d1, d2, y, x1, route, gate, gf)


def _band_bias(rel_bias, rows, keys):
    n = rows - 1 + keys
    dist = WINDOW + rows - 1 - np.arange(n + 1)
    flipped = rel_bias[:, np.clip(dist, -MAX_REL, MAX_REL) + MAX_REL]
    skew = jnp.tile(flipped, (1, rows))[:, :rows * n].reshape(N_HEADS, rows, n)
    b = skew[:, :, rows - 1:rows - 1 + keys]
    return b.reshape(2, 4 * rows, keys)


def kernel(x_prompt, x_sample, cache_attn_k, cache_attn_v, state_conv, c_prompt, c_sample,
           w_ada, b_ada, norm1_g, norm2_g, w_in, rel_bias, conv_w, conv_b, w_pa, w_pb, w_o,
           w_group, b_group, w_expert, b_expert, w_e_gate, w_e_up, w_e_down, final_g):
    assert w_ada.shape[0] == 1, "single trunk layer"
    nb, seq, _ = x_prompt.shape
    nseq, slen, _ = x_sample.shape
    ntok_p = nb * seq
    ntok_s = nseq * slen
    ntok = ntok_p + ntok_s
    assert seq % TL == 0 and WINDOW % TL == 0 and seq % TM == 0 and ntok_s % TM == 0
    assert slen >= 2 and slen & (slen - 1) == 0 and slen % 16 == 0 and nseq % SAMPLE_SEQS == 0
    assert MOE_BLK % ZERO_ROWS == 0

    n_c = nb + nseq
    n_pad = -(-n_c // 8) * 8
    c_all = jnp.concatenate([c_prompt, c_sample, jnp.zeros((n_pad - n_c, D_MODEL), F32)], axis=0)
    mod = _ada(c_all, w_ada[0], b_ada[0]).reshape(n_pad, 6, D_MODEL)
    mod_p = mod[:nb]
    mod_s = mod[nb:n_c]

    win = w_in[0].astype(BF16)
    wpa = w_pa[0].astype(BF16)
    wpb = w_pb[0].astype(BF16)
    wo = w_o[0].astype(BF16)
    g1 = norm1_g[0].reshape(1, D_MODEL)
    g2 = norm2_g[0].reshape(1, D_MODEL)
    gf = final_g.reshape(1, D_MODEL)
    cw = jnp.concatenate([conv_w[0], jnp.zeros((8 - conv_w.shape[1], D_CONV), F32)], axis=0)
    cbias = conv_b[0].reshape(1, D_CONV)
    wr = jnp.concatenate([w_expert[0], w_group[0],
                          jnp.zeros((D_MODEL, RLANES - N_EXPERTS - N_GROUPS), F32)], axis=1)
    whi = wr.astype(BF16)
    br = jnp.concatenate([b_expert[0], b_group[0],
                          jnp.zeros((RLANES - N_EXPERTS - N_GROUPS,), F32)]).reshape(1, RLANES)
    in_band = np.arange(PAIR_BAND)[None, :] - (np.arange(PAIR)[:, None] // CHUNK) * CHUNK
    in_band = np.tile((in_band >= 0) & (in_band < BAND), (4, 1))
    bias_p = jnp.where(in_band, _band_bias(rel_bias[0] * LOG2E, PAIR, PAIR_BAND), NEG)
    bias_s = _band_bias(rel_bias[0] * LOG2E, slen, WINDOW + slen).reshape(
        N_HEADS * slen, WINDOW + slen)
    bias_old = bias_s[:, :WINDOW]
    bias_new = bias_s[:, WINDOW:]

    x1p, h2p, kp, vp, up8, route_p, cnt_p = _prompt_main(
        x_prompt, mod_p, g1, g2, win, bias_p, cw, cbias, wpa, wpb, wo, whi, br)

    st = state_conv[0]
    up1 = jnp.zeros((nseq, slen, D_CONV), F32).at[:, 0].set(st[:, 1]).reshape(ntok_s, D_CONV)
    up2 = (jnp.zeros((nseq, slen, D_CONV), F32).at[:, 0].set(st[:, 0]).at[:, 1].set(st[:, 1])
           .reshape(ntok_s, D_CONV))
    ck = jnp.transpose(cache_attn_k[0], (0, 2, 3, 1)).reshape(nseq, D_ATT, WINDOW)
    cv = jnp.transpose(cache_attn_v[0], (0, 2, 3, 1)).reshape(nseq, D_ATT, WINDOW)
    x1s, h2s, ks, vs, us, route_s, cnt = _sample_main(
        x_sample.reshape(ntok_s, D_MODEL), mod_s, ck, cv, up1, up2, cnt_p,
        g1, g2, win, bias_old, bias_new, cw, cbias, wpa, wpb, wo, whi, br, nseq, slen)

    route_all = jnp.concatenate([route_p.reshape(ntok_p, RLANES)[:, :4], route_s[:, :4]], axis=0)
    experts = route_all[:, 0:2].astype(jnp.int32)
    ranks = route_all[:, 2:4].astype(jnp.int32)
    counts = cnt[0, :N_EXPERTS].astype(jnp.int32)
    pcounts = (counts + MOE_BLK - 1) // MOE_BLK * MOE_BLK
    pend = jnp.cumsum(pcounts)
    pstart = pend - pcounts
    eids = jnp.arange(N_EXPERTS, dtype=jnp.int32)
    dest = jnp.sum(jnp.where(experts[..., None] == eids, pstart, 0), axis=-1) + ranks
    d1 = dest[:, 0]
    d2 = dest[:, 1]
    nblocks = (2 * ntok) // MOE_BLK + N_EXPERTS
    blk_start = jnp.arange(nblocks, dtype=jnp.int32) * MOE_BLK
    blk_e = jnp.minimum(jnp.sum((pend[None, :] <= blk_start[:, None]).astype(jnp.int32), axis=1),
                        N_EXPERTS - 1)
    nblk = (pend[-1:] // MOE_BLK).astype(jnp.int32)
    pend0 = jnp.concatenate([jnp.zeros((1,), jnp.int32), pend.astype(jnp.int32)])

    xs = _dispatch(pend0, counts, d1, d2, h2p, h2s, nblocks * MOE_BLK)
    blk_id = jnp.arange(nblocks, dtype=jnp.int32)
    first = (blk_id < nblk[0]) & ((blk_id == 0) | (blk_e != jnp.roll(blk_e, 1)))
    wslot = (jnp.cumsum(first.astype(jnp.int32)) - 1) % 2
    later_first = lax.cummin(jnp.where(first, blk_id, nblocks)[::-1])[::-1]
    next_first = jnp.concatenate([later_first[1:], jnp.full((1,), nblocks, jnp.int32)])
    next_e = jnp.where(next_first < nblocks, blk_e[jnp.minimum(next_first, nblocks - 1)], -1)
    y = _experts(blk_e, nblk, first.astype(jnp.int32), wslot.astype(jnp.int32),
                 next_e.astype(jnp.int32), xs, w_e_gate[0], w_e_up[0], w_e_down[0])

    y_prompt = _combine(d1, d2, y, x1p, route_p, mod_p[:, 5:6, :], gf, 0)
    gate_s = jnp.repeat(mod_s[:, 5, :], slen, axis=0).reshape(1, ntok_s, D_MODEL)
    y_sample = _combine(d1, d2, y, x1s.reshape(1, ntok_s, D_MODEL),
                        route_s.reshape(1, ntok_s, RLANES), gate_s, gf, ntok_p)

    new_k_p = kp.reshape(1, nb, WINDOW, N_HEADS, HEAD_DIM)
    new_v_p = vp.reshape(1, nb, WINDOW, N_HEADS, HEAD_DIM)
    new_conv_p = up8[:, 6:8, :].reshape(1, nb, 2, D_CONV)
    new_k_s = ks.reshape(1, nseq, slen, N_HEADS, HEAD_DIM)
    new_v_s = vs.reshape(1, nseq, slen, N_HEADS, HEAD_DIM)
    new_conv_s = us.reshape(nseq, slen, D_CONV)[:, slen - 2:, :].reshape(1, nseq, 2, D_CONV)
    return (y_prompt, y_sample.reshape(nseq, slen, D_MODEL), new_k_p, new_v_p, new_conv_p,
            new_k_s, new_v_s, new_conv_s)
```

```python
import functools

import numpy as np
import jax
import jax.numpy as jnp
from jax import lax
from jax.experimental import pallas as pl
from jax.experimental.pallas import tpu as pltpu

F32 = jnp.float32
BF16 = jnp.bfloat16
I32 = jnp.int32

D_MODEL = 1024
CHUNK = 64
LEFT = 8
WINDOW = LEFT * CHUNK
BAND = WINDOW + CHUNK
PAIR = 2 * CHUNK
PAIR_BAND = WINDOW + PAIR
N_HEADS = 8
HEAD_DIM = 64
HEAD_SHIFT = HEAD_DIM.bit_length() - 1
D_ATT = N_HEADS * HEAD_DIM
QUAD = 256
MAX_REL = 128
D_CONV = 512
N_GROUPS = 4
EPG = 8
N_EXPERTS = 32
D_EXPERT = 512
EPS = 1e-6
NEG = -1e30
LOG2E = float(np.log2(np.e))
Q_SCALE = HEAD_DIM ** -0.5 * LOG2E

TL = 512
RING = WINDOW + TL
MOE_BLK = 512
TM = 512
SAMPLE_SEQS = 2
ZERO_ROWS = 64
RLANES = 128
LANES = 128
HALF = D_MODEL // 2
ROW_TILES = HALF // LANES
ISSUE_UNROLL = 8
VMEM_LIMIT = 56 * 1024 * 1024


def _const_spec(shape):
    nd = len(shape)
    return pl.BlockSpec(shape, lambda *_: (0,) * nd, pipeline_mode=pl.Buffered(1))


def _dot(a, b):
    return jnp.dot(a, b, preferred_element_type=F32)


def _sigmoid(x):
    return 1.0 / (1.0 + jnp.exp(-x))


def _rms(x, g):
    ms = jnp.mean(x * x, axis=-1, keepdims=True)
    return x * lax.rsqrt(ms + EPS) * g


def _pack_rows(val):
    lo = lax.bitcast_convert_type(val[:, :HALF], I32) + 0x8000
    hi = lax.bitcast_convert_type(val[:, HALF:], I32) + 0x8000
    return (hi & -65536) | lax.shift_right_logical(lo, 16)


def _unpack_rows(packed):
    lo = lax.bitcast_convert_type(lax.shift_left(packed, 16), F32)
    hi = lax.bitcast_convert_type(packed & -65536, F32)
    return lo, hi


def _store_rows_as_tiles(ref, packed):
    r = packed.shape[0]
    for c in range(ROW_TILES):
        ref[pl.ds(c, r, stride=ROW_TILES), :] = packed[:, c * LANES:(c + 1) * LANES]


def _load_tiles_as_rows(ref, r):
    return jnp.concatenate(
        [ref[pl.ds(c, r, stride=ROW_TILES), :] for c in range(ROW_TILES)], axis=1)


def _tile(ref, row):
    return ref.at[pl.ds(pl.multiple_of(row * ROW_TILES, ROW_TILES), ROW_TILES)]


def _ada_kernel(c_ref, w_ref, b_ref, o_ref):
    c = c_ref[...]
    s = c * _sigmoid(c)
    o_ref[...] = _dot(s.astype(BF16), w_ref[...].astype(BF16)) + b_ref[...]


def _ada(c_all, w_ada, b_ada):
    n = c_all.shape[0]
    nb = 1024
    return pl.pallas_call(
        _ada_kernel,
        grid=(6 * D_MODEL // nb,),
        in_specs=[pl.BlockSpec((n, D_MODEL), lambda i: (0, 0)),
                  pl.BlockSpec((D_MODEL, nb), lambda i: (0, i)),
                  pl.BlockSpec((1, nb), lambda i: (0, i))],
        out_specs=pl.BlockSpec((n, nb), lambda i: (0, i)),
        out_shape=jax.ShapeDtypeStruct((n, 6 * D_MODEL), F32),
        name="ada",
    )(c_all, w_ada, b_ada.reshape(1, -1))


def _attend(q, kb, vb, bias, lim):
    r = q.shape[0]
    nk = kb.shape[0]
    assert r & (r - 1) == 0
    qt = jnp.concatenate([q] * 4, axis=0)
    rowh = lax.broadcasted_iota(jnp.int32, (4 * r, QUAD), 0) >> (r.bit_length() - 1)
    laneh = lax.broadcasted_iota(jnp.int32, (4 * r, QUAD), 1) >> HEAD_SHIFT
    qm = jnp.where(rowh == laneh, qt, jnp.zeros_like(qt))
    s = lax.dot_general(qm, kb, (((1,), (1,)), ((), ())), preferred_element_type=F32)
    valid = None
    if lim is not None:
        valid = lax.broadcasted_iota(jnp.int32, (r, nk), 1) >= lim
    ps, inv_l = [], []
    for h in range(4):
        sh = s[h * r:(h + 1) * r] + bias[h * r:(h + 1) * r, :]
        if valid is not None:
            sh = jnp.where(valid, sh, NEG)
        m = jnp.max(sh, axis=1, keepdims=True)
        ph = jnp.exp2(sh - m)
        inv_l.append(1.0 / jnp.sum(ph, axis=1, keepdims=True))
        ps.append(ph.astype(BF16))
    o = _dot(jnp.concatenate(ps, axis=0), vb)
    lane_o = lax.broadcasted_iota(jnp.int32, (r, QUAD), 1) >> HEAD_SHIFT
    out = o[0:r] * inv_l[0]
    for h in range(1, 4):
        out = jnp.where(lane_o == h, o[h * r:(h + 1) * r] * inv_l[h], out)
    return out


def _route_stages(h2, whi_ref, br_ref, cnt, valid=None):
    r = h2.shape[0]
    lane = lax.broadcasted_iota(jnp.int32, (r, RLANES), 1)
    lane_f = lane.astype(F32)
    big = jnp.float32(1000.0)
    v = {}

    def logits():
        v["logits"] = _dot(h2.astype(BF16), whi_ref[...]) + br_ref[...]

    def group():
        lg = jnp.where((lane >= N_EXPERTS) & (lane < N_EXPERTS + N_GROUPS), v["logits"], NEG)
        mg = jnp.max(lg, axis=1, keepdims=True)
        v["gi"] = jnp.min(jnp.where(lg == mg, lane_f, big), axis=1, keepdims=True) - N_EXPERTS
        v["pg"] = 1.0 / jnp.sum(jnp.exp(lg - mg), axis=1, keepdims=True)

    def top1():
        grp_of_lane = (lane >> (EPG.bit_length() - 1)).astype(F32)
        le = jnp.where((lane < N_EXPERTS) & (grp_of_lane == v["gi"]), v["logits"], NEG)
        v["m1"] = jnp.max(le, axis=1, keepdims=True)
        v["i1"] = jnp.min(jnp.where(le == v["m1"], lane_f, big), axis=1, keepdims=True)
        v["le"] = le

    def top2():
        sel1 = lane_f == v["i1"]
        le2 = jnp.where(sel1, NEG, v["le"])
        m2 = jnp.max(le2, axis=1, keepdims=True)
        v["i2"] = jnp.min(jnp.where(le2 == m2, lane_f, big), axis=1, keepdims=True)
        rr = jnp.exp(m2 - v["m1"])
        inv = v["pg"] / (1.0 + rr)
        v["w1"] = inv
        v["w2"] = inv * rr
        v["sel1"] = sel1
        v["sel2"] = lane_f == v["i2"]

    def ranks():
        oh = jnp.where(v["sel1"] | v["sel2"], 1.0 if valid is None else valid, 0.0).astype(F32)
        ri = lax.broadcasted_iota(jnp.int32, (r, r), 0)
        ci = lax.broadcasted_iota(jnp.int32, (r, r), 1)
        tri = jnp.where(ri > ci, 1.0, 0.0).astype(BF16)
        v["before"] = _dot(tri, oh.astype(BF16)) + cnt
        v["new_cnt"] = cnt + jnp.sum(oh, axis=0, keepdims=True)

    def assemble():
        r1 = jnp.sum(jnp.where(v["sel1"], v["before"], 0.0), axis=1, keepdims=True)
        r2 = jnp.sum(jnp.where(v["sel2"], v["before"], 0.0), axis=1, keepdims=True)
        route = jnp.where(lane == 0, v["i1"], 0.0)
        route = jnp.where(lane == 1, v["i2"], route)
        route = jnp.where(lane == 2, r1, route)
        route = jnp.where(lane == 3, r2, route)
        route = jnp.where(lane == 4, v["w1"], route)
        route = jnp.where(lane == 5, v["w2"], route)
        return route, v["new_cnt"]

    return [logits, group, top1, top2, ranks, assemble]


def _route(h2, whi_ref, br_ref, cnt):
    stages = _route_stages(h2, whi_ref, br_ref, cnt)
    for stage in stages[:-1]:
        stage()
    return stages[-1]()


def _prompt_kernel(x_ref, mod_ref, g1_ref, g2_ref, win_ref, bias_ref, cw_ref, cbias_ref,
                   wpa_ref, wpb_ref, wo_ref, whi_ref, br_ref,
                   x1_ref, h2_ref, ko_ref, vo_ref, uo_ref, route_ref, cnt_ref,
                   kring, vring, att_s, ucarry, cnt_s, h2_prev, *, nt, ntiles):
    g = pl.program_id(0)

    @pl.when(g == 0)
    def _():
        cnt_s[...] = jnp.zeros_like(cnt_s)
        h2_prev[...] = jnp.zeros_like(h2_prev)

    @pl.when(g < ntiles)
    def _():
        _prompt_tile(g, nt, x_ref, mod_ref, g1_ref, g2_ref, win_ref, bias_ref, cw_ref, cbias_ref,
                     wpa_ref, wpb_ref, wo_ref, whi_ref, br_ref,
                     x1_ref, h2_ref, ko_ref, vo_ref, uo_ref, route_ref, cnt_ref,
                     kring, vring, att_s, ucarry, cnt_s, h2_prev)

    @pl.when(g == ntiles)
    def _():
        route, new_cnt = _route(h2_prev[...], whi_ref, br_ref, cnt_s[...])
        route_ref[0] = route
        cnt_ref[...] = new_cnt


def _prompt_tile(g, nt, x_ref, mod_ref, g1_ref, g2_ref, win_ref, bias_ref, cw_ref, cbias_ref,
                 wpa_ref, wpb_ref, wo_ref, whi_ref, br_ref,
                 x1_ref, h2_ref, ko_ref, vo_ref, uo_ref, route_ref, cnt_ref,
                 kring, vring, att_s, ucarry, cnt_s, h2_prev):
    j = g % nt

    @pl.when(j == 0)
    def _():
        kring[0:WINDOW, :] = jnp.zeros((WINDOW, D_ATT), BF16)
        vring[0:WINDOW, :] = jnp.zeros((WINDOW, D_ATT), BF16)
        ucarry[...] = jnp.zeros_like(ucarry)

    route_stages = _route_stages(h2_prev[...], whi_ref, br_ref, cnt_s[...],
                                 valid=jnp.where(g > 0, 1.0, 0.0).astype(F32))
    route_stages[0]()

    sh1 = mod_ref[0, 0:1, :]
    sc1 = mod_ref[0, 1:2, :]
    gt1 = mod_ref[0, 2:3, :]
    sh2 = mod_ref[0, 3:4, :]
    sc2 = mod_ref[0, 4:5, :]

    x = x_ref[0]
    h = _rms(x, g1_ref[...]) * (1.0 + sc1) + sh1
    hb = h.astype(BF16)

    qkv = _dot(hb, win_ref[:, 0:3 * D_ATT])
    q = (qkv[:, 0:D_ATT] * Q_SCALE).astype(BF16)
    k = qkv[:, D_ATT:2 * D_ATT]
    v = qkv[:, 2 * D_ATT:3 * D_ATT]
    ko_ref[0] = k
    vo_ref[0] = v
    kring[WINDOW:RING, :] = k.astype(BF16)
    vring[WINDOW:RING, :] = v.astype(BF16)

    base = j * TL
    for c in range(TL // PAIR):
        lim = WINDOW - (base + c * PAIR)
        for qd in range(2):
            ls = slice(qd * QUAD, (qd + 1) * QUAD)
            o = _attend(q[c * PAIR:(c + 1) * PAIR, ls],
                        kring[c * PAIR:c * PAIR + PAIR_BAND, ls],
                        vring[c * PAIR:c * PAIR + PAIR_BAND, ls],
                        bias_ref.at[qd], lim)
            att_s[c * PAIR:(c + 1) * PAIR, ls] = o.astype(BF16)
        if c + 1 < len(route_stages) - 1:
            route_stages[c + 1]()
    route, new_cnt = route_stages[-1]()
    route_ref[0] = route
    cnt_s[...] = new_cnt
    cnt_ref[...] = new_cnt

    kring[0:WINDOW, :] = kring[TL:RING, :]
    vring[0:WINDOW, :] = vring[TL:RING, :]

    cbcv = _dot(hb, win_ref[:, 3 * D_ATT:3 * D_ATT + 3 * D_CONV])
    gates = _dot(hb, win_ref[:, 3 * D_ATT + 3 * D_CONV:])
    cb = cbcv[:, 0:D_CONV]
    u = cbcv[:, D_CONV:2 * D_CONV] * cbcv[:, 2 * D_CONV:3 * D_CONV]
    row = lax.broadcasted_iota(jnp.int32, (8, D_CONV), 0)
    prev = ucarry[...]
    r1 = pltpu.roll(u, 1, axis=0)
    r2 = pltpu.roll(u, 2, axis=0)
    u_m1 = jnp.concatenate(
        [jnp.where(row < 1, pltpu.roll(prev, 1, axis=0), r1[0:8]), r1[8:]], axis=0)
    u_m2 = jnp.concatenate(
        [jnp.where(row < 2, pltpu.roll(prev, 2, axis=0), r2[0:8]), r2[8:]], axis=0)
    yc = cw_ref[0:1, :] * u_m2 + cw_ref[1:2, :] * u_m1 + cw_ref[2:3, :] * u + cbias_ref[...]
    conv_out = (cb * yc).astype(BF16)
    ucarry[...] = u[TL - 8:TL, :]
    uo_ref[0] = u[TL - 8:TL, :]

    pa = _dot(att_s[...], wpa_ref[...])
    pb = _dot(conv_out, wpb_ref[...])
    mixin = _sigmoid(gates[:, 0:D_MODEL]) * pa + _sigmoid(gates[:, D_MODEL:]) * pb
    mix = _dot(mixin.astype(BF16), wo_ref[...])
    x1 = x + gt1 * mix
    x1_ref[0] = x1
    h2 = _rms(x1, g2_ref[...]) * (1.0 + sc2) + sh2
    _store_rows_as_tiles(h2_ref, _pack_rows(h2))

    h2_prev[...] = h2


def _prompt_main(x, mod, g1, g2, win, bias_q, cw, cbias, wpa, wpb, wo, whi, br):
    nb, seq, _ = x.shape
    nt = seq // TL
    ntiles = nb * nt
    keep = WINDOW // TL
    cur = lambda g: jnp.minimum(g, ntiles - 1)
    prv = lambda g: jnp.maximum(g - 1, 0)
    tile = lambda g: (cur(g) // nt, cur(g) % nt, 0)
    last = lambda g: (cur(g) // nt, jnp.maximum(cur(g) % nt - (nt - keep), 0), 0)
    perb = lambda g: (cur(g) // nt, 0, 0)
    in_specs = [
        pl.BlockSpec((1, TL, D_MODEL), tile),
        pl.BlockSpec((1, 6, D_MODEL), perb),
        _const_spec(g1.shape), _const_spec(g2.shape), _const_spec(win.shape),
        _const_spec(bias_q.shape), _const_spec(cw.shape), _const_spec(cbias.shape),
        _const_spec(wpa.shape), _const_spec(wpb.shape), _const_spec(wo.shape),
        _const_spec(whi.shape), _const_spec(br.shape),
    ]
    out_specs = [
        pl.BlockSpec((1, TL, D_MODEL), tile),
        pl.BlockSpec((TL * ROW_TILES, LANES), lambda g: (cur(g), 0)),
        pl.BlockSpec((1, TL, D_ATT), last),
        pl.BlockSpec((1, TL, D_ATT), last),
        pl.BlockSpec((1, 8, D_CONV), perb),
        pl.BlockSpec((1, TL, RLANES), lambda g: (prv(g) // nt, prv(g) % nt, 0)),
        pl.BlockSpec((1, RLANES), lambda g: (0, 0)),
    ]
    out_shape = [
        jax.ShapeDtypeStruct((nb, seq, D_MODEL), F32),
        jax.ShapeDtypeStruct((nb * seq * ROW_TILES, LANES), I32),
        jax.ShapeDtypeStruct((nb, WINDOW, D_ATT), F32),
        jax.ShapeDtypeStruct((nb, WINDOW, D_ATT), F32),
        jax.ShapeDtypeStruct((nb, 8, D_CONV), F32),
        jax.ShapeDtypeStruct((nb, seq, RLANES), F32),
        jax.ShapeDtypeStruct((1, RLANES), F32),
    ]
    scratch = [
        pltpu.VMEM((RING, D_ATT), BF16), pltpu.VMEM((RING, D_ATT), BF16),
        pltpu.VMEM((TL, D_ATT), BF16), pltpu.VMEM((8, D_CONV), F32),
        pltpu.VMEM((1, RLANES), F32), pltpu.VMEM((TL, D_MODEL), F32),
    ]
    return pl.pallas_call(
        functools.partial(_prompt_kernel, nt=nt, ntiles=ntiles),
        grid=(ntiles + 1,),
        in_specs=in_specs, out_specs=out_specs, out_shape=out_shape,
        scratch_shapes=scratch,
        compiler_params=pltpu.CompilerParams(
            dimension_semantics=("arbitrary",), vmem_limit_bytes=VMEM_LIMIT),
        name="prompt_main",
    )(x, mod, g1, g2, win, bias_q, cw, cbias, wpa, wpb, wo, whi, br)


def _sample_kernel(x_ref, mod_ref, ck_ref, cv_ref, up1_ref, up2_ref, cnt_in_ref,
                   g1_ref, g2_ref, win_ref, bias_old_ref, bias_new_ref, cw_ref, cbias_ref,
                   wpa_ref, wpb_ref, wo_ref, whi_ref, br_ref,
                   x1_ref, h2_ref, ko_ref, vo_ref, uo_ref, route_ref, cnt_ref,
                   h_s, q_s, kn_s, vn_s, att_s, conv_s, h2_s, *, nseq, slen):
    n = pl.program_id(0)
    ntok = nseq * slen

    @pl.when(n == 0)
    def _():
        def norm_body(i, carry):
            rows = pl.ds(pl.multiple_of(i * slen, slen), slen)
            xi = x_ref[rows, :]
            m = mod_ref[i]
            hi = _rms(xi, g1_ref[...]) * (1.0 + m[1:2, :]) + m[0:1, :]
            h_s[rows, :] = hi.astype(BF16)
            return carry
        lax.fori_loop(0, nseq, norm_body, 0)
        hb = h_s[...]
        qkv = _dot(hb, win_ref[:, 0:3 * D_ATT])
        q_s[...] = (qkv[:, 0:D_ATT] * Q_SCALE).astype(BF16)
        k = qkv[:, D_ATT:2 * D_ATT]
        v = qkv[:, 2 * D_ATT:3 * D_ATT]
        ko_ref[...] = k
        vo_ref[...] = v
        kn_s[...] = k.astype(BF16)
        vn_s[...] = v.astype(BF16)

        cbcv = _dot(hb, win_ref[:, 3 * D_ATT:3 * D_ATT + 3 * D_CONV])
        cb = cbcv[:, 0:D_CONV]
        u = cbcv[:, D_CONV:2 * D_CONV] * cbcv[:, 2 * D_CONV:3 * D_CONV]
        pos = lax.broadcasted_iota(jnp.int32, (ntok, D_CONV), 0) & (slen - 1)
        u_m1 = jnp.where(pos < 1, up1_ref[...], pltpu.roll(u, 1, axis=0))
        u_m2 = jnp.where(pos < 2, up2_ref[...], pltpu.roll(u, 2, axis=0))
        yc = cw_ref[0:1, :] * u_m2 + cw_ref[1:2, :] * u_m1 + cw_ref[2:3, :] * u + cbias_ref[...]
        conv_s[...] = (cb * yc).astype(BF16)
        uo_ref[...] = u

    nt_dims = (((1,), (1,)), ((), ()))
    shape = (N_HEADS * slen, D_ATT)
    rowh = lax.broadcasted_iota(jnp.int32, shape, 0) >> (slen.bit_length() - 1)
    laneh = lax.broadcasted_iota(jnp.int32, shape, 1) >> HEAD_SHIFT
    lane_o = lax.broadcasted_iota(jnp.int32, (slen, D_ATT), 1) >> HEAD_SHIFT
    for sq in range(SAMPLE_SEQS):
        rows = pl.ds(pl.multiple_of((n * SAMPLE_SEQS + sq) * slen, slen), slen)
        qt = jnp.concatenate([q_s[rows, :]] * N_HEADS, axis=0)
        qm = jnp.where(rowh == laneh, qt, jnp.zeros_like(qt))
        s_old = _dot(qm, ck_ref[sq].astype(BF16)) + bias_old_ref[...]
        s_new = lax.dot_general(qm, kn_s[rows, :], nt_dims,
                                preferred_element_type=F32) + bias_new_ref[...]
        m = jnp.maximum(jnp.max(s_old, axis=1, keepdims=True),
                        jnp.max(s_new, axis=1, keepdims=True))
        p_old = jnp.exp2(s_old - m)
        p_new = jnp.exp2(s_new - m)
        l = jnp.sum(p_old, axis=1, keepdims=True) + jnp.sum(p_new, axis=1, keepdims=True)
        o = lax.dot_general(p_old.astype(BF16), cv_ref[sq].astype(BF16), nt_dims,
                            preferred_element_type=F32)
        o = (o + _dot(p_new.astype(BF16), vn_s[rows, :])) * (1.0 / l)
        att = o[0:slen]
        for h in range(1, N_HEADS):
            att = jnp.where(lane_o == h, o[h * slen:(h + 1) * slen], att)
        att_s[rows, :] = att.astype(BF16)

    @pl.when(n == nseq // SAMPLE_SEQS - 1)
    def _():
        gates = _dot(h_s[...], win_ref[:, 3 * D_ATT + 3 * D_CONV:])
        pa = _dot(att_s[...], wpa_ref[...])
        pb = _dot(conv_s[...], wpb_ref[...])
        mixin = _sigmoid(gates[:, 0:D_MODEL]) * pa + _sigmoid(gates[:, D_MODEL:]) * pb
        x1_ref[...] = _dot(mixin.astype(BF16), wo_ref[...])

        def res_body(i, carry):
            r = pl.ds(pl.multiple_of(i * slen, slen), slen)
            m = mod_ref[i]
            x1 = x_ref[r, :] + m[2:3, :] * x1_ref[r, :]
            x1_ref[r, :] = x1
            h2_s[r, :] = _rms(x1, g2_ref[...]) * (1.0 + m[4:5, :]) + m[3:4, :]
            return carry
        lax.fori_loop(0, nseq, res_body, 0)

        h2 = h2_s[...]
        _store_rows_as_tiles(h2_ref, _pack_rows(h2))
        route, new_cnt = _route(h2, whi_ref, br_ref, cnt_in_ref[...])
        route_ref[...] = route
        cnt_ref[...] = new_cnt


def _sample_main(x2d, mod, ck, cv, up1, up2, cnt_in, g1, g2, win, bias_old, bias_new, cw, cbias,
                 wpa, wpb, wo, whi, br, nseq, slen):
    ntok = nseq * slen
    args = (x2d, mod, ck, cv, up1, up2, cnt_in, g1, g2, win, bias_old, bias_new, cw, cbias,
            wpa, wpb, wo, whi, br)
    in_specs = []
    for idx, a in enumerate(args):
        if idx in (2, 3):
            in_specs.append(pl.BlockSpec((SAMPLE_SEQS,) + a.shape[1:], lambda n: (n, 0, 0)))
        else:
            in_specs.append(_const_spec(a.shape))
    whole = lambda shape: pl.BlockSpec(shape, lambda n: (0,) * len(shape))
    outs = [((ntok, D_MODEL), F32), ((ntok * ROW_TILES, LANES), I32), ((ntok, D_ATT), F32),
            ((ntok, D_ATT), F32), ((ntok, D_CONV), F32), ((ntok, RLANES), F32), ((1, RLANES), F32)]
    scratch = [
        pltpu.VMEM((ntok, D_MODEL), BF16), pltpu.VMEM((ntok, D_ATT), BF16),
        pltpu.VMEM((ntok, D_ATT), BF16), pltpu.VMEM((ntok, D_ATT), BF16),
        pltpu.VMEM((ntok, D_ATT), BF16), pltpu.VMEM((ntok, D_CONV), BF16),
        pltpu.VMEM((ntok, D_MODEL), F32),
    ]
    return pl.pallas_call(
        functools.partial(_sample_kernel, nseq=nseq, slen=slen),
        grid=(nseq // SAMPLE_SEQS,),
        in_specs=in_specs,
        out_specs=[whole(s) for s, _ in outs],
        out_shape=[jax.ShapeDtypeStruct(s, d) for s, d in outs],
        scratch_shapes=scratch,
        compiler_params=pltpu.CompilerParams(
            dimension_semantics=("arbitrary",), vmem_limit_bytes=VMEM_LIMIT),
        name="sample_main",
    )(*args)


def _issue_rows(n, body):
    def group(g, carry):
        for u in range(ISSUE_UNROLL):
            body(g * ISSUE_UNROLL + u, u)
        return carry
    lax.fori_loop(0, n // ISSUE_UNROLL, group, 0)


def _dispatch_kernel(pend_ref, cnt_ref, d1_ref, d2_ref, hp_ref, hs_ref, xs_out, zbuf, sem, zsem, *,
                     np_tiles, nslots):
    i = pl.program_id(0)
    zrows = ZERO_ROWS * ROW_TILES

    @pl.when(i == 0)
    def _():
        zbuf[...] = jnp.zeros_like(zbuf)

        def zcopy(piece):
            start = pl.multiple_of(piece * zrows, zrows)
            return pltpu.make_async_copy(zbuf, xs_out.at[pl.ds(start, zrows)], zsem)

        def pieces(e):
            lo = (pend_ref[e] + cnt_ref[jnp.minimum(e, N_EXPERTS - 1)]) // ZERO_ROWS
            hi = pend_ref[jnp.minimum(e + 1, N_EXPERTS)] // ZERO_ROWS
            lo = jnp.where(e == N_EXPERTS, pend_ref[N_EXPERTS] // ZERO_ROWS, lo)
            hi = jnp.where(e == N_EXPERTS, nslots // ZERO_ROWS, hi)
            return lo, hi

        def start_all(e, carry):
            lo, hi = pieces(e)
            return lax.fori_loop(lo, hi, lambda p, c: (zcopy(p).start(), c)[1], carry)

        def wait_all(e, carry):
            lo, hi = pieces(e)
            return lax.fori_loop(lo, hi, lambda p, c: (zcopy(p).wait(), c)[1], carry)
        lax.fori_loop(0, N_EXPERTS + 1, start_all, 0)
        lax.fori_loop(0, N_EXPERTS + 1, wait_all, 0)

    def scatter_tile(src):
        def row(r, u):
            pltpu.make_async_copy(_tile(src, r), _tile(xs_out, d1_ref[r]), sem).start(priority=u % 2)
            pltpu.make_async_copy(_tile(src, r), _tile(xs_out, d2_ref[r]), sem).start(
                priority=(u + 1) % 2)
        _issue_rows(TM, row)
        for _ in range(2):
            pltpu.make_async_copy(src, xs_out.at[pl.ds(0, TM * ROW_TILES)], sem).wait()

    @pl.when(i < np_tiles)
    def _():
        scatter_tile(hp_ref)

    @pl.when(i >= np_tiles)
    def _():
        scatter_tile(hs_ref)


def _dispatch(pend, counts, d1, d2, h2p, h2s, nslots):
    np_tiles = h2p.shape[0] // (TM * ROW_TILES)
    ns_tiles = h2s.shape[0] // (TM * ROW_TILES)
    smem_tile = pl.BlockSpec((TM,), lambda i, *_: (i,), memory_space=pltpu.SMEM)
    rows = TM * ROW_TILES
    return pl.pallas_call(
        functools.partial(_dispatch_kernel, np_tiles=np_tiles, nslots=nslots),
        grid_spec=pltpu.PrefetchScalarGridSpec(
            num_scalar_prefetch=2,
            grid=(np_tiles + ns_tiles,),
            in_specs=[smem_tile, smem_tile,
                      pl.BlockSpec((rows, LANES), lambda i, *_: (jnp.minimum(i, np_tiles - 1), 0)),
                      pl.BlockSpec((rows, LANES), lambda i, *_: (jnp.maximum(i - np_tiles, 0), 0))],
            out_specs=pl.BlockSpec(memory_space=pl.ANY),
            scratch_shapes=[pltpu.VMEM((ZERO_ROWS * ROW_TILES, LANES), I32),
                            pltpu.SemaphoreType.DMA(()), pltpu.SemaphoreType.DMA(())],
        ),
        out_shape=jax.ShapeDtypeStruct((nslots * ROW_TILES, LANES), I32),
        compiler_params=pltpu.CompilerParams(dimension_semantics=("arbitrary",)),
        name="dispatch",
    )(pend, counts, d1, d2, h2p, h2s)


def _expert_kernel(blk_e_ref, nblk_ref, first_ref, wslot_ref, next_e_ref,
                   xs_ref, wg_hbm, wu_hbm, wd_hbm, y_ref,
                   wg_f, wu_f, wd_f, wg_b, wu_b, wd_b, wsem):
    i = pl.program_id(0)
    live = i < nblk_ref[0]

    def weight_copies(e, slot):
        return (pltpu.make_async_copy(wg_hbm.at[e], wg_f.at[slot], wsem.at[slot, 0]),
                pltpu.make_async_copy(wu_hbm.at[e], wu_f.at[slot], wsem.at[slot, 1]),
                pltpu.make_async_copy(wd_hbm.at[e], wd_f.at[slot], wsem.at[slot, 2]))

    @pl.when(i == 0)
    def _():
        for cp in weight_copies(blk_e_ref[0], 0):
            cp.start()

    @pl.when(live & (first_ref[i] == 1))
    def _():
        slot = wslot_ref[i]
        for cp in weight_copies(blk_e_ref[i], slot):
            cp.wait()
        wg_b[...] = wg_f[slot].astype(BF16)
        wu_b[...] = wu_f[slot].astype(BF16)
        wd_b[...] = wd_f[slot].astype(BF16)

        @pl.when(next_e_ref[i] >= 0)
        def _():
            for cp in weight_copies(next_e_ref[i], 1 - slot):
                cp.start()

    @pl.when(live)
    def _():
        x_lo, x_hi = _unpack_rows(_load_tiles_as_rows(xs_ref, MOE_BLK))
        x_lo = x_lo.astype(BF16)
        x_hi = x_hi.astype(BF16)
        g = _dot(x_lo, wg_b[0:HALF, :]) + _dot(x_hi, wg_b[HALF:, :])
        u = _dot(x_lo, wu_b[0:HALF, :]) + _dot(x_hi, wu_b[HALF:, :])
        a = (g * _sigmoid(g)) * u
        _store_rows_as_tiles(y_ref, _pack_rows(_dot(a.astype(BF16), wd_b[...])))

    @pl.when(jnp.logical_not(live))
    def _():
        y_ref[...] = jnp.zeros_like(y_ref)


def _experts(blk_e, nblk, first, wslot, next_e, xs, wg, wu, wd):
    blk_rows = MOE_BLK * ROW_TILES
    nblocks = xs.shape[0] // blk_rows
    row_map = lambda i, be, nb, *_: (jnp.minimum(i, nb[0] - 1), 0)
    any_spec = pl.BlockSpec(memory_space=pl.ANY)
    return pl.pallas_call(
        _expert_kernel,
        grid_spec=pltpu.PrefetchScalarGridSpec(
            num_scalar_prefetch=5,
            grid=(nblocks,),
            in_specs=[pl.BlockSpec((blk_rows, LANES), row_map), any_spec, any_spec, any_spec],
            out_specs=pl.BlockSpec((blk_rows, LANES), lambda i, *_: (i, 0)),
            scratch_shapes=[pltpu.VMEM((2, D_MODEL, D_EXPERT), F32),
                            pltpu.VMEM((2, D_MODEL, D_EXPERT), F32),
                            pltpu.VMEM((2, D_EXPERT, D_MODEL), F32),
                            pltpu.VMEM((D_MODEL, D_EXPERT), BF16),
                            pltpu.VMEM((D_MODEL, D_EXPERT), BF16),
                            pltpu.VMEM((D_EXPERT, D_MODEL), BF16),
                            pltpu.SemaphoreType.DMA((2, 3))],
        ),
        out_shape=jax.ShapeDtypeStruct(xs.shape, I32),
        compiler_params=pltpu.CompilerParams(
            dimension_semantics=("arbitrary",), vmem_limit_bytes=VMEM_LIMIT),
        name="experts",
    )(blk_e, nblk, first, wslot, next_e, xs, wg, wu, wd)


def _combine_kernel(d1_ref, d2_ref, d1n_ref, d2n_ref, y_hbm, x1_ref, route_ref, gate_ref, gf_ref,
                    o_ref, a0, b0, a1, b1, sem, *, ntiles):
    t = pl.program_id(0)
    bufs = ((a0, b0), (a1, b1))

    def gather(i1_ref, i2_ref, par):
        buf_a, buf_b = bufs[par]

        def row(r, u):
            pltpu.make_async_copy(_tile(y_hbm, i1_ref[r]), _tile(buf_a, r),
                                  sem.at[par]).start(priority=u % 2)
            pltpu.make_async_copy(_tile(y_hbm, i2_ref[r]), _tile(buf_b, r),
                                  sem.at[par]).start(priority=(u + 1) % 2)
        _issue_rows(TM, row)

    @pl.when(t == 0)
    def _():
        gather(d1_ref, d2_ref, 0)

    def step(par):
        buf_a, buf_b = bufs[par]
        for buf in (buf_a, buf_b):
            pltpu.make_async_copy(y_hbm.at[pl.ds(0, TM * ROW_TILES)], buf, sem.at[par]).wait()

        @pl.when(t + 1 < ntiles)
        def _():
            gather(d1n_ref, d2n_ref, 1 - par)

        route = route_ref[0]
        w1 = route[:, 4:5]
        w2 = route[:, 5:6]
        a_lo, a_hi = _unpack_rows(_load_tiles_as_rows(buf_a, TM))
        b_lo, b_hi = _unpack_rows(_load_tiles_as_rows(buf_b, TM))
        ffn = jnp.concatenate([w1 * a_lo + w2 * b_lo, w1 * a_hi + w2 * b_hi], axis=1)
        x2 = x1_ref[0] + gate_ref[0] * ffn
        o_ref[0] = _rms(x2, gf_ref[...])

    for par in range(2):
        pl.when(t % 2 == par)(functools.partial(step, par))


def _combine(d1, d2, y, x1, route, gate, gf, tok_base):
    nb, seq, _ = x1.shape
    nt = seq // TM
    ntiles = nb * nt
    blk0 = tok_base // TM
    smem = lambda fn: pl.BlockSpec((TM,), fn, memory_space=pltpu.SMEM)
    cur = lambda t: (blk0 + t,)
    nxt = lambda t: (blk0 + jnp.minimum(t + 1, ntiles - 1),)
    tile = lambda t: (t // nt, t % nt, 0)
    grows = gate.shape[1]
    gate_spec = (pl.BlockSpec((1, 1, D_MODEL), lambda t: (t // nt, 0, 0)) if grows == 1
                 else pl.BlockSpec((1, TM, D_MODEL), tile))
    return pl.pallas_call(
        functools.partial(_combine_kernel, ntiles=ntiles),
        grid=(ntiles,),
        in_specs=[smem(cur), smem(cur), smem(nxt), smem(nxt), pl.BlockSpec(memory_space=pl.ANY),
                  pl.BlockSpec((1, TM, D_MODEL), tile),
                  pl.BlockSpec((1, TM, RLANES), tile),
                  gate_spec,
                  pl.BlockSpec((1, D_MODEL), lambda t: (0, 0))],
        out_specs=pl.BlockSpec((1, TM, D_MODEL), tile),
        out_shape=jax.ShapeDtypeStruct(x1.shape, F32),
        scratch_shapes=[pltpu.VMEM((TM * ROW_TILES, LANES), I32)] * 4 + [
            pltpu.SemaphoreType.DMA((2,))],
        compiler_params=pltpu.CompilerParams(
            dimension_semantics=("arbitrary",), vmem_limit_bytes=VMEM_LIMIT),
        name="combine",
    )(d1, d2, d1, d2, y, x1, route, gate, gf)


def _band_bias(rel_bias, rows, keys):
    n = rows - 1 + keys
    dist = WINDOW + rows - 1 - np.arange(n + 1)
    flipped = rel_bias[:, np.clip(dist, -MAX_REL, MAX_REL) + MAX_REL]
    skew = jnp.tile(flipped, (1, rows))[:, :rows * n].reshape(N_HEADS, rows, n)
    b = skew[:, :, rows - 1:rows - 1 + keys]
    return b.reshape(2, 4 * rows, keys)


def kernel(x_prompt, x_sample, cache_attn_k, cache_attn_v, state_conv, c_prompt, c_sample,
           w_ada, b_ada, norm1_g, norm2_g, w_in, rel_bias, conv_w, conv_b, w_pa, w_pb, w_o,
           w_group, b_group, w_expert, b_expert, w_e_gate, w_e_up, w_e_down, final_g):
    assert w_ada.shape[0] == 1, "single trunk layer"
    nb, seq, _ = x_prompt.shape
    nseq, slen, _ = x_sample.shape
    ntok_p = nb * seq
    ntok_s = nseq * slen
    ntok = ntok_p + ntok_s
    assert seq % TL == 0 and WINDOW % TL == 0 and seq % TM == 0 and ntok_s % TM == 0
    assert slen >= 2 and slen & (slen - 1) == 0 and slen % 16 == 0 and nseq % SAMPLE_SEQS == 0
    assert MOE_BLK % ZERO_ROWS == 0

    n_c = nb + nseq
    n_pad = -(-n_c // 8) * 8
    c_all = jnp.concatenate([c_prompt, c_sample, jnp.zeros((n_pad - n_c, D_MODEL), F32)], axis=0)
    mod = _ada(c_all, w_ada[0], b_ada[0]).reshape(n_pad, 6, D_MODEL)
    mod_p = mod[:nb]
    mod_s = mod[nb:n_c]

    win = w_in[0].astype(BF16)
    wpa = w_pa[0].astype(BF16)
    wpb = w_pb[0].astype(BF16)
    wo = w_o[0].astype(BF16)
    g1 = norm1_g[0].reshape(1, D_MODEL)
    g2 = norm2_g[0].reshape(1, D_MODEL)
    gf = final_g.reshape(1, D_MODEL)
    cw = jnp.concatenate([conv_w[0], jnp.zeros((8 - conv_w.shape[1], D_CONV), F32)], axis=0)
    cbias = conv_b[0].reshape(1, D_CONV)
    wr = jnp.concatenate([w_expert[0], w_group[0],
                          jnp.zeros((D_MODEL, RLANES - N_EXPERTS - N_GROUPS), F32)], axis=1)
    whi = wr.astype(BF16)
    br = jnp.concatenate([b_expert[0], b_group[0],
                          jnp.zeros((RLANES - N_EXPERTS - N_GROUPS,), F32)]).reshape(1, RLANES)
    in_band = np.arange(PAIR_BAND)[None, :] - (np.arange(PAIR)[:, None] // CHUNK) * CHUNK
    in_band = np.tile((in_band >= 0) & (in_band < BAND), (4, 1))
    bias_p = jnp.where(in_band, _band_bias(rel_bias[0] * LOG2E, PAIR, PAIR_BAND), NEG)
    bias_s = _band_bias(rel_bias[0] * LOG2E, slen, WINDOW + slen).reshape(
        N_HEADS * slen, WINDOW + slen)
    bias_old = bias_s[:, :WINDOW]
    bias_new = bias_s[:, WINDOW:]

    x1p, h2p, kp, vp, up8, route_p, cnt_p = _prompt_main(
        x_prompt, mod_p, g1, g2, win, bias_p, cw, cbias, wpa, wpb, wo, whi, br)

    st = state_conv[0]
    up1 = jnp.zeros((nseq, slen, D_CONV), F32).at[:, 0].set(st[:, 1]).reshape(ntok_s, D_CONV)
    up2 = (jnp.zeros((nseq, slen, D_CONV), F32).at[:, 0].set(st[:, 0]).at[:, 1].set(st[:, 1])
           .reshape(ntok_s, D_CONV))
    ck = jnp.transpose(cache_attn_k[0], (0, 2, 3, 1)).reshape(nseq, D_ATT, WINDOW)
    cv = jnp.transpose(cache_attn_v[0], (0, 2, 3, 1)).reshape(nseq, D_ATT, WINDOW)
    x1s, h2s, ks, vs, us, route_s, cnt = _sample_main(
        x_sample.reshape(ntok_s, D_MODEL), mod_s, ck, cv, up1, up2, cnt_p,
        g1, g2, win, bias_old, bias_new, cw, cbias, wpa, wpb, wo, whi, br, nseq, slen)

    route_all = jnp.concatenate([route_p.reshape(ntok_p, RLANES)[:, :4], route_s[:, :4]], axis=0)
    experts = route_all[:, 0:2].astype(jnp.int32)
    ranks = route_all[:, 2:4].astype(jnp.int32)
    counts = cnt[0, :N_EXPERTS].astype(jnp.int32)
    pcounts = (counts + MOE_BLK - 1) // MOE_BLK * MOE_BLK
    pend = jnp.cumsum(pcounts)
    pstart = pend - pcounts
    eids = jnp.arange(N_EXPERTS, dtype=jnp.int32)
    dest = jnp.sum(jnp.where(experts[..., None] == eids, pstart, 0), axis=-1) + ranks
    d1 = dest[:, 0]
    d2 = dest[:, 1]
    nblocks = (2 * ntok) // MOE_BLK + N_EXPERTS
    blk_start = jnp.arange(nblocks, dtype=jnp.int32) * MOE_BLK
    blk_e = jnp.minimum(jnp.sum((pend[None, :] <= blk_start[:, None]).astype(jnp.int32), axis=1),
                        N_EXPERTS - 1)
    nblk = (pend[-1:] // MOE_BLK).astype(jnp.int32)
    pend0 = jnp.concatenate([jnp.zeros((1,), jnp.int32), pend.astype(jnp.int32)])

    xs = _dispatch(pend0, counts, d1, d2, h2p, h2s, nblocks * MOE_BLK)
    blk_id = jnp.arange(nblocks, dtype=jnp.int32)
    first = (blk_id < nblk[0]) & ((blk_id == 0) | (blk_e != jnp.roll(blk_e, 1)))
    wslot = (jnp.cumsum(first.astype(jnp.int32)) - 1) % 2
    later_first = lax.cummin(jnp.where(first, blk_id, nblocks)[::-1])[::-1]
    next_first = jnp.concatenate([later_first[1:], jnp.full((1,), nblocks, jnp.int32)])
    next_e = jnp.where(next_first < nblocks, blk_e[jnp.minimum(next_first, nblocks - 1)], -1)
    y = _experts(blk_e, nblk, first.astype(jnp.int32), wslot.astype(jnp.int32),
                 next_e.astype(jnp.int32), xs, w_e_gate[0], w_e_up[0], w_e_down[0])

    y_prompt = _combine(d1, d2, y, x1p, route_p, mod_p[:, 5:6, :], gf, 0)
    gate_s = jnp.repeat(mod_s[:, 5, :], slen, axis=0).reshape(1, ntok_s, D_MODEL)
    y_sample = _combine(d1, d2, y, x1s.reshape(1, ntok_s, D_MODEL),
                        route_s.reshape(1, ntok_s, RLANES), gate_s, gf, ntok_p)

    new_k_p = kp.reshape(1, nb, WINDOW, N_HEADS, HEAD_DIM)
    new_v_p = vp.reshape(1, nb, WINDOW, N_HEADS, HEAD_DIM)
    new_conv_p = up8[:, 6:8, :].reshape(1, nb, 2, D_CONV)
    new_k_s = ks.reshape(1, nseq, slen, N_HEADS, HEAD_DIM)
    new_v_s = vs.reshape(1, nseq, slen, N_HEADS, HEAD_DIM)
    new_conv_s = us.reshape(nseq, slen, D_CONV)[:, slen - 2:, :].reshape(1, nseq, 2, D_CONV)
    return (y_prompt, y_sample.reshape(nseq, slen, D_MODEL), new_k_p, new_v_p, new_conv_p,
            new_k_s, new_v_s, new_conv_s)
```
